```python
import math
import jax, jax.numpy as jnp
from jax import lax
import numpy as np

D_MODEL = 2048
BATCH = 2
SEQ = 4096
DEPTH = 1
DEC_BATCH = 8
DEC_SEQ = 32
PAST_LEN = 4096

CHUNK = 64
E_SSD = D_MODEL
SSD_HEAD_DIM = 64
SSD_HEADS = E_SSD // SSD_HEAD_DIM
SSD_GROUPS = 4
D_STATE = 128
CONV_W = 4
CONV_DIM = E_SSD + 2 * SSD_GROUPS * D_STATE
E_MLP = D_MODEL
MLP_CHUNK = 128
MLP_GROUPS = 8
MLP_GROUP_DIM = E_MLP // MLP_GROUPS
IN_SIZES = (E_SSD, CONV_DIM, SSD_HEADS, E_MLP, E_MLP, E_MLP, D_MODEL, D_MODEL)
D_IN = E_SSD + CONV_DIM + SSD_HEADS + 3 * E_MLP + 2 * D_MODEL
EPS = 1e-5

kernel_name = "hybrid_ssd_chunkmlp_stream_step"


def _split(a, sizes):
    idx, s = [], 0
    for n in sizes[:-1]:
        s += n
        idx.append(s)
    return jnp.split(a, idx, axis=-1)


def _rmsnorm(x, g):
    xf = x.astype(jnp.float32)
    y = xf * lax.rsqrt(jnp.mean(xf * xf, axis=-1, keepdims=True) + EPS)
    return (y * g.astype(jnp.float32)).astype(x.dtype)


def _layernorm(x, g, b):
    xf = x.astype(jnp.float32)
    mu = jnp.mean(xf, axis=-1, keepdims=True)
    var = jnp.mean(jnp.square(xf - mu), axis=-1, keepdims=True)
    y = (xf - mu) * lax.rsqrt(var + EPS)
    return (y * g.astype(jnp.float32) + b.astype(jnp.float32)).astype(x.dtype)


def _causal_dwconv(xbc, hist, w, b):
    L = xbc.shape[1]
    full = jnp.concatenate([hist.astype(xbc.dtype), xbc], axis=1)
    out = b
    for k in range(CONV_W):
        out = out + full[:, k:k + L] * w[k]
    return jax.nn.silu(out), full[:, L:]


def _ssd_scan(x, dt, a, bm, cm, h0, q):
    f32 = jnp.float32
    bsz, L = x.shape[:2]
    nc = L // q
    G, R = SSD_GROUPS, SSD_HEADS // SSD_GROUPS
    P, N = SSD_HEAD_DIM, D_STATE
    xc = x.astype(f32).reshape(bsz, nc, q, G, R, P)
    dtc = dt.astype(f32).reshape(bsz, nc, q, G, R)
    bc = bm.astype(f32).reshape(bsz, nc, q, G, N)
    cc = cm.astype(f32).reshape(bsz, nc, q, G, N)
    acum = jnp.cumsum(dtc * a.astype(f32).reshape(G, R), axis=2)
    xdt = xc * dtc[..., None]
    seg = acum[:, :, :, None] - acum[:, :, None, :]
    mask = jnp.tril(jnp.ones((q, q), bool))[:, :, None, None]
    lmat = jnp.exp(jnp.where(mask, seg, -jnp.inf))
    cb = jnp.einsum('bcign,bcjgn->bcijg', cc, bc)
    y_diag = jnp.einsum('bcijg,bcijgr,bcjgrp->bcigrp', cb, lmat, xdt)
    decay_to_end = jnp.exp(acum[:, :, -1:] - acum)
    states = jnp.einsum('bcjgn,bcjgr,bcjgrp->bcgrpn', bc, decay_to_end, xdt)
    block_decay = jnp.exp(acum[:, :, -1])

    def step(h, inp):
        s, d = inp
        return d[..., None, None] * h + s, h

    h_init = h0.astype(f32).reshape(bsz, G, R, P, N)
    h_last, h_prev = lax.scan(step, h_init, (jnp.swapaxes(states, 0, 1), jnp.swapaxes(block_decay, 0, 1)))
    h_prev = jnp.swapaxes(h_prev, 0, 1)
    y_off = jnp.einsum('bcign,bcgrpn,bcigr->bcigrp', cc, h_prev, jnp.exp(acum))
    y = (y_diag + y_off).reshape(bsz, L, SSD_HEADS, P)
    return y, h_last.reshape(bsz, SSD_HEADS, P, N).astype(h0.dtype)


def _layer(x, c, conv_hist, h0, w_ada, b_ada, g_pre, w_in, conv_w, conv_b, dt_bias, a_log,
           d_skip, norm_ssd, ln_g, ln_b, w_spatial, b_spatial, w_bp_ssd, w_bp_mlp, w_o):
    f32 = jnp.float32
    bsz, L, _ = x.shape
    mods = jax.nn.silu(c) @ w_ada + b_ada
    shift, scale, gate = jnp.split(mods, 3, axis=-1)
    h = _rmsnorm(x, g_pre) * (1 + scale[:, None]) + shift[:, None]
    proj = h @ w_in
    z_s, xbc, dt_raw, u, v, z_m, g_s, g_m = _split(proj, IN_SIZES)
    xbc, new_hist = _causal_dwconv(xbc, conv_hist, conv_w, conv_b)
    xs, bm, cm = _split(xbc, (E_SSD, SSD_GROUPS * D_STATE, SSD_GROUPS * D_STATE))
    dt = jax.nn.softplus(dt_raw.astype(f32) + dt_bias.astype(f32))
    a = -jnp.exp(a_log.astype(f32))
    xs_h = xs.reshape(bsz, L, SSD_HEADS, SSD_HEAD_DIM).astype(f32)
    y, h_new = _ssd_scan(xs_h, dt, a,
                         bm.reshape(bsz, L, SSD_GROUPS, D_STATE),
                         cm.reshape(bsz, L, SSD_GROUPS, D_STATE),
                         h0, min(L, CHUNK))
    y = y + d_skip.astype(f32)[:, None] * xs_h
    y_ssd = _rmsnorm(y.reshape(bsz, L, E_SSD) * jax.nn.silu(z_s.astype(f32)), norm_ssd)
    vn = _layernorm(v, ln_g, ln_b)
    qm = min(L, MLP_CHUNK)
    wm = jnp.where(jnp.tril(jnp.ones((qm, qm), bool)), w_spatial[:, :qm, :qm], 0)
    vr = vn.reshape(bsz, L // qm, qm, MLP_GROUPS, MLP_GROUP_DIM)
    mixed = jnp.einsum('gij,bnjgc->bnigc', wm, vr) + b_spatial[:, :qm].T[:, :, None]
    y_mlp = u * mixed.reshape(bsz, L, E_MLP) * jax.nn.silu(z_m)
    merged = (jax.nn.sigmoid(g_s) * (y_ssd @ w_bp_ssd)
              + jax.nn.sigmoid(g_m) * (y_mlp @ w_bp_mlp))
    out = x + gate[:, None] * (merged @ w_o)
    return out.astype(x.dtype), new_hist, h_new, vn


def setup_inputs(seed: int = 0) -> dict:
    key = jax.random.key(seed)
    ks = jax.random.split(key, 32)
    nrm = jax.random.normal
    dt0 = jnp.exp(jax.random.uniform(ks[14], (DEPTH, SSD_HEADS), minval=math.log(1e-3), maxval=math.log(0.1)))
    return {
        "x_prompt": nrm(ks[0], (BATCH, SEQ, D_MODEL), jnp.float32),
        "x_sample": nrm(ks[1], (DEC_BATCH, DEC_SEQ, D_MODEL), jnp.float32),
        "state_ssm": 0.1 * nrm(ks[2], (DEPTH, DEC_BATCH, SSD_HEADS, SSD_HEAD_DIM, D_STATE), jnp.float32),
        "cache_conv": nrm(ks[3], (DEPTH, DEC_BATCH, CONV_W - 1, CONV_DIM), jnp.float32),
        "c_prompt": nrm(ks[4], (BATCH, D_MODEL), jnp.float32),
        "c_sample": nrm(ks[5], (DEC_BATCH, D_MODEL), jnp.float32),
        "w_ada": 0.5 * D_MODEL ** -0.5 * nrm(ks[6], (DEPTH, D_MODEL, 3 * D_MODEL), jnp.float32),
        "b_ada": 0.02 * nrm(ks[7], (DEPTH, 3 * D_MODEL), jnp.float32),
        "g_pre": 1.0 + 0.02 * nrm(ks[8], (DEPTH, D_MODEL), jnp.float32),
        "w_in": D_MODEL ** -0.5 * nrm(ks[9], (DEPTH, D_MODEL, D_IN), jnp.float32),
        "conv_w": CONV_W ** -0.5 * nrm(ks[10], (DEPTH, CONV_W, CONV_DIM), jnp.float32),
        "conv_b": 0.02 * nrm(ks[11], (DEPTH, CONV_DIM), jnp.float32),
        "dt_bias": dt0 + jnp.log(-jnp.expm1(-dt0)),
        "a_log": jnp.log(jax.random.uniform(ks[12], (DEPTH, SSD_HEADS), minval=1.0, maxval=16.0)),
        "d_skip": 1.0 + 0.02 * nrm(ks[13], (DEPTH, SSD_HEADS), jnp.float32),
        "norm_ssd": 1.0 + 0.02 * nrm(ks[15], (DEPTH, E_SSD), jnp.float32),
        "ln_g": 1.0 + 0.02 * nrm(ks[16], (DEPTH, E_MLP), jnp.float32),
        "ln_b": 0.02 * nrm(ks[17], (DEPTH, E_MLP), jnp.float32),
        "w_spatial": MLP_CHUNK ** -0.5 * nrm(ks[18], (DEPTH, MLP_GROUPS, MLP_CHUNK, MLP_CHUNK), jnp.float32),
        "b_spatial": 1.0 + 0.02 * nrm(ks[19], (DEPTH, MLP_GROUPS, MLP_CHUNK), jnp.float32),
        "w_bp_ssd": E_SSD ** -0.5 * nrm(ks[20], (DEPTH, E_SSD, D_MODEL), jnp.float32),
        "w_bp_mlp": E_MLP ** -0.5 * nrm(ks[21], (DEPTH, E_MLP, D_MODEL), jnp.float32),
        "w_o": D_MODEL ** -0.5 * nrm(ks[22], (DEPTH, D_MODEL, D_MODEL), jnp.float32),
        "g_final": 1.0 + 0.02 * nrm(ks[23], (D_MODEL,), jnp.float32),
    }


def reference(x_prompt, x_sample, state_ssm, cache_conv, c_prompt, c_sample, w_ada, b_ada, g_pre,
              w_in, conv_w, conv_b, dt_bias, a_log, d_skip, norm_ssd, ln_g, ln_b, w_spatial,
              b_spatial, w_bp_ssd, w_bp_mlp, w_o, g_final):
    yp, ys = x_prompt, x_sample
    bp = x_prompt.shape[0]
    ssm_p, conv_p, ssm_s, conv_s, v_s = [], [], [], [], []
    for l in range(DEPTH):
        w = (w_ada[l], b_ada[l], g_pre[l], w_in[l], conv_w[l], conv_b[l], dt_bias[l], a_log[l],
             d_skip[l], norm_ssd[l], ln_g[l], ln_b[l], w_spatial[l], b_spatial[l],
             w_bp_ssd[l], w_bp_mlp[l], w_o[l])
        hist0 = jnp.zeros((bp, CONV_W - 1, CONV_DIM), x_prompt.dtype)
        h0 = jnp.zeros((bp, SSD_HEADS, SSD_HEAD_DIM, D_STATE), state_ssm.dtype)
        yp, hist_p, h_p, _ = _layer(yp, c_prompt, hist0, h0, *w)
        ys, hist_s, h_s, vn_s = _layer(ys, c_sample, cache_conv[l], state_ssm[l], *w)
        ssm_p.append(h_p)
        conv_p.append(hist_p)
        ssm_s.append(h_s)
        conv_s.append(hist_s)
        v_s.append(vn_s)
    y_prompt = _rmsnorm(yp, g_final)
    y_sample = _rmsnorm(ys, g_final)
    return (y_prompt, y_sample, jnp.stack(ssm_p), jnp.stack(conv_p), jnp.stack(ssm_s), jnp.stack(conv_s), jnp.stack(v_s))
```

```python
import functools

import numpy as np
import jax
import jax.numpy as jnp
from jax import lax
from jax.experimental import pallas as pl
from jax.experimental.pallas import tpu as pltpu

F32 = jnp.float32
BF16 = jnp.bfloat16

D_MODEL = 2048
SSD_HEADS = 32
SSD_HEAD_DIM = 64
SSD_GROUPS = 4
HEADS_PER_GROUP = SSD_HEADS // SSD_GROUPS
D_STATE = 128
GROUP_COLS = HEADS_PER_GROUP * SSD_HEAD_DIM
CONV_W = 4
CONV_DIM = D_MODEL + 2 * SSD_GROUPS * D_STATE
MLP_CHUNK = 128
MLP_GROUPS = 8
MLP_GROUP_DIM = D_MODEL // MLP_GROUPS
EPS = 1e-5

SUBLANES = 8
LANES = 128
SSD_BLOCK = 128
PROJ_TN = 1024
VMEM_LIMIT = 56 * 1024 * 1024

N_ACT = 7 * D_MODEL + 2 * SSD_GROUPS * D_STATE
N_PROJ_TILES = N_ACT // PROJ_TN
SILU_TILES = (0, 1, 8, 9)
CONV_TILES = (2, 3, 14)
IDENT_TILES = (4, 5, 6, 7)
SIGMOID_TILES = (10, 11, 12, 13)


def _sigmoid(x):
    return 1.0 / (1.0 + jnp.exp(-x))


def _silu(x):
    return x * _sigmoid(x)


def _softplus(x):
    return jnp.maximum(x, 0.0) + jnp.log1p(jnp.exp(-jnp.abs(x)))


def _mods_kernel(c_ref, w_ref, b_ref, o_ref):
    s = _silu(c_ref[...]).astype(BF16)
    o_ref[...] = jnp.dot(s, w_ref[...].astype(BF16), preferred_element_type=F32) + b_ref[...]


def _mods(c_pad, w_ada, b_ada):
    rows, d = c_pad.shape
    n = w_ada.shape[1]
    tn = 1024
    return pl.pallas_call(
        _mods_kernel,
        grid=(n // tn,),
        in_specs=[
            pl.BlockSpec((rows, d), lambda j: (0, 0)),
            pl.BlockSpec((d, tn), lambda j: (0, j)),
            pl.BlockSpec((1, tn), lambda j: (0, j)),
        ],
        out_specs=pl.BlockSpec((rows, tn), lambda j: (0, j)),
        out_shape=jax.ShapeDtypeStruct((rows, n), F32),
        compiler_params=pltpu.CompilerParams(
            dimension_semantics=("arbitrary",), vmem_limit_bytes=VMEM_LIMIT),
        name="mods",
    )(c_pad, w_ada, b_ada)


def _conv_tile_index(j):
    return jnp.where(j <= CONV_TILES[0], 0, jnp.where(j < CONV_TILES[2], 1, 2))


def _proj_kernel(x_ref, sc_ref, sh_ref, gpre_ref, w_ref, wdt_ref, dtb_ref, cw_ref, cb_ref, hist_ref,
                 act_ref, dt_ref, tail_ref, h_scr, acc_scr, halo_scr,
                 *, tm, rb, spt, tiles_per_seq):
    i = pl.program_id(0)
    j = pl.program_id(1)
    rps = tm // spt
    n_rb = tm // rb

    @pl.when(j == 0)
    def _():
        for s in range(spt):
            rows = slice(s * rps, (s + 1) * rps)
            x = x_ref[rows, :]
            y = x * lax.rsqrt(jnp.mean(x * x, axis=-1, keepdims=True) + EPS)
            y = y * gpre_ref[...]
            h = y * (1.0 + sc_ref[s]) + sh_ref[s]
            h_scr[rows, :] = h.astype(BF16)
        dt_raw = jnp.dot(h_scr[...], wdt_ref[...], preferred_element_type=F32) + dtb_ref[...]
        lane = lax.broadcasted_iota(jnp.int32, dt_raw.shape, 1)
        dt_ref[...] = jnp.where(lane < SSD_HEADS, _softplus(dt_raw), 0.0)

    def matmul_rows(r):
        return jnp.dot(h_scr[r * rb:(r + 1) * rb, :], w_ref[...], preferred_element_type=F32)

    def elementwise_tiles(tiles, fn):
        @pl.when(functools.reduce(jnp.logical_or, [j == t for t in tiles]))
        def _():
            for r in range(n_rb):
                act_ref[r * rb:(r + 1) * rb, :] = fn(matmul_rows(r)).astype(BF16)

    elementwise_tiles(SILU_TILES, _silu)
    elementwise_tiles(IDENT_TILES, lambda a: a)
    elementwise_tiles(SIGMOID_TILES, _sigmoid)

    @pl.when(functools.reduce(jnp.logical_or, [j == t for t in CONV_TILES]))
    def _():
        c = _conv_tile_index(j)
        stride = rps + SUBLANES
        first = (i % tiles_per_seq) == 0
        for s in range(spt):
            prev = jnp.where(first, hist_ref[s], halo_scr[c]) if spt == 1 else hist_ref[s]
            acc_scr[s * stride:s * stride + SUBLANES, :] = prev
        w0, w1, w2, w3 = (cw_ref[k:k + 1, :] for k in range(CONV_W))
        for r in range(n_rb):
            acc = matmul_rows(r)
            sub = min(rb, rps)
            for p in range(rb // sub):
                row0 = r * rb + p * sub
                s, off = divmod(row0, rps)
                base = s * stride + SUBLANES + off
                cur = acc[p * sub:(p + 1) * sub, :]
                acc_scr[base:base + sub, :] = cur
                out = cb_ref[...] + acc_scr[base - 3:base - 3 + sub, :] * w0
                out = out + acc_scr[base - 2:base - 2 + sub, :] * w1
                out = out + acc_scr[base - 1:base - 1 + sub, :] * w2
                out = out + cur * w3
                act_ref[row0:row0 + sub, :] = _silu(out).astype(BF16)
        for s in range(spt):
            last = acc_scr[s * stride + rps:s * stride + rps + SUBLANES, :]
            tail_ref[s] = last
            if spt == 1:
                halo_scr[c] = last


def _proj(x2d, scale, shift, g_pre, w_main, w_dt, dt_bias, conv_w, conv_b, hist8, *, tm, rb, spt):
    rows = x2d.shape[0]
    n_seq = scale.shape[0]
    n_tiles = rows // tm
    tiles_per_seq = max(1, n_tiles // n_seq) if spt == 1 else 1
    seq_blk = (lambda i: i // tiles_per_seq) if spt == 1 else (lambda i: i)
    kern = functools.partial(_proj_kernel, tm=tm, rb=rb, spt=spt, tiles_per_seq=tiles_per_seq)
    rps = tm // spt
    return pl.pallas_call(
        kern,
        grid=(n_tiles, N_PROJ_TILES),
        in_specs=[
            pl.BlockSpec((tm, D_MODEL), lambda i, j: (i, 0)),
            pl.BlockSpec((spt, 1, D_MODEL), lambda i, j: (seq_blk(i), 0, 0)),
            pl.BlockSpec((spt, 1, D_MODEL), lambda i, j: (seq_blk(i), 0, 0)),
            pl.BlockSpec((1, D_MODEL), lambda i, j: (0, 0)),
            pl.BlockSpec((D_MODEL, PROJ_TN), lambda i, j: (0, j)),
            pl.BlockSpec((D_MODEL, LANES), lambda i, j: (0, 0)),
            pl.BlockSpec((1, LANES), lambda i, j: (0, 0)),
            pl.BlockSpec((CONV_W, PROJ_TN), lambda i, j: (0, _conv_tile_index(j))),
            pl.BlockSpec((1, PROJ_TN), lambda i, j: (0, _conv_tile_index(j))),
            pl.BlockSpec((spt, SUBLANES, PROJ_TN), lambda i, j: (seq_blk(i), 0, _conv_tile_index(j))),
        ],
        out_specs=[
            pl.BlockSpec((tm, PROJ_TN), lambda i, j: (i, j)),
            pl.BlockSpec((tm, LANES), lambda i, j: (i, 0)),
            pl.BlockSpec((spt, SUBLANES, PROJ_TN), lambda i, j: (i, 0, _conv_tile_index(j))),
        ],
        out_shape=[
            jax.ShapeDtypeStruct((rows, N_ACT), BF16),
            jax.ShapeDtypeStruct((rows, LANES), F32),
            jax.ShapeDtypeStruct((n_tiles * spt, SUBLANES, CONV_DIM), F32),
        ],
        scratch_shapes=[
            pltpu.VMEM((tm, D_MODEL), BF16),
            pltpu.VMEM((spt * (rps + SUBLANES), PROJ_TN), F32),
            pltpu.VMEM((CONV_DIM // PROJ_TN, SUBLANES, PROJ_TN), F32),
        ],
        compiler_params=pltpu.CompilerParams(
            dimension_semantics=("arbitrary", "arbitrary"), vmem_limit_bytes=VMEM_LIMIT),
        name="proj",
    )(x2d, scale, shift, g_pre, w_main, w_dt, dt_bias, conv_w, conv_b, hist8)


def _pad_rows(v, rows):
    if v.shape[0] == rows:
        return v
    return jnp.concatenate([v, jnp.zeros((rows - v.shape[0], v.shape[1]), v.dtype)], axis=0)


def _ssd_kernel(szs_ref, xs_ref, b_ref, c_ref, dt_ref, alog_ref, dskip_ref, nrm_ref, e3_ref, h0_ref,
                y_ref, hout_ref, st_scr, *, q_in, n_chunks):
    q = SSD_BLOCK
    ci = pl.program_id(1)

    @pl.when(ci == 0)
    def _():
        st_scr[...] = h0_ref[0]

    dt = _pad_rows(dt_ref[...], q)
    a = -jnp.exp(alog_ref[...])
    adt = dt * a
    row = lax.broadcasted_iota(jnp.int32, (q, q), 0)
    col = lax.broadcasted_iota(jnp.int32, (q, q), 1)
    causal = row >= col
    acum = jnp.dot(causal.astype(F32), adt, precision=lax.Precision.HIGHEST,
                   preferred_element_type=F32)
    alast = acum[q - 1:q, :]
    dte = jnp.exp(alast - acum)
    ea = jnp.exp(acum)

    stack = jnp.concatenate([dt, dt * dte, ea], axis=0)
    lane3 = lax.broadcasted_iota(jnp.int32, stack.shape, 1)
    stack = jnp.where(lane3 < SSD_HEADS, stack, 0.0)
    hi = stack.astype(BF16).astype(F32)
    r1 = stack - hi
    mid = r1.astype(BF16).astype(F32)
    lo = (r1 - mid).astype(BF16).astype(F32)
    comb = hi + pltpu.roll(mid, SSD_HEADS, axis=1) + pltpu.roll(lo, 2 * SSD_HEADS, axis=1)
    expd = jnp.dot(comb.astype(BF16), e3_ref[...], preferred_element_type=F32)
    dt_e, w_e, ea_e = expd[0:q], expd[q:2 * q], expd[2 * q:3 * q]

    x = _pad_rows(xs_ref[...], q).astype(F32)
    xdt = (x * dt_e).astype(BF16)
    xdtd = (x * w_e).astype(BF16)
    bm = _pad_rows(b_ref[...], q)
    cm = _pad_rows(c_ref[...], q)

    acum_t = acum.T
    dec = jnp.exp(acum_t[0:SSD_HEADS, q - 1:q])
    dec = jnp.broadcast_to(dec, (SSD_HEADS, D_STATE))
    dec = jnp.broadcast_to(dec[:, None, :], (SSD_HEADS, SSD_HEAD_DIM, D_STATE))
    dec = dec.reshape(SSD_HEADS * SSD_HEAD_DIM, D_STATE)

    lane = lax.broadcasted_iota(jnp.int32, (q, LANES), 1)
    nt = (((1,), (1,)), ((), ()))
    tn = (((0,), (0,)), ((), ()))
    y_groups = []
    for g in range(SSD_GROUPS):
        gc = slice(g * GROUP_COLS, (g + 1) * GROUP_COLS)
        bg = bm[:, g * D_STATE:(g + 1) * D_STATE]
        cg = cm[:, g * D_STATE:(g + 1) * D_STATE]
        cb = lax.dot_general(cg, bg, nt, preferred_element_type=F32)
        sg = st_scr[gc, :]
        y_off = lax.dot_general(cg, sg.astype(BF16), nt, preferred_element_type=F32)
        pieces = []
        for pr in range(HEADS_PER_GROUP // 2):
            ms = []
            for hh in range(2):
                h = g * HEADS_PER_GROUP + 2 * pr + hh
                seg = jnp.broadcast_to(acum[:, h:h + 1], (q, q)) - acum_t[h:h + 1, :]
                lm = jnp.exp(jnp.where(causal, seg, -jnp.inf))
                ms.append((cb * lm).astype(BF16))
            m = jnp.concatenate(ms, axis=1)
            c0 = g * GROUP_COLS + pr * LANES
            xp = xdt[:, c0:c0 + LANES]
            rhs = jnp.concatenate([jnp.where(lane < SSD_HEAD_DIM, xp, jnp.zeros_like(xp)),
                                   jnp.where(lane >= SSD_HEAD_DIM, xp, jnp.zeros_like(xp))], axis=0)
            pieces.append(jnp.dot(m, rhs, preferred_element_type=F32))
        y_diag = jnp.concatenate(pieces, axis=1)
        y_groups.append(y_diag + y_off * ea_e[:, gc])
        new = lax.dot_general(xdtd[:, gc], bg, tn, preferred_element_type=F32)
        st_scr[gc, :] = sg * dec[gc, :] + new
    y = jnp.concatenate(y_groups, axis=1)
    y = y + dskip_ref[...] * x
    yz = y * _pad_rows(szs_ref[...], q).astype(F32)
    yn = yz * lax.rsqrt(jnp.mean(yz * yz, axis=-1, keepdims=True) + EPS) * nrm_ref[...]
    y_ref[...] = yn[0:q_in].astype(BF16)

    @pl.when(ci == n_chunks - 1)
    def _():
        hout_ref[0] = st_scr[...]


def _ssd(act, dt, a_log, dskip_e, norm_ssd, e3, h0, *, n_seq, q_in):
    rows = act.shape[0]
    n_chunks = rows // (n_seq * q_in)
    kern = functools.partial(_ssd_kernel, q_in=q_in, n_chunks=n_chunks)
    rblk = lambda b, c: b * n_chunks + c
    bc_blk0 = 7 * D_MODEL // GROUP_COLS
    return pl.pallas_call(
        kern,
        grid=(n_seq, n_chunks),
        in_specs=[
            pl.BlockSpec((q_in, D_MODEL), lambda b, c: (rblk(b, c), 0)),
            pl.BlockSpec((q_in, D_MODEL), lambda b, c: (rblk(b, c), 1)),
            pl.BlockSpec((q_in, GROUP_COLS), lambda b, c: (rblk(b, c), bc_blk0)),
            pl.BlockSpec((q_in, GROUP_COLS), lambda b, c: (rblk(b, c), bc_blk0 + 1)),
            pl.BlockSpec((q_in, LANES), lambda b, c: (rblk(b, c), 0)),
            pl.BlockSpec((1, LANES), lambda b, c: (0, 0)),
            pl.BlockSpec((1, D_MODEL), lambda b, c: (0, 0)),
            pl.BlockSpec((1, D_MODEL), lambda b, c: (0, 0)),
            pl.BlockSpec((LANES, D_MODEL), lambda b, c: (0, 0)),
            pl.BlockSpec((1, D_MODEL, D_STATE), lambda b, c: (b, 0, 0)),
        ],
        out_specs=[
            pl.BlockSpec((q_in, D_MODEL), lambda b, c: (rblk(b, c), 0)),
            pl.BlockSpec((1, D_MODEL, D_STATE), lambda b, c: (b, 0, 0)),
        ],
        out_shape=[
            jax.ShapeDtypeStruct((rows, D_MODEL), BF16),
            jax.ShapeDtypeStruct((n_seq, D_MODEL, D_STATE), F32),
        ],
        scratch_shapes=[pltpu.VMEM((D_MODEL, D_STATE), F32)],
        compiler_params=pltpu.CompilerParams(
            dimension_semantics=("arbitrary", "arbitrary"), vmem_limit_bytes=VMEM_LIMIT),
        name="ssd",
    )(act, act, act, act, dt, a_log, dskip_e, norm_ssd, e3, h0)


def _merge_kernel(yssd_ref, u_ref, v_ref, szm_ref, sgs_ref, sgm_ref, x_ref, gate_ref,
                  lng_ref, lnb_ref, gfin_ref, wsp_ref, bsp_ref, wbs_ref, wbm_ref, wo_ref,
                  y_ref, *rest, tm, qm, spt, emit_vn):
    if emit_vn:
        vn_ref, ymlp_scr = rest
    else:
        (ymlp_scr,) = rest
    v = v_ref[...].astype(F32)
    mu = jnp.mean(v, axis=-1, keepdims=True)
    vc = v - mu
    var = jnp.mean(vc * vc, axis=-1, keepdims=True)
    vn = vc * lax.rsqrt(var + EPS) * lng_ref[...] + lnb_ref[...]
    if emit_vn:
        vn_ref[...] = vn
    vn_b = vn.astype(BF16)

    row = lax.broadcasted_iota(jnp.int32, (MLP_CHUNK, MLP_CHUNK), 0)
    col = lax.broadcasted_iota(jnp.int32, (MLP_CHUNK, MLP_CHUNK), 1)
    wms = [jnp.where(row >= col, wsp_ref[g], 0.0).astype(BF16) for g in range(MLP_GROUPS)]
    for ck in range(tm // qm):
        rows = slice(ck * qm, (ck + 1) * qm)
        vck = _pad_rows(vn_b[rows, :], MLP_CHUNK)
        mixed = jnp.concatenate(
            [jnp.dot(wms[g], vck[:, g * MLP_GROUP_DIM:(g + 1) * MLP_GROUP_DIM],
                     preferred_element_type=F32) for g in range(MLP_GROUPS)], axis=1)
        mixed = mixed[0:qm] + bsp_ref[...]
        y_mlp = u_ref[rows, :].astype(F32) * mixed * szm_ref[rows, :].astype(F32)
        ymlp_scr[rows, :] = y_mlp.astype(BF16)

    a = jnp.dot(yssd_ref[...], wbs_ref[...], preferred_element_type=F32)
    b = jnp.dot(ymlp_scr[...], wbm_ref[...], preferred_element_type=F32)
    merged = sgs_ref[...].astype(F32) * a + sgm_ref[...].astype(F32) * b
    o = jnp.dot(merged.astype(BF16), wo_ref[...], preferred_element_type=F32)
    rps = tm // spt
    for s in range(spt):
        rows = slice(s * rps, (s + 1) * rps)
        out = x_ref[rows, :] + gate_ref[s] * o[rows, :]
        y = out * lax.rsqrt(jnp.mean(out * out, axis=-1, keepdims=True) + EPS)
        y_ref[rows, :] = y * gfin_ref[...]


def _merge(yssd, act, x2d, gate, ln_g, ln_b, g_final, w_spatial, bsp_e, wbs, wbm, wo,
           *, tm, qm, spt, emit_vn):
    rows = x2d.shape[0]
    n_seq = gate.shape[0]
    n_tiles = rows // tm
    tiles_per_seq = max(1, n_tiles // n_seq) if spt == 1 else 1
    seq_blk = (lambda i: i // tiles_per_seq) if spt == 1 else (lambda i: i)
    kern = functools.partial(_merge_kernel, tm=tm, qm=qm, spt=spt, emit_vn=emit_vn)
    const = lambda shape: pl.BlockSpec(shape, lambda i: (0,) * len(shape),
                                       pipeline_mode=pl.Buffered(1))
    seg = lambda k: pl.BlockSpec((tm, D_MODEL), lambda i: (i, k))
    out_specs = [pl.BlockSpec((tm, D_MODEL), lambda i: (i, 0))]
    out_shape = [jax.ShapeDtypeStruct((rows, D_MODEL), F32)]
    if emit_vn:
        out_specs.append(pl.BlockSpec((tm, D_MODEL), lambda i: (i, 0)))
        out_shape.append(jax.ShapeDtypeStruct((rows, D_MODEL), F32))
    return pl.pallas_call(
        kern,
        grid=(n_tiles,),
        in_specs=[
            pl.BlockSpec((tm, D_MODEL), lambda i: (i, 0)),
            seg(2), seg(3), seg(4), seg(5), seg(6),
            pl.BlockSpec((tm, D_MODEL), lambda i: (i, 0)),
            pl.BlockSpec((spt, 1, D_MODEL), lambda i: (seq_blk(i), 0, 0)),
            const((1, D_MODEL)), const((1, D_MODEL)), const((1, D_MODEL)),
            const((MLP_GROUPS, MLP_CHUNK, MLP_CHUNK)),
            const((qm, D_MODEL)),
            const((D_MODEL, D_MODEL)), const((D_MODEL, D_MODEL)), const((D_MODEL, D_MODEL)),
        ],
        out_specs=out_specs,
        out_shape=out_shape,
        scratch_shapes=[pltpu.VMEM((tm, D_MODEL), BF16)],
        compiler_params=pltpu.CompilerParams(
            dimension_semantics=("arbitrary",), vmem_limit_bytes=VMEM_LIMIT),
        name="merge",
    )(yssd, act, act, act, act, act, x2d, gate, ln_g, ln_b, g_final, w_spatial, bsp_e, wbs, wbm, wo)


def _expansion_matrix():
    e = np.zeros((LANES, D_MODEL), np.float32)
    cols = np.arange(D_MODEL)
    for piece in range(3):
        e[piece * SSD_HEADS + cols // SSD_HEAD_DIM, cols] = 1.0
    return jnp.asarray(e, BF16)


def _stream(x, c_mods, hist8, h0, wts, *, tm, rb, spt, q_in, tm_merge, qm, emit_vn):
    n_seq, seq_len, d = x.shape
    x2d = x.reshape(n_seq * seq_len, d)
    shift, scale, gate = (c_mods[:, k * d:(k + 1) * d].reshape(n_seq, 1, d) for k in range(3))
    act, dt, tail = _proj(x2d, scale, shift, wts["g_pre"], wts["w_main"], wts["w_dt"], wts["dt_bias"],
                          wts["conv_w"], wts["conv_b"], hist8, tm=tm, rb=rb, spt=spt)
    yssd, h_new = _ssd(act, dt, wts["a_log"], wts["dskip_e"], wts["norm_ssd"], wts["e3"], h0,
                       n_seq=n_seq, q_in=q_in)
    outs = _merge(yssd, act, x2d, gate, wts["ln_g"], wts["ln_b"], wts["g_final"], wts["w_spatial"],
                  wts["bsp_e"][:qm], wts["wbs"], wts["wbm"], wts["wo"],
                  tm=tm_merge, qm=qm, spt=(spt if spt > 1 else 1), emit_vn=emit_vn)
    tiles_per_seq = (n_seq * seq_len // tm) // n_seq if spt == 1 else 1
    if spt == 1:
        new_hist = tail[tiles_per_seq - 1::tiles_per_seq, SUBLANES - (CONV_W - 1):, :]
    else:
        new_hist = tail[:, SUBLANES - (CONV_W - 1):, :]
    y = outs[0].reshape(n_seq, seq_len, d)
    h_new = h_new.reshape(n_seq, SSD_HEADS, SSD_HEAD_DIM, D_STATE)
    vn = outs[1].reshape(n_seq, seq_len, d) if emit_vn else None
    return y, h_new, new_hist, vn


def kernel(x_prompt, x_sample, state_ssm, cache_conv, c_prompt, c_sample, w_ada, b_ada, g_pre, w_in,
           conv_w, conv_b, dt_bias, a_log, d_skip, norm_ssd, ln_g, ln_b, w_spatial, b_spatial,
           w_bp_ssd, w_bp_mlp, w_o, g_final):
    depth = w_ada.shape[0]
    assert depth == 1, "single-layer trunk only"
    bp, seq, d = x_prompt.shape
    bs, dec_seq, _ = x_sample.shape
    assert d == D_MODEL and seq % 1024 == 0 and dec_seq % 16 == 0 and dec_seq <= SSD_BLOCK

    w = w_in[0]
    o_zs, o_xbc, o_dt = 0, D_MODEL, D_MODEL + CONV_DIM
    o_u = o_dt + SSD_HEADS
    seg = lambda k: w[:, o_u + k * D_MODEL:o_u + (k + 1) * D_MODEL]
    w_main = jnp.concatenate(
        [w[:, o_zs:o_zs + D_MODEL], w[:, o_xbc:o_xbc + D_MODEL], seg(0), seg(1), seg(2), seg(3), seg(4),
         w[:, o_xbc + D_MODEL:o_dt]], axis=1).astype(BF16)
    w_dt = jnp.pad(w[:, o_dt:o_dt + SSD_HEADS], ((0, 0), (0, LANES - SSD_HEADS))).astype(BF16)
    pad_heads = lambda v: jnp.pad(v.reshape(1, SSD_HEADS), ((0, 0), (0, LANES - SSD_HEADS)))
    wts = dict(
        g_pre=g_pre[0].reshape(1, d), w_main=w_main, w_dt=w_dt, dt_bias=pad_heads(dt_bias[0]),
        conv_w=conv_w[0], conv_b=conv_b[0].reshape(1, CONV_DIM),
        a_log=pad_heads(a_log[0]), dskip_e=jnp.repeat(d_skip[0], SSD_HEAD_DIM).reshape(1, d),
        norm_ssd=norm_ssd[0].reshape(1, d), e3=_expansion_matrix(),
        ln_g=ln_g[0].reshape(1, d), ln_b=ln_b[0].reshape(1, d), g_final=g_final.reshape(1, d),
        w_spatial=w_spatial[0], bsp_e=jnp.repeat(b_spatial[0].T, MLP_GROUP_DIM, axis=1),
        wbs=w_bp_ssd[0].astype(BF16), wbm=w_bp_mlp[0].astype(BF16), wo=w_o[0].astype(BF16),
    )

    c_all = jnp.concatenate([c_prompt, c_sample], axis=0)
    c_pad = jnp.pad(c_all, ((0, 16 - (bp + bs)), (0, 0)))
    mods = _mods(c_pad, w_ada[0], b_ada[0].reshape(1, 3 * d))

    hist_p = jnp.zeros((bp, SUBLANES, CONV_DIM), F32)
    h0_p = jnp.zeros((bp, d, D_STATE), F32)
    yp, ssm_p, conv_p, _ = _stream(
        x_prompt, mods[:bp], hist_p, h0_p, wts,
        tm=1024, rb=512, spt=1, q_in=SSD_BLOCK, tm_merge=256, qm=MLP_CHUNK, emit_vn=False)

    hist_s = jnp.pad(cache_conv[0], ((0, 0), (SUBLANES - (CONV_W - 1), 0), (0, 0)))
    h0_s = state_ssm[0].reshape(bs, d, D_STATE)
    ys, ssm_s, conv_s, vn_s = _stream(
        x_sample, mods[bp:bp + bs], hist_s, h0_s, wts,
        tm=bs * dec_seq, rb=bs * dec_seq, spt=bs, q_in=dec_seq, tm_merge=bs * dec_seq,
        qm=dec_seq, emit_vn=True)

    return (yp, ys, ssm_p[None], conv_p[None], ssm_s[None], conv_s[None], vn_s[None])
```

```python
import functools

import numpy as np
import jax
import jax.numpy as jnp
from jax import lax
from jax.experimental import pallas as pl
from jax.experimental.pallas import tpu as pltpu

F32 = jnp.float32
BF16 = jnp.bfloat16

D_MODEL = 2048
SSD_HEADS = 32
SSD_HEAD_DIM = 64
SSD_GROUPS = 4
HEADS_PER_GROUP = SSD_HEADS // SSD_GROUPS
D_STATE = 128
GROUP_COLS = HEADS_PER_GROUP * SSD_HEAD_DIM
CONV_W = 4
CONV_DIM = D_MODEL + 2 * SSD_GROUPS * D_STATE
MLP_CHUNK = 128
MLP_GROUPS = 8
MLP_GROUP_DIM = D_MODEL // MLP_GROUPS
EPS = 1e-5

SUBLANES = 8
LANES = 128
SSD_BLOCK = 128
PROJ_TN = 1024
VMEM_LIMIT = 56 * 1024 * 1024

N_ACT = 7 * D_MODEL + 2 * SSD_GROUPS * D_STATE
N_PROJ_TILES = N_ACT // PROJ_TN
DT_OFFSET = D_MODEL + CONV_DIM
FIRST_SHIFTED_TILE = DT_OFFSET // PROJ_TN
CONV_STEPS = (2, 3, 4)
SILU_STEPS = (0, 1, 9, 10)
IDENT_STEPS = (5, 6, 7, 8)
FIRST_SIGMOID_STEP = 11
PROJ_NSPLIT = 2
PROJ_MSPLIT = 4


def _sigmoid(x):
    return 0.5 + 0.5 * jnp.tanh(0.5 * x)


def _silu(x):
    hx = 0.5 * x
    return hx + hx * jnp.tanh(hx)


def _softplus(x):
    return jnp.maximum(x, 0.0) + jnp.log1p(jnp.exp(-jnp.abs(x)))


def _mods_kernel(c_ref, w_ref, b_ref, o_ref):
    s = _silu(c_ref[...]).astype(BF16)
    o_ref[...] = jnp.dot(s, w_ref[...].astype(BF16), preferred_element_type=F32) + b_ref[...]


def _mods(c_pad, w_ada, b_ada):
    rows, d = c_pad.shape
    n = w_ada.shape[1]
    tn = 1024
    return pl.pallas_call(
        _mods_kernel,
        grid=(n // tn,),
        in_specs=[
            pl.BlockSpec((rows, d), lambda j: (0, 0)),
            pl.BlockSpec((d, tn), lambda j: (0, j)),
            pl.BlockSpec((1, tn), lambda j: (0, j)),
        ],
        out_specs=pl.BlockSpec((rows, tn), lambda j: (0, j)),
        out_shape=jax.ShapeDtypeStruct((rows, n), F32),
        compiler_params=pltpu.CompilerParams(
            dimension_semantics=("arbitrary",), vmem_limit_bytes=VMEM_LIMIT),
        name="mods",
    )(c_pad, w_ada, b_ada)


def _norm_kernel(x_ref, sc_ref, sh_ref, gpre_ref, wdt_ref, dtb_ref, h_ref, dt_ref, *, tm, spt):
    rps = tm // spt
    for s in range(spt):
        rows = slice(s * rps, (s + 1) * rps)
        x = x_ref[rows, :]
        y = x * lax.rsqrt(jnp.mean(x * x, axis=-1, keepdims=True) + EPS)
        y = y * gpre_ref[...]
        h_ref[rows, :] = (y * (1.0 + sc_ref[s]) + sh_ref[s]).astype(BF16)
    dt_raw = jnp.dot(h_ref[...], wdt_ref[...], preferred_element_type=F32) + dtb_ref[...]
    lane = lax.broadcasted_iota(jnp.int32, dt_raw.shape, 1)
    dt_ref[...] = jnp.where(lane < SSD_HEADS, _softplus(dt_raw), 0.0)


def _norm(x2d, scale, shift, g_pre, w_dt, dt_bias, *, tm, spt):
    rows = x2d.shape[0]
    n_seq = scale.shape[0]
    n_tiles = rows // tm
    tiles_per_seq = max(1, n_tiles // n_seq) if spt == 1 else 1
    seq_blk = (lambda i: i // tiles_per_seq) if spt == 1 else (lambda i: i)
    return pl.pallas_call(
        functools.partial(_norm_kernel, tm=tm, spt=spt),
        grid=(n_tiles,),
        in_specs=[
            pl.BlockSpec((tm, D_MODEL), lambda i: (i, 0)),
            pl.BlockSpec((spt, 1, D_MODEL), lambda i: (seq_blk(i), 0, 0)),
            pl.BlockSpec((spt, 1, D_MODEL), lambda i: (seq_blk(i), 0, 0)),
            pl.BlockSpec((1, D_MODEL), lambda i: (0, 0)),
            pl.BlockSpec((D_MODEL, LANES), lambda i: (0, 0)),
            pl.BlockSpec((1, LANES), lambda i: (0, 0)),
        ],
        out_specs=[
            pl.BlockSpec((tm, D_MODEL), lambda i: (i, 0)),
            pl.BlockSpec((tm, LANES), lambda i: (i, 0)),
        ],
        out_shape=[
            jax.ShapeDtypeStruct((rows, D_MODEL), BF16),
            jax.ShapeDtypeStruct((rows, LANES), F32),
        ],
        compiler_params=pltpu.CompilerParams(
            dimension_semantics=("arbitrary",), vmem_limit_bytes=VMEM_LIMIT),
        name="norm",
    )(x2d, scale, shift, g_pre, w_dt, dt_bias)


def _act_tile(j):
    return jnp.where(j < CONV_STEPS[2], j, jnp.where(j == CONV_STEPS[2], N_PROJ_TILES - 1, j - 1))


def _conv_in_tile(j):
    return jnp.clip(j - CONV_STEPS[0], 0, len(CONV_STEPS) - 1)


def _proj_kernel(h_ref, w_ref, whi_ref, cw_ref, cb_ref, hist_ref, act_ref, tail_ref,
                 wbf_scr, halo_scr, stg0_scr, stg1_scr, *, tm, spt, nsplit, msplit, tiles_per_seq):
    j = pl.program_id(0)
    i = pl.program_id(1)
    rps = tm // spt
    cw = PROJ_TN // nsplit
    rb = tm // msplit
    wrows = 256

    @pl.when(jnp.logical_and(i == 0, j < FIRST_SHIFTED_TILE))
    def _():
        for r in range(D_MODEL // wrows):
            rows = slice(r * wrows, (r + 1) * wrows)
            wbf_scr[rows, :] = w_ref[rows, :].astype(BF16)

    @pl.when(jnp.logical_and(i == 0, j >= FIRST_SHIFTED_TILE))
    def _():
        lane = lax.broadcasted_iota(jnp.int32, (wrows, LANES), 1)
        for r in range(D_MODEL // wrows):
            rows = slice(r * wrows, (r + 1) * wrows)
            lo = pltpu.roll(w_ref[rows, :], PROJ_TN - SSD_HEADS, axis=1)
            hi = pltpu.roll(whi_ref[rows, :], LANES - SSD_HEADS, axis=1)
            wbf_scr[rows, 0:PROJ_TN - LANES] = lo[:, 0:PROJ_TN - LANES].astype(BF16)
            last = jnp.where(lane >= LANES - SSD_HEADS, hi, lo[:, PROJ_TN - LANES:])
            wbf_scr[rows, PROJ_TN - LANES:] = last.astype(BF16)

    stgs = (stg0_scr, stg1_scr)
    blocks = [(nb, mb) for nb in range(nsplit) for mb in range(msplit)]

    def dot_block(k):
        nb, mb = blocks[k]
        stgs[k % 2][SUBLANES:SUBLANES + rb, :] = jnp.dot(
            h_ref[mb * rb:(mb + 1) * rb, :], wbf_scr[:, nb * cw:(nb + 1) * cw],
            preferred_element_type=F32)

    def run(epilogue):
        dot_block(0)
        for k, (nb, mb) in enumerate(blocks):
            if k + 1 < len(blocks):
                dot_block(k + 1)
            epilogue(nb, mb, stgs[k % 2])

    def elementwise(fn):
        def ep(nb, mb, stg):
            act_ref[mb * rb:(mb + 1) * rb, nb * cw:(nb + 1) * cw] = (
                fn(stg[SUBLANES:SUBLANES + rb, :]).astype(BF16))
        return ep

    is_conv = jnp.logical_and(j >= CONV_STEPS[0], j <= CONV_STEPS[2])

    @pl.when(jnp.logical_not(is_conv))
    def _():
        tail_ref[...] = jnp.zeros_like(tail_ref)

    @pl.when(functools.reduce(jnp.logical_or, [j == t for t in SILU_STEPS]))
    def _():
        run(elementwise(_silu))

    @pl.when(jnp.logical_and(j >= IDENT_STEPS[0], j <= IDENT_STEPS[-1]))
    def _():
        run(elementwise(lambda a: a))

    @pl.when(j >= FIRST_SIGMOID_STEP)
    def _():
        run(elementwise(_sigmoid))

    @pl.when(is_conv)
    def _():
        if spt == 1:
            first = (i % tiles_per_seq) == 0
            halo_scr[...] = jnp.where(first, hist_ref[0], halo_scr[...])

        def conv(ext, cols):
            w0, w1, w2, w3 = (cw_ref[k:k + 1, cols] for k in range(CONV_W))
            t = ext * w0
            t = pltpu.roll(t, 1, axis=0) + ext * w1
            t = pltpu.roll(t, 1, axis=0) + ext * w2
            t = pltpu.roll(t, 1, axis=0) + ext * w3 + cb_ref[:, cols]
            return _silu(t[SUBLANES:, :]).astype(BF16)

        def ep(nb, mb, stg):
            cols = slice(nb * cw, (nb + 1) * cw)
            if spt == 1:
                stg[0:SUBLANES, :] = halo_scr[:, cols]
                act_ref[mb * rb:(mb + 1) * rb, cols] = conv(stg[...], cols)
                last = stg[rb:rb + SUBLANES, :]
                halo_scr[:, cols] = last
                tail_ref[0, :, cols] = last
            else:
                for s in range(rb // rps):
                    rows = slice(SUBLANES + s * rps, SUBLANES + (s + 1) * rps)
                    seq = mb * (rb // rps) + s
                    ext = jnp.concatenate([hist_ref[seq, :, cols], stg[rows, :]], axis=0)
                    act_ref[mb * rb + s * rps:mb * rb + (s + 1) * rps, cols] = conv(ext, cols)
                    tail_ref[seq, :, cols] = stg[rows.stop - SUBLANES:rows.stop, :]

        run(ep)


def _proj(h2d, w_in, conv_w, conv_b, hist8, *, tm, spt, nsplit, msplit):
    rows = h2d.shape[0]
    n_seq = hist8.shape[0]
    n_tiles = rows // tm
    tiles_per_seq = max(1, n_tiles // n_seq) if spt == 1 else 1
    seq_blk = (lambda i: i // tiles_per_seq) if spt == 1 else (lambda i: i)
    kern = functools.partial(_proj_kernel, tm=tm, spt=spt, nsplit=nsplit, msplit=msplit,
                             tiles_per_seq=tiles_per_seq)
    return pl.pallas_call(
        kern,
        grid=(N_PROJ_TILES, n_tiles),
        in_specs=[
            pl.BlockSpec((tm, D_MODEL), lambda j, i: (i, 0)),
            pl.BlockSpec((D_MODEL, PROJ_TN), lambda j, i: (0, j)),
            pl.BlockSpec((D_MODEL, LANES), lambda j, i: (0, (j + 1) * (PROJ_TN // LANES))),
            pl.BlockSpec((CONV_W, PROJ_TN), lambda j, i: (0, _conv_in_tile(j))),
            pl.BlockSpec((1, PROJ_TN), lambda j, i: (0, _conv_in_tile(j))),
            pl.BlockSpec((spt, SUBLANES, PROJ_TN), lambda j, i: (seq_blk(i), 0, _conv_in_tile(j))),
        ],
        out_specs=[
            pl.BlockSpec((tm, PROJ_TN), lambda j, i: (i, _act_tile(j))),
            pl.BlockSpec((spt, SUBLANES, PROJ_TN), lambda j, i: (i, 0, j)),
        ],
        out_shape=[
            jax.ShapeDtypeStruct((rows, N_ACT), BF16),
            jax.ShapeDtypeStruct((n_tiles * spt, SUBLANES, N_PROJ_TILES * PROJ_TN), F32),
        ],
        scratch_shapes=[
            pltpu.VMEM((D_MODEL, PROJ_TN), BF16),
            pltpu.VMEM((SUBLANES, PROJ_TN), F32),
            pltpu.VMEM((tm // msplit + SUBLANES, PROJ_TN // nsplit), F32),
            pltpu.VMEM((tm // msplit + SUBLANES, PROJ_TN // nsplit), F32),
        ],
        compiler_params=pltpu.CompilerParams(
            dimension_semantics=("arbitrary", "arbitrary"), vmem_limit_bytes=VMEM_LIMIT),
        name="proj",
    )(h2d, w_in, w_in, conv_w, conv_b, hist8)


def _pad_rows(v, rows):
    if v.shape[0] == rows:
        return v
    return jnp.concatenate([v, jnp.zeros((rows - v.shape[0], v.shape[1]), v.dtype)], axis=0)


def _ssd_kernel(szs_ref, xs_ref, b_ref, c_ref, dt_ref, alog_ref, dskip_ref, nrm_ref, e3_ref, h0_ref,
                y_ref, hout_ref, st_scr, *, q_in, n_chunks):
    q = SSD_BLOCK
    ci = pl.program_id(1)

    @pl.when(ci == 0)
    def _():
        st_scr[...] = h0_ref[0]

    dt = _pad_rows(dt_ref[...], q)
    a = -jnp.exp(alog_ref[...])
    adt = dt * a
    row = lax.broadcasted_iota(jnp.int32, (q, q), 0)
    col = lax.broadcasted_iota(jnp.int32, (q, q), 1)
    causal = row >= col
    acum = jnp.dot(causal.astype(F32), adt, precision=lax.Precision.HIGHEST,
                   preferred_element_type=F32)
    alast = acum[q - 1:q, :]
    dte = jnp.exp(alast - acum)
    ea = jnp.exp(acum)

    stack = jnp.concatenate([dt, dt * dte, ea], axis=0)
    lane3 = lax.broadcasted_iota(jnp.int32, stack.shape, 1)
    stack = jnp.where(lane3 < SSD_HEADS, stack, 0.0)
    hi = stack.astype(BF16).astype(F32)
    r1 = stack - hi
    mid = r1.astype(BF16).astype(F32)
    lo = (r1 - mid).astype(BF16).astype(F32)
    comb = hi + pltpu.roll(mid, SSD_HEADS, axis=1) + pltpu.roll(lo, 2 * SSD_HEADS, axis=1)
    expd = jnp.dot(comb.astype(BF16), e3_ref[...], preferred_element_type=F32)
    dt_e, w_e, ea_e = expd[0:q], expd[q:2 * q], expd[2 * q:3 * q]

    x = _pad_rows(xs_ref[...], q).astype(F32)
    xdt = (x * dt_e).astype(BF16)
    xdtd = (x * w_e).astype(BF16)
    bm = _pad_rows(b_ref[...], q)
    cm = _pad_rows(c_ref[...], q)

    acum_t = acum.T
    dec = jnp.exp(acum_t[0:SSD_HEADS, q - 1:q])
    dec = jnp.broadcast_to(dec, (SSD_HEADS, D_STATE))
    dec = jnp.broadcast_to(dec[:, None, :], (SSD_HEADS, SSD_HEAD_DIM, D_STATE))
    dec = dec.reshape(SSD_HEADS * SSD_HEAD_DIM, D_STATE)

    lane = lax.broadcasted_iota(jnp.int32, (q, LANES), 1)
    nt = (((1,), (1,)), ((), ()))
    tn = (((0,), (0,)), ((), ()))
    y_groups = []
    for g in range(SSD_GROUPS):
        gc = slice(g * GROUP_COLS, (g + 1) * GROUP_COLS)
        bg = bm[:, g * D_STATE:(g + 1) * D_STATE]
        cg = cm[:, g * D_STATE:(g + 1) * D_STATE]
        cb = lax.dot_general(cg, bg, nt, preferred_element_type=F32)
        sg = st_scr[gc, :]
        y_off = lax.dot_general(cg, sg.astype(BF16), nt, preferred_element_type=F32)
        pieces = []
        for pr in range(HEADS_PER_GROUP // 2):
            ms = []
            for hh in range(2):
                h = g * HEADS_PER_GROUP + 2 * pr + hh
                seg = jnp.broadcast_to(acum[:, h:h + 1], (q, q)) - acum_t[h:h + 1, :]
                lm = jnp.exp(jnp.where(causal, seg, -jnp.inf))
                ms.append((cb * lm).astype(BF16))
            m = jnp.concatenate(ms, axis=1)
            c0 = g * GROUP_COLS + pr * LANES
            xp = xdt[:, c0:c0 + LANES]
            rhs = jnp.concatenate([jnp.where(lane < SSD_HEAD_DIM, xp, jnp.zeros_like(xp)),
                                   jnp.where(lane >= SSD_HEAD_DIM, xp, jnp.zeros_like(xp))], axis=0)
            pieces.append(jnp.dot(m, rhs, preferred_element_type=F32))
        y_diag = jnp.concatenate(pieces, axis=1)
        y_groups.append(y_diag + y_off * ea_e[:, gc])
        new = lax.dot_general(xdtd[:, gc], bg, tn, preferred_element_type=F32)
        st_scr[gc, :] = sg * dec[gc, :] + new
    y = jnp.concatenate(y_groups, axis=1)
    y = y + dskip_ref[...] * x
    yz = y * _pad_rows(szs_ref[...], q).astype(F32)
    yn = yz * lax.rsqrt(jnp.mean(yz * yz, axis=-1, keepdims=True) + EPS) * nrm_ref[...]
    y_ref[...] = yn[0:q_in].astype(BF16)

    @pl.when(ci == n_chunks - 1)
    def _():
        hout_ref[0] = st_scr[...]


def _ssd(act, dt, a_log, dskip_e, norm_ssd, e3, h0, *, n_seq, q_in):
    rows = act.shape[0]
    n_chunks = rows // (n_seq * q_in)
    kern = functools.partial(_ssd_kernel, q_in=q_in, n_chunks=n_chunks)
    rblk = lambda b, c: b * n_chunks + c
    bc_blk0 = 7 * D_MODEL // GROUP_COLS
    return pl.pallas_call(
        kern,
        grid=(n_seq, n_chunks),
        in_specs=[
            pl.BlockSpec((q_in, D_MODEL), lambda b, c: (rblk(b, c), 0)),
            pl.BlockSpec((q_in, D_MODEL), lambda b, c: (rblk(b, c), 1)),
            pl.BlockSpec((q_in, GROUP_COLS), lambda b, c: (rblk(b, c), bc_blk0)),
            pl.BlockSpec((q_in, GROUP_COLS), lambda b, c: (rblk(b, c), bc_blk0 + 1)),
            pl.BlockSpec((q_in, LANES), lambda b, c: (rblk(b, c), 0)),
            pl.BlockSpec((1, LANES), lambda b, c: (0, 0)),
            pl.BlockSpec((1, D_MODEL), lambda b, c: (0, 0)),
            pl.BlockSpec((1, D_MODEL), lambda b, c: (0, 0)),
            pl.BlockSpec((LANES, D_MODEL), lambda b, c: (0, 0)),
            pl.BlockSpec((1, D_MODEL, D_STATE), lambda b, c: (b, 0, 0)),
        ],
        out_specs=[
            pl.BlockSpec((q_in, D_MODEL), lambda b, c: (rblk(b, c), 0)),
            pl.BlockSpec((1, D_MODEL, D_STATE), lambda b, c: (b, 0, 0)),
        ],
        out_shape=[
            jax.ShapeDtypeStruct((rows, D_MODEL), BF16),
            jax.ShapeDtypeStruct((n_seq, D_MODEL, D_STATE), F32),
        ],
        scratch_shapes=[pltpu.VMEM((D_MODEL, D_STATE), F32)],
        compiler_params=pltpu.CompilerParams(
            dimension_semantics=("arbitrary", "arbitrary"), vmem_limit_bytes=VMEM_LIMIT),
        name="ssd",
    )(act, act, act, act, dt, a_log, dskip_e, norm_ssd, e3, h0)


def _merge_kernel(yssd_ref, u_ref, v_ref, szm_ref, sgs_ref, sgm_ref, x_ref, gate_ref,
                  lng_ref, lnb_ref, gfin_ref, wsp_ref, bsp_ref, wbs_ref, wbm_ref, wo_ref,
                  y_ref, *rest, tm, qm, spt, emit_vn):
    if emit_vn:
        vn_ref, ymlp_scr = rest
    else:
        (ymlp_scr,) = rest
    v = v_ref[...].astype(F32)
    mu = jnp.mean(v, axis=-1, keepdims=True)
    vc = v - mu
    var = jnp.mean(vc * vc, axis=-1, keepdims=True)
    vn = vc * lax.rsqrt(var + EPS) * lng_ref[...] + lnb_ref[...]
    if emit_vn:
        vn_ref[...] = vn
    vn_b = vn.astype(BF16)

    row = lax.broadcasted_iota(jnp.int32, (MLP_CHUNK, MLP_CHUNK), 0)
    col = lax.broadcasted_iota(jnp.int32, (MLP_CHUNK, MLP_CHUNK), 1)
    wms = [jnp.where(row >= col, wsp_ref[g], 0.0).astype(BF16) for g in range(MLP_GROUPS)]
    for ck in range(tm // qm):
        rows = slice(ck * qm, (ck + 1) * qm)
        vck = _pad_rows(vn_b[rows, :], MLP_CHUNK)
        mixed = jnp.concatenate(
            [jnp.dot(wms[g], vck[:, g * MLP_GROUP_DIM:(g + 1) * MLP_GROUP_DIM],
                     preferred_element_type=F32) for g in range(MLP_GROUPS)], axis=1)
        mixed = mixed[0:qm] + bsp_ref[...]
        y_mlp = u_ref[rows, :].astype(F32) * mixed * szm_ref[rows, :].astype(F32)
        ymlp_scr[rows, :] = y_mlp.astype(BF16)

    a = jnp.dot(yssd_ref[...], wbs_ref[...], preferred_element_type=F32)
    b = jnp.dot(ymlp_scr[...], wbm_ref[...], preferred_element_type=F32)
    merged = sgs_ref[...].astype(F32) * a + sgm_ref[...].astype(F32) * b
    o = jnp.dot(merged.astype(BF16), wo_ref[...], preferred_element_type=F32)
    rps = tm // spt
    for s in range(spt):
        rows = slice(s * rps, (s + 1) * rps)
        out = x_ref[rows, :] + gate_ref[s] * o[rows, :]
        y = out * lax.rsqrt(jnp.mean(out * out, axis=-1, keepdims=True) + EPS)
        y_ref[rows, :] = y * gfin_ref[...]


def _merge(yssd, act, x2d, gate, ln_g, ln_b, g_final, w_spatial, bsp_e, wbs, wbm, wo,
           *, tm, qm, spt, emit_vn):
    rows = x2d.shape[0]
    n_seq = gate.shape[0]
    n_tiles = rows // tm
    tiles_per_seq = max(1, n_tiles // n_seq) if spt == 1 else 1
    seq_blk = (lambda i: i // tiles_per_seq) if spt == 1 else (lambda i: i)
    kern = functools.partial(_merge_kernel, tm=tm, qm=qm, spt=spt, emit_vn=emit_vn)
    const = lambda shape: pl.BlockSpec(shape, lambda i: (0,) * len(shape),
                                       pipeline_mode=pl.Buffered(1))
    seg = lambda k: pl.BlockSpec((tm, D_MODEL), lambda i: (i, k))
    out_specs = [pl.BlockSpec((tm, D_MODEL), lambda i: (i, 0))]
    out_shape = [jax.ShapeDtypeStruct((rows, D_MODEL), F32)]
    if emit_vn:
        out_specs.append(pl.BlockSpec((tm, D_MODEL), lambda i: (i, 0)))
        out_shape.append(jax.ShapeDtypeStruct((rows, D_MODEL), F32))
    return pl.pallas_call(
        kern,
        grid=(n_tiles,),
        in_specs=[
            pl.BlockSpec((tm, D_MODEL), lambda i: (i, 0)),
            seg(2), seg(3), seg(4), seg(5), seg(6),
            pl.BlockSpec((tm, D_MODEL), lambda i: (i, 0)),
            pl.BlockSpec((spt, 1, D_MODEL), lambda i: (seq_blk(i), 0, 0)),
            const((1, D_MODEL)), const((1, D_MODEL)), const((1, D_MODEL)),
            const((MLP_GROUPS, MLP_CHUNK, MLP_CHUNK)),
            const((qm, D_MODEL)),
            const((D_MODEL, D_MODEL)), const((D_MODEL, D_MODEL)), const((D_MODEL, D_MODEL)),
        ],
        out_specs=out_specs,
        out_shape=out_shape,
        scratch_shapes=[pltpu.VMEM((tm, D_MODEL), BF16)],
        compiler_params=pltpu.CompilerParams(
            dimension_semantics=("arbitrary",), vmem_limit_bytes=VMEM_LIMIT),
        name="merge",
    )(yssd, act, act, act, act, act, x2d, gate, ln_g, ln_b, g_final, w_spatial, bsp_e, wbs, wbm, wo)


def _expansion_matrix():
    e = np.zeros((LANES, D_MODEL), np.float32)
    cols = np.arange(D_MODEL)
    for piece in range(3):
        e[piece * SSD_HEADS + cols // SSD_HEAD_DIM, cols] = 1.0
    return jnp.asarray(e, BF16)


def _stream(x, c_mods, hist8, h0, wts, *, tm_norm, tm, msplit, spt, q_in, tm_merge, qm, emit_vn):
    n_seq, seq_len, d = x.shape
    x2d = x.reshape(n_seq * seq_len, d)
    shift, scale, gate = (c_mods[:, k * d:(k + 1) * d].reshape(n_seq, 1, d) for k in range(3))
    h2d, dt = _norm(x2d, scale, shift, wts["g_pre"], wts["w_dt"], wts["dt_bias"], tm=tm_norm, spt=spt)
    act, tail = _proj(h2d, wts["w_in"], wts["conv_w"], wts["conv_b"], hist8,
                      tm=tm, spt=spt, nsplit=PROJ_NSPLIT, msplit=msplit)
    tail = tail[:, :, CONV_STEPS[0] * PROJ_TN:CONV_STEPS[0] * PROJ_TN + CONV_DIM]
    yssd, h_new = _ssd(act, dt, wts["a_log"], wts["dskip_e"], wts["norm_ssd"], wts["e3"], h0,
                       n_seq=n_seq, q_in=q_in)
    outs = _merge(yssd, act, x2d, gate, wts["ln_g"], wts["ln_b"], wts["g_final"], wts["w_spatial"],
                  wts["bsp_e"][:qm], wts["wbs"], wts["wbm"], wts["wo"],
                  tm=tm_merge, qm=qm, spt=(spt if spt > 1 else 1), emit_vn=emit_vn)
    tiles_per_seq = (n_seq * seq_len // tm) // n_seq if spt == 1 else 1
    if spt == 1:
        new_hist = tail[tiles_per_seq - 1::tiles_per_seq, SUBLANES - (CONV_W - 1):, :]
    else:
        new_hist = tail[:, SUBLANES - (CONV_W - 1):, :]
    y = outs[0].reshape(n_seq, seq_len, d)
    h_new = h_new.reshape(n_seq, SSD_HEADS, SSD_HEAD_DIM, D_STATE)
    vn = outs[1].reshape(n_seq, seq_len, d) if emit_vn else None
    return y, h_new, new_hist, vn


def kernel(x_prompt, x_sample, state_ssm, cache_conv, c_prompt, c_sample, w_ada, b_ada, g_pre, w_in,
           conv_w, conv_b, dt_bias, a_log, d_skip, norm_ssd, ln_g, ln_b, w_spatial, b_spatial,
           w_bp_ssd, w_bp_mlp, w_o, g_final):
    depth = w_ada.shape[0]
    assert depth == 1, "single-layer trunk only"
    bp, seq, d = x_prompt.shape
    bs, dec_seq, _ = x_sample.shape
    assert d == D_MODEL and seq % 1024 == 0 and dec_seq % 16 == 0 and dec_seq <= SSD_BLOCK

    w = w_in[0]
    w_dt = jnp.pad(w[:, DT_OFFSET:DT_OFFSET + SSD_HEADS], ((0, 0), (0, LANES - SSD_HEADS))).astype(BF16)
    pad_heads = lambda v: jnp.pad(v.reshape(1, SSD_HEADS), ((0, 0), (0, LANES - SSD_HEADS)))
    wts = dict(
        g_pre=g_pre[0].reshape(1, d), w_in=w, w_dt=w_dt, dt_bias=pad_heads(dt_bias[0]),
        conv_w=conv_w[0], conv_b=conv_b[0].reshape(1, CONV_DIM),
        a_log=pad_heads(a_log[0]), dskip_e=jnp.repeat(d_skip[0], SSD_HEAD_DIM).reshape(1, d),
        norm_ssd=norm_ssd[0].reshape(1, d), e3=_expansion_matrix(),
        ln_g=ln_g[0].reshape(1, d), ln_b=ln_b[0].reshape(1, d), g_final=g_final.reshape(1, d),
        w_spatial=w_spatial[0], bsp_e=jnp.repeat(b_spatial[0].T, MLP_GROUP_DIM, axis=1),
        wbs=w_bp_ssd[0].astype(BF16), wbm=w_bp_mlp[0].astype(BF16), wo=w_o[0].astype(BF16),
    )

    c_all = jnp.concatenate([c_prompt, c_sample], axis=0)
    c_pad = jnp.pad(c_all, ((0, 16 - (bp + bs)), (0, 0)))
    mods = _mods(c_pad, w_ada[0], b_ada[0].reshape(1, 3 * d))

    hist_p = jnp.zeros((bp, SUBLANES, CONV_DIM), F32)
    h0_p = jnp.zeros((bp, d, D_STATE), F32)
    yp, ssm_p, conv_p, _ = _stream(
        x_prompt, mods[:bp], hist_p, h0_p, wts,
        tm_norm=512, tm=1024, msplit=PROJ_MSPLIT, spt=1, q_in=SSD_BLOCK, tm_merge=256, qm=MLP_CHUNK,
        emit_vn=False)

    hist_s = jnp.pad(cache_conv[0], ((0, 0), (SUBLANES - (CONV_W - 1), 0), (0, 0)))
    h0_s = state_ssm[0].reshape(bs, d, D_STATE)
    ys, ssm_s, conv_s, vn_s = _stream(
        x_sample, mods[bp:bp + bs], hist_s, h0_s, wts,
        tm_norm=bs * dec_seq, tm=bs * dec_seq, msplit=1, spt=bs, q_in=dec_seq,
        tm_merge=bs * dec_seq, qm=dec_seq, emit_vn=True)

    return (yp, ys, ssm_p[None], conv_p[None], ssm_s[None], conv_s[None], vn_s[None])
```

```python
import functools

import numpy as np
import jax
import jax.numpy as jnp
from jax import lax
from jax.experimental import pallas as pl
from jax.experimental.pallas import tpu as pltpu

F32 = jnp.float32
BF16 = jnp.bfloat16

D_MODEL = 2048
SSD_HEADS = 32
SSD_HEAD_DIM = 64
SSD_GROUPS = 4
HEADS_PER_GROUP = SSD_HEADS // SSD_GROUPS
D_STATE = 128
GROUP_COLS = HEADS_PER_GROUP * SSD_HEAD_DIM
CONV_W = 4
CONV_DIM = D_MODEL + 2 * SSD_GROUPS * D_STATE
MLP_CHUNK = 128
MLP_GROUPS = 8
MLP_GROUP_DIM = D_MODEL // MLP_GROUPS
EPS = 1e-5

SUBLANES = 8
LANES = 128
SSD_BLOCK = 128
PROJ_TN = 1024
VMEM_LIMIT = 56 * 1024 * 1024

N_ACT = 7 * D_MODEL + 2 * SSD_GROUPS * D_STATE
N_PROJ_TILES = N_ACT // PROJ_TN
DT_OFFSET = D_MODEL + CONV_DIM
FIRST_SHIFTED_TILE = DT_OFFSET // PROJ_TN
CONV_STEPS = (2, 3, 4)
SILU_STEPS = (0, 1, 9, 10)
IDENT_STEPS = (5, 6, 7, 8)
FIRST_SIGMOID_STEP = 11
PROJ_NSPLIT = 2
PROJ_MSPLIT = 4


def _sigmoid(x):
    return 0.5 + 0.5 * jnp.tanh(0.5 * x)


def _silu(x):
    hx = 0.5 * x
    return hx + hx * jnp.tanh(hx)


def _softplus(x):
    return jnp.maximum(x, 0.0) + jnp.log1p(jnp.exp(-jnp.abs(x)))


def _mods_kernel(c_ref, w_ref, b_ref, o_ref):
    s = _silu(c_ref[...]).astype(BF16)
    o_ref[...] = jnp.dot(s, w_ref[...].astype(BF16), preferred_element_type=F32) + b_ref[...]


def _mods(c_pad, w_ada, b_ada):
    rows, d = c_pad.shape
    n = w_ada.shape[1]
    tn = 1024
    return pl.pallas_call(
        _mods_kernel,
        grid=(n // tn,),
        in_specs=[
            pl.BlockSpec((rows, d), lambda j: (0, 0)),
            pl.BlockSpec((d, tn), lambda j: (0, j)),
            pl.BlockSpec((1, tn), lambda j: (0, j)),
        ],
        out_specs=pl.BlockSpec((rows, tn), lambda j: (0, j)),
        out_shape=jax.ShapeDtypeStruct((rows, n), F32),
        compiler_params=pltpu.CompilerParams(
            dimension_semantics=("arbitrary",), vmem_limit_bytes=VMEM_LIMIT),
        name="mods",
    )(c_pad, w_ada, b_ada)


def _norm_kernel(x_ref, sc_ref, sh_ref, gpre_ref, wdt_ref, dtb_ref, h_ref, dt_ref, *, tm, spt):
    rps = tm // spt
    for s in range(spt):
        rows = slice(s * rps, (s + 1) * rps)
        x = x_ref[rows, :]
        y = x * lax.rsqrt(jnp.mean(x * x, axis=-1, keepdims=True) + EPS)
        y = y * gpre_ref[...]
        h_ref[rows, :] = (y * (1.0 + sc_ref[s]) + sh_ref[s]).astype(BF16)
    dt_raw = lax.dot_general(h_ref[...], wdt_ref[...], (((1,), (1,)), ((), ())),
                             preferred_element_type=F32) + dtb_ref[...]
    lane = lax.broadcasted_iota(jnp.int32, dt_raw.shape, 1)
    dt_ref[...] = jnp.where(lane < SSD_HEADS, _softplus(dt_raw), 0.0)


def _norm(x2d, scale, shift, g_pre, w_dt, dt_bias, *, tm, spt):
    rows = x2d.shape[0]
    n_seq = scale.shape[0]
    n_tiles = rows // tm
    tiles_per_seq = max(1, n_tiles // n_seq) if spt == 1 else 1
    seq_blk = (lambda i: i // tiles_per_seq) if spt == 1 else (lambda i: i)
    return pl.pallas_call(
        functools.partial(_norm_kernel, tm=tm, spt=spt),
        grid=(n_tiles,),
        in_specs=[
            pl.BlockSpec((tm, D_MODEL), lambda i: (i, 0)),
            pl.BlockSpec((spt, 1, D_MODEL), lambda i: (seq_blk(i), 0, 0)),
            pl.BlockSpec((spt, 1, D_MODEL), lambda i: (seq_blk(i), 0, 0)),
            pl.BlockSpec((1, D_MODEL), lambda i: (0, 0)),
            pl.BlockSpec((LANES, D_MODEL), lambda i: (0, 0)),
            pl.BlockSpec((1, LANES), lambda i: (0, 0)),
        ],
        out_specs=[
            pl.BlockSpec((tm, D_MODEL), lambda i: (i, 0)),
            pl.BlockSpec((tm, LANES), lambda i: (i, 0)),
        ],
        out_shape=[
            jax.ShapeDtypeStruct((rows, D_MODEL), BF16),
            jax.ShapeDtypeStruct((rows, LANES), F32),
        ],
        compiler_params=pltpu.CompilerParams(
            dimension_semantics=("arbitrary",), vmem_limit_bytes=VMEM_LIMIT),
        name="norm",
    )(x2d, scale, shift, g_pre, w_dt, dt_bias)


def _act_tile(j):
    return jnp.where(j < CONV_STEPS[2], j, jnp.where(j == CONV_STEPS[2], N_PROJ_TILES - 1, j - 1))


def _wt_row(j):
    return pl.multiple_of(j * PROJ_TN + jnp.where(j >= FIRST_SHIFTED_TILE, SSD_HEADS, 0), SSD_HEADS)


def _conv_in_tile(j):
    return jnp.clip(j - CONV_STEPS[0], 0, len(CONV_STEPS) - 1)


def _proj_kernel(h_ref, wt_ref, cw_ref, cb_ref, hist_ref, act_ref, tail_ref,
                 wbf_scr, halo_scr, stg0_scr, stg1_scr, *, tm, spt, nsplit, msplit, tiles_per_seq):
    j = pl.program_id(0)
    i = pl.program_id(1)
    rps = tm // spt
    cw = PROJ_TN // nsplit
    rb = tm // msplit
    wrows = 256

    @pl.when(i == 0)
    def _():
        for r in range(PROJ_TN // wrows):
            rows = slice(r * wrows, (r + 1) * wrows)
            wbf_scr[rows, :] = wt_ref[rows, :].astype(BF16)

    stgs = (stg0_scr, stg1_scr)
    blocks = [(nb, mb) for nb in range(nsplit) for mb in range(msplit)]

    def dot_block(k):
        nb, mb = blocks[k]
        stgs[k % 2][SUBLANES:SUBLANES + rb, :] = lax.dot_general(
            h_ref[mb * rb:(mb + 1) * rb, :], wbf_scr[nb * cw:(nb + 1) * cw, :],
            (((1,), (1,)), ((), ())), preferred_element_type=F32)

    def run(epilogue):
        dot_block(0)
        for k, (nb, mb) in enumerate(blocks):
            if k + 1 < len(blocks):
                dot_block(k + 1)
            epilogue(nb, mb, stgs[k % 2])

    def elementwise(fn):
        def ep(nb, mb, stg):
            act_ref[mb * rb:(mb + 1) * rb, nb * cw:(nb + 1) * cw] = (
                fn(stg[SUBLANES:SUBLANES + rb, :]).astype(BF16))
        return ep

    is_conv = jnp.logical_and(j >= CONV_STEPS[0], j <= CONV_STEPS[2])

    @pl.when(jnp.logical_not(is_conv))
    def _():
        tail_ref[...] = jnp.zeros_like(tail_ref)

    @pl.when(functools.reduce(jnp.logical_or, [j == t for t in SILU_STEPS]))
    def _():
        run(elementwise(_silu))

    @pl.when(jnp.logical_and(j >= IDENT_STEPS[0], j <= IDENT_STEPS[-1]))
    def _():
        run(elementwise(lambda a: a))

    @pl.when(j >= FIRST_SIGMOID_STEP)
    def _():
        run(elementwise(_sigmoid))

    @pl.when(is_conv)
    def _():
        if spt == 1:
            first = (i % tiles_per_seq) == 0
            halo_scr[...] = jnp.where(first, hist_ref[0], halo_scr[...])

        def conv(ext, cols):
            w0, w1, w2, w3 = (cw_ref[k:k + 1, cols] for k in range(CONV_W))
            t = ext * w0
            t = pltpu.roll(t, 1, axis=0) + ext * w1
            t = pltpu.roll(t, 1, axis=0) + ext * w2
            t = pltpu.roll(t, 1, axis=0) + ext * w3 + cb_ref[:, cols]
            return _silu(t[SUBLANES:, :]).astype(BF16)

        def ep(nb, mb, stg):
            cols = slice(nb * cw, (nb + 1) * cw)
            if spt == 1:
                stg[0:SUBLANES, :] = halo_scr[:, cols]
                act_ref[mb * rb:(mb + 1) * rb, cols] = conv(stg[...], cols)
                last = stg[rb:rb + SUBLANES, :]
                halo_scr[:, cols] = last
                tail_ref[0, :, cols] = last
            else:
                for s in range(rb // rps):
                    rows = slice(SUBLANES + s * rps, SUBLANES + (s + 1) * rps)
                    seq = mb * (rb // rps) + s
                    ext = jnp.concatenate([hist_ref[seq, :, cols], stg[rows, :]], axis=0)
                    act_ref[mb * rb + s * rps:mb * rb + (s + 1) * rps, cols] = conv(ext, cols)
                    tail_ref[seq, :, cols] = stg[rows.stop - SUBLANES:rows.stop, :]

        run(ep)


def _proj(h2d, w_t, conv_w, conv_b, hist8, *, tm, spt, nsplit, msplit):
    rows = h2d.shape[0]
    n_seq = hist8.shape[0]
    n_tiles = rows // tm
    tiles_per_seq = max(1, n_tiles // n_seq) if spt == 1 else 1
    seq_blk = (lambda i: i // tiles_per_seq) if spt == 1 else (lambda i: i)
    kern = functools.partial(_proj_kernel, tm=tm, spt=spt, nsplit=nsplit, msplit=msplit,
                             tiles_per_seq=tiles_per_seq)
    return pl.pallas_call(
        kern,
        grid=(N_PROJ_TILES, n_tiles),
        in_specs=[
            pl.BlockSpec((tm, D_MODEL), lambda j, i: (i, 0)),
            pl.BlockSpec((pl.Element(PROJ_TN), pl.Element(D_MODEL)), lambda j, i: (_wt_row(j), 0)),
            pl.BlockSpec((CONV_W, PROJ_TN), lambda j, i: (0, _conv_in_tile(j))),
            pl.BlockSpec((1, PROJ_TN), lambda j, i: (0, _conv_in_tile(j))),
            pl.BlockSpec((spt, SUBLANES, PROJ_TN), lambda j, i: (seq_blk(i), 0, _conv_in_tile(j))),
        ],
        out_specs=[
            pl.BlockSpec((tm, PROJ_TN), lambda j, i: (i, _act_tile(j))),
            pl.BlockSpec((spt, SUBLANES, PROJ_TN), lambda j, i: (i, 0, j)),
        ],
        out_shape=[
            jax.ShapeDtypeStruct((rows, N_ACT), BF16),
            jax.ShapeDtypeStruct((n_tiles * spt, SUBLANES, N_PROJ_TILES * PROJ_TN), F32),
        ],
        scratch_shapes=[
            pltpu.VMEM((PROJ_TN, D_MODEL), BF16),
            pltpu.VMEM((SUBLANES, PROJ_TN), F32),
            pltpu.VMEM((tm // msplit + SUBLANES, PROJ_TN // nsplit), F32),
            pltpu.VMEM((tm // msplit + SUBLANES, PROJ_TN // nsplit), F32),
        ],
        compiler_params=pltpu.CompilerParams(
            dimension_semantics=("arbitrary", "arbitrary"), vmem_limit_bytes=VMEM_LIMIT),
        name="proj",
    )(h2d, w_t, conv_w, conv_b, hist8)


def _pad_rows(v, rows):
    if v.shape[0] == rows:
        return v
    return jnp.concatenate([v, jnp.zeros((rows - v.shape[0], v.shape[1]), v.dtype)], axis=0)


def _ssd_kernel(szs_ref, xs_ref, b_ref, c_ref, dt_ref, alog_ref, dskip_ref, nrm_ref, e3_ref, h0_ref,
                y_ref, hout_ref, st_scr, *, q_in, n_chunks):
    q = SSD_BLOCK
    ci = pl.program_id(1)

    @pl.when(ci == 0)
    def _():
        st_scr[...] = h0_ref[0]

    dt = _pad_rows(dt_ref[...], q)
    a = -jnp.exp(alog_ref[...])
    adt = dt * a
    row = lax.broadcasted_iota(jnp.int32, (q, q), 0)
    col = lax.broadcasted_iota(jnp.int32, (q, q), 1)
    causal = row >= col
    acum = jnp.dot(causal.astype(F32), adt, precision=lax.Precision.HIGHEST,
                   preferred_element_type=F32)
    alast = acum[q - 1:q, :]
    dte = jnp.exp(alast - acum)
    ea = jnp.exp(acum)

    stack = jnp.concatenate([dt, dt * dte, ea], axis=0)
    lane3 = lax.broadcasted_iota(jnp.int32, stack.shape, 1)
    stack = jnp.where(lane3 < SSD_HEADS, stack, 0.0)
    hi = stack.astype(BF16).astype(F32)
    r1 = stack - hi
    mid = r1.astype(BF16).astype(F32)
    lo = (r1 - mid).astype(BF16).astype(F32)
    comb = hi + pltpu.roll(mid, SSD_HEADS, axis=1) + pltpu.roll(lo, 2 * SSD_HEADS, axis=1)
    expd = jnp.dot(comb.astype(BF16), e3_ref[...], preferred_element_type=F32)
    dt_e, w_e, ea_e = expd[0:q], expd[q:2 * q], expd[2 * q:3 * q]

    x = _pad_rows(xs_ref[...], q).astype(F32)
    xdt = (x * dt_e).astype(BF16)
    xdtd = (x * w_e).astype(BF16)
    bm = _pad_rows(b_ref[...], q)
    cm = _pad_rows(c_ref[...], q)

    acum_t = acum.T
    dec = jnp.exp(acum_t[0:SSD_HEADS, q - 1:q])
    dec = jnp.broadcast_to(dec, (SSD_HEADS, D_STATE))
    dec = jnp.broadcast_to(dec[:, None, :], (SSD_HEADS, SSD_HEAD_DIM, D_STATE))
    dec = dec.reshape(SSD_HEADS * SSD_HEAD_DIM, D_STATE)

    lane = lax.broadcasted_iota(jnp.int32, (q, LANES), 1)
    nt = (((1,), (1,)), ((), ()))
    tn = (((0,), (0,)), ((), ()))
    y_groups = []
    for g in range(SSD_GROUPS):
        gc = slice(g * GROUP_COLS, (g + 1) * GROUP_COLS)
        bg = bm[:, g * D_STATE:(g + 1) * D_STATE]
        cg = cm[:, g * D_STATE:(g + 1) * D_STATE]
        cb = lax.dot_general(cg, bg, nt, preferred_element_type=F32)
        sg = st_scr[gc, :]
        y_off = lax.dot_general(cg, sg.astype(BF16), nt, preferred_element_type=F32)
        pieces = []
        for pr in range(HEADS_PER_GROUP // 2):
            ms = []
            for hh in range(2):
                h = g * HEADS_PER_GROUP + 2 * pr + hh
                seg = jnp.broadcast_to(acum[:, h:h + 1], (q, q)) - acum_t[h:h + 1, :]
                lm = jnp.exp(jnp.where(causal, seg, -jnp.inf))
                ms.append((cb * lm).astype(BF16))
            m = jnp.concatenate(ms, axis=1)
            c0 = g * GROUP_COLS + pr * LANES
            xp = xdt[:, c0:c0 + LANES]
            rhs = jnp.concatenate([jnp.where(lane < SSD_HEAD_DIM, xp, jnp.zeros_like(xp)),
                                   jnp.where(lane >= SSD_HEAD_DIM, xp, jnp.zeros_like(xp))], axis=0)
            pieces.append(jnp.dot(m, rhs, preferred_element_type=F32))
        y_diag = jnp.concatenate(pieces, axis=1)
        y_groups.append(y_diag + y_off * ea_e[:, gc])
        new = lax.dot_general(xdtd[:, gc], bg, tn, preferred_element_type=F32)
        st_scr[gc, :] = sg * dec[gc, :] + new
    y = jnp.concatenate(y_groups, axis=1)
    y = y + dskip_ref[...] * x
    yz = y * _pad_rows(szs_ref[...], q).astype(F32)
    yn = yz * lax.rsqrt(jnp.mean(yz * yz, axis=-1, keepdims=True) + EPS) * nrm_ref[...]
    y_ref[...] = yn[0:q_in].astype(BF16)

    @pl.when(ci == n_chunks - 1)
    def _():
        hout_ref[0] = st_scr[...]


def _ssd(act, dt, a_log, dskip_e, norm_ssd, e3, h0, *, n_seq, q_in):
    rows = act.shape[0]
    n_chunks = rows // (n_seq * q_in)
    kern = functools.partial(_ssd_kernel, q_in=q_in, n_chunks=n_chunks)
    rblk = lambda b, c: b * n_chunks + c
    bc_blk0 = 7 * D_MODEL // GROUP_COLS
    return pl.pallas_call(
        kern,
        grid=(n_seq, n_chunks),
        in_specs=[
            pl.BlockSpec((q_in, D_MODEL), lambda b, c: (rblk(b, c), 0)),
            pl.BlockSpec((q_in, D_MODEL), lambda b, c: (rblk(b, c), 1)),
            pl.BlockSpec((q_in, GROUP_COLS), lambda b, c: (rblk(b, c), bc_blk0)),
            pl.BlockSpec((q_in, GROUP_COLS), lambda b, c: (rblk(b, c), bc_blk0 + 1)),
            pl.BlockSpec((q_in, LANES), lambda b, c: (rblk(b, c), 0)),
            pl.BlockSpec((1, LANES), lambda b, c: (0, 0)),
            pl.BlockSpec((1, D_MODEL), lambda b, c: (0, 0)),
            pl.BlockSpec((1, D_MODEL), lambda b, c: (0, 0)),
            pl.BlockSpec((LANES, D_MODEL), lambda b, c: (0, 0)),
            pl.BlockSpec((1, D_MODEL, D_STATE), lambda b, c: (b, 0, 0)),
        ],
        out_specs=[
            pl.BlockSpec((q_in, D_MODEL), lambda b, c: (rblk(b, c), 0)),
            pl.BlockSpec((1, D_MODEL, D_STATE), lambda b, c: (b, 0, 0)),
        ],
        out_shape=[
            jax.ShapeDtypeStruct((rows, D_MODEL), BF16),
            jax.ShapeDtypeStruct((n_seq, D_MODEL, D_STATE), F32),
        ],
        scratch_shapes=[pltpu.VMEM((D_MODEL, D_STATE), F32)],
        compiler_params=pltpu.CompilerParams(
            dimension_semantics=("arbitrary", "arbitrary"), vmem_limit_bytes=VMEM_LIMIT),
        name="ssd",
    )(act, act, act, act, dt, a_log, dskip_e, norm_ssd, e3, h0)


def _merge_kernel(yssd_ref, u_ref, v_ref, szm_ref, sgs_ref, sgm_ref, x_ref, gate_ref,
                  lng_ref, lnb_ref, gfin_ref, wsp_ref, bsp_ref, wbs_ref, wbm_ref, wo_ref,
                  y_ref, *rest, tm, qm, spt, emit_vn):
    if emit_vn:
        vn_ref, ymlp_scr = rest
    else:
        (ymlp_scr,) = rest
    v = v_ref[...].astype(F32)
    mu = jnp.mean(v, axis=-1, keepdims=True)
    vc = v - mu
    var = jnp.mean(vc * vc, axis=-1, keepdims=True)
    vn = vc * lax.rsqrt(var + EPS) * lng_ref[...] + lnb_ref[...]
    if emit_vn:
        vn_ref[...] = vn
    vn_b = vn.astype(BF16)

    row = lax.broadcasted_iota(jnp.int32, (MLP_CHUNK, MLP_CHUNK), 0)
    col = lax.broadcasted_iota(jnp.int32, (MLP_CHUNK, MLP_CHUNK), 1)
    wms = [jnp.where(row >= col, wsp_ref[g], 0.0).astype(BF16) for g in range(MLP_GROUPS)]
    for ck in range(tm // qm):
        rows = slice(ck * qm, (ck + 1) * qm)
        vck = _pad_rows(vn_b[rows, :], MLP_CHUNK)
        mixed = jnp.concatenate(
            [jnp.dot(wms[g], vck[:, g * MLP_GROUP_DIM:(g + 1) * MLP_GROUP_DIM],
                     preferred_element_type=F32) for g in range(MLP_GROUPS)], axis=1)
        mixed = mixed[0:qm] + bsp_ref[...]
        y_mlp = u_ref[rows, :].astype(F32) * mixed * szm_ref[rows, :].astype(F32)
        ymlp_scr[rows, :] = y_mlp.astype(BF16)

    a = jnp.dot(yssd_ref[...], wbs_ref[...], preferred_element_type=F32)
    b = jnp.dot(ymlp_scr[...], wbm_ref[...], preferred_element_type=F32)
    merged = sgs_ref[...].astype(F32) * a + sgm_ref[...].astype(F32) * b
    o = jnp.dot(merged.astype(BF16), wo_ref[...], preferred_element_type=F32)
    rps = tm // spt
    for s in range(spt):
        rows = slice(s * rps, (s + 1) * rps)
        out = x_ref[rows, :] + gate_ref[s] * o[rows, :]
        y = out * lax.rsqrt(jnp.mean(out * out, axis=-1, keepdims=True) + EPS)
        y_ref[rows, :] = y * gfin_ref[...]


def _merge(yssd, act, x2d, gate, ln_g, ln_b, g_final, w_spatial, bsp_e, wbs, wbm, wo,
           *, tm, qm, spt, emit_vn):
    rows = x2d.shape[0]
    n_seq = gate.shape[0]
    n_tiles = rows // tm
    tiles_per_seq = max(1, n_tiles // n_seq) if spt == 1 else 1
    seq_blk = (lambda i: i // tiles_per_seq) if spt == 1 else (lambda i: i)
    kern = functools.partial(_merge_kernel, tm=tm, qm=qm, spt=spt, emit_vn=emit_vn)
    const = lambda shape: pl.BlockSpec(shape, lambda i: (0,) * len(shape),
                                       pipeline_mode=pl.Buffered(1))
    seg = lambda k: pl.BlockSpec((tm, D_MODEL), lambda i: (i, k))
    out_specs = [pl.BlockSpec((tm, D_MODEL), lambda i: (i, 0))]
    out_shape = [jax.ShapeDtypeStruct((rows, D_MODEL), F32)]
    if emit_vn:
        out_specs.append(pl.BlockSpec((tm, D_MODEL), lambda i: (i, 0)))
        out_shape.append(jax.ShapeDtypeStruct((rows, D_MODEL), F32))
    return pl.pallas_call(
        kern,
        grid=(n_tiles,),
        in_specs=[
            pl.BlockSpec((tm, D_MODEL), lambda i: (i, 0)),
            seg(2), seg(3), seg(4), seg(5), seg(6),
            pl.BlockSpec((tm, D_MODEL), lambda i: (i, 0)),
            pl.BlockSpec((spt, 1, D_MODEL), lambda i: (seq_blk(i), 0, 0)),
            const((1, D_MODEL)), const((1, D_MODEL)), const((1, D_MODEL)),
            const((MLP_GROUPS, MLP_CHUNK, MLP_CHUNK)),
            const((qm, D_MODEL)),
            const((D_MODEL, D_MODEL)), const((D_MODEL, D_MODEL)), const((D_MODEL, D_MODEL)),
        ],
        out_specs=out_specs,
        out_shape=out_shape,
        scratch_shapes=[pltpu.VMEM((tm, D_MODEL), BF16)],
        compiler_params=pltpu.CompilerParams(
            dimension_semantics=("arbitrary",), vmem_limit_bytes=VMEM_LIMIT),
        name="merge",
    )(yssd, act, act, act, act, act, x2d, gate, ln_g, ln_b, g_final, w_spatial, bsp_e, wbs, wbm, wo)


def _expansion_matrix():
    e = np.zeros((LANES, D_MODEL), np.float32)
    cols = np.arange(D_MODEL)
    for piece in range(3):
        e[piece * SSD_HEADS + cols // SSD_HEAD_DIM, cols] = 1.0
    return jnp.asarray(e, BF16)


def _stream(x, c_mods, hist8, h0, wts, *, tm_norm, tm, msplit, spt, q_in, tm_merge, qm, emit_vn):
    n_seq, seq_len, d = x.shape
    x2d = x.reshape(n_seq * seq_len, d)
    shift, scale, gate = (c_mods[:, k * d:(k + 1) * d].reshape(n_seq, 1, d) for k in range(3))
    h2d, dt = _norm(x2d, scale, shift, wts["g_pre"], wts["w_dt"], wts["dt_bias"], tm=tm_norm, spt=spt)
    act, tail = _proj(h2d, wts["w_t"], wts["conv_w"], wts["conv_b"], hist8,
                      tm=tm, spt=spt, nsplit=PROJ_NSPLIT, msplit=msplit)
    tail = tail[:, :, CONV_STEPS[0] * PROJ_TN:CONV_STEPS[0] * PROJ_TN + CONV_DIM]
    yssd, h_new = _ssd(act, dt, wts["a_log"], wts["dskip_e"], wts["norm_ssd"], wts["e3"], h0,
                       n_seq=n_seq, q_in=q_in)
    outs = _merge(yssd, act, x2d, gate, wts["ln_g"], wts["ln_b"], wts["g_final"], wts["w_spatial"],
                  wts["bsp_e"][:qm], wts["wbs"], wts["wbm"], wts["wo"],
                  tm=tm_merge, qm=qm, spt=(spt if spt > 1 else 1), emit_vn=emit_vn)
    tiles_per_seq = (n_seq * seq_len // tm) // n_seq if spt == 1 else 1
    if spt == 1:
        new_hist = tail[tiles_per_seq - 1::tiles_per_seq, SUBLANES - (CONV_W - 1):, :]
    else:
        new_hist = tail[:, SUBLANES - (CONV_W - 1):, :]
    y = outs[0].reshape(n_seq, seq_len, d)
    h_new = h_new.reshape(n_seq, SSD_HEADS, SSD_HEAD_DIM, D_STATE)
    vn = outs[1].reshape(n_seq, seq_len, d) if emit_vn else None
    return y, h_new, new_hist, vn


def kernel(x_prompt, x_sample, state_ssm, cache_conv, c_prompt, c_sample, w_ada, b_ada, g_pre, w_in,
           conv_w, conv_b, dt_bias, a_log, d_skip, norm_ssd, ln_g, ln_b, w_spatial, b_spatial,
           w_bp_ssd, w_bp_mlp, w_o, g_final):
    depth = w_ada.shape[0]
    assert depth == 1, "single-layer trunk only"
    bp, seq, d = x_prompt.shape
    bs, dec_seq, _ = x_sample.shape
    assert d == D_MODEL and seq % 1024 == 0 and dec_seq % 16 == 0 and dec_seq <= SSD_BLOCK

    w_t = jnp.swapaxes(w_in[0], 0, 1)
    w_dt = jnp.pad(w_t[DT_OFFSET:DT_OFFSET + SSD_HEADS], ((0, LANES - SSD_HEADS), (0, 0))).astype(BF16)
    pad_heads = lambda v: jnp.pad(v.reshape(1, SSD_HEADS), ((0, 0), (0, LANES - SSD_HEADS)))
    wts = dict(
        g_pre=g_pre[0].reshape(1, d), w_t=w_t, w_dt=w_dt, dt_bias=pad_heads(dt_bias[0]),
        conv_w=conv_w[0], conv_b=conv_b[0].reshape(1, CONV_DIM),
        a_log=pad_heads(a_log[0]), dskip_e=jnp.repeat(d_skip[0], SSD_HEAD_DIM).reshape(1, d),
        norm_ssd=norm_ssd[0].reshape(1, d), e3=_expansion_matrix(),
        ln_g=ln_g[0].reshape(1, d), ln_b=ln_b[0].reshape(1, d), g_final=g_final.reshape(1, d),
        w_spatial=w_spatial[0], bsp_e=jnp.repeat(b_spatial[0].T, MLP_GROUP_DIM, axis=1),
        wbs=w_bp_ssd[0].astype(BF16), wbm=w_bp_mlp[0].astype(BF16), wo=w_o[0].astype(BF16),
    )

    c_all = jnp.concatenate([c_prompt, c_sample], axis=0)
    c_pad = jnp.pad(c_all, ((0, 16 - (bp + bs)), (0, 0)))
    mods = _mods(c_pad, w_ada[0], b_ada[0].reshape(1, 3 * d))

    hist_p = jnp.zeros((bp, SUBLANES, CONV_DIM), F32)
    h0_p = jnp.zeros((bp, d, D_STATE), F32)
    yp, ssm_p, conv_p, _ = _stream(
        x_prompt, mods[:bp], hist_p, h0_p, wts,
        tm_norm=512, tm=1024, msplit=PROJ_MSPLIT, spt=1, q_in=SSD_BLOCK, tm_merge=256, qm=MLP_CHUNK,
        emit_vn=False)

    hist_s = jnp.pad(cache_conv[0], ((0, 0), (SUBLANES - (CONV_W - 1), 0), (0, 0)))
    h0_s = state_ssm[0].reshape(bs, d, D_STATE)
    ys, ssm_s, conv_s, vn_s = _stream(
        x_sample, mods[bp:bp + bs], hist_s, h0_s, wts,
        tm_norm=bs * dec_seq, tm=bs * dec_seq, msplit=1, spt=bs, q_in=dec_seq,
        tm_merge=bs * dec_seq, qm=dec_seq, emit_vn=True)

    return (yp, ys, ssm_p[None], conv_p[None], ssm_s[None], conv_s[None], vn_s[None])
```

```python
import functools

import numpy as np
import jax
import jax.numpy as jnp
from jax import lax
from jax.experimental import pallas as pl
from jax.experimental.pallas import tpu as pltpu

F32 = jnp.float32
BF16 = jnp.bfloat16

D_MODEL = 2048
SSD_HEADS = 32
SSD_HEAD_DIM = 64
SSD_GROUPS = 4
HEADS_PER_GROUP = SSD_HEADS // SSD_GROUPS
D_STATE = 128
GROUP_COLS = HEADS_PER_GROUP * SSD_HEAD_DIM
CONV_W = 4
CONV_DIM = D_MODEL + 2 * SSD_GROUPS * D_STATE
MLP_CHUNK = 128
MLP_GROUPS = 8
MLP_GROUP_DIM = D_MODEL // MLP_GROUPS
EPS = 1e-5

SUBLANES = 8
LANES = 128
SSD_BLOCK = 128
PROJ_TN = 1024
VMEM_LIMIT = 56 * 1024 * 1024

N_ACT = 7 * D_MODEL + 2 * SSD_GROUPS * D_STATE
N_PROJ_TILES = N_ACT // PROJ_TN
DT_OFFSET = D_MODEL + CONV_DIM
FIRST_SHIFTED_TILE = DT_OFFSET // PROJ_TN
CONV_STEPS = (2, 3, 4)
SILU_STEPS = (0, 1, 9, 10)
IDENT_STEPS = (5, 6, 7, 8)
FIRST_SIGMOID_STEP = 11
PROJ_TM = 1024
PROJ_NSPLIT = 2
PROJ_MSPLIT = 4
NORM_TM = 512
MERGE_TM = 256


def _sigmoid(x):
    return 0.5 + 0.5 * jnp.tanh(0.5 * x)


def _silu(x):
    hx = 0.5 * x
    return hx + hx * jnp.tanh(hx)


def _softplus(x):
    return jnp.maximum(x, 0.0) + jnp.log1p(jnp.exp(-jnp.abs(x)))


def _mods_kernel(c_ref, w_ref, b_ref, o_ref):
    s = _silu(c_ref[...]).astype(BF16)
    o_ref[...] = jnp.dot(s, w_ref[...].astype(BF16), preferred_element_type=F32) + b_ref[...]


def _mods(c_pad, w_ada, b_ada):
    rows, d = c_pad.shape
    n = w_ada.shape[1]
    tn = 1024
    return pl.pallas_call(
        _mods_kernel,
        grid=(n // tn,),
        in_specs=[
            pl.BlockSpec((rows, d), lambda j: (0, 0)),
            pl.BlockSpec((d, tn), lambda j: (0, j)),
            pl.BlockSpec((1, tn), lambda j: (0, j)),
        ],
        out_specs=pl.BlockSpec((rows, tn), lambda j: (0, j)),
        out_shape=jax.ShapeDtypeStruct((rows, n), F32),
        compiler_params=pltpu.CompilerParams(
            dimension_semantics=("arbitrary",), vmem_limit_bytes=VMEM_LIMIT),
        name="mods",
    )(c_pad, w_ada, b_ada)


def _norm_kernel(x_ref, sc_ref, sh_ref, gpre_ref, wdt_ref, dtb_ref, h_ref, dt_ref, *, tm, spt):
    rps = tm // spt
    for s in range(spt):
        rows = slice(s * rps, (s + 1) * rps)
        x = x_ref[rows, :]
        y = x * lax.rsqrt(jnp.mean(x * x, axis=-1, keepdims=True) + EPS)
        y = y * gpre_ref[...]
        h_ref[rows, :] = (y * (1.0 + sc_ref[s]) + sh_ref[s]).astype(BF16)
    dt_raw = lax.dot_general(h_ref[...], wdt_ref[...], (((1,), (1,)), ((), ())),
                             preferred_element_type=F32) + dtb_ref[...]
    lane = lax.broadcasted_iota(jnp.int32, dt_raw.shape, 1)
    dt_ref[...] = jnp.where(lane < SSD_HEADS, _softplus(dt_raw), 0.0)


def _norm(x2d, scale, shift, g_pre, w_dt, dt_bias, *, tm, spt):
    rows = x2d.shape[0]
    n_seq = scale.shape[0]
    n_tiles = rows // tm
    tiles_per_seq = max(1, n_tiles // n_seq) if spt == 1 else 1
    seq_blk = (lambda i: i // tiles_per_seq) if spt == 1 else (lambda i: i)
    return pl.pallas_call(
        functools.partial(_norm_kernel, tm=tm, spt=spt),
        grid=(n_tiles,),
        in_specs=[
            pl.BlockSpec((tm, D_MODEL), lambda i: (i, 0)),
            pl.BlockSpec((spt, 1, D_MODEL), lambda i: (seq_blk(i), 0, 0)),
            pl.BlockSpec((spt, 1, D_MODEL), lambda i: (seq_blk(i), 0, 0)),
            pl.BlockSpec((1, D_MODEL), lambda i: (0, 0)),
            pl.BlockSpec((LANES, D_MODEL), lambda i: (0, 0)),
            pl.BlockSpec((1, LANES), lambda i: (0, 0)),
        ],
        out_specs=[
            pl.BlockSpec((tm, D_MODEL), lambda i: (i, 0)),
            pl.BlockSpec((tm, LANES), lambda i: (i, 0)),
        ],
        out_shape=[
            jax.ShapeDtypeStruct((rows, D_MODEL), BF16),
            jax.ShapeDtypeStruct((rows, LANES), F32),
        ],
        compiler_params=pltpu.CompilerParams(
            dimension_semantics=("arbitrary",), vmem_limit_bytes=VMEM_LIMIT),
        name="norm",
    )(x2d, scale, shift, g_pre, w_dt, dt_bias)


def _act_tile(j):
    return jnp.where(j < CONV_STEPS[2], j, jnp.where(j == CONV_STEPS[2], N_PROJ_TILES - 1, j - 1))


def _wt_row(j):
    return pl.multiple_of(j * PROJ_TN + jnp.where(j >= FIRST_SHIFTED_TILE, SSD_HEADS, 0), SSD_HEADS)


def _conv_in_tile(j):
    return jnp.clip(j - CONV_STEPS[0], 0, len(CONV_STEPS) - 1)


def _proj_tile(j, first, h_ref, act_ref, tail_ref, hist_ref, cw_ref, cb_ref, wbf_scr, halo_scr, stgs,
               *, tm, spt, nsplit, msplit):
    rps = tm // spt
    cw = PROJ_TN // nsplit
    rb = tm // msplit
    blocks = [(nb, mb) for nb in range(nsplit) for mb in range(msplit)]

    def dot_block(k):
        nb, mb = blocks[k]
        stgs[k % 2][SUBLANES:SUBLANES + rb, :] = lax.dot_general(
            h_ref[mb * rb:(mb + 1) * rb, :], wbf_scr[nb * cw:(nb + 1) * cw, :],
            (((1,), (1,)), ((), ())), preferred_element_type=F32)

    def run(epilogue):
        dot_block(0)
        for k, (nb, mb) in enumerate(blocks):
            if k + 1 < len(blocks):
                dot_block(k + 1)
            epilogue(nb, mb, stgs[k % 2])

    def elementwise(fn):
        def ep(nb, mb, stg):
            act_ref[mb * rb:(mb + 1) * rb, nb * cw:(nb + 1) * cw] = (
                fn(stg[SUBLANES:SUBLANES + rb, :]).astype(BF16))
        return ep

    is_conv = jnp.logical_and(j >= CONV_STEPS[0], j <= CONV_STEPS[2])

    @pl.when(jnp.logical_not(is_conv))
    def _():
        tail_ref[...] = jnp.zeros_like(tail_ref)

    @pl.when(functools.reduce(jnp.logical_or, [j == t for t in SILU_STEPS]))
    def _():
        run(elementwise(_silu))

    @pl.when(jnp.logical_and(j >= IDENT_STEPS[0], j <= IDENT_STEPS[-1]))
    def _():
        run(elementwise(lambda a: a))

    @pl.when(j >= FIRST_SIGMOID_STEP)
    def _():
        run(elementwise(_sigmoid))

    @pl.when(is_conv)
    def _():
        if spt == 1:
            halo_scr[...] = jnp.where(first, hist_ref[0], halo_scr[...])

        def conv(ext, cols):
            w0, w1, w2, w3 = (cw_ref[k:k + 1, cols] for k in range(CONV_W))
            t = ext * w0
            t = pltpu.roll(t, 1, axis=0) + ext * w1
            t = pltpu.roll(t, 1, axis=0) + ext * w2
            t = pltpu.roll(t, 1, axis=0) + ext * w3 + cb_ref[:, cols]
            return _silu(t[SUBLANES:, :]).astype(BF16)

        def ep(nb, mb, stg):
            cols = slice(nb * cw, (nb + 1) * cw)
            if spt == 1:
                stg[0:SUBLANES, :] = halo_scr[:, cols]
                act_ref[mb * rb:(mb + 1) * rb, cols] = conv(stg[0:SUBLANES + rb, :], cols)
                last = stg[rb:rb + SUBLANES, :]
                halo_scr[:, cols] = last
                tail_ref[0, :, cols] = last
            else:
                for s in range(rb // rps):
                    rows = slice(SUBLANES + s * rps, SUBLANES + (s + 1) * rps)
                    seq = mb * (rb // rps) + s
                    ext = jnp.concatenate([hist_ref[seq, :, cols], stg[rows, :]], axis=0)
                    act_ref[mb * rb + s * rps:mb * rb + (s + 1) * rps, cols] = conv(ext, cols)
                    tail_ref[seq, :, cols] = stg[rows.stop - SUBLANES:rows.stop, :]

        run(ep)


def _proj_kernel(hp_ref, hs_ref, wt_ref, cw_ref, cb_ref, histp_ref, hists_ref,
                 actp_ref, acts_ref, tailp_ref, tails_ref, wbf_scr, halo_scr, stg0_scr, stg1_scr,
                 *, n_p, tm_p, tm_s, spt_s, nsplit, msplit_p, tiles_per_seq):
    j = pl.program_id(0)
    i = pl.program_id(1)
    wrows = 256

    @pl.when(i == 0)
    def _():
        for r in range(PROJ_TN // wrows):
            rows = slice(r * wrows, (r + 1) * wrows)
            wbf_scr[rows, :] = wt_ref[rows, :].astype(BF16)

    shared = (cw_ref, cb_ref, wbf_scr, halo_scr, (stg0_scr, stg1_scr))

    @pl.when(i < n_p)
    def _():
        _proj_tile(j, (i % tiles_per_seq) == 0, hp_ref, actp_ref, tailp_ref, histp_ref, *shared,
                   tm=tm_p, spt=1, nsplit=nsplit, msplit=msplit_p)

    @pl.when(i == n_p)
    def _():
        _proj_tile(j, None, hs_ref, acts_ref, tails_ref, hists_ref, *shared,
                   tm=tm_s, spt=spt_s, nsplit=nsplit, msplit=1)


def _proj(hp, hs, w_t, conv_w, conv_b, hist_p, hist_s, *, tm_p, nsplit, msplit_p):
    rows_p, rows_s = hp.shape[0], hs.shape[0]
    n_p = rows_p // tm_p
    tiles_per_seq = n_p // hist_p.shape[0]
    spt_s = hist_s.shape[0]
    rb = max(tm_p // msplit_p, rows_s)
    kern = functools.partial(_proj_kernel, n_p=n_p, tm_p=tm_p, tm_s=rows_s, spt_s=spt_s,
                             nsplit=nsplit, msplit_p=msplit_p, tiles_per_seq=tiles_per_seq)
    ip = lambda i: jnp.minimum(i, n_p - 1)
    return pl.pallas_call(
        kern,
        grid=(N_PROJ_TILES, n_p + 1),
        in_specs=[
            pl.BlockSpec((tm_p, D_MODEL), lambda j, i: (ip(i), 0)),
            pl.BlockSpec((rows_s, D_MODEL), lambda j, i: (0, 0)),
            pl.BlockSpec((pl.Element(PROJ_TN), pl.Element(D_MODEL)), lambda j, i: (_wt_row(j), 0)),
            pl.BlockSpec((CONV_W, PROJ_TN), lambda j, i: (0, _conv_in_tile(j))),
            pl.BlockSpec((1, PROJ_TN), lambda j, i: (0, _conv_in_tile(j))),
            pl.BlockSpec((1, SUBLANES, PROJ_TN), lambda j, i: (ip(i) // tiles_per_seq, 0, _conv_in_tile(j))),
            pl.BlockSpec((spt_s, SUBLANES, PROJ_TN), lambda j, i: (0, 0, _conv_in_tile(j))),
        ],
        out_specs=[
            pl.BlockSpec((tm_p, PROJ_TN), lambda j, i: (ip(i), _act_tile(j))),
            pl.BlockSpec((rows_s, PROJ_TN), lambda j, i: (0, _act_tile(j))),
            pl.BlockSpec((1, SUBLANES, PROJ_TN), lambda j, i: (ip(i), 0, j)),
            pl.BlockSpec((spt_s, SUBLANES, PROJ_TN), lambda j, i: (0, 0, j)),
        ],
        out_shape=[
            jax.ShapeDtypeStruct((rows_p, N_ACT), BF16),
            jax.ShapeDtypeStruct((rows_s, N_ACT), BF16),
            jax.ShapeDtypeStruct((n_p, SUBLANES, N_PROJ_TILES * PROJ_TN), F32),
            jax.ShapeDtypeStruct((spt_s, SUBLANES, N_PROJ_TILES * PROJ_TN), F32),
        ],
        scratch_shapes=[
            pltpu.VMEM((PROJ_TN, D_MODEL), BF16),
            pltpu.VMEM((SUBLANES, PROJ_TN), F32),
            pltpu.VMEM((rb + SUBLANES, PROJ_TN // nsplit), F32),
            pltpu.VMEM((rb + SUBLANES, PROJ_TN // nsplit), F32),
        ],
        compiler_params=pltpu.CompilerParams(
            dimension_semantics=("arbitrary", "arbitrary"), vmem_limit_bytes=VMEM_LIMIT),
        name="proj",
    )(hp, hs, w_t, conv_w, conv_b, hist_p, hist_s)


def _pad_rows(v, rows):
    if v.shape[0] == rows:
        return v
    return jnp.concatenate([v, jnp.zeros((rows - v.shape[0], v.shape[1]), v.dtype)], axis=0)


def _ssd_kernel(szs_ref, xs_ref, b_ref, c_ref, dt_ref, alog_ref, dskip_ref, nrm_ref, e3_ref, h0_ref,
                y_ref, hout_ref, st_scr, *, q_in, n_chunks):
    q = SSD_BLOCK
    ci = pl.program_id(1)

    @pl.when(ci == 0)
    def _():
        st_scr[...] = h0_ref[0]

    dt = _pad_rows(dt_ref[...], q)
    a = -jnp.exp(alog_ref[...])
    adt = dt * a
    row = lax.broadcasted_iota(jnp.int32, (q, q), 0)
    col = lax.broadcasted_iota(jnp.int32, (q, q), 1)
    causal = row >= col
    acum = jnp.dot(causal.astype(F32), adt, precision=lax.Precision.HIGHEST,
                   preferred_element_type=F32)
    alast = acum[q - 1:q, :]
    dte = jnp.exp(alast - acum)
    ea = jnp.exp(acum)

    stack = jnp.concatenate([dt, dt * dte, ea], axis=0)
    lane3 = lax.broadcasted_iota(jnp.int32, stack.shape, 1)
    stack = jnp.where(lane3 < SSD_HEADS, stack, 0.0)
    hi = stack.astype(BF16).astype(F32)
    r1 = stack - hi
    mid = r1.astype(BF16).astype(F32)
    lo = (r1 - mid).astype(BF16).astype(F32)
    comb = hi + pltpu.roll(mid, SSD_HEADS, axis=1) + pltpu.roll(lo, 2 * SSD_HEADS, axis=1)
    expd = jnp.dot(comb.astype(BF16), e3_ref[...], preferred_element_type=F32)
    dt_e, w_e, ea_e = expd[0:q], expd[q:2 * q], expd[2 * q:3 * q]

    x = _pad_rows(xs_ref[...], q).astype(F32)
    xdt = (x * dt_e).astype(BF16)
    xdtd = (x * w_e).astype(BF16)
    bm = _pad_rows(b_ref[...], q)
    cm = _pad_rows(c_ref[...], q)

    acum_t = acum.T
    dec = jnp.exp(acum_t[0:SSD_HEADS, q - 1:q])
    dec = jnp.broadcast_to(dec, (SSD_HEADS, D_STATE))
    dec = jnp.broadcast_to(dec[:, None, :], (SSD_HEADS, SSD_HEAD_DIM, D_STATE))
    dec = dec.reshape(SSD_HEADS * SSD_HEAD_DIM, D_STATE)

    lane = lax.broadcasted_iota(jnp.int32, (q, LANES), 1)
    nt = (((1,), (1,)), ((), ()))
    tn = (((0,), (0,)), ((), ()))
    y_groups = []
    for g in range(SSD_GROUPS):
        gc = slice(g * GROUP_COLS, (g + 1) * GROUP_COLS)
        bg = bm[:, g * D_STATE:(g + 1) * D_STATE]
        cg = cm[:, g * D_STATE:(g + 1) * D_STATE]
        cb = lax.dot_general(cg, bg, nt, preferred_element_type=F32)
        sg = st_scr[gc, :]
        y_off = lax.dot_general(cg, sg.astype(BF16), nt, preferred_element_type=F32)
        pieces = []
        for pr in range(HEADS_PER_GROUP // 2):
            ms = []
            for hh in range(2):
                h = g * HEADS_PER_GROUP + 2 * pr + hh
                seg = jnp.broadcast_to(acum[:, h:h + 1], (q, q)) - acum_t[h:h + 1, :]
                lm = jnp.exp(jnp.where(causal, seg, -jnp.inf))
                ms.append((cb * lm).astype(BF16))
            m = jnp.concatenate(ms, axis=1)
            c0 = g * GROUP_COLS + pr * LANES
            xp = xdt[:, c0:c0 + LANES]
            rhs = jnp.concatenate([jnp.where(lane < SSD_HEAD_DIM, xp, jnp.zeros_like(xp)),
                                   jnp.where(lane >= SSD_HEAD_DIM, xp, jnp.zeros_like(xp))], axis=0)
            pieces.append(jnp.dot(m, rhs, preferred_element_type=F32))
        y_diag = jnp.concatenate(pieces, axis=1)
        y_groups.append(y_diag + y_off * ea_e[:, gc])
        new = lax.dot_general(xdtd[:, gc], bg, tn, preferred_element_type=F32)
        st_scr[gc, :] = sg * dec[gc, :] + new
    y = jnp.concatenate(y_groups, axis=1)
    y = y + dskip_ref[...] * x
    yz = y * _pad_rows(szs_ref[...], q).astype(F32)
    yn = yz * lax.rsqrt(jnp.mean(yz * yz, axis=-1, keepdims=True) + EPS) * nrm_ref[...]
    y_ref[...] = yn[0:q_in].astype(BF16)

    @pl.when(ci == n_chunks - 1)
    def _():
        hout_ref[0] = st_scr[...]


def _ssd(act, dt, a_log, dskip_e, norm_ssd, e3, h0, *, n_seq, q_in):
    rows = act.shape[0]
    n_chunks = rows // (n_seq * q_in)
    kern = functools.partial(_ssd_kernel, q_in=q_in, n_chunks=n_chunks)
    rblk = lambda b, c: b * n_chunks + c
    bc_blk0 = 7 * D_MODEL // GROUP_COLS
    return pl.pallas_call(
        kern,
        grid=(n_seq, n_chunks),
        in_specs=[
            pl.BlockSpec((q_in, D_MODEL), lambda b, c: (rblk(b, c), 0)),
            pl.BlockSpec((q_in, D_MODEL), lambda b, c: (rblk(b, c), 1)),
            pl.BlockSpec((q_in, GROUP_COLS), lambda b, c: (rblk(b, c), bc_blk0)),
            pl.BlockSpec((q_in, GROUP_COLS), lambda b, c: (rblk(b, c), bc_blk0 + 1)),
            pl.BlockSpec((q_in, LANES), lambda b, c: (rblk(b, c), 0)),
            pl.BlockSpec((1, LANES), lambda b, c: (0, 0)),
            pl.BlockSpec((1, D_MODEL), lambda b, c: (0, 0)),
            pl.BlockSpec((1, D_MODEL), lambda b, c: (0, 0)),
            pl.BlockSpec((LANES, D_MODEL), lambda b, c: (0, 0)),
            pl.BlockSpec((1, D_MODEL, D_STATE), lambda b, c: (b, 0, 0)),
        ],
        out_specs=[
            pl.BlockSpec((q_in, D_MODEL), lambda b, c: (rblk(b, c), 0)),
            pl.BlockSpec((1, D_MODEL, D_STATE), lambda b, c: (b, 0, 0)),
        ],
        out_shape=[
            jax.ShapeDtypeStruct((rows, D_MODEL), BF16),
            jax.ShapeDtypeStruct((n_seq, D_MODEL, D_STATE), F32),
        ],
        scratch_shapes=[pltpu.VMEM((D_MODEL, D_STATE), F32)],
        compiler_params=pltpu.CompilerParams(
            dimension_semantics=("arbitrary", "arbitrary"), vmem_limit_bytes=VMEM_LIMIT),
        name="ssd",
    )(act, act, act, act, dt, a_log, dskip_e, norm_ssd, e3, h0)


def _merge_kernel(yssd_ref, u_ref, v_ref, szm_ref, sgs_ref, sgm_ref, x_ref, gate_ref,
                  lng_ref, lnb_ref, gfin_ref, wsp_ref, bsp_ref, wbs_ref, wbm_ref, wo_ref,
                  y_ref, *rest, tm, qm, spt, emit_vn):
    if emit_vn:
        vn_ref, ymlp_scr = rest
    else:
        (ymlp_scr,) = rest
    v = v_ref[...].astype(F32)
    mu = jnp.mean(v, axis=-1, keepdims=True)
    vc = v - mu
    var = jnp.mean(vc * vc, axis=-1, keepdims=True)
    vn = vc * lax.rsqrt(var + EPS) * lng_ref[...] + lnb_ref[...]
    if emit_vn:
        vn_ref[...] = vn
    vn_b = vn.astype(BF16)

    row = lax.broadcasted_iota(jnp.int32, (MLP_CHUNK, MLP_CHUNK), 0)
    col = lax.broadcasted_iota(jnp.int32, (MLP_CHUNK, MLP_CHUNK), 1)
    wms = [jnp.where(row >= col, wsp_ref[g], 0.0).astype(BF16) for g in range(MLP_GROUPS)]
    for ck in range(tm // qm):
        rows = slice(ck * qm, (ck + 1) * qm)
        vck = _pad_rows(vn_b[rows, :], MLP_CHUNK)
        mixed = jnp.concatenate(
            [jnp.dot(wms[g], vck[:, g * MLP_GROUP_DIM:(g + 1) * MLP_GROUP_DIM],
                     preferred_element_type=F32) for g in range(MLP_GROUPS)], axis=1)
        mixed = mixed[0:qm] + bsp_ref[...]
        y_mlp = u_ref[rows, :].astype(F32) * mixed * szm_ref[rows, :].astype(F32)
        ymlp_scr[rows, :] = y_mlp.astype(BF16)

    a = jnp.dot(yssd_ref[...], wbs_ref[...], preferred_element_type=F32)
    b = jnp.dot(ymlp_scr[...], wbm_ref[...], preferred_element_type=F32)
    merged = sgs_ref[...].astype(F32) * a + sgm_ref[...].astype(F32) * b
    o = jnp.dot(merged.astype(BF16), wo_ref[...], preferred_element_type=F32)
    rps = tm // spt
    for s in range(spt):
        rows = slice(s * rps, (s + 1) * rps)
        out = x_ref[rows, :] + gate_ref[s] * o[rows, :]
        y = out * lax.rsqrt(jnp.mean(out * out, axis=-1, keepdims=True) + EPS)
        y_ref[rows, :] = y * gfin_ref[...]


def _merge(yssd, act, x2d, gate, ln_g, ln_b, g_final, w_spatial, bsp_e, wbs, wbm, wo,
           *, tm, qm, spt, emit_vn):
    rows = x2d.shape[0]
    n_seq = gate.shape[0]
    n_tiles = rows // tm
    tiles_per_seq = max(1, n_tiles // n_seq) if spt == 1 else 1
    seq_blk = (lambda i: i // tiles_per_seq) if spt == 1 else (lambda i: i)
    kern = functools.partial(_merge_kernel, tm=tm, qm=qm, spt=spt, emit_vn=emit_vn)
    const = lambda shape: pl.BlockSpec(shape, lambda i: (0,) * len(shape),
                                       pipeline_mode=pl.Buffered(1))
    seg = lambda k: pl.BlockSpec((tm, D_MODEL), lambda i: (i, k))
    out_specs = [pl.BlockSpec((tm, D_MODEL), lambda i: (i, 0))]
    out_shape = [jax.ShapeDtypeStruct((rows, D_MODEL), F32)]
    if emit_vn:
        out_specs.append(pl.BlockSpec((tm, D_MODEL), lambda i: (i, 0)))
        out_shape.append(jax.ShapeDtypeStruct((rows, D_MODEL), F32))
    return pl.pallas_call(
        kern,
        grid=(n_tiles,),
        in_specs=[
            pl.BlockSpec((tm, D_MODEL), lambda i: (i, 0)),
            seg(2), seg(3), seg(4), seg(5), seg(6),
            pl.BlockSpec((tm, D_MODEL), lambda i: (i, 0)),
            pl.BlockSpec((spt, 1, D_MODEL), lambda i: (seq_blk(i), 0, 0)),
            const((1, D_MODEL)), const((1, D_MODEL)), const((1, D_MODEL)),
            const((MLP_GROUPS, MLP_CHUNK, MLP_CHUNK)),
            const((qm, D_MODEL)),
            const((D_MODEL, D_MODEL)), const((D_MODEL, D_MODEL)), const((D_MODEL, D_MODEL)),
        ],
        out_specs=out_specs,
        out_shape=out_shape,
        scratch_shapes=[pltpu.VMEM((tm, D_MODEL), BF16)],
        compiler_params=pltpu.CompilerParams(
            dimension_semantics=("arbitrary",), vmem_limit_bytes=VMEM_LIMIT),
        name="merge",
    )(yssd, act, act, act, act, act, x2d, gate, ln_g, ln_b, g_final, w_spatial, bsp_e, wbs, wbm, wo)


def _expansion_matrix():
    e = np.zeros((LANES, D_MODEL), np.float32)
    cols = np.arange(D_MODEL)
    for piece in range(3):
        e[piece * SSD_HEADS + cols // SSD_HEAD_DIM, cols] = 1.0
    return jnp.asarray(e, BF16)


def kernel(x_prompt, x_sample, state_ssm, cache_conv, c_prompt, c_sample, w_ada, b_ada, g_pre, w_in,
           conv_w, conv_b, dt_bias, a_log, d_skip, norm_ssd, ln_g, ln_b, w_spatial, b_spatial,
           w_bp_ssd, w_bp_mlp, w_o, g_final):
    depth = w_ada.shape[0]
    assert depth == 1, "single-layer trunk only"
    bp, seq, d = x_prompt.shape
    bs, dec_seq, _ = x_sample.shape
    assert d == D_MODEL and seq % 1024 == 0 and dec_seq % 16 == 0 and dec_seq <= SSD_BLOCK

    w_t = jnp.swapaxes(w_in[0], 0, 1)
    w_dt = jnp.pad(w_t[DT_OFFSET:DT_OFFSET + SSD_HEADS], ((0, LANES - SSD_HEADS), (0, 0))).astype(BF16)
    pad_heads = lambda v: jnp.pad(v.reshape(1, SSD_HEADS), ((0, 0), (0, LANES - SSD_HEADS)))
    wts = dict(
        g_pre=g_pre[0].reshape(1, d), w_t=w_t, w_dt=w_dt, dt_bias=pad_heads(dt_bias[0]),
        conv_w=conv_w[0], conv_b=conv_b[0].reshape(1, CONV_DIM),
        a_log=pad_heads(a_log[0]), dskip_e=jnp.repeat(d_skip[0], SSD_HEAD_DIM).reshape(1, d),
        norm_ssd=norm_ssd[0].reshape(1, d), e3=_expansion_matrix(),
        ln_g=ln_g[0].reshape(1, d), ln_b=ln_b[0].reshape(1, d), g_final=g_final.reshape(1, d),
        w_spatial=w_spatial[0], bsp_e=jnp.repeat(b_spatial[0].T, MLP_GROUP_DIM, axis=1),
        wbs=w_bp_ssd[0].astype(BF16), wbm=w_bp_mlp[0].astype(BF16), wo=w_o[0].astype(BF16),
    )

    c_all = jnp.concatenate([c_prompt, c_sample], axis=0)
    c_pad = jnp.pad(c_all, ((0, 16 - (bp + bs)), (0, 0)))
    mods = _mods(c_pad, w_ada[0], b_ada[0].reshape(1, 3 * d))

    split_mods = lambda m: (m[:, k * d:(k + 1) * d].reshape(m.shape[0], 1, d) for k in range(3))
    shift_p, scale_p, gate_p = split_mods(mods[:bp])
    shift_s, scale_s, gate_s = split_mods(mods[bp:bp + bs])
    x2p = x_prompt.reshape(bp * seq, d)
    x2s = x_sample.reshape(bs * dec_seq, d)
    rows_s = bs * dec_seq

    hist_p = jnp.zeros((bp, SUBLANES, CONV_DIM), F32)
    h0_p = jnp.zeros((bp, d, D_STATE), F32)
    hist_s = jnp.pad(cache_conv[0], ((0, 0), (SUBLANES - (CONV_W - 1), 0), (0, 0)))
    h0_s = state_ssm[0].reshape(bs, d, D_STATE)

    norm_w = (wts["g_pre"], wts["w_dt"], wts["dt_bias"])
    hp, dt_p = _norm(x2p, scale_p, shift_p, *norm_w, tm=NORM_TM, spt=1)
    hs, dt_s = _norm(x2s, scale_s, shift_s, *norm_w, tm=rows_s, spt=bs)
    act_p, act_s, tail_p, tail_s = _proj(hp, hs, wts["w_t"], wts["conv_w"], wts["conv_b"], hist_p, hist_s,
                                         tm_p=PROJ_TM, nsplit=PROJ_NSPLIT, msplit_p=PROJ_MSPLIT)

    ssd_w = (wts["a_log"], wts["dskip_e"], wts["norm_ssd"], wts["e3"])
    yssd_p, ssm_p = _ssd(act_p, dt_p, *ssd_w, h0_p, n_seq=bp, q_in=SSD_BLOCK)
    yssd_s, ssm_s = _ssd(act_s, dt_s, *ssd_w, h0_s, n_seq=bs, q_in=dec_seq)

    merge_w = (wts["ln_g"], wts["ln_b"], wts["g_final"], wts["w_spatial"])
    proj_w = (wts["wbs"], wts["wbm"], wts["wo"])
    (yp,) = _merge(yssd_p, act_p, x2p, gate_p, *merge_w, wts["bsp_e"], *proj_w,
                   tm=MERGE_TM, qm=MLP_CHUNK, spt=1, emit_vn=False)
    ys, vn_s = _merge(yssd_s, act_s, x2s, gate_s, *merge_w, wts["bsp_e"][:dec_seq], *proj_w,
                      tm=rows_s, qm=dec_seq, spt=bs, emit_vn=True)

    conv_cols = slice(CONV_STEPS[0] * PROJ_TN, CONV_STEPS[0] * PROJ_TN + CONV_DIM)
    hist_rows = slice(SUBLANES - (CONV_W - 1), SUBLANES)
    tiles_per_seq = seq // PROJ_TM
    conv_p = tail_p[tiles_per_seq - 1::tiles_per_seq, hist_rows, conv_cols]
    conv_s = tail_s[:, hist_rows, conv_cols]

    state_shape = (SSD_HEADS, SSD_HEAD_DIM, D_STATE)
    return (yp.reshape(bp, seq, d), ys.reshape(bs, dec_seq, d),
            ssm_p.reshape(1, bp, *state_shape), conv_p[None],
            ssm_s.reshape(1, bs, *state_shape), conv_s[None], vn_s.reshape(1, bs, dec_seq, d))
```

```python
import functools

import numpy as np
import jax
import jax.numpy as jnp
from jax import lax
from jax.experimental import pallas as pl
from jax.experimental.pallas import tpu as pltpu

F32 = jnp.float32
BF16 = jnp.bfloat16

D_MODEL = 2048
SSD_HEADS = 32
SSD_HEAD_DIM = 64
SSD_GROUPS = 4
HEADS_PER_GROUP = SSD_HEADS // SSD_GROUPS
D_STATE = 128
GROUP_COLS = HEADS_PER_GROUP * SSD_HEAD_DIM
CONV_W = 4
CONV_DIM = D_MODEL + 2 * SSD_GROUPS * D_STATE
MLP_CHUNK = 128
MLP_GROUPS = 8
MLP_GROUP_DIM = D_MODEL // MLP_GROUPS
EPS = 1e-5

SUBLANES = 8
LANES = 128
SSD_BLOCK = 128
PROJ_TN = 1024
VMEM_LIMIT = 56 * 1024 * 1024

N_ACT = 7 * D_MODEL + 2 * SSD_GROUPS * D_STATE
N_PROJ_TILES = N_ACT // PROJ_TN
DT_OFFSET = D_MODEL + CONV_DIM
FIRST_SHIFTED_TILE = DT_OFFSET // PROJ_TN
CONV_STEPS = (2, 3, 4)
SILU_STEPS = (0, 1, 9, 10)
IDENT_STEPS = (5, 6, 7, 8)
FIRST_SIGMOID_STEP = 11
PROJ_TM = 1024
PROJ_NSPLIT = 2
PROJ_MSPLIT = 4
NORM_TM = 512
MERGE_TM = 256


def _sigmoid(x):
    return 0.5 + 0.5 * jnp.tanh(0.5 * x)


def _silu(x):
    hx = 0.5 * x
    return hx + hx * jnp.tanh(hx)


def _softplus(x):
    return jnp.maximum(x, 0.0) + jnp.log1p(jnp.exp(-jnp.abs(x)))


def _mods_kernel(c_ref, w_ref, b_ref, o_ref):
    s = _silu(c_ref[...]).astype(BF16)
    o_ref[...] = jnp.dot(s, w_ref[...].astype(BF16), preferred_element_type=F32) + b_ref[...]


def _mods(c_pad, w_ada, b_ada):
    rows, d = c_pad.shape
    n = w_ada.shape[1]
    tn = 1024
    return pl.pallas_call(
        _mods_kernel,
        grid=(n // tn,),
        in_specs=[
            pl.BlockSpec((rows, d), lambda j: (0, 0)),
            pl.BlockSpec((d, tn), lambda j: (0, j)),
            pl.BlockSpec((1, tn), lambda j: (0, j)),
        ],
        out_specs=pl.BlockSpec((rows, tn), lambda j: (0, j)),
        out_shape=jax.ShapeDtypeStruct((rows, n), F32),
        compiler_params=pltpu.CompilerParams(
            dimension_semantics=("arbitrary",), vmem_limit_bytes=VMEM_LIMIT),
        name="mods",
    )(c_pad, w_ada, b_ada)


def _norm_kernel(x_ref, sc_ref, sh_ref, gpre_ref, wdt_ref, dtb_ref, h_ref, dt_ref, *, tm, spt):
    rps = tm // spt
    for s in range(spt):
        rows = slice(s * rps, (s + 1) * rps)
        x = x_ref[rows, :]
        y = x * lax.rsqrt(jnp.mean(x * x, axis=-1, keepdims=True) + EPS)
        y = y * gpre_ref[...]
        h_ref[rows, :] = (y * (1.0 + sc_ref[s]) + sh_ref[s]).astype(BF16)
    dt_raw = lax.dot_general(h_ref[...], wdt_ref[...], (((1,), (1,)), ((), ())),
                             preferred_element_type=F32) + dtb_ref[...]
    lane = lax.broadcasted_iota(jnp.int32, dt_raw.shape, 1)
    dt_ref[...] = jnp.where(lane < SSD_HEADS, _softplus(dt_raw), 0.0)


def _norm(x2d, scale, shift, g_pre, w_dt, dt_bias, *, tm, spt):
    rows = x2d.shape[0]
    n_seq = scale.shape[0]
    n_tiles = rows // tm
    tiles_per_seq = max(1, n_tiles // n_seq) if spt == 1 else 1
    seq_blk = (lambda i: i // tiles_per_seq) if spt == 1 else (lambda i: i)
    return pl.pallas_call(
        functools.partial(_norm_kernel, tm=tm, spt=spt),
        grid=(n_tiles,),
        in_specs=[
            pl.BlockSpec((tm, D_MODEL), lambda i: (i, 0)),
            pl.BlockSpec((spt, 1, D_MODEL), lambda i: (seq_blk(i), 0, 0)),
            pl.BlockSpec((spt, 1, D_MODEL), lambda i: (seq_blk(i), 0, 0)),
            pl.BlockSpec((1, D_MODEL), lambda i: (0, 0)),
            pl.BlockSpec((LANES, D_MODEL), lambda i: (0, 0)),
            pl.BlockSpec((1, LANES), lambda i: (0, 0)),
        ],
        out_specs=[
            pl.BlockSpec((tm, D_MODEL), lambda i: (i, 0)),
            pl.BlockSpec((tm, LANES), lambda i: (i, 0)),
        ],
        out_shape=[
            jax.ShapeDtypeStruct((rows, D_MODEL), BF16),
            jax.ShapeDtypeStruct((rows, LANES), F32),
        ],
        compiler_params=pltpu.CompilerParams(
            dimension_semantics=("arbitrary",), vmem_limit_bytes=VMEM_LIMIT),
        name="norm",
    )(x2d, scale, shift, g_pre, w_dt, dt_bias)


def _act_tile(j):
    return jnp.where(j < CONV_STEPS[2], j, jnp.where(j == CONV_STEPS[2], N_PROJ_TILES - 1, j - 1))


def _wt_row(j):
    return pl.multiple_of(j * PROJ_TN + jnp.where(j >= FIRST_SHIFTED_TILE, SSD_HEADS, 0), SSD_HEADS)


def _conv_in_tile(j):
    return jnp.clip(j - CONV_STEPS[0], 0, len(CONV_STEPS) - 1)


def _proj_tile(j, first, h_ref, act_ref, tail_ref, hist_ref, cw_ref, cb_ref, wbf_scr, halo_scr, stgs,
               *, tm, spt, nsplit, msplit):
    rps = tm // spt
    cw = PROJ_TN // nsplit
    rb = tm // msplit
    blocks = [(nb, mb) for nb in range(nsplit) for mb in range(msplit)]

    def dot_block(k):
        nb, mb = blocks[k]
        stgs[k % 2][SUBLANES:SUBLANES + rb, :] = lax.dot_general(
            h_ref[mb * rb:(mb + 1) * rb, :], wbf_scr[nb * cw:(nb + 1) * cw, :],
            (((1,), (1,)), ((), ())), preferred_element_type=F32)

    def run(epilogue):
        dot_block(0)
        for k, (nb, mb) in enumerate(blocks):
            if k + 1 < len(blocks):
                dot_block(k + 1)
            epilogue(nb, mb, stgs[k % 2])

    def elementwise(fn):
        def ep(nb, mb, stg):
            act_ref[mb * rb:(mb + 1) * rb, nb * cw:(nb + 1) * cw] = (
                fn(stg[SUBLANES:SUBLANES + rb, :]).astype(BF16))
        return ep

    is_conv = jnp.logical_and(j >= CONV_STEPS[0], j <= CONV_STEPS[2])

    @pl.when(jnp.logical_not(is_conv))
    def _():
        tail_ref[...] = jnp.zeros_like(tail_ref)

    @pl.when(functools.reduce(jnp.logical_or, [j == t for t in SILU_STEPS]))
    def _():
        run(elementwise(_silu))

    @pl.when(jnp.logical_and(j >= IDENT_STEPS[0], j <= IDENT_STEPS[-1]))
    def _():
        run(elementwise(lambda a: a))

    @pl.when(j >= FIRST_SIGMOID_STEP)
    def _():
        run(elementwise(_sigmoid))

    @pl.when(is_conv)
    def _():
        if spt == 1:
            halo_scr[...] = jnp.where(first, hist_ref[0], halo_scr[...])

        def conv(ext, cols):
            w0, w1, w2, w3 = (cw_ref[k:k + 1, cols] for k in range(CONV_W))
            t = ext * w0
            t = pltpu.roll(t, 1, axis=0) + ext * w1
            t = pltpu.roll(t, 1, axis=0) + ext * w2
            t = pltpu.roll(t, 1, axis=0) + ext * w3 + cb_ref[:, cols]
            return _silu(t[SUBLANES:, :]).astype(BF16)

        def ep(nb, mb, stg):
            cols = slice(nb * cw, (nb + 1) * cw)
            if spt == 1:
                stg[0:SUBLANES, :] = halo_scr[:, cols]
                act_ref[mb * rb:(mb + 1) * rb, cols] = conv(stg[0:SUBLANES + rb, :], cols)
                last = stg[rb:rb + SUBLANES, :]
                halo_scr[:, cols] = last
                tail_ref[0, :, cols] = last
            else:
                for s in range(rb // rps):
                    rows = slice(SUBLANES + s * rps, SUBLANES + (s + 1) * rps)
                    seq = mb * (rb // rps) + s
                    ext = jnp.concatenate([hist_ref[seq, :, cols], stg[rows, :]], axis=0)
                    act_ref[mb * rb + s * rps:mb * rb + (s + 1) * rps, cols] = conv(ext, cols)
                    tail_ref[seq, :, cols] = stg[rows.stop - SUBLANES:rows.stop, :]

        run(ep)


def _proj_kernel(hp_ref, hs_ref, wt_ref, cw_ref, cb_ref, histp_ref, hists_ref,
                 actp_ref, acts_ref, tailp_ref, tails_ref, wbf_scr, halo_scr, stg0_scr, stg1_scr,
                 *, n_p, tm_p, tm_s, spt_s, nsplit, msplit_p, tiles_per_seq):
    j = pl.program_id(0)
    i = pl.program_id(1)
    wrows = 256

    @pl.when(i == 0)
    def _():
        for r in range(PROJ_TN // wrows):
            rows = slice(r * wrows, (r + 1) * wrows)
            wbf_scr[rows, :] = wt_ref[rows, :].astype(BF16)

    shared = (cw_ref, cb_ref, wbf_scr, halo_scr, (stg0_scr, stg1_scr))

    @pl.when(i == 0)
    def _():
        _proj_tile(j, None, hs_ref, acts_ref, tails_ref, hists_ref, *shared,
                   tm=tm_s, spt=spt_s, nsplit=nsplit, msplit=1)

    @pl.when(i > 0)
    def _():
        _proj_tile(j, ((i - 1) % tiles_per_seq) == 0, hp_ref, actp_ref, tailp_ref, histp_ref, *shared,
                   tm=tm_p, spt=1, nsplit=nsplit, msplit=msplit_p)


def _proj(hp, hs, w_t, conv_w, conv_b, hist_p, hist_s, *, tm_p, nsplit, msplit_p):
    rows_p, rows_s = hp.shape[0], hs.shape[0]
    n_p = rows_p // tm_p
    tiles_per_seq = n_p // hist_p.shape[0]
    spt_s = hist_s.shape[0]
    rb = max(tm_p // msplit_p, rows_s)
    kern = functools.partial(_proj_kernel, n_p=n_p, tm_p=tm_p, tm_s=rows_s, spt_s=spt_s,
                             nsplit=nsplit, msplit_p=msplit_p, tiles_per_seq=tiles_per_seq)
    ip = lambda i: jnp.maximum(i - 1, 0)
    return pl.pallas_call(
        kern,
        grid=(N_PROJ_TILES, n_p + 1),
        in_specs=[
            pl.BlockSpec((tm_p, D_MODEL), lambda j, i: (ip(i), 0)),
            pl.BlockSpec((rows_s, D_MODEL), lambda j, i: (0, 0)),
            pl.BlockSpec((pl.Element(PROJ_TN), pl.Element(D_MODEL)), lambda j, i: (_wt_row(j), 0)),
            pl.BlockSpec((CONV_W, PROJ_TN), lambda j, i: (0, _conv_in_tile(j))),
            pl.BlockSpec((1, PROJ_TN), lambda j, i: (0, _conv_in_tile(j))),
            pl.BlockSpec((1, SUBLANES, PROJ_TN), lambda j, i: (ip(i) // tiles_per_seq, 0, _conv_in_tile(j))),
            pl.BlockSpec((spt_s, SUBLANES, PROJ_TN), lambda j, i: (0, 0, _conv_in_tile(j))),
        ],
        out_specs=[
            pl.BlockSpec((tm_p, PROJ_TN), lambda j, i: (ip(i), _act_tile(j))),
            pl.BlockSpec((rows_s, PROJ_TN), lambda j, i: (0, _act_tile(j))),
            pl.BlockSpec((1, SUBLANES, PROJ_TN), lambda j, i: (ip(i), 0, j)),
            pl.BlockSpec((spt_s, SUBLANES, PROJ_TN), lambda j, i: (0, 0, j)),
        ],
        out_shape=[
            jax.ShapeDtypeStruct((rows_p, N_ACT), BF16),
            jax.ShapeDtypeStruct((rows_s, N_ACT), BF16),
            jax.ShapeDtypeStruct((n_p, SUBLANES, N_PROJ_TILES * PROJ_TN), F32),
            jax.ShapeDtypeStruct((spt_s, SUBLANES, N_PROJ_TILES * PROJ_TN), F32),
        ],
        scratch_shapes=[
            pltpu.VMEM((PROJ_TN, D_MODEL), BF16),
            pltpu.VMEM((SUBLANES, PROJ_TN), F32),
            pltpu.VMEM((rb + SUBLANES, PROJ_TN // nsplit), F32),
            pltpu.VMEM((rb + SUBLANES, PROJ_TN // nsplit), F32),
        ],
        compiler_params=pltpu.CompilerParams(
            dimension_semantics=("arbitrary", "arbitrary"), vmem_limit_bytes=VMEM_LIMIT),
        name="proj",
    )(hp, hs, w_t, conv_w, conv_b, hist_p, hist_s)


def _pad_rows(v, rows):
    if v.shape[0] == rows:
        return v
    return jnp.concatenate([v, jnp.zeros((rows - v.shape[0], v.shape[1]), v.dtype)], axis=0)


def _ssd_kernel(szs_ref, xs_ref, b_ref, c_ref, dt_ref, alog_ref, dskip_ref, nrm_ref, e3_ref, h0_ref,
                y_ref, hout_ref, st_scr, *, q_in, n_chunks):
    q = SSD_BLOCK
    ci = pl.program_id(1)

    @pl.when(ci == 0)
    def _():
        st_scr[...] = h0_ref[0]

    dt = _pad_rows(dt_ref[...], q)
    a = -jnp.exp(alog_ref[...])
    adt = dt * a
    row = lax.broadcasted_iota(jnp.int32, (q, q), 0)
    col = lax.broadcasted_iota(jnp.int32, (q, q), 1)
    causal = row >= col
    acum = jnp.dot(causal.astype(F32), adt, precision=lax.Precision.HIGHEST,
                   preferred_element_type=F32)
    alast = acum[q - 1:q, :]
    dte = jnp.exp(alast - acum)
    ea = jnp.exp(acum)

    stack = jnp.concatenate([dt, dt * dte, ea], axis=0)
    lane3 = lax.broadcasted_iota(jnp.int32, stack.shape, 1)
    stack = jnp.where(lane3 < SSD_HEADS, stack, 0.0)
    hi = stack.astype(BF16).astype(F32)
    r1 = stack - hi
    mid = r1.astype(BF16).astype(F32)
    lo = (r1 - mid).astype(BF16).astype(F32)
    comb = hi + pltpu.roll(mid, SSD_HEADS, axis=1) + pltpu.roll(lo, 2 * SSD_HEADS, axis=1)
    expd = jnp.dot(comb.astype(BF16), e3_ref[...], preferred_element_type=F32)
    dt_e, w_e, ea_e = expd[0:q], expd[q:2 * q], expd[2 * q:3 * q]

    x = _pad_rows(xs_ref[...], q).astype(F32)
    xdt = (x * dt_e).astype(BF16)
    xdtd = (x * w_e).astype(BF16)
    bm = _pad_rows(b_ref[...], q)
    cm = _pad_rows(c_ref[...], q)

    acum_t = acum.T
    dec = jnp.exp(acum_t[0:SSD_HEADS, q - 1:q])
    dec = jnp.broadcast_to(dec, (SSD_HEADS, D_STATE))
    dec = jnp.broadcast_to(dec[:, None, :], (SSD_HEADS, SSD_HEAD_DIM, D_STATE))
    dec = dec.reshape(SSD_HEADS * SSD_HEAD_DIM, D_STATE)

    lane = lax.broadcasted_iota(jnp.int32, (q, LANES), 1)
    nt = (((1,), (1,)), ((), ()))
    tn = (((0,), (0,)), ((), ()))
    y_groups = []
    for g in range(SSD_GROUPS):
        gc = slice(g * GROUP_COLS, (g + 1) * GROUP_COLS)
        bg = bm[:, g * D_STATE:(g + 1) * D_STATE]
        cg = cm[:, g * D_STATE:(g + 1) * D_STATE]
        cb = lax.dot_general(cg, bg, nt, preferred_element_type=F32)
        sg = st_scr[gc, :]
        y_off = lax.dot_general(cg, sg.astype(BF16), nt, preferred_element_type=F32)
        pieces = []
        for pr in range(HEADS_PER_GROUP // 2):
            ms = []
            for hh in range(2):
                h = g * HEADS_PER_GROUP + 2 * pr + hh
                seg = jnp.broadcast_to(acum[:, h:h + 1], (q, q)) - acum_t[h:h + 1, :]
                lm = jnp.exp(jnp.where(causal, seg, -jnp.inf))
                ms.append((cb * lm).astype(BF16))
            m = jnp.concatenate(ms, axis=1)
            c0 = g * GROUP_COLS + pr * LANES
            xp = xdt[:, c0:c0 + LANES]
            rhs = jnp.concatenate([jnp.where(lane < SSD_HEAD_DIM, xp, jnp.zeros_like(xp)),
                                   jnp.where(lane >= SSD_HEAD_DIM, xp, jnp.zeros_like(xp))], axis=0)
            pieces.append(jnp.dot(m, rhs, preferred_element_type=F32))
        y_diag = jnp.concatenate(pieces, axis=1)
        y_groups.append(y_diag + y_off * ea_e[:, gc])
        new = lax.dot_general(xdtd[:, gc], bg, tn, preferred_element_type=F32)
        st_scr[gc, :] = sg * dec[gc, :] + new
    y = jnp.concatenate(y_groups, axis=1)
    y = y + dskip_ref[...] * x
    yz = y * _pad_rows(szs_ref[...], q).astype(F32)
    yn = yz * lax.rsqrt(jnp.mean(yz * yz, axis=-1, keepdims=True) + EPS) * nrm_ref[...]
    y_ref[...] = yn[0:q_in].astype(BF16)

    @pl.when(ci == n_chunks - 1)
    def _():
        hout_ref[0] = st_scr[...]


def _ssd(act, dt, a_log, dskip_e, norm_ssd, e3, h0, *, n_seq, q_in):
    rows = act.shape[0]
    n_chunks = rows // (n_seq * q_in)
    kern = functools.partial(_ssd_kernel, q_in=q_in, n_chunks=n_chunks)
    rblk = lambda b, c: b * n_chunks + c
    bc_blk0 = 7 * D_MODEL // GROUP_COLS
    return pl.pallas_call(
        kern,
        grid=(n_seq, n_chunks),
        in_specs=[
            pl.BlockSpec((q_in, D_MODEL), lambda b, c: (rblk(b, c), 0)),
            pl.BlockSpec((q_in, D_MODEL), lambda b, c: (rblk(b, c), 1)),
            pl.BlockSpec((q_in, GROUP_COLS), lambda b, c: (rblk(b, c), bc_blk0)),
            pl.BlockSpec((q_in, GROUP_COLS), lambda b, c: (rblk(b, c), bc_blk0 + 1)),
            pl.BlockSpec((q_in, LANES), lambda b, c: (rblk(b, c), 0)),
            pl.BlockSpec((1, LANES), lambda b, c: (0, 0)),
            pl.BlockSpec((1, D_MODEL), lambda b, c: (0, 0)),
            pl.BlockSpec((1, D_MODEL), lambda b, c: (0, 0)),
            pl.BlockSpec((LANES, D_MODEL), lambda b, c: (0, 0)),
            pl.BlockSpec((1, D_MODEL, D_STATE), lambda b, c: (b, 0, 0)),
        ],
        out_specs=[
            pl.BlockSpec((q_in, D_MODEL), lambda b, c: (rblk(b, c), 0)),
            pl.BlockSpec((1, D_MODEL, D_STATE), lambda b, c: (b, 0, 0)),
        ],
        out_shape=[
            jax.ShapeDtypeStruct((rows, D_MODEL), BF16),
            jax.ShapeDtypeStruct((n_seq, D_MODEL, D_STATE), F32),
        ],
        scratch_shapes=[pltpu.VMEM((D_MODEL, D_STATE), F32)],
        compiler_params=pltpu.CompilerParams(
            dimension_semantics=("arbitrary", "arbitrary"), vmem_limit_bytes=VMEM_LIMIT),
        name="ssd",
    )(act, act, act, act, dt, a_log, dskip_e, norm_ssd, e3, h0)


def _merge_kernel(yssd_ref, u_ref, v_ref, szm_ref, sgs_ref, sgm_ref, x_ref, gate_ref,
                  lng_ref, lnb_ref, gfin_ref, wsp_ref, bsp_ref, wbs_ref, wbm_ref, wo_ref,
                  y_ref, *rest, tm, qm, spt, emit_vn):
    if emit_vn:
        vn_ref, ymlp_scr = rest
    else:
        (ymlp_scr,) = rest
    v = v_ref[...].astype(F32)
    mu = jnp.mean(v, axis=-1, keepdims=True)
    vc = v - mu
    var = jnp.mean(vc * vc, axis=-1, keepdims=True)
    vn = vc * lax.rsqrt(var + EPS) * lng_ref[...] + lnb_ref[...]
    if emit_vn:
        vn_ref[...] = vn
    vn_b = vn.astype(BF16)

    row = lax.broadcasted_iota(jnp.int32, (MLP_CHUNK, MLP_CHUNK), 0)
    col = lax.broadcasted_iota(jnp.int32, (MLP_CHUNK, MLP_CHUNK), 1)
    wms = [jnp.where(row >= col, wsp_ref[g], 0.0).astype(BF16) for g in range(MLP_GROUPS)]
    for ck in range(tm // qm):
        rows = slice(ck * qm, (ck + 1) * qm)
        vck = _pad_rows(vn_b[rows, :], MLP_CHUNK)
        mixed = jnp.concatenate(
            [jnp.dot(wms[g], vck[:, g * MLP_GROUP_DIM:(g + 1) * MLP_GROUP_DIM],
                     preferred_element_type=F32) for g in range(MLP_GROUPS)], axis=1)
        mixed = mixed[0:qm] + bsp_ref[...]
        y_mlp = u_ref[rows, :].astype(F32) * mixed * szm_ref[rows, :].astype(F32)
        ymlp_scr[rows, :] = y_mlp.astype(BF16)

    a = jnp.dot(yssd_ref[...], wbs_ref[...], preferred_element_type=F32)
    b = jnp.dot(ymlp_scr[...], wbm_ref[...], preferred_element_type=F32)
    merged = sgs_ref[...].astype(F32) * a + sgm_ref[...].astype(F32) * b
    o = jnp.dot(merged.astype(BF16), wo_ref[...], preferred_element_type=F32)
    rps = tm // spt
    for s in range(spt):
        rows = slice(s * rps, (s + 1) * rps)
        out = x_ref[rows, :] + gate_ref[s] * o[rows, :]
        y = out * lax.rsqrt(jnp.mean(out * out, axis=-1, keepdims=True) + EPS)
        y_ref[rows, :] = y * gfin_ref[...]


def _merge(yssd, act, x2d, gate, ln_g, ln_b, g_final, w_spatial, bsp_e, wbs, wbm, wo,
           *, tm, qm, spt, emit_vn):
    rows = x2d.shape[0]
    n_seq = gate.shape[0]
    n_tiles = rows // tm
    tiles_per_seq = max(1, n_tiles // n_seq) if spt == 1 else 1
    seq_blk = (lambda i: i // tiles_per_seq) if spt == 1 else (lambda i: i)
    kern = functools.partial(_merge_kernel, tm=tm, qm=qm, spt=spt, emit_vn=emit_vn)
    const = lambda shape: pl.BlockSpec(shape, lambda i: (0,) * len(shape),
                                       pipeline_mode=pl.Buffered(1))
    seg = lambda k: pl.BlockSpec((tm, D_MODEL), lambda i: (i, k))
    out_specs = [pl.BlockSpec((tm, D_MODEL), lambda i: (i, 0))]
    out_shape = [jax.ShapeDtypeStruct((rows, D_MODEL), F32)]
    if emit_vn:
        out_specs.append(pl.BlockSpec((tm, D_MODEL), lambda i: (i, 0)))
        out_shape.append(jax.ShapeDtypeStruct((rows, D_MODEL), F32))
    return pl.pallas_call(
        kern,
        grid=(n_tiles,),
        in_specs=[
            pl.BlockSpec((tm, D_MODEL), lambda i: (i, 0)),
            seg(2), seg(3), seg(4), seg(5), seg(6),
            pl.BlockSpec((tm, D_MODEL), lambda i: (i, 0)),
            pl.BlockSpec((spt, 1, D_MODEL), lambda i: (seq_blk(i), 0, 0)),
            const((1, D_MODEL)), const((1, D_MODEL)), const((1, D_MODEL)),
            const((MLP_GROUPS, MLP_CHUNK, MLP_CHUNK)),
            const((qm, D_MODEL)),
            const((D_MODEL, D_MODEL)), const((D_MODEL, D_MODEL)), const((D_MODEL, D_MODEL)),
        ],
        out_specs=out_specs,
        out_shape=out_shape,
        scratch_shapes=[pltpu.VMEM((tm, D_MODEL), BF16)],
        compiler_params=pltpu.CompilerParams(
            dimension_semantics=("arbitrary",), vmem_limit_bytes=VMEM_LIMIT),
        name="merge",
    )(yssd, act, act, act, act, act, x2d, gate, ln_g, ln_b, g_final, w_spatial, bsp_e, wbs, wbm, wo)


def _expansion_matrix():
    e = np.zeros((LANES, D_MODEL), np.float32)
    cols = np.arange(D_MODEL)
    for piece in range(3):
        e[piece * SSD_HEADS + cols // SSD_HEAD_DIM, cols] = 1.0
    return jnp.asarray(e, BF16)


def kernel(x_prompt, x_sample, state_ssm, cache_conv, c_prompt, c_sample, w_ada, b_ada, g_pre, w_in,
           conv_w, conv_b, dt_bias, a_log, d_skip, norm_ssd, ln_g, ln_b, w_spatial, b_spatial,
           w_bp_ssd, w_bp_mlp, w_o, g_final):
    depth = w_ada.shape[0]
    assert depth == 1, "single-layer trunk only"
    bp, seq, d = x_prompt.shape
    bs, dec_seq, _ = x_sample.shape
    assert d == D_MODEL and seq % 1024 == 0 and dec_seq % 16 == 0 and dec_seq <= SSD_BLOCK

    w_t = jnp.swapaxes(w_in[0], 0, 1)
    w_dt = jnp.pad(w_t[DT_OFFSET:DT_OFFSET + SSD_HEADS], ((0, LANES - SSD_HEADS), (0, 0))).astype(BF16)
    pad_heads = lambda v: jnp.pad(v.reshape(1, SSD_HEADS), ((0, 0), (0, LANES - SSD_HEADS)))
    wts = dict(
        g_pre=g_pre[0].reshape(1, d), w_t=w_t, w_dt=w_dt, dt_bias=pad_heads(dt_bias[0]),
        conv_w=conv_w[0], conv_b=conv_b[0].reshape(1, CONV_DIM),
        a_log=pad_heads(a_log[0]), dskip_e=jnp.repeat(d_skip[0], SSD_HEAD_DIM).reshape(1, d),
        norm_ssd=norm_ssd[0].reshape(1, d), e3=_expansion_matrix(),
        ln_g=ln_g[0].reshape(1, d), ln_b=ln_b[0].reshape(1, d), g_final=g_final.reshape(1, d),
        w_spatial=w_spatial[0], bsp_e=jnp.repeat(b_spatial[0].T, MLP_GROUP_DIM, axis=1),
        wbs=w_bp_ssd[0].astype(BF16), wbm=w_bp_mlp[0].astype(BF16), wo=w_o[0].astype(BF16),
    )

    c_all = jnp.concatenate([c_prompt, c_sample], axis=0)
    c_pad = jnp.pad(c_all, ((0, 16 - (bp + bs)), (0, 0)))
    mods = _mods(c_pad, w_ada[0], b_ada[0].reshape(1, 3 * d))

    split_mods = lambda m: (m[:, k * d:(k + 1) * d].reshape(m.shape[0], 1, d) for k in range(3))
    shift_p, scale_p, gate_p = split_mods(mods[:bp])
    shift_s, scale_s, gate_s = split_mods(mods[bp:bp + bs])
    x2p = x_prompt.reshape(bp * seq, d)
    x2s = x_sample.reshape(bs * dec_seq, d)
    rows_s = bs * dec_seq

    hist_p = jnp.zeros((bp, SUBLANES, CONV_DIM), F32)
    h0_p = jnp.zeros((bp, d, D_STATE), F32)
    hist_s = jnp.pad(cache_conv[0], ((0, 0), (SUBLANES - (CONV_W - 1), 0), (0, 0)))
    h0_s = state_ssm[0].reshape(bs, d, D_STATE)

    norm_w = (wts["g_pre"], wts["w_dt"], wts["dt_bias"])
    hp, dt_p = _norm(x2p, scale_p, shift_p, *norm_w, tm=NORM_TM, spt=1)
    hs, dt_s = _norm(x2s, scale_s, shift_s, *norm_w, tm=rows_s, spt=bs)
    act_p, act_s, tail_p, tail_s = _proj(hp, hs, wts["w_t"], wts["conv_w"], wts["conv_b"], hist_p, hist_s,
                                         tm_p=PROJ_TM, nsplit=PROJ_NSPLIT, msplit_p=PROJ_MSPLIT)

    ssd_w = (wts["a_log"], wts["dskip_e"], wts["norm_ssd"], wts["e3"])
    yssd_p, ssm_p = _ssd(act_p, dt_p, *ssd_w, h0_p, n_seq=bp, q_in=SSD_BLOCK)
    yssd_s, ssm_s = _ssd(act_s, dt_s, *ssd_w, h0_s, n_seq=bs, q_in=dec_seq)

    merge_w = (wts["ln_g"], wts["ln_b"], wts["g_final"], wts["w_spatial"])
    proj_w = (wts["wbs"], wts["wbm"], wts["wo"])
    (yp,) = _merge(yssd_p, act_p, x2p, gate_p, *merge_w, wts["bsp_e"], *proj_w,
                   tm=MERGE_TM, qm=MLP_CHUNK, spt=1, emit_vn=False)
    ys, vn_s = _merge(yssd_s, act_s, x2s, gate_s, *merge_w, wts["bsp_e"][:dec_seq], *proj_w,
                      tm=rows_s, qm=dec_seq, spt=bs, emit_vn=True)

    conv_cols = slice(CONV_STEPS[0] * PROJ_TN, CONV_STEPS[0] * PROJ_TN + CONV_DIM)
    hist_rows = slice(SUBLANES - (CONV_W - 1), SUBLANES)
    tiles_per_seq = seq // PROJ_TM
    conv_p = tail_p[tiles_per_seq - 1::tiles_per_seq, hist_rows, conv_cols]
    conv_s = tail_s[:, hist_rows, conv_cols]

    state_shape = (SSD_HEADS, SSD_HEAD_DIM, D_STATE)
    return (yp.reshape(bp, seq, d), ys.reshape(bs, dec_seq, d),
            ssm_p.reshape(1, bp, *state_shape), conv_p[None],
            ssm_s.reshape(1, bs, *state_shape), conv_s[None], vn_s.reshape(1, bs, dec_seq, d))
```

```python
import functools

import numpy as np
import jax
import jax.numpy as jnp
from jax import lax
from jax.experimental import pallas as pl
from jax.experimental.pallas import tpu as pltpu

F32 = jnp.float32
BF16 = jnp.bfloat16

D_MODEL = 2048
SSD_HEADS = 32
SSD_HEAD_DIM = 64
SSD_GROUPS = 4
HEADS_PER_GROUP = SSD_HEADS // SSD_GROUPS
D_STATE = 128
GROUP_COLS = HEADS_PER_GROUP * SSD_HEAD_DIM
CONV_W = 4
CONV_DIM = D_MODEL + 2 * SSD_GROUPS * D_STATE
MLP_CHUNK = 128
MLP_GROUPS = 8
MLP_GROUP_DIM = D_MODEL // MLP_GROUPS
EPS = 1e-5

SUBLANES = 8
LANES = 128
SSD_BLOCK = 128
SSD_BLOCKS_PER_STEP = 4
PROJ_TN = 1024
VMEM_LIMIT = 56 * 1024 * 1024

N_ACT = 7 * D_MODEL + 2 * SSD_GROUPS * D_STATE
N_PROJ_TILES = N_ACT // PROJ_TN
DT_OFFSET = D_MODEL + CONV_DIM
FIRST_SHIFTED_TILE = DT_OFFSET // PROJ_TN
CONV_STEPS = (2, 3, 4)
SILU_STEPS = (0, 1, 9, 10)
IDENT_STEPS = (5, 6, 7, 8)
FIRST_SIGMOID_STEP = 11
PROJ_TM = 1024
PROJ_NSPLIT = 2
PROJ_MSPLIT = 4
NORM_TM = 512
MERGE_TM = 256


def _sigmoid(x):
    return 0.5 + 0.5 * jnp.tanh(0.5 * x)


def _silu(x):
    hx = 0.5 * x
    return hx + hx * jnp.tanh(hx)


def _softplus(x):
    return jnp.maximum(x, 0.0) + jnp.log1p(jnp.exp(-jnp.abs(x)))


def _mods_kernel(c_ref, w_ref, b_ref, o_ref):
    s = _silu(c_ref[...]).astype(BF16)
    o_ref[...] = jnp.dot(s, w_ref[...].astype(BF16), preferred_element_type=F32) + b_ref[...]


def _mods(c_pad, w_ada, b_ada):
    rows, d = c_pad.shape
    n = w_ada.shape[1]
    tn = 1024
    return pl.pallas_call(
        _mods_kernel,
        grid=(n // tn,),
        in_specs=[
            pl.BlockSpec((rows, d), lambda j: (0, 0)),
            pl.BlockSpec((d, tn), lambda j: (0, j)),
            pl.BlockSpec((1, tn), lambda j: (0, j)),
        ],
        out_specs=pl.BlockSpec((rows, tn), lambda j: (0, j)),
        out_shape=jax.ShapeDtypeStruct((rows, n), F32),
        compiler_params=pltpu.CompilerParams(
            dimension_semantics=("arbitrary",), vmem_limit_bytes=VMEM_LIMIT),
        name="mods",
    )(c_pad, w_ada, b_ada)


def _norm_kernel(x_ref, sc_ref, sh_ref, gpre_ref, wdt_ref, dtb_ref, h_ref, dt_ref, *, tm, spt):
    rps = tm // spt
    for s in range(spt):
        rows = slice(s * rps, (s + 1) * rps)
        x = x_ref[rows, :]
        y = x * lax.rsqrt(jnp.mean(x * x, axis=-1, keepdims=True) + EPS)
        y = y * gpre_ref[...]
        h_ref[rows, :] = (y * (1.0 + sc_ref[s]) + sh_ref[s]).astype(BF16)
    dt_raw = lax.dot_general(h_ref[...], wdt_ref[...], (((1,), (1,)), ((), ())),
                             preferred_element_type=F32) + dtb_ref[...]
    lane = lax.broadcasted_iota(jnp.int32, dt_raw.shape, 1)
    dt_ref[...] = jnp.where(lane < SSD_HEADS, _softplus(dt_raw), 0.0)


def _norm(x2d, scale, shift, g_pre, w_dt, dt_bias, *, tm, spt):
    rows = x2d.shape[0]
    n_seq = scale.shape[0]
    n_tiles = rows // tm
    tiles_per_seq = max(1, n_tiles // n_seq) if spt == 1 else 1
    seq_blk = (lambda i: i // tiles_per_seq) if spt == 1 else (lambda i: i)
    return pl.pallas_call(
        functools.partial(_norm_kernel, tm=tm, spt=spt),
        grid=(n_tiles,),
        in_specs=[
            pl.BlockSpec((tm, D_MODEL), lambda i: (i, 0)),
            pl.BlockSpec((spt, 1, D_MODEL), lambda i: (seq_blk(i), 0, 0)),
            pl.BlockSpec((spt, 1, D_MODEL), lambda i: (seq_blk(i), 0, 0)),
            pl.BlockSpec((1, D_MODEL), lambda i: (0, 0)),
            pl.BlockSpec((LANES, D_MODEL), lambda i: (0, 0)),
            pl.BlockSpec((1, LANES), lambda i: (0, 0)),
        ],
        out_specs=[
            pl.BlockSpec((tm, D_MODEL), lambda i: (i, 0)),
            pl.BlockSpec((tm, LANES), lambda i: (i, 0)),
        ],
        out_shape=[
            jax.ShapeDtypeStruct((rows, D_MODEL), BF16),
            jax.ShapeDtypeStruct((rows, LANES), F32),
        ],
        compiler_params=pltpu.CompilerParams(
            dimension_semantics=("arbitrary",), vmem_limit_bytes=VMEM_LIMIT),
        name="norm",
    )(x2d, scale, shift, g_pre, w_dt, dt_bias)


def _act_tile(j):
    return jnp.where(j < CONV_STEPS[2], j, jnp.where(j == CONV_STEPS[2], N_PROJ_TILES - 1, j - 1))


def _wt_row(j):
    return pl.multiple_of(j * PROJ_TN + jnp.where(j >= FIRST_SHIFTED_TILE, SSD_HEADS, 0), SSD_HEADS)


def _conv_in_tile(j):
    return jnp.clip(j - CONV_STEPS[0], 0, len(CONV_STEPS) - 1)


def _proj_tile(j, first, h_ref, act_ref, tail_ref, hist_ref, cw_ref, cb_ref, wbf_scr, halo_scr, stgs,
               *, tm, spt, nsplit, msplit):
    rps = tm // spt
    cw = PROJ_TN // nsplit
    rb = tm // msplit
    blocks = [(nb, mb) for nb in range(nsplit) for mb in range(msplit)]

    def dot_block(k):
        nb, mb = blocks[k]
        stgs[k % 2][SUBLANES:SUBLANES + rb, :] = lax.dot_general(
            h_ref[mb * rb:(mb + 1) * rb, :], wbf_scr[nb * cw:(nb + 1) * cw, :],
            (((1,), (1,)), ((), ())), preferred_element_type=F32)

    def run(epilogue):
        dot_block(0)
        for k, (nb, mb) in enumerate(blocks):
            if k + 1 < len(blocks):
                dot_block(k + 1)
            epilogue(nb, mb, stgs[k % 2])

    def elementwise(fn):
        def ep(nb, mb, stg):
            act_ref[mb * rb:(mb + 1) * rb, nb * cw:(nb + 1) * cw] = (
                fn(stg[SUBLANES:SUBLANES + rb, :]).astype(BF16))
        return ep

    is_conv = jnp.logical_and(j >= CONV_STEPS[0], j <= CONV_STEPS[2])

    @pl.when(jnp.logical_not(is_conv))
    def _():
        tail_ref[...] = jnp.zeros_like(tail_ref)

    @pl.when(functools.reduce(jnp.logical_or, [j == t for t in SILU_STEPS]))
    def _():
        run(elementwise(_silu))

    @pl.when(jnp.logical_and(j >= IDENT_STEPS[0], j <= IDENT_STEPS[-1]))
    def _():
        run(elementwise(lambda a: a))

    @pl.when(j >= FIRST_SIGMOID_STEP)
    def _():
        run(elementwise(_sigmoid))

    @pl.when(is_conv)
    def _():
        if spt == 1:
            halo_scr[...] = jnp.where(first, hist_ref[0], halo_scr[...])

        def conv(ext, cols):
            w0, w1, w2, w3 = (cw_ref[k:k + 1, cols] for k in range(CONV_W))
            t = ext * w0
            t = pltpu.roll(t, 1, axis=0) + ext * w1
            t = pltpu.roll(t, 1, axis=0) + ext * w2
            t = pltpu.roll(t, 1, axis=0) + ext * w3 + cb_ref[:, cols]
            return _silu(t[SUBLANES:, :]).astype(BF16)

        def ep(nb, mb, stg):
            cols = slice(nb * cw, (nb + 1) * cw)
            if spt == 1:
                stg[0:SUBLANES, :] = halo_scr[:, cols]
                act_ref[mb * rb:(mb + 1) * rb, cols] = conv(stg[0:SUBLANES + rb, :], cols)
                last = stg[rb:rb + SUBLANES, :]
                halo_scr[:, cols] = last
                tail_ref[0, :, cols] = last
            else:
                for s in range(rb // rps):
                    rows = slice(SUBLANES + s * rps, SUBLANES + (s + 1) * rps)
                    seq = mb * (rb // rps) + s
                    ext = jnp.concatenate([hist_ref[seq, :, cols], stg[rows, :]], axis=0)
                    act_ref[mb * rb + s * rps:mb * rb + (s + 1) * rps, cols] = conv(ext, cols)
                    tail_ref[seq, :, cols] = stg[rows.stop - SUBLANES:rows.stop, :]

        run(ep)


def _proj_kernel(hp_ref, hs_ref, wt_ref, cw_ref, cb_ref, histp_ref, hists_ref,
                 actp_ref, acts_ref, tailp_ref, tails_ref, wbf_scr, halo_scr, stg0_scr, stg1_scr,
                 *, n_p, tm_p, tm_s, spt_s, nsplit, msplit_p, tiles_per_seq):
    j = pl.program_id(0)
    i = pl.program_id(1)
    wrows = 256

    @pl.when(i == 0)
    def _():
        for r in range(PROJ_TN // wrows):
            rows = slice(r * wrows, (r + 1) * wrows)
            wbf_scr[rows, :] = wt_ref[rows, :].astype(BF16)

    shared = (cw_ref, cb_ref, wbf_scr, halo_scr, (stg0_scr, stg1_scr))

    @pl.when(i == 0)
    def _():
        _proj_tile(j, None, hs_ref, acts_ref, tails_ref, hists_ref, *shared,
                   tm=tm_s, spt=spt_s, nsplit=nsplit, msplit=1)

    @pl.when(i > 0)
    def _():
        _proj_tile(j, ((i - 1) % tiles_per_seq) == 0, hp_ref, actp_ref, tailp_ref, histp_ref, *shared,
                   tm=tm_p, spt=1, nsplit=nsplit, msplit=msplit_p)


def _proj(hp, hs, w_t, conv_w, conv_b, hist_p, hist_s, *, tm_p, nsplit, msplit_p):
    rows_p, rows_s = hp.shape[0], hs.shape[0]
    n_p = rows_p // tm_p
    tiles_per_seq = n_p // hist_p.shape[0]
    spt_s = hist_s.shape[0]
    rb = max(tm_p // msplit_p, rows_s)
    kern = functools.partial(_proj_kernel, n_p=n_p, tm_p=tm_p, tm_s=rows_s, spt_s=spt_s,
                             nsplit=nsplit, msplit_p=msplit_p, tiles_per_seq=tiles_per_seq)
    ip = lambda i: jnp.maximum(i - 1, 0)
    return pl.pallas_call(
        kern,
        grid=(N_PROJ_TILES, n_p + 1),
        in_specs=[
            pl.BlockSpec((tm_p, D_MODEL), lambda j, i: (ip(i), 0)),
            pl.BlockSpec((rows_s, D_MODEL), lambda j, i: (0, 0)),
            pl.BlockSpec((pl.Element(PROJ_TN), pl.Element(D_MODEL)), lambda j, i: (_wt_row(j), 0)),
            pl.BlockSpec((CONV_W, PROJ_TN), lambda j, i: (0, _conv_in_tile(j))),
            pl.BlockSpec((1, PROJ_TN), lambda j, i: (0, _conv_in_tile(j))),
            pl.BlockSpec((1, SUBLANES, PROJ_TN), lambda j, i: (ip(i) // tiles_per_seq, 0, _conv_in_tile(j))),
            pl.BlockSpec((spt_s, SUBLANES, PROJ_TN), lambda j, i: (0, 0, _conv_in_tile(j))),
        ],
        out_specs=[
            pl.BlockSpec((tm_p, PROJ_TN), lambda j, i: (ip(i), _act_tile(j))),
            pl.BlockSpec((rows_s, PROJ_TN), lambda j, i: (0, _act_tile(j))),
            pl.BlockSpec((1, SUBLANES, PROJ_TN), lambda j, i: (ip(i), 0, j)),
            pl.BlockSpec((spt_s, SUBLANES, PROJ_TN), lambda j, i: (0, 0, j)),
        ],
        out_shape=[
            jax.ShapeDtypeStruct((rows_p, N_ACT), BF16),
            jax.ShapeDtypeStruct((rows_s, N_ACT), BF16),
            jax.ShapeDtypeStruct((n_p, SUBLANES, N_PROJ_TILES * PROJ_TN), F32),
            jax.ShapeDtypeStruct((spt_s, SUBLANES, N_PROJ_TILES * PROJ_TN), F32),
        ],
        scratch_shapes=[
            pltpu.VMEM((PROJ_TN, D_MODEL), BF16),
            pltpu.VMEM((SUBLANES, PROJ_TN), F32),
            pltpu.VMEM((rb + SUBLANES, PROJ_TN // nsplit), F32),
            pltpu.VMEM((rb + SUBLANES, PROJ_TN // nsplit), F32),
        ],
        compiler_params=pltpu.CompilerParams(
            dimension_semantics=("arbitrary", "arbitrary"), vmem_limit_bytes=VMEM_LIMIT),
        name="proj",
    )(hp, hs, w_t, conv_w, conv_b, hist_p, hist_s)


def _pad_rows(v, rows):
    if v.shape[0] == rows:
        return v
    return jnp.concatenate([v, jnp.zeros((rows - v.shape[0], v.shape[1]), v.dtype)], axis=0)


def _ssd_kernel(szs_ref, xs_ref, b_ref, c_ref, dt_ref, alog_ref, dskip_ref, nrm_ref, e3_ref, h0_ref,
                y_ref, hout_ref, st_scr, *, q_in, blocks_per_step, n_steps):
    ci = pl.program_id(1)

    @pl.when(ci == 0)
    def _():
        st_scr[...] = h0_ref[0]

    for k in range(blocks_per_step):
        rs = pl.ds(k * q_in, q_in)
        _ssd_block(szs_ref.at[rs], xs_ref.at[rs], b_ref.at[rs], c_ref.at[rs], dt_ref.at[rs],
                   alog_ref, dskip_ref, nrm_ref, e3_ref, y_ref.at[rs], st_scr, q_in=q_in)

    @pl.when(ci == n_steps - 1)
    def _():
        hout_ref[0] = st_scr[...]


def _ssd_block(szs_ref, xs_ref, b_ref, c_ref, dt_ref, alog_ref, dskip_ref, nrm_ref, e3_ref,
               y_ref, st_scr, *, q_in):
    q = SSD_BLOCK
    dt = _pad_rows(dt_ref[...], q)
    a = -jnp.exp(alog_ref[...])
    adt = dt * a
    row = lax.broadcasted_iota(jnp.int32, (q, q), 0)
    col = lax.broadcasted_iota(jnp.int32, (q, q), 1)
    causal = row >= col
    acum = jnp.dot(causal.astype(F32), adt, precision=lax.Precision.HIGHEST,
                   preferred_element_type=F32)
    alast = acum[q - 1:q, :]
    dte = jnp.exp(alast - acum)
    ea = jnp.exp(acum)

    stack = jnp.concatenate([dt, dt * dte, ea], axis=0)
    lane3 = lax.broadcasted_iota(jnp.int32, stack.shape, 1)
    stack = jnp.where(lane3 < SSD_HEADS, stack, 0.0)
    hi = stack.astype(BF16).astype(F32)
    r1 = stack - hi
    mid = r1.astype(BF16).astype(F32)
    lo = (r1 - mid).astype(BF16).astype(F32)
    comb = hi + pltpu.roll(mid, SSD_HEADS, axis=1) + pltpu.roll(lo, 2 * SSD_HEADS, axis=1)
    expd = jnp.dot(comb.astype(BF16), e3_ref[...], preferred_element_type=F32)
    dt_e, w_e, ea_e = expd[0:q], expd[q:2 * q], expd[2 * q:3 * q]

    x = _pad_rows(xs_ref[...], q).astype(F32)
    xdt = (x * dt_e).astype(BF16)
    xdtd = (x * w_e).astype(BF16)
    bm = _pad_rows(b_ref[...], q)
    cm = _pad_rows(c_ref[...], q)

    acum_t = acum.T
    dec = jnp.exp(acum_t[0:SSD_HEADS, q - 1:q])
    dec = jnp.broadcast_to(dec, (SSD_HEADS, D_STATE))
    dec = jnp.broadcast_to(dec[:, None, :], (SSD_HEADS, SSD_HEAD_DIM, D_STATE))
    dec = dec.reshape(SSD_HEADS * SSD_HEAD_DIM, D_STATE)

    lane = lax.broadcasted_iota(jnp.int32, (q, LANES), 1)
    nt = (((1,), (1,)), ((), ()))
    tn = (((0,), (0,)), ((), ()))
    y_groups = []
    for g in range(SSD_GROUPS):
        gc = slice(g * GROUP_COLS, (g + 1) * GROUP_COLS)
        bg = bm[:, g * D_STATE:(g + 1) * D_STATE]
        cg = cm[:, g * D_STATE:(g + 1) * D_STATE]
        cb = lax.dot_general(cg, bg, nt, preferred_element_type=F32)
        sg = st_scr[gc, :]
        y_off = lax.dot_general(cg, sg.astype(BF16), nt, preferred_element_type=F32)
        pieces = []
        for pr in range(HEADS_PER_GROUP // 2):
            ms = []
            for hh in range(2):
                h = g * HEADS_PER_GROUP + 2 * pr + hh
                seg = jnp.broadcast_to(acum[:, h:h + 1], (q, q)) - acum_t[h:h + 1, :]
                lm = jnp.exp(jnp.where(causal, seg, -jnp.inf))
                ms.append((cb * lm).astype(BF16))
            m = jnp.concatenate(ms, axis=1)
            c0 = g * GROUP_COLS + pr * LANES
            xp = xdt[:, c0:c0 + LANES]
            rhs = jnp.concatenate([jnp.where(lane < SSD_HEAD_DIM, xp, jnp.zeros_like(xp)),
                                   jnp.where(lane >= SSD_HEAD_DIM, xp, jnp.zeros_like(xp))], axis=0)
            pieces.append(jnp.dot(m, rhs, preferred_element_type=F32))
        y_diag = jnp.concatenate(pieces, axis=1)
        y_groups.append(y_diag + y_off * ea_e[:, gc])
        new = lax.dot_general(xdtd[:, gc], bg, tn, preferred_element_type=F32)
        st_scr[gc, :] = sg * dec[gc, :] + new
    y = jnp.concatenate(y_groups, axis=1)
    y = y + dskip_ref[...] * x
    yz = y * _pad_rows(szs_ref[...], q).astype(F32)
    yn = yz * lax.rsqrt(jnp.mean(yz * yz, axis=-1, keepdims=True) + EPS) * nrm_ref[...]
    y_ref[...] = yn[0:q_in].astype(BF16)


def _ssd(act, dt, a_log, dskip_e, norm_ssd, e3, h0, *, n_seq, q_in, blocks_per_step):
    rows = act.shape[0]
    qb = q_in * blocks_per_step
    n_chunks = rows // (n_seq * qb)
    kern = functools.partial(_ssd_kernel, q_in=q_in, blocks_per_step=blocks_per_step, n_steps=n_chunks)
    rblk = lambda b, c: b * n_chunks + c
    bc_blk0 = 7 * D_MODEL // GROUP_COLS
    return pl.pallas_call(
        kern,
        grid=(n_seq, n_chunks),
        in_specs=[
            pl.BlockSpec((qb,D_MODEL), lambda b, c: (rblk(b, c), 0)),
            pl.BlockSpec((qb,D_MODEL), lambda b, c: (rblk(b, c), 1)),
            pl.BlockSpec((qb,GROUP_COLS), lambda b, c: (rblk(b, c), bc_blk0)),
            pl.BlockSpec((qb,GROUP_COLS), lambda b, c: (rblk(b, c), bc_blk0 + 1)),
            pl.BlockSpec((qb,LANES), lambda b, c: (rblk(b, c), 0)),
            pl.BlockSpec((1, LANES), lambda b, c: (0, 0)),
            pl.BlockSpec((1, D_MODEL), lambda b, c: (0, 0)),
            pl.BlockSpec((1, D_MODEL), lambda b, c: (0, 0)),
            pl.BlockSpec((LANES, D_MODEL), lambda b, c: (0, 0)),
            pl.BlockSpec((1, D_MODEL, D_STATE), lambda b, c: (b, 0, 0)),
        ],
        out_specs=[
            pl.BlockSpec((qb,D_MODEL), lambda b, c: (rblk(b, c), 0)),
            pl.BlockSpec((1, D_MODEL, D_STATE), lambda b, c: (b, 0, 0)),
        ],
        out_shape=[
            jax.ShapeDtypeStruct((rows, D_MODEL), BF16),
            jax.ShapeDtypeStruct((n_seq, D_MODEL, D_STATE), F32),
        ],
        scratch_shapes=[pltpu.VMEM((D_MODEL, D_STATE), F32)],
        compiler_params=pltpu.CompilerParams(
            dimension_semantics=("arbitrary", "arbitrary"), vmem_limit_bytes=VMEM_LIMIT),
        name="ssd",
    )(act, act, act, act, dt, a_log, dskip_e, norm_ssd, e3, h0)


def _mlp_stage(u_ref, v_ref, szm_ref, lng_ref, lnb_ref, wsp_ref, bsp_ref, ymlp_ref, vn_ref, *, tm, qm):
    v = v_ref[...].astype(F32)
    mu = jnp.mean(v, axis=-1, keepdims=True)
    vc = v - mu
    var = jnp.mean(vc * vc, axis=-1, keepdims=True)
    vn = vc * lax.rsqrt(var + EPS) * lng_ref[...] + lnb_ref[...]
    if vn_ref is not None:
        vn_ref[...] = vn
    vn_b = vn.astype(BF16)

    row = lax.broadcasted_iota(jnp.int32, (MLP_CHUNK, MLP_CHUNK), 0)
    col = lax.broadcasted_iota(jnp.int32, (MLP_CHUNK, MLP_CHUNK), 1)
    wms = [jnp.where(row >= col, wsp_ref[g], 0.0).astype(BF16) for g in range(MLP_GROUPS)]
    for ck in range(tm // qm):
        rows = slice(ck * qm, (ck + 1) * qm)
        vck = _pad_rows(vn_b[rows, :], MLP_CHUNK)
        mixed = jnp.concatenate(
            [jnp.dot(wms[g], vck[:, g * MLP_GROUP_DIM:(g + 1) * MLP_GROUP_DIM],
                     preferred_element_type=F32) for g in range(MLP_GROUPS)], axis=1)
        mixed = mixed[0:qm] + bsp_ref[...]
        y_mlp = u_ref[rows, :].astype(F32) * mixed * szm_ref[rows, :].astype(F32)
        ymlp_ref[rows, :] = y_mlp.astype(BF16)


def _mix_stage(yssd_ref, ymlp_ref, sgs_ref, sgm_ref, wbs_ref, wbm_ref, wo_ref):
    a = jnp.dot(yssd_ref[...], wbs_ref[...], preferred_element_type=F32)
    b = jnp.dot(ymlp_ref[...], wbm_ref[...], preferred_element_type=F32)
    merged = sgs_ref[...].astype(F32) * a + sgm_ref[...].astype(F32) * b
    return jnp.dot(merged.astype(BF16), wo_ref[...], preferred_element_type=F32)


def _out_stage(o, x_ref, gate_ref, gfin_ref, y_ref, *, tm, spt):
    rps = tm // spt
    for s in range(spt):
        rows = slice(s * rps, (s + 1) * rps)
        out = x_ref[rows, :] + gate_ref[s] * o[rows, :]
        y = out * lax.rsqrt(jnp.mean(out * out, axis=-1, keepdims=True) + EPS)
        y_ref[rows, :] = y * gfin_ref[...]


def _merge_kernel(yssd_ref, u_ref, v_ref, szm_ref, sgs_ref, sgm_ref, x_ref, gate_ref,
                  lng_ref, lnb_ref, gfin_ref, wsp_ref, bsp_ref, wbs_ref, wbm_ref, wo_ref,
                  y_ref, *rest, tm, qm, spt, emit_vn):
    if emit_vn:
        vn_ref, ymlp_scr = rest
    else:
        vn_ref, (ymlp_scr,) = None, rest
    _mlp_stage(u_ref, v_ref, szm_ref, lng_ref, lnb_ref, wsp_ref, bsp_ref, ymlp_scr, vn_ref, tm=tm, qm=qm)
    o = _mix_stage(yssd_ref, ymlp_scr, sgs_ref, sgm_ref, wbs_ref, wbm_ref, wo_ref)
    _out_stage(o, x_ref, gate_ref, gfin_ref, y_ref, tm=tm, spt=spt)


def _merge(yssd, act, x2d, gate, ln_g, ln_b, g_final, w_spatial, bsp_e, wbs, wbm, wo,
           *, tm, qm, spt, emit_vn):
    rows = x2d.shape[0]
    n_seq = gate.shape[0]
    n_tiles = rows // tm
    tiles_per_seq = max(1, n_tiles // n_seq) if spt == 1 else 1
    seq_blk = (lambda i: i // tiles_per_seq) if spt == 1 else (lambda i: i)
    kern = functools.partial(_merge_kernel, tm=tm, qm=qm, spt=spt, emit_vn=emit_vn)
    const = lambda shape: pl.BlockSpec(shape, lambda i: (0,) * len(shape),
                                       pipeline_mode=pl.Buffered(1))
    seg = lambda k: pl.BlockSpec((tm, D_MODEL), lambda i: (i, k))
    out_specs = [pl.BlockSpec((tm, D_MODEL), lambda i: (i, 0))]
    out_shape = [jax.ShapeDtypeStruct((rows, D_MODEL), F32)]
    if emit_vn:
        out_specs.append(pl.BlockSpec((tm, D_MODEL), lambda i: (i, 0)))
        out_shape.append(jax.ShapeDtypeStruct((rows, D_MODEL), F32))
    return pl.pallas_call(
        kern,
        grid=(n_tiles,),
        in_specs=[
            pl.BlockSpec((tm, D_MODEL), lambda i: (i, 0)),
            seg(2), seg(3), seg(4), seg(5), seg(6),
            pl.BlockSpec((tm, D_MODEL), lambda i: (i, 0)),
            pl.BlockSpec((spt, 1, D_MODEL), lambda i: (seq_blk(i), 0, 0)),
            const((1, D_MODEL)), const((1, D_MODEL)), const((1, D_MODEL)),
            const((MLP_GROUPS, MLP_CHUNK, MLP_CHUNK)),
            const((qm, D_MODEL)),
            const((D_MODEL, D_MODEL)), const((D_MODEL, D_MODEL)), const((D_MODEL, D_MODEL)),
        ],
        out_specs=out_specs,
        out_shape=out_shape,
        scratch_shapes=[pltpu.VMEM((tm, D_MODEL), BF16)],
        compiler_params=pltpu.CompilerParams(
            dimension_semantics=("arbitrary",), vmem_limit_bytes=VMEM_LIMIT),
        name="merge",
    )(yssd, act, act, act, act, act, x2d, gate, ln_g, ln_b, g_final, w_spatial, bsp_e, wbs, wbm, wo)


def _cast_kernel(a_ref, b_ref, c_ref, ao_ref, bo_ref, co_ref):
    ao_ref[...] = a_ref[...].astype(BF16)
    bo_ref[...] = b_ref[...].astype(BF16)
    co_ref[...] = c_ref[...].astype(BF16)


def _cast_bf16(a, b, c, *, tm=512):
    rows, cols = a.shape
    spec = pl.BlockSpec((tm, cols), lambda i: (i, 0))
    return pl.pallas_call(
        _cast_kernel,
        grid=(rows // tm,),
        in_specs=[spec] * 3,
        out_specs=[spec] * 3,
        out_shape=[jax.ShapeDtypeStruct((rows, cols), BF16)] * 3,
        compiler_params=pltpu.CompilerParams(
            dimension_semantics=("arbitrary",), vmem_limit_bytes=VMEM_LIMIT),
        name="cast",
    )(a, b, c)


def _expansion_matrix():
    e = np.zeros((LANES, D_MODEL), np.float32)
    cols = np.arange(D_MODEL)
    for piece in range(3):
        e[piece * SSD_HEADS + cols // SSD_HEAD_DIM, cols] = 1.0
    return jnp.asarray(e, BF16)


def kernel(x_prompt, x_sample, state_ssm, cache_conv, c_prompt, c_sample, w_ada, b_ada, g_pre, w_in,
           conv_w, conv_b, dt_bias, a_log, d_skip, norm_ssd, ln_g, ln_b, w_spatial, b_spatial,
           w_bp_ssd, w_bp_mlp, w_o, g_final):
    depth = w_ada.shape[0]
    assert depth == 1, "single-layer trunk only"
    bp, seq, d = x_prompt.shape
    bs, dec_seq, _ = x_sample.shape
    assert d == D_MODEL and seq % 1024 == 0 and dec_seq % 16 == 0 and dec_seq <= SSD_BLOCK

    w_t = jnp.swapaxes(w_in[0], 0, 1)
    w_dt = jnp.pad(w_t[DT_OFFSET:DT_OFFSET + SSD_HEADS], ((0, LANES - SSD_HEADS), (0, 0))).astype(BF16)
    pad_heads = lambda v: jnp.pad(v.reshape(1, SSD_HEADS), ((0, 0), (0, LANES - SSD_HEADS)))
    wts = dict(
        g_pre=g_pre[0].reshape(1, d), w_t=w_t, w_dt=w_dt, dt_bias=pad_heads(dt_bias[0]),
        conv_w=conv_w[0], conv_b=conv_b[0].reshape(1, CONV_DIM),
        a_log=pad_heads(a_log[0]), dskip_e=jnp.repeat(d_skip[0], SSD_HEAD_DIM).reshape(1, d),
        norm_ssd=norm_ssd[0].reshape(1, d), e3=_expansion_matrix(),
        ln_g=ln_g[0].reshape(1, d), ln_b=ln_b[0].reshape(1, d), g_final=g_final.reshape(1, d),
        w_spatial=w_spatial[0], bsp_e=jnp.repeat(b_spatial[0].T, MLP_GROUP_DIM, axis=1),
    )

    c_all = jnp.concatenate([c_prompt, c_sample], axis=0)
    c_pad = jnp.pad(c_all, ((0, 16 - (bp + bs)), (0, 0)))
    mods = _mods(c_pad, w_ada[0], b_ada[0].reshape(1, 3 * d))

    split_mods = lambda m: (m[:, k * d:(k + 1) * d].reshape(m.shape[0], 1, d) for k in range(3))
    shift_p, scale_p, gate_p = split_mods(mods[:bp])
    shift_s, scale_s, gate_s = split_mods(mods[bp:bp + bs])
    x2p = x_prompt.reshape(bp * seq, d)
    x2s = x_sample.reshape(bs * dec_seq, d)
    rows_s = bs * dec_seq

    hist_p = jnp.zeros((bp, SUBLANES, CONV_DIM), F32)
    h0_p = jnp.zeros((bp, d, D_STATE), F32)
    hist_s = jnp.pad(cache_conv[0], ((0, 0), (SUBLANES - (CONV_W - 1), 0), (0, 0)))
    h0_s = state_ssm[0].reshape(bs, d, D_STATE)

    norm_w = (wts["g_pre"], wts["w_dt"], wts["dt_bias"])
    hp, dt_p = _norm(x2p, scale_p, shift_p, *norm_w, tm=NORM_TM, spt=1)
    hs, dt_s = _norm(x2s, scale_s, shift_s, *norm_w, tm=rows_s, spt=bs)
    act_p, act_s, tail_p, tail_s = _proj(hp, hs, wts["w_t"], wts["conv_w"], wts["conv_b"], hist_p, hist_s,
                                         tm_p=PROJ_TM, nsplit=PROJ_NSPLIT, msplit_p=PROJ_MSPLIT)

    ssd_w = (wts["a_log"], wts["dskip_e"], wts["norm_ssd"], wts["e3"])
    yssd_p, ssm_p = _ssd(act_p, dt_p, *ssd_w, h0_p, n_seq=bp, q_in=SSD_BLOCK,
                         blocks_per_step=SSD_BLOCKS_PER_STEP)
    yssd_s, ssm_s = _ssd(act_s, dt_s, *ssd_w, h0_s, n_seq=bs, q_in=dec_seq, blocks_per_step=1)

    merge_w = (wts["ln_g"], wts["ln_b"], wts["g_final"], wts["w_spatial"])
    proj_w = _cast_bf16(w_bp_ssd[0], w_bp_mlp[0], w_o[0])
    (yp,) = _merge(yssd_p, act_p, x2p, gate_p, *merge_w, wts["bsp_e"], *proj_w,
                   tm=MERGE_TM, qm=MLP_CHUNK, spt=1, emit_vn=False)
    ys, vn_s = _merge(yssd_s, act_s, x2s, gate_s, *merge_w, wts["bsp_e"][:dec_seq], *proj_w,
                      tm=rows_s, qm=dec_seq, spt=bs, emit_vn=True)

    conv_cols = slice(CONV_STEPS[0] * PROJ_TN, CONV_STEPS[0] * PROJ_TN + CONV_DIM)
    hist_rows = slice(SUBLANES - (CONV_W - 1), SUBLANES)
    tiles_per_seq = seq // PROJ_TM
    conv_p = tail_p[tiles_per_seq - 1::tiles_per_seq, hist_rows, conv_cols]
    conv_s = tail_s[:, hist_rows, conv_cols]

    state_shape = (SSD_HEADS, SSD_HEAD_DIM, D_STATE)
    return (yp.reshape(bp, seq, d), ys.reshape(bs, dec_seq, d),
            ssm_p.reshape(1, bp, *state_shape), conv_p[None],
            ssm_s.reshape(1, bs, *state_shape), conv_s[None], vn_s.reshape(1, bs, dec_seq, d))
```

```python
import functools
from typing import Any, NamedTuple

import numpy as np
import jax
import jax.numpy as jnp
from jax import lax
from jax.experimental import pallas as pl
from jax.experimental.pallas import tpu as pltpu

F32 = jnp.float32
BF16 = jnp.bfloat16

D_MODEL = 2048
SSD_HEADS = 32
SSD_HEAD_DIM = 64
SSD_GROUPS = 4
HEADS_PER_GROUP = SSD_HEADS // SSD_GROUPS
D_STATE = 128
GROUP_COLS = HEADS_PER_GROUP * SSD_HEAD_DIM
CONV_W = 4
CONV_DIM = D_MODEL + 2 * SSD_GROUPS * D_STATE
MLP_CHUNK = 128
MLP_GROUPS = 8
MLP_GROUP_DIM = D_MODEL // MLP_GROUPS
EPS = 1e-5

SUBLANES = 8
LANES = 128
SSD_BLOCK = 128
SSD_BLOCKS_PER_STEP = 4
PROJ_TN = 1024
VMEM_LIMIT = 56 * 1024 * 1024

N_ACT = 7 * D_MODEL + 2 * SSD_GROUPS * D_STATE
N_PROJ_TILES = N_ACT // PROJ_TN
DT_OFFSET = D_MODEL + CONV_DIM
FIRST_SHIFTED_TILE = DT_OFFSET // PROJ_TN
CONV_STEPS = (2, 3, 4)
SILU_STEPS = (0, 1, 9, 10)
IDENT_STEPS = (5, 6, 7, 8)
FIRST_SIGMOID_STEP = 11
PROJ_TM = 1024
PROJ_NSPLIT = 2
PROJ_MSPLIT = 4
NORM_TM = 512
MERGE_TM = 256


def _sigmoid(x):
    return 0.5 + 0.5 * jnp.tanh(0.5 * x)


def _silu(x):
    hx = 0.5 * x
    return hx + hx * jnp.tanh(hx)


def _softplus(x):
    return jnp.maximum(x, 0.0) + jnp.log1p(jnp.exp(-jnp.abs(x)))


def _mods_kernel(c_ref, w_ref, b_ref, o_ref):
    s = _silu(c_ref[...]).astype(BF16)
    o_ref[...] = jnp.dot(s, w_ref[...].astype(BF16), preferred_element_type=F32) + b_ref[...]


def _mods(c_pad, w_ada, b_ada):
    rows, d = c_pad.shape
    n = w_ada.shape[1]
    tn = 1024
    return pl.pallas_call(
        _mods_kernel,
        grid=(n // tn,),
        in_specs=[
            pl.BlockSpec((rows, d), lambda j: (0, 0)),
            pl.BlockSpec((d, tn), lambda j: (0, j)),
            pl.BlockSpec((1, tn), lambda j: (0, j)),
        ],
        out_specs=pl.BlockSpec((rows, tn), lambda j: (0, j)),
        out_shape=jax.ShapeDtypeStruct((rows, n), F32),
        compiler_params=pltpu.CompilerParams(
            dimension_semantics=("arbitrary",), vmem_limit_bytes=VMEM_LIMIT),
        name="mods",
    )(c_pad, w_ada, b_ada)


def _norm_kernel(x_ref, sc_ref, sh_ref, gpre_ref, wdt_ref, dtb_ref, h_ref, dt_ref, *, tm, spt):
    rps = tm // spt
    for s in range(spt):
        rows = slice(s * rps, (s + 1) * rps)
        x = x_ref[rows, :]
        y = x * lax.rsqrt(jnp.mean(x * x, axis=-1, keepdims=True) + EPS)
        y = y * gpre_ref[...]
        h_ref[rows, :] = (y * (1.0 + sc_ref[s]) + sh_ref[s]).astype(BF16)
    dt_raw = lax.dot_general(h_ref[...], wdt_ref[...], (((1,), (1,)), ((), ())),
                             preferred_element_type=F32) + dtb_ref[...]
    lane = lax.broadcasted_iota(jnp.int32, dt_raw.shape, 1)
    dt_ref[...] = jnp.where(lane < SSD_HEADS, _softplus(dt_raw), 0.0)


def _norm(x2d, scale, shift, g_pre, w_dt, dt_bias, *, tm, spt):
    rows = x2d.shape[0]
    n_seq = scale.shape[0]
    n_tiles = rows // tm
    tiles_per_seq = max(1, n_tiles // n_seq) if spt == 1 else 1
    seq_blk = (lambda i: i // tiles_per_seq) if spt == 1 else (lambda i: i)
    return pl.pallas_call(
        functools.partial(_norm_kernel, tm=tm, spt=spt),
        grid=(n_tiles,),
        in_specs=[
            pl.BlockSpec((tm, D_MODEL), lambda i: (i, 0)),
            pl.BlockSpec((spt, 1, D_MODEL), lambda i: (seq_blk(i), 0, 0)),
            pl.BlockSpec((spt, 1, D_MODEL), lambda i: (seq_blk(i), 0, 0)),
            pl.BlockSpec((1, D_MODEL), lambda i: (0, 0)),
            pl.BlockSpec((LANES, D_MODEL), lambda i: (0, 0)),
            pl.BlockSpec((1, LANES), lambda i: (0, 0)),
        ],
        out_specs=[
            pl.BlockSpec((tm, D_MODEL), lambda i: (i, 0)),
            pl.BlockSpec((tm, LANES), lambda i: (i, 0)),
        ],
        out_shape=[
            jax.ShapeDtypeStruct((rows, D_MODEL), BF16),
            jax.ShapeDtypeStruct((rows, LANES), F32),
        ],
        compiler_params=pltpu.CompilerParams(
            dimension_semantics=("arbitrary",), vmem_limit_bytes=VMEM_LIMIT),
        name="norm",
    )(x2d, scale, shift, g_pre, w_dt, dt_bias)


def _act_tile(j):
    return jnp.where(j < CONV_STEPS[2], j, jnp.where(j == CONV_STEPS[2], N_PROJ_TILES - 1, j - 1))


def _wt_row(j):
    return pl.multiple_of(j * PROJ_TN + jnp.where(j >= FIRST_SHIFTED_TILE, SSD_HEADS, 0), SSD_HEADS)


def _conv_in_tile(j):
    return jnp.clip(j - CONV_STEPS[0], 0, len(CONV_STEPS) - 1)


class _Rows(NamedTuple):
    h_ref: Any
    act_ref: Any
    tail_ref: Any
    hist_ref: Any
    tm: int
    spt: int
    msplit: int
    first: Any


def _proj_step(j, row_sets, cw_ref, cb_ref, wbf_scr, halo_scr, stgs, *, nsplit):
    cw = PROJ_TN // nsplit
    blocks = [(rs, nb, mb) for nb in range(nsplit) for rs in row_sets for mb in range(rs.msplit)]

    def dot_block(k):
        rs, nb, mb = blocks[k]
        rb = rs.tm // rs.msplit
        stgs[k % 2][SUBLANES:SUBLANES + rb, :] = lax.dot_general(
            rs.h_ref[mb * rb:(mb + 1) * rb, :], wbf_scr[nb * cw:(nb + 1) * cw, :],
            (((1,), (1,)), ((), ())), preferred_element_type=F32)

    def run(epilogue):
        dot_block(0)
        for k, (rs, nb, mb) in enumerate(blocks):
            if k + 1 < len(blocks):
                dot_block(k + 1)
            epilogue(rs, nb, mb, stgs[k % 2])

    def elementwise(fn):
        def ep(rs, nb, mb, stg):
            rb = rs.tm // rs.msplit
            rs.act_ref[mb * rb:(mb + 1) * rb, nb * cw:(nb + 1) * cw] = (
                fn(stg[SUBLANES:SUBLANES + rb, :]).astype(BF16))
        return ep

    is_conv = jnp.logical_and(j >= CONV_STEPS[0], j <= CONV_STEPS[2])

    @pl.when(jnp.logical_not(is_conv))
    def _():
        for rs in row_sets:
            rs.tail_ref[...] = jnp.zeros_like(rs.tail_ref)

    @pl.when(functools.reduce(jnp.logical_or, [j == t for t in SILU_STEPS]))
    def _():
        run(elementwise(_silu))

    @pl.when(jnp.logical_and(j >= IDENT_STEPS[0], j <= IDENT_STEPS[-1]))
    def _():
        run(elementwise(lambda a: a))

    @pl.when(j >= FIRST_SIGMOID_STEP)
    def _():
        run(elementwise(_sigmoid))

    @pl.when(is_conv)
    def _():
        for rs in row_sets:
            if rs.spt == 1:
                halo_scr[...] = jnp.where(rs.first, rs.hist_ref[0], halo_scr[...])

        def conv(ext, cols):
            w0, w1, w2, w3 = (0.5 * cw_ref[k:k + 1, cols] for k in range(CONV_W))
            t = ext * w0
            t = pltpu.roll(t, 1, axis=0) + ext * w1
            t = pltpu.roll(t, 1, axis=0) + ext * w2
            t = pltpu.roll(t, 1, axis=0) + ext * w3 + 0.5 * cb_ref[:, cols]
            t = t[SUBLANES:, :]
            return (t + t * jnp.tanh(t)).astype(BF16)

        def ep(rs, nb, mb, stg):
            cols = slice(nb * cw, (nb + 1) * cw)
            rb = rs.tm // rs.msplit
            if rs.spt == 1:
                stg[0:SUBLANES, :] = halo_scr[:, cols]
                rs.act_ref[mb * rb:(mb + 1) * rb, cols] = conv(stg[0:SUBLANES + rb, :], cols)
                last = stg[rb:rb + SUBLANES, :]
                halo_scr[:, cols] = last
                rs.tail_ref[0, :, cols] = last
            else:
                rps = rs.tm // rs.spt
                for s in range(rb // rps):
                    rows = slice(SUBLANES + s * rps, SUBLANES + (s + 1) * rps)
                    seq = mb * (rb // rps) + s
                    ext = jnp.concatenate([rs.hist_ref[seq, :, cols], stg[rows, :]], axis=0)
                    rs.act_ref[mb * rb + s * rps:mb * rb + (s + 1) * rps, cols] = conv(ext, cols)
                    rs.tail_ref[seq, :, cols] = stg[rows.stop - SUBLANES:rows.stop, :]

        run(ep)


def _proj_kernel(hp_ref, hs_ref, wt_ref, cw_ref, cb_ref, histp_ref, hists_ref,
                 actp_ref, acts_ref, tailp_ref, tails_ref, wbf_scr, halo_scr, stg0_scr, stg1_scr,
                 *, n_p, tm_p, tm_s, spt_s, nsplit, msplit_p, tiles_per_seq):
    j = pl.program_id(0)
    i = pl.program_id(1)
    wrows = 256

    @pl.when(i == 0)
    def _():
        for r in range(PROJ_TN // wrows):
            rows = slice(r * wrows, (r + 1) * wrows)
            wbf_scr[rows, :] = wt_ref[rows, :].astype(BF16)

    shared = (cw_ref, cb_ref, wbf_scr, halo_scr, (stg0_scr, stg1_scr))
    prompt = _Rows(hp_ref, actp_ref, tailp_ref, histp_ref, tm=tm_p, spt=1, msplit=msplit_p,
                   first=(i % tiles_per_seq) == 0)
    sample = _Rows(hs_ref, acts_ref, tails_ref, hists_ref, tm=tm_s, spt=spt_s, msplit=1, first=None)

    @pl.when(i < n_p - 1)
    def _():
        _proj_step(j, [prompt], *shared, nsplit=nsplit)

    @pl.when(i == n_p - 1)
    def _():
        _proj_step(j, [prompt, sample], *shared, nsplit=nsplit)


def _proj(hp, hs, w_t, conv_w, conv_b, hist_p, hist_s, *, tm_p, nsplit, msplit_p):
    rows_p, rows_s = hp.shape[0], hs.shape[0]
    n_p = rows_p // tm_p
    tiles_per_seq = n_p // hist_p.shape[0]
    spt_s = hist_s.shape[0]
    rb = max(tm_p // msplit_p, rows_s)
    kern = functools.partial(_proj_kernel, n_p=n_p, tm_p=tm_p, tm_s=rows_s, spt_s=spt_s,
                             nsplit=nsplit, msplit_p=msplit_p, tiles_per_seq=tiles_per_seq)
    ip = lambda i: i
    return pl.pallas_call(
        kern,
        grid=(N_PROJ_TILES, n_p),
        in_specs=[
            pl.BlockSpec((tm_p, D_MODEL), lambda j, i: (ip(i), 0)),
            pl.BlockSpec((rows_s, D_MODEL), lambda j, i: (0, 0)),
            pl.BlockSpec((pl.Element(PROJ_TN), pl.Element(D_MODEL)), lambda j, i: (_wt_row(j), 0)),
            pl.BlockSpec((CONV_W, PROJ_TN), lambda j, i: (0, _conv_in_tile(j))),
            pl.BlockSpec((1, PROJ_TN), lambda j, i: (0, _conv_in_tile(j))),
            pl.BlockSpec((1, SUBLANES, PROJ_TN), lambda j, i: (ip(i) // tiles_per_seq, 0, _conv_in_tile(j))),
            pl.BlockSpec((spt_s, SUBLANES, PROJ_TN), lambda j, i: (0, 0, _conv_in_tile(j))),
        ],
        out_specs=[
            pl.BlockSpec((tm_p, PROJ_TN), lambda j, i: (ip(i), _act_tile(j))),
            pl.BlockSpec((rows_s, PROJ_TN), lambda j, i: (0, _act_tile(j))),
            pl.BlockSpec((1, SUBLANES, PROJ_TN), lambda j, i: (ip(i), 0, j)),
            pl.BlockSpec((spt_s, SUBLANES, PROJ_TN), lambda j, i: (0, 0, j)),
        ],
        out_shape=[
            jax.ShapeDtypeStruct((rows_p, N_ACT), BF16),
            jax.ShapeDtypeStruct((rows_s, N_ACT), BF16),
            jax.ShapeDtypeStruct((n_p, SUBLANES, N_PROJ_TILES * PROJ_TN), F32),
            jax.ShapeDtypeStruct((spt_s, SUBLANES, N_PROJ_TILES * PROJ_TN), F32),
        ],
        scratch_shapes=[
            pltpu.VMEM((PROJ_TN, D_MODEL), BF16),
            pltpu.VMEM((SUBLANES, PROJ_TN), F32),
            pltpu.VMEM((rb + SUBLANES, PROJ_TN // nsplit), F32),
            pltpu.VMEM((rb + SUBLANES, PROJ_TN // nsplit), F32),
        ],
        compiler_params=pltpu.CompilerParams(
            dimension_semantics=("arbitrary", "arbitrary"), vmem_limit_bytes=VMEM_LIMIT),
        name="proj",
    )(hp, hs, w_t, conv_w, conv_b, hist_p, hist_s)


def _pad_rows(v, rows):
    if v.shape[0] == rows:
        return v
    return jnp.concatenate([v, jnp.zeros((rows - v.shape[0], v.shape[1]), v.dtype)], axis=0)


def _ssd_kernel(szs_ref, xs_ref, b_ref, c_ref, dt_ref, alog_ref, dskip_ref, nrm_ref, e3_ref, h0_ref,
                y_ref, hout_ref, st_scr, *, q_in, blocks_per_step, n_steps):
    ci = pl.program_id(1)

    @pl.when(ci == 0)
    def _():
        st_scr[...] = h0_ref[0]

    for k in range(blocks_per_step):
        rs = pl.ds(k * q_in, q_in)
        _ssd_block(szs_ref.at[rs], xs_ref.at[rs], b_ref.at[rs], c_ref.at[rs], dt_ref.at[rs],
                   alog_ref, dskip_ref, nrm_ref, e3_ref, y_ref.at[rs], st_scr, q_in=q_in)

    @pl.when(ci == n_steps - 1)
    def _():
        hout_ref[0] = st_scr[...]


def _ssd_block(szs_ref, xs_ref, b_ref, c_ref, dt_ref, alog_ref, dskip_ref, nrm_ref, e3_ref,
               y_ref, st_scr, *, q_in):
    q = SSD_BLOCK
    dt = _pad_rows(dt_ref[...], q)
    a = -jnp.exp(alog_ref[...])
    adt = dt * a
    row = lax.broadcasted_iota(jnp.int32, (q, q), 0)
    col = lax.broadcasted_iota(jnp.int32, (q, q), 1)
    causal = row >= col
    acum = jnp.dot(causal.astype(F32), adt, precision=lax.Precision.HIGHEST,
                   preferred_element_type=F32)
    alast = acum[q - 1:q, :]
    dte = jnp.exp(alast - acum)
    ea = jnp.exp(acum)

    stack = jnp.concatenate([dt, dt * dte, ea], axis=0)
    lane3 = lax.broadcasted_iota(jnp.int32, stack.shape, 1)
    stack = jnp.where(lane3 < SSD_HEADS, stack, 0.0)
    hi = stack.astype(BF16).astype(F32)
    r1 = stack - hi
    mid = r1.astype(BF16).astype(F32)
    lo = (r1 - mid).astype(BF16).astype(F32)
    comb = hi + pltpu.roll(mid, SSD_HEADS, axis=1) + pltpu.roll(lo, 2 * SSD_HEADS, axis=1)
    expd = jnp.dot(comb.astype(BF16), e3_ref[...], preferred_element_type=F32)
    dt_e, w_e, ea_e = expd[0:q], expd[q:2 * q], expd[2 * q:3 * q]

    x = _pad_rows(xs_ref[...], q).astype(F32)
    xdt = (x * dt_e).astype(BF16)
    xdtd = (x * w_e).astype(BF16)
    bm = _pad_rows(b_ref[...], q)
    cm = _pad_rows(c_ref[...], q)

    acum_t = acum.T
    dec = jnp.exp(acum_t[0:SSD_HEADS, q - 1:q])
    dec = jnp.broadcast_to(dec, (SSD_HEADS, D_STATE))
    dec = jnp.broadcast_to(dec[:, None, :], (SSD_HEADS, SSD_HEAD_DIM, D_STATE))
    dec = dec.reshape(SSD_HEADS * SSD_HEAD_DIM, D_STATE)

    lane = lax.broadcasted_iota(jnp.int32, (q, LANES), 1)
    nt = (((1,), (1,)), ((), ()))
    tn = (((0,), (0,)), ((), ()))
    y_groups = []
    for g in range(SSD_GROUPS):
        gc = slice(g * GROUP_COLS, (g + 1) * GROUP_COLS)
        bg = bm[:, g * D_STATE:(g + 1) * D_STATE]
        cg = cm[:, g * D_STATE:(g + 1) * D_STATE]
        cb = lax.dot_general(cg, bg, nt, preferred_element_type=F32)
        sg = st_scr[gc, :]
        y_off = lax.dot_general(cg, sg.astype(BF16), nt, preferred_element_type=F32)
        pieces = []
        for pr in range(HEADS_PER_GROUP // 2):
            ms = []
            for hh in range(2):
                h = g * HEADS_PER_GROUP + 2 * pr + hh
                seg = jnp.broadcast_to(acum[:, h:h + 1], (q, q)) - acum_t[h:h + 1, :]
                lm = jnp.exp(jnp.where(causal, seg, -jnp.inf))
                ms.append((cb * lm).astype(BF16))
            m = jnp.concatenate(ms, axis=1)
            c0 = g * GROUP_COLS + pr * LANES
            xp = xdt[:, c0:c0 + LANES]
            rhs = jnp.concatenate([jnp.where(lane < SSD_HEAD_DIM, xp, jnp.zeros_like(xp)),
                                   jnp.where(lane >= SSD_HEAD_DIM, xp, jnp.zeros_like(xp))], axis=0)
            pieces.append(jnp.dot(m, rhs, preferred_element_type=F32))
        y_diag = jnp.concatenate(pieces, axis=1)
        y_groups.append(y_diag + y_off * ea_e[:, gc])
        new = lax.dot_general(xdtd[:, gc], bg, tn, preferred_element_type=F32)
        st_scr[gc, :] = sg * dec[gc, :] + new
    y = jnp.concatenate(y_groups, axis=1)
    y = y + dskip_ref[...] * x
    yz = y * _pad_rows(szs_ref[...], q).astype(F32)
    yn = yz * lax.rsqrt(jnp.mean(yz * yz, axis=-1, keepdims=True) + EPS) * nrm_ref[...]
    y_ref[...] = yn[0:q_in].astype(BF16)


def _ssd(act, dt, a_log, dskip_e, norm_ssd, e3, h0, *, n_seq, q_in, blocks_per_step):
    rows = act.shape[0]
    qb = q_in * blocks_per_step
    n_chunks = rows // (n_seq * qb)
    kern = functools.partial(_ssd_kernel, q_in=q_in, blocks_per_step=blocks_per_step, n_steps=n_chunks)
    rblk = lambda b, c: b * n_chunks + c
    bc_blk0 = 7 * D_MODEL // GROUP_COLS
    return pl.pallas_call(
        kern,
        grid=(n_seq, n_chunks),
        in_specs=[
            pl.BlockSpec((qb,D_MODEL), lambda b, c: (rblk(b, c), 0)),
            pl.BlockSpec((qb,D_MODEL), lambda b, c: (rblk(b, c), 1)),
            pl.BlockSpec((qb,GROUP_COLS), lambda b, c: (rblk(b, c), bc_blk0)),
            pl.BlockSpec((qb,GROUP_COLS), lambda b, c: (rblk(b, c), bc_blk0 + 1)),
            pl.BlockSpec((qb,LANES), lambda b, c: (rblk(b, c), 0)),
            pl.BlockSpec((1, LANES), lambda b, c: (0, 0)),
            pl.BlockSpec((1, D_MODEL), lambda b, c: (0, 0)),
            pl.BlockSpec((1, D_MODEL), lambda b, c: (0, 0)),
            pl.BlockSpec((LANES, D_MODEL), lambda b, c: (0, 0)),
            pl.BlockSpec((1, D_MODEL, D_STATE), lambda b, c: (b, 0, 0)),
        ],
        out_specs=[
            pl.BlockSpec((qb,D_MODEL), lambda b, c: (rblk(b, c), 0)),
            pl.BlockSpec((1, D_MODEL, D_STATE), lambda b, c: (b, 0, 0)),
        ],
        out_shape=[
            jax.ShapeDtypeStruct((rows, D_MODEL), BF16),
            jax.ShapeDtypeStruct((n_seq, D_MODEL, D_STATE), F32),
        ],
        scratch_shapes=[pltpu.VMEM((D_MODEL, D_STATE), F32)],
        compiler_params=pltpu.CompilerParams(
            dimension_semantics=("arbitrary", "arbitrary"), vmem_limit_bytes=VMEM_LIMIT),
        name="ssd",
    )(act, act, act, act, dt, a_log, dskip_e, norm_ssd, e3, h0)


def _mlp_stage(u_ref, v_ref, szm_ref, lng_ref, lnb_ref, wsp_ref, bsp_ref, ymlp_ref, vn_ref, *, tm, qm):
    v = v_ref[...].astype(F32)
    mu = jnp.mean(v, axis=-1, keepdims=True)
    vc = v - mu
    var = jnp.mean(vc * vc, axis=-1, keepdims=True)
    vn = vc * lax.rsqrt(var + EPS) * lng_ref[...] + lnb_ref[...]
    if vn_ref is not None:
        vn_ref[...] = vn
    vn_b = vn.astype(BF16)

    row = lax.broadcasted_iota(jnp.int32, (MLP_CHUNK, MLP_CHUNK), 0)
    col = lax.broadcasted_iota(jnp.int32, (MLP_CHUNK, MLP_CHUNK), 1)
    wms = [jnp.where(row >= col, wsp_ref[g], 0.0).astype(BF16) for g in range(MLP_GROUPS)]
    for ck in range(tm // qm):
        rows = slice(ck * qm, (ck + 1) * qm)
        vck = _pad_rows(vn_b[rows, :], MLP_CHUNK)
        mixed = jnp.concatenate(
            [jnp.dot(wms[g], vck[:, g * MLP_GROUP_DIM:(g + 1) * MLP_GROUP_DIM],
                     preferred_element_type=F32) for g in range(MLP_GROUPS)], axis=1)
        mixed = mixed[0:qm] + bsp_ref[...]
        y_mlp = u_ref[rows, :].astype(F32) * mixed * szm_ref[rows, :].astype(F32)
        ymlp_ref[rows, :] = y_mlp.astype(BF16)


def _mix_stage(yssd_ref, ymlp_ref, sgs_ref, sgm_ref, wbs_ref, wbm_ref, wo_ref):
    a = jnp.dot(yssd_ref[...], wbs_ref[...], preferred_element_type=F32)
    b = jnp.dot(ymlp_ref[...], wbm_ref[...], preferred_element_type=F32)
    merged = sgs_ref[...].astype(F32) * a + sgm_ref[...].astype(F32) * b
    return jnp.dot(merged.astype(BF16), wo_ref[...], preferred_element_type=F32)


def _out_stage(o, x_ref, gate_ref, gfin_ref, y_ref, *, tm, spt):
    rps = tm // spt
    for s in range(spt):
        rows = slice(s * rps, (s + 1) * rps)
        out = x_ref[rows, :] + gate_ref[s] * o[rows, :]
        y = out * lax.rsqrt(jnp.mean(out * out, axis=-1, keepdims=True) + EPS)
        y_ref[rows, :] = y * gfin_ref[...]


def _merge_kernel(yssd_ref, u_ref, v_ref, szm_ref, sgs_ref, sgm_ref, x_ref, gate_ref,
                  lng_ref, lnb_ref, gfin_ref, wsp_ref, bsp_ref, wbs_ref, wbm_ref, wo_ref,
                  y_ref, *rest, tm, qm, spt, emit_vn):
    if emit_vn:
        vn_ref, ymlp_scr = rest
    else:
        vn_ref, (ymlp_scr,) = None, rest
    _mlp_stage(u_ref, v_ref, szm_ref, lng_ref, lnb_ref, wsp_ref, bsp_ref, ymlp_scr, vn_ref, tm=tm, qm=qm)
    o = _mix_stage(yssd_ref, ymlp_scr, sgs_ref, sgm_ref, wbs_ref, wbm_ref, wo_ref)
    _out_stage(o, x_ref, gate_ref, gfin_ref, y_ref, tm=tm, spt=spt)


def _merge(yssd, act, x2d, gate, ln_g, ln_b, g_final, w_spatial, bsp_e, wbs, wbm, wo,
           *, tm, qm, spt, emit_vn):
    rows = x2d.shape[0]
    n_seq = gate.shape[0]
    n_tiles = rows // tm
    tiles_per_seq = max(1, n_tiles // n_seq) if spt == 1 else 1
    seq_blk = (lambda i: i // tiles_per_seq) if spt == 1 else (lambda i: i)
    kern = functools.partial(_merge_kernel, tm=tm, qm=qm, spt=spt, emit_vn=emit_vn)
    const = lambda shape: pl.BlockSpec(shape, lambda i: (0,) * len(shape),
                                       pipeline_mode=pl.Buffered(1))
    seg = lambda k: pl.BlockSpec((tm, D_MODEL), lambda i: (i, k))
    out_specs = [pl.BlockSpec((tm, D_MODEL), lambda i: (i, 0))]
    out_shape = [jax.ShapeDtypeStruct((rows, D_MODEL), F32)]
    if emit_vn:
        out_specs.append(pl.BlockSpec((tm, D_MODEL), lambda i: (i, 0)))
        out_shape.append(jax.ShapeDtypeStruct((rows, D_MODEL), F32))
    return pl.pallas_call(
        kern,
        grid=(n_tiles,),
        in_specs=[
            pl.BlockSpec((tm, D_MODEL), lambda i: (i, 0)),
            seg(2), seg(3), seg(4), seg(5), seg(6),
            pl.BlockSpec((tm, D_MODEL), lambda i: (i, 0)),
            pl.BlockSpec((spt, 1, D_MODEL), lambda i: (seq_blk(i), 0, 0)),
            const((1, D_MODEL)), const((1, D_MODEL)), const((1, D_MODEL)),
            const((MLP_GROUPS, MLP_CHUNK, MLP_CHUNK)),
            const((qm, D_MODEL)),
            const((D_MODEL, D_MODEL)), const((D_MODEL, D_MODEL)), const((D_MODEL, D_MODEL)),
        ],
        out_specs=out_specs,
        out_shape=out_shape,
        scratch_shapes=[pltpu.VMEM((tm, D_MODEL), BF16)],
        compiler_params=pltpu.CompilerParams(
            dimension_semantics=("arbitrary",), vmem_limit_bytes=VMEM_LIMIT),
        name="merge",
    )(yssd, act, act, act, act, act, x2d, gate, ln_g, ln_b, g_final, w_spatial, bsp_e, wbs, wbm, wo)


def _cast_kernel(a_ref, b_ref, c_ref, ao_ref, bo_ref, co_ref):
    ao_ref[...] = a_ref[...].astype(BF16)
    bo_ref[...] = b_ref[...].astype(BF16)
    co_ref[...] = c_ref[...].astype(BF16)


def _cast_bf16(a, b, c, *, tm=512):
    rows, cols = a.shape
    spec = pl.BlockSpec((tm, cols), lambda i: (i, 0))
    return pl.pallas_call(
        _cast_kernel,
        grid=(rows // tm,),
        in_specs=[spec] * 3,
        out_specs=[spec] * 3,
        out_shape=[jax.ShapeDtypeStruct((rows, cols), BF16)] * 3,
        compiler_params=pltpu.CompilerParams(
            dimension_semantics=("arbitrary",), vmem_limit_bytes=VMEM_LIMIT),
        name="cast",
    )(a, b, c)


def _expansion_matrix():
    e = np.zeros((LANES, D_MODEL), np.float32)
    cols = np.arange(D_MODEL)
    for piece in range(3):
        e[piece * SSD_HEADS + cols // SSD_HEAD_DIM, cols] = 1.0
    return jnp.asarray(e, BF16)


def kernel(x_prompt, x_sample, state_ssm, cache_conv, c_prompt, c_sample, w_ada, b_ada, g_pre, w_in,
           conv_w, conv_b, dt_bias, a_log, d_skip, norm_ssd, ln_g, ln_b, w_spatial, b_spatial,
           w_bp_ssd, w_bp_mlp, w_o, g_final):
    depth = w_ada.shape[0]
    assert depth == 1, "single-layer trunk only"
    bp, seq, d = x_prompt.shape
    bs, dec_seq, _ = x_sample.shape
    assert d == D_MODEL and seq % 1024 == 0 and dec_seq % 16 == 0 and dec_seq <= SSD_BLOCK

    w_t = jnp.swapaxes(w_in[0], 0, 1)
    w_dt = jnp.pad(w_t[DT_OFFSET:DT_OFFSET + SSD_HEADS], ((0, LANES - SSD_HEADS), (0, 0))).astype(BF16)
    pad_heads = lambda v: jnp.pad(v.reshape(1, SSD_HEADS), ((0, 0), (0, LANES - SSD_HEADS)))
    wts = dict(
        g_pre=g_pre[0].reshape(1, d), w_t=w_t, w_dt=w_dt, dt_bias=pad_heads(dt_bias[0]),
        conv_w=conv_w[0], conv_b=conv_b[0].reshape(1, CONV_DIM),
        a_log=pad_heads(a_log[0]), dskip_e=jnp.repeat(d_skip[0], SSD_HEAD_DIM).reshape(1, d),
        norm_ssd=norm_ssd[0].reshape(1, d), e3=_expansion_matrix(),
        ln_g=ln_g[0].reshape(1, d), ln_b=ln_b[0].reshape(1, d), g_final=g_final.reshape(1, d),
        w_spatial=w_spatial[0], bsp_e=jnp.repeat(b_spatial[0].T, MLP_GROUP_DIM, axis=1),
    )

    c_all = jnp.concatenate([c_prompt, c_sample], axis=0)
    c_pad = jnp.pad(c_all, ((0, 16 - (bp + bs)), (0, 0)))
    mods = _mods(c_pad, w_ada[0], b_ada[0].reshape(1, 3 * d))

    split_mods = lambda m: (m[:, k * d:(k + 1) * d].reshape(m.shape[0], 1, d) for k in range(3))
    shift_p, scale_p, gate_p = split_mods(mods[:bp])
    shift_s, scale_s, gate_s = split_mods(mods[bp:bp + bs])
    x2p = x_prompt.reshape(bp * seq, d)
    x2s = x_sample.reshape(bs * dec_seq, d)
    rows_s = bs * dec_seq

    hist_p = jnp.zeros((bp, SUBLANES, CONV_DIM), F32)
    h0_p = jnp.zeros((bp, d, D_STATE), F32)
    hist_s = jnp.pad(cache_conv[0], ((0, 0), (SUBLANES - (CONV_W - 1), 0), (0, 0)))
    h0_s = state_ssm[0].reshape(bs, d, D_STATE)

    norm_w = (wts["g_pre"], wts["w_dt"], wts["dt_bias"])
    hp, dt_p = _norm(x2p, scale_p, shift_p, *norm_w, tm=NORM_TM, spt=1)
    hs, dt_s = _norm(x2s, scale_s, shift_s, *norm_w, tm=rows_s, spt=bs)
    act_p, act_s, tail_p, tail_s = _proj(hp, hs, wts["w_t"], wts["conv_w"], wts["conv_b"], hist_p, hist_s,
                                         tm_p=PROJ_TM, nsplit=PROJ_NSPLIT, msplit_p=PROJ_MSPLIT)

    ssd_w = (wts["a_log"], wts["dskip_e"], wts["norm_ssd"], wts["e3"])
    yssd_p, ssm_p = _ssd(act_p, dt_p, *ssd_w, h0_p, n_seq=bp, q_in=SSD_BLOCK,
                         blocks_per_step=SSD_BLOCKS_PER_STEP)
    yssd_s, ssm_s = _ssd(act_s, dt_s, *ssd_w, h0_s, n_seq=bs, q_in=dec_seq, blocks_per_step=1)

    merge_w = (wts["ln_g"], wts["ln_b"], wts["g_final"], wts["w_spatial"])
    proj_w = _cast_bf16(w_bp_ssd[0], w_bp_mlp[0], w_o[0])
    (yp,) = _merge(yssd_p, act_p, x2p, gate_p, *merge_w, wts["bsp_e"], *proj_w,
                   tm=MERGE_TM, qm=MLP_CHUNK, spt=1, emit_vn=False)
    ys, vn_s = _merge(yssd_s, act_s, x2s, gate_s, *merge_w, wts["bsp_e"][:dec_seq], *proj_w,
                      tm=rows_s, qm=dec_seq, spt=bs, emit_vn=True)

    conv_cols = slice(CONV_STEPS[0] * PROJ_TN, CONV_STEPS[0] * PROJ_TN + CONV_DIM)
    hist_rows = slice(SUBLANES - (CONV_W - 1), SUBLANES)
    tiles_per_seq = seq // PROJ_TM
    conv_p = tail_p[tiles_per_seq - 1::tiles_per_seq, hist_rows, conv_cols]
    conv_s = tail_s[:, hist_rows, conv_cols]

    state_shape = (SSD_HEADS, SSD_HEAD_DIM, D_STATE)
    return (yp.reshape(bp, seq, d), ys.reshape(bs, dec_seq, d),
            ssm_p.reshape(1, bp, *state_shape), conv_p[None],
            ssm_s.reshape(1, bs, *state_shape), conv_s[None], vn_s.reshape(1, bs, dec_seq, d))
```

```python
import functools
from typing import Any, NamedTuple

import numpy as np
import jax
import jax.numpy as jnp
from jax import lax
from jax.experimental import pallas as pl
from jax.experimental.pallas import tpu as pltpu

F32 = jnp.float32
BF16 = jnp.bfloat16

D_MODEL = 2048
SSD_HEADS = 32
SSD_HEAD_DIM = 64
SSD_GROUPS = 4
HEADS_PER_GROUP = SSD_HEADS // SSD_GROUPS
D_STATE = 128
GROUP_COLS = HEADS_PER_GROUP * SSD_HEAD_DIM
CONV_W = 4
CONV_DIM = D_MODEL + 2 * SSD_GROUPS * D_STATE
MLP_CHUNK = 128
MLP_GROUPS = 8
MLP_GROUP_DIM = D_MODEL // MLP_GROUPS
EPS = 1e-5

SUBLANES = 8
LANES = 128
SSD_BLOCK = 128
SSD_BLOCKS_PER_STEP = 4
PROJ_TN = 1024
VMEM_LIMIT = 56 * 1024 * 1024

N_ACT = 7 * D_MODEL + 2 * SSD_GROUPS * D_STATE
N_PROJ_TILES = N_ACT // PROJ_TN
DT_OFFSET = D_MODEL + CONV_DIM
FIRST_SHIFTED_TILE = DT_OFFSET // PROJ_TN
CONV_STEPS = (2, 3, 4)
SILU_STEPS = (0, 1, 9, 10)
IDENT_STEPS = (5, 6, 7, 8)
FIRST_SIGMOID_STEP = 11
PROJ_TM = 1024
PROJ_NSPLIT = 2
PROJ_MSPLIT = 4
NORM_TM = 512
MERGE_TM = 256


def _sigmoid(x):
    return 0.5 + 0.5 * jnp.tanh(0.5 * x)


def _silu(x):
    hx = 0.5 * x
    return hx + hx * jnp.tanh(hx)


def _softplus(x):
    return jnp.maximum(x, 0.0) + jnp.log1p(jnp.exp(-jnp.abs(x)))


def _mods_kernel(c_ref, w_ref, b_ref, o_ref):
    s = _silu(c_ref[...]).astype(BF16)
    o_ref[...] = jnp.dot(s, w_ref[...].astype(BF16), preferred_element_type=F32) + b_ref[...]


def _mods(c_pad, w_ada, b_ada):
    rows, d = c_pad.shape
    n = w_ada.shape[1]
    tn = 1024
    return pl.pallas_call(
        _mods_kernel,
        grid=(n // tn,),
        in_specs=[
            pl.BlockSpec((rows, d), lambda j: (0, 0)),
            pl.BlockSpec((d, tn), lambda j: (0, j)),
            pl.BlockSpec((1, tn), lambda j: (0, j)),
        ],
        out_specs=pl.BlockSpec((rows, tn), lambda j: (0, j)),
        out_shape=jax.ShapeDtypeStruct((rows, n), F32),
        compiler_params=pltpu.CompilerParams(
            dimension_semantics=("arbitrary",), vmem_limit_bytes=VMEM_LIMIT),
        name="mods",
    )(c_pad, w_ada, b_ada)


def _norm_kernel(x_ref, sc_ref, sh_ref, gpre_ref, wdt_ref, dtb_ref, h_ref, dt_ref, *, tm, spt):
    rps = tm // spt
    for s in range(spt):
        rows = slice(s * rps, (s + 1) * rps)
        x = x_ref[rows, :]
        y = x * lax.rsqrt(jnp.mean(x * x, axis=-1, keepdims=True) + EPS)
        y = y * gpre_ref[...]
        h_ref[rows, :] = (y * (1.0 + sc_ref[s]) + sh_ref[s]).astype(BF16)
    dt_raw = lax.dot_general(h_ref[...], wdt_ref[...], (((1,), (1,)), ((), ())),
                             preferred_element_type=F32) + dtb_ref[...]
    lane = lax.broadcasted_iota(jnp.int32, dt_raw.shape, 1)
    dt_ref[...] = jnp.where(lane < SSD_HEADS, _softplus(dt_raw), 0.0)


def _norm(x2d, scale, shift, g_pre, w_dt, dt_bias, *, tm, spt):
    rows = x2d.shape[0]
    n_seq = scale.shape[0]
    n_tiles = rows // tm
    tiles_per_seq = max(1, n_tiles // n_seq) if spt == 1 else 1
    seq_blk = (lambda i: i // tiles_per_seq) if spt == 1 else (lambda i: i)
    return pl.pallas_call(
        functools.partial(_norm_kernel, tm=tm, spt=spt),
        grid=(n_tiles,),
        in_specs=[
            pl.BlockSpec((tm, D_MODEL), lambda i: (i, 0)),
            pl.BlockSpec((spt, 1, D_MODEL), lambda i: (seq_blk(i), 0, 0)),
            pl.BlockSpec((spt, 1, D_MODEL), lambda i: (seq_blk(i), 0, 0)),
            pl.BlockSpec((1, D_MODEL), lambda i: (0, 0)),
            pl.BlockSpec((LANES, D_MODEL), lambda i: (0, 0)),
            pl.BlockSpec((1, LANES), lambda i: (0, 0)),
        ],
        out_specs=[
            pl.BlockSpec((tm, D_MODEL), lambda i: (i, 0)),
            pl.BlockSpec((tm, LANES), lambda i: (i, 0)),
        ],
        out_shape=[
            jax.ShapeDtypeStruct((rows, D_MODEL), BF16),
            jax.ShapeDtypeStruct((rows, LANES), F32),
        ],
        compiler_params=pltpu.CompilerParams(
            dimension_semantics=("arbitrary",), vmem_limit_bytes=VMEM_LIMIT),
        name="norm",
    )(x2d, scale, shift, g_pre, w_dt, dt_bias)


def _act_tile(j):
    return jnp.where(j < CONV_STEPS[2], j, jnp.where(j == CONV_STEPS[2], N_PROJ_TILES - 1, j - 1))


def _wt_row(j):
    return pl.multiple_of(j * PROJ_TN + jnp.where(j >= FIRST_SHIFTED_TILE, SSD_HEADS, 0), SSD_HEADS)


def _conv_in_tile(j):
    return jnp.clip(j - CONV_STEPS[0], 0, len(CONV_STEPS) - 1)


class _Rows(NamedTuple):
    h_ref: Any
    act_ref: Any
    tail_ref: Any
    hist_ref: Any
    tm: int
    spt: int
    msplit: int
    first: Any


def _proj_step(j, row_sets, cw_ref, cb_ref, wbf_scr, halo_scr, stgs, *, nsplit):
    cw = PROJ_TN // nsplit
    blocks = [(rs, nb, mb) for nb in range(nsplit) for rs in row_sets for mb in range(rs.msplit)]

    def dot_block(k):
        rs, nb, mb = blocks[k]
        rb = rs.tm // rs.msplit
        stgs[k % 2][SUBLANES:SUBLANES + rb, :] = lax.dot_general(
            rs.h_ref[mb * rb:(mb + 1) * rb, :], wbf_scr[nb * cw:(nb + 1) * cw, :],
            (((1,), (1,)), ((), ())), preferred_element_type=F32)

    def run(epilogue):
        dot_block(0)
        for k, (rs, nb, mb) in enumerate(blocks):
            if k + 1 < len(blocks):
                dot_block(k + 1)
            epilogue(rs, nb, mb, stgs[k % 2])

    is_conv = jnp.logical_and(j >= CONV_STEPS[0], j <= CONV_STEPS[2])
    is_ident = jnp.logical_and(j >= IDENT_STEPS[0], j <= IDENT_STEPS[-1])
    is_sigmoid = j >= FIRST_SIGMOID_STEP

    p_x = jnp.where(is_ident, 1.0, 0.5).astype(F32)
    p_xt = jnp.where(is_ident, 0.0, 0.5).astype(F32)

    def elementwise(rs, nb, mb, stg):
        rb = rs.tm // rs.msplit
        x = stg[SUBLANES:SUBLANES + rb, :]
        t = jnp.tanh(0.5 * x)
        y = jnp.where(is_sigmoid, 0.5 + 0.5 * t, x * (p_x + p_xt * t))
        rs.act_ref[mb * rb:(mb + 1) * rb, nb * cw:(nb + 1) * cw] = y.astype(BF16)

    @pl.when(jnp.logical_not(is_conv))
    def _():
        for rs in row_sets:
            rs.tail_ref[...] = jnp.zeros_like(rs.tail_ref)
        run(elementwise)

    @pl.when(is_conv)
    def _():
        for rs in row_sets:
            if rs.spt == 1:
                halo_scr[...] = jnp.where(rs.first, rs.hist_ref[0], halo_scr[...])

        def conv(ext, cols):
            w0, w1, w2, w3 = (0.5 * cw_ref[k:k + 1, cols] for k in range(CONV_W))
            t = ext * w0
            t = pltpu.roll(t, 1, axis=0) + ext * w1
            t = pltpu.roll(t, 1, axis=0) + ext * w2
            t = pltpu.roll(t, 1, axis=0) + ext * w3 + 0.5 * cb_ref[:, cols]
            t = t[SUBLANES:, :]
            return (t + t * jnp.tanh(t)).astype(BF16)

        def ep(rs, nb, mb, stg):
            cols = slice(nb * cw, (nb + 1) * cw)
            rb = rs.tm // rs.msplit
            if rs.spt == 1:
                stg[0:SUBLANES, :] = halo_scr[:, cols]
                rs.act_ref[mb * rb:(mb + 1) * rb, cols] = conv(stg[0:SUBLANES + rb, :], cols)
                last = stg[rb:rb + SUBLANES, :]
                halo_scr[:, cols] = last
                rs.tail_ref[0, :, cols] = last
            else:
                rps = rs.tm // rs.spt
                for s in range(rb // rps):
                    rows = slice(SUBLANES + s * rps, SUBLANES + (s + 1) * rps)
                    seq = mb * (rb // rps) + s
                    ext = jnp.concatenate([rs.hist_ref[seq, :, cols], stg[rows, :]], axis=0)
                    rs.act_ref[mb * rb + s * rps:mb * rb + (s + 1) * rps, cols] = conv(ext, cols)
                    rs.tail_ref[seq, :, cols] = stg[rows.stop - SUBLANES:rows.stop, :]

        run(ep)


def _proj_kernel(hp_ref, hs_ref, wt_ref, cw_ref, cb_ref, histp_ref, hists_ref,
                 actp_ref, acts_ref, tailp_ref, tails_ref, wbf_scr, halo_scr, stg0_scr, stg1_scr,
                 *, n_p, tm_p, tm_s, spt_s, nsplit, msplit_p, tiles_per_seq):
    j = pl.program_id(0)
    i = pl.program_id(1)
    wrows = 256

    @pl.when(i == 0)
    def _():
        for r in range(PROJ_TN // wrows):
            rows = slice(r * wrows, (r + 1) * wrows)
            wbf_scr[rows, :] = wt_ref[rows, :].astype(BF16)

    shared = (cw_ref, cb_ref, wbf_scr, halo_scr, (stg0_scr, stg1_scr))
    prompt = _Rows(hp_ref, actp_ref, tailp_ref, histp_ref, tm=tm_p, spt=1, msplit=msplit_p,
                   first=(i % tiles_per_seq) == 0)
    sample = _Rows(hs_ref, acts_ref, tails_ref, hists_ref, tm=tm_s, spt=spt_s, msplit=1, first=None)

    @pl.when(i < n_p - 1)
    def _():
        _proj_step(j, [prompt], *shared, nsplit=nsplit)

    @pl.when(i == n_p - 1)
    def _():
        _proj_step(j, [prompt, sample], *shared, nsplit=nsplit)


def _proj(hp, hs, w_t, conv_w, conv_b, hist_p, hist_s, *, tm_p, nsplit, msplit_p):
    rows_p, rows_s = hp.shape[0], hs.shape[0]
    n_p = rows_p // tm_p
    tiles_per_seq = n_p // hist_p.shape[0]
    spt_s = hist_s.shape[0]
    rb = max(tm_p // msplit_p, rows_s)
    kern = functools.partial(_proj_kernel, n_p=n_p, tm_p=tm_p, tm_s=rows_s, spt_s=spt_s,
                             nsplit=nsplit, msplit_p=msplit_p, tiles_per_seq=tiles_per_seq)
    ip = lambda i: i
    return pl.pallas_call(
        kern,
        grid=(N_PROJ_TILES, n_p),
        in_specs=[
            pl.BlockSpec((tm_p, D_MODEL), lambda j, i: (ip(i), 0)),
            pl.BlockSpec((rows_s, D_MODEL), lambda j, i: (0, 0)),
            pl.BlockSpec((pl.Element(PROJ_TN), pl.Element(D_MODEL)), lambda j, i: (_wt_row(j), 0)),
            pl.BlockSpec((CONV_W, PROJ_TN), lambda j, i: (0, _conv_in_tile(j))),
            pl.BlockSpec((1, PROJ_TN), lambda j, i: (0, _conv_in_tile(j))),
            pl.BlockSpec((1, SUBLANES, PROJ_TN), lambda j, i: (ip(i) // tiles_per_seq, 0, _conv_in_tile(j))),
            pl.BlockSpec((spt_s, SUBLANES, PROJ_TN), lambda j, i: (0, 0, _conv_in_tile(j))),
        ],
        out_specs=[
            pl.BlockSpec((tm_p, PROJ_TN), lambda j, i: (ip(i), _act_tile(j))),
            pl.BlockSpec((rows_s, PROJ_TN), lambda j, i: (0, _act_tile(j))),
            pl.BlockSpec((1, SUBLANES, PROJ_TN), lambda j, i: (ip(i), 0, j)),
            pl.BlockSpec((spt_s, SUBLANES, PROJ_TN), lambda j, i: (0, 0, j)),
        ],
        out_shape=[
            jax.ShapeDtypeStruct((rows_p, N_ACT), BF16),
            jax.ShapeDtypeStruct((rows_s, N_ACT), BF16),
            jax.ShapeDtypeStruct((n_p, SUBLANES, N_PROJ_TILES * PROJ_TN), F32),
            jax.ShapeDtypeStruct((spt_s, SUBLANES, N_PROJ_TILES * PROJ_TN), F32),
        ],
        scratch_shapes=[
            pltpu.VMEM((PROJ_TN, D_MODEL), BF16),
            pltpu.VMEM((SUBLANES, PROJ_TN), F32),
            pltpu.VMEM((rb + SUBLANES, PROJ_TN // nsplit), F32),
            pltpu.VMEM((rb + SUBLANES, PROJ_TN // nsplit), F32),
        ],
        compiler_params=pltpu.CompilerParams(
            dimension_semantics=("arbitrary", "arbitrary"), vmem_limit_bytes=VMEM_LIMIT),
        name="proj",
    )(hp, hs, w_t, conv_w, conv_b, hist_p, hist_s)


def _pad_rows(v, rows):
    if v.shape[0] == rows:
        return v
    return jnp.concatenate([v, jnp.zeros((rows - v.shape[0], v.shape[1]), v.dtype)], axis=0)


def _ssd_kernel(szs_ref, xs_ref, b_ref, c_ref, dt_ref, alog_ref, dskip_ref, nrm_ref, e3_ref, h0_ref,
                y_ref, hout_ref, st_scr, *, q_in, blocks_per_step, n_steps):
    ci = pl.program_id(1)

    @pl.when(ci == 0)
    def _():
        st_scr[...] = h0_ref[0]

    for k in range(blocks_per_step):
        rs = pl.ds(k * q_in, q_in)
        _ssd_block(szs_ref.at[rs], xs_ref.at[rs], b_ref.at[rs], c_ref.at[rs], dt_ref.at[rs],
                   alog_ref, dskip_ref, nrm_ref, e3_ref, y_ref.at[rs], st_scr, q_in=q_in)

    @pl.when(ci == n_steps - 1)
    def _():
        hout_ref[0] = st_scr[...]


def _ssd_block(szs_ref, xs_ref, b_ref, c_ref, dt_ref, alog_ref, dskip_ref, nrm_ref, e3_ref,
               y_ref, st_scr, *, q_in):
    q = SSD_BLOCK
    dt = _pad_rows(dt_ref[...], q)
    a = -jnp.exp(alog_ref[...])
    adt = dt * a
    row = lax.broadcasted_iota(jnp.int32, (q, q), 0)
    col = lax.broadcasted_iota(jnp.int32, (q, q), 1)
    causal = row >= col
    acum = jnp.dot(causal.astype(F32), adt, precision=lax.Precision.HIGHEST,
                   preferred_element_type=F32)
    alast = acum[q - 1:q, :]
    dte = jnp.exp(alast - acum)
    ea = jnp.exp(acum)

    stack = jnp.concatenate([dt, dt * dte, ea], axis=0)
    lane3 = lax.broadcasted_iota(jnp.int32, stack.shape, 1)
    stack = jnp.where(lane3 < SSD_HEADS, stack, 0.0)
    hi = stack.astype(BF16).astype(F32)
    r1 = stack - hi
    mid = r1.astype(BF16).astype(F32)
    lo = (r1 - mid).astype(BF16).astype(F32)
    comb = hi + pltpu.roll(mid, SSD_HEADS, axis=1) + pltpu.roll(lo, 2 * SSD_HEADS, axis=1)
    expd = jnp.dot(comb.astype(BF16), e3_ref[...], preferred_element_type=F32)
    dt_e, w_e, ea_e = expd[0:q], expd[q:2 * q], expd[2 * q:3 * q]

    x = _pad_rows(xs_ref[...], q).astype(F32)
    xdt = (x * dt_e).astype(BF16)
    xdtd = (x * w_e).astype(BF16)
    bm = _pad_rows(b_ref[...], q)
    cm = _pad_rows(c_ref[...], q)

    acum_t = acum.T
    dec = jnp.exp(acum_t[0:SSD_HEADS, q - 1:q])
    dec = jnp.broadcast_to(dec, (SSD_HEADS, D_STATE))
    dec = jnp.broadcast_to(dec[:, None, :], (SSD_HEADS, SSD_HEAD_DIM, D_STATE))
    dec = dec.reshape(SSD_HEADS * SSD_HEAD_DIM, D_STATE)

    lane = lax.broadcasted_iota(jnp.int32, (q, LANES), 1)
    nt = (((1,), (1,)), ((), ()))
    tn = (((0,), (0,)), ((), ()))
    y_groups = []
    for g in range(SSD_GROUPS):
        gc = slice(g * GROUP_COLS, (g + 1) * GROUP_COLS)
        bg = bm[:, g * D_STATE:(g + 1) * D_STATE]
        cg = cm[:, g * D_STATE:(g + 1) * D_STATE]
        cb = lax.dot_general(cg, bg, nt, preferred_element_type=F32)
        sg = st_scr[gc, :]
        y_off = lax.dot_general(cg, sg.astype(BF16), nt, preferred_element_type=F32)
        pieces = []
        for pr in range(HEADS_PER_GROUP // 2):
            ms = []
            for hh in range(2):
                h = g * HEADS_PER_GROUP + 2 * pr + hh
                seg = jnp.broadcast_to(acum[:, h:h + 1], (q, q)) - acum_t[h:h + 1, :]
                lm = jnp.exp(jnp.where(causal, seg, -jnp.inf))
                ms.append((cb * lm).astype(BF16))
            m = jnp.concatenate(ms, axis=1)
            c0 = g * GROUP_COLS + pr * LANES
            xp = xdt[:, c0:c0 + LANES]
            rhs = jnp.concatenate([jnp.where(lane < SSD_HEAD_DIM, xp, jnp.zeros_like(xp)),
                                   jnp.where(lane >= SSD_HEAD_DIM, xp, jnp.zeros_like(xp))], axis=0)
            pieces.append(jnp.dot(m, rhs, preferred_element_type=F32))
        y_diag = jnp.concatenate(pieces, axis=1)
        y_groups.append(y_diag + y_off * ea_e[:, gc])
        new = lax.dot_general(xdtd[:, gc], bg, tn, preferred_element_type=F32)
        st_scr[gc, :] = sg * dec[gc, :] + new
    y = jnp.concatenate(y_groups, axis=1)
    y = y + dskip_ref[...] * x
    yz = y * _pad_rows(szs_ref[...], q).astype(F32)
    yn = yz * lax.rsqrt(jnp.mean(yz * yz, axis=-1, keepdims=True) + EPS) * nrm_ref[...]
    y_ref[...] = yn[0:q_in].astype(BF16)


def _ssd(act, dt, a_log, dskip_e, norm_ssd, e3, h0, *, n_seq, q_in, blocks_per_step):
    rows = act.shape[0]
    qb = q_in * blocks_per_step
    n_chunks = rows // (n_seq * qb)
    kern = functools.partial(_ssd_kernel, q_in=q_in, blocks_per_step=blocks_per_step, n_steps=n_chunks)
    rblk = lambda b, c: b * n_chunks + c
    bc_blk0 = 7 * D_MODEL // GROUP_COLS
    return pl.pallas_call(
        kern,
        grid=(n_seq, n_chunks),
        in_specs=[
            pl.BlockSpec((qb,D_MODEL), lambda b, c: (rblk(b, c), 0)),
            pl.BlockSpec((qb,D_MODEL), lambda b, c: (rblk(b, c), 1)),
            pl.BlockSpec((qb,GROUP_COLS), lambda b, c: (rblk(b, c), bc_blk0)),
            pl.BlockSpec((qb,GROUP_COLS), lambda b, c: (rblk(b, c), bc_blk0 + 1)),
            pl.BlockSpec((qb,LANES), lambda b, c: (rblk(b, c), 0)),
            pl.BlockSpec((1, LANES), lambda b, c: (0, 0)),
            pl.BlockSpec((1, D_MODEL), lambda b, c: (0, 0)),
            pl.BlockSpec((1, D_MODEL), lambda b, c: (0, 0)),
            pl.BlockSpec((LANES, D_MODEL), lambda b, c: (0, 0)),
            pl.BlockSpec((1, D_MODEL, D_STATE), lambda b, c: (b, 0, 0)),
        ],
        out_specs=[
            pl.BlockSpec((qb,D_MODEL), lambda b, c: (rblk(b, c), 0)),
            pl.BlockSpec((1, D_MODEL, D_STATE), lambda b, c: (b, 0, 0)),
        ],
        out_shape=[
            jax.ShapeDtypeStruct((rows, D_MODEL), BF16),
            jax.ShapeDtypeStruct((n_seq, D_MODEL, D_STATE), F32),
        ],
        scratch_shapes=[pltpu.VMEM((D_MODEL, D_STATE), F32)],
        compiler_params=pltpu.CompilerParams(
            dimension_semantics=("arbitrary", "arbitrary"), vmem_limit_bytes=VMEM_LIMIT),
        name="ssd",
    )(act, act, act, act, dt, a_log, dskip_e, norm_ssd, e3, h0)


def _mlp_stage(u_ref, v_ref, szm_ref, lng_ref, lnb_ref, wsp_ref, bsp_ref, ymlp_ref, vn_ref, *, tm, qm):
    v = v_ref[...].astype(F32)
    mu = jnp.mean(v, axis=-1, keepdims=True)
    vc = v - mu
    var = jnp.mean(vc * vc, axis=-1, keepdims=True)
    vn = vc * lax.rsqrt(var + EPS) * lng_ref[...] + lnb_ref[...]
    if vn_ref is not None:
        vn_ref[...] = vn
    vn_b = vn.astype(BF16)

    row = lax.broadcasted_iota(jnp.int32, (MLP_CHUNK, MLP_CHUNK), 0)
    col = lax.broadcasted_iota(jnp.int32, (MLP_CHUNK, MLP_CHUNK), 1)
    wms = [jnp.where(row >= col, wsp_ref[g], 0.0).astype(BF16) for g in range(MLP_GROUPS)]
    for ck in range(tm // qm):
        rows = slice(ck * qm, (ck + 1) * qm)
        vck = _pad_rows(vn_b[rows, :], MLP_CHUNK)
        mixed = jnp.concatenate(
            [jnp.dot(wms[g], vck[:, g * MLP_GROUP_DIM:(g + 1) * MLP_GROUP_DIM],
                     preferred_element_type=F32) for g in range(MLP_GROUPS)], axis=1)
        mixed = mixed[0:qm] + bsp_ref[...]
        y_mlp = u_ref[rows, :].astype(F32) * mixed * szm_ref[rows, :].astype(F32)
        ymlp_ref[rows, :] = y_mlp.astype(BF16)


def _mix_stage(yssd_ref, ymlp_ref, sgs_ref, sgm_ref, wbs_ref, wbm_ref, wo_ref):
    a = jnp.dot(yssd_ref[...], wbs_ref[...], preferred_element_type=F32)
    b = jnp.dot(ymlp_ref[...], wbm_ref[...], preferred_element_type=F32)
    merged = sgs_ref[...].astype(F32) * a + sgm_ref[...].astype(F32) * b
    return jnp.dot(merged.astype(BF16), wo_ref[...], preferred_element_type=F32)


def _out_stage(o, x_ref, gate_ref, gfin_ref, y_ref, *, tm, spt):
    rps = tm // spt
    for s in range(spt):
        rows = slice(s * rps, (s + 1) * rps)
        out = x_ref[rows, :] + gate_ref[s] * o[rows, :]
        y = out * lax.rsqrt(jnp.mean(out * out, axis=-1, keepdims=True) + EPS)
        y_ref[rows, :] = y * gfin_ref[...]


def _merge_kernel(yssd_ref, u_ref, v_ref, szm_ref, sgs_ref, sgm_ref, x_ref, gate_ref,
                  lng_ref, lnb_ref, gfin_ref, wsp_ref, bsp_ref, wbs_ref, wbm_ref, wo_ref,
                  y_ref, *rest, tm, qm, spt, emit_vn):
    if emit_vn:
        vn_ref, ymlp_scr = rest
    else:
        vn_ref, (ymlp_scr,) = None, rest
    _mlp_stage(u_ref, v_ref, szm_ref, lng_ref, lnb_ref, wsp_ref, bsp_ref, ymlp_scr, vn_ref, tm=tm, qm=qm)
    o = _mix_stage(yssd_ref, ymlp_scr, sgs_ref, sgm_ref, wbs_ref, wbm_ref, wo_ref)
    _out_stage(o, x_ref, gate_ref, gfin_ref, y_ref, tm=tm, spt=spt)


def _merge(yssd, act, x2d, gate, ln_g, ln_b, g_final, w_spatial, bsp_e, wbs, wbm, wo,
           *, tm, qm, spt, emit_vn):
    rows = x2d.shape[0]
    n_seq = gate.shape[0]
    n_tiles = rows // tm
    tiles_per_seq = max(1, n_tiles // n_seq) if spt == 1 else 1
    seq_blk = (lambda i: i // tiles_per_seq) if spt == 1 else (lambda i: i)
    kern = functools.partial(_merge_kernel, tm=tm, qm=qm, spt=spt, emit_vn=emit_vn)
    const = lambda shape: pl.BlockSpec(shape, lambda i: (0,) * len(shape),
                                       pipeline_mode=pl.Buffered(1))
    seg = lambda k: pl.BlockSpec((tm, D_MODEL), lambda i: (i, k))
    out_specs = [pl.BlockSpec((tm, D_MODEL), lambda i: (i, 0))]
    out_shape = [jax.ShapeDtypeStruct((rows, D_MODEL), F32)]
    if emit_vn:
        out_specs.append(pl.BlockSpec((tm, D_MODEL), lambda i: (i, 0)))
        out_shape.append(jax.ShapeDtypeStruct((rows, D_MODEL), F32))
    return pl.pallas_call(
        kern,
        grid=(n_tiles,),
        in_specs=[
            pl.BlockSpec((tm, D_MODEL), lambda i: (i, 0)),
            seg(2), seg(3), seg(4), seg(5), seg(6),
            pl.BlockSpec((tm, D_MODEL), lambda i: (i, 0)),
            pl.BlockSpec((spt, 1, D_MODEL), lambda i: (seq_blk(i), 0, 0)),
            const((1, D_MODEL)), const((1, D_MODEL)), const((1, D_MODEL)),
            const((MLP_GROUPS, MLP_CHUNK, MLP_CHUNK)),
            const((qm, D_MODEL)),
            const((D_MODEL, D_MODEL)), const((D_MODEL, D_MODEL)), const((D_MODEL, D_MODEL)),
        ],
        out_specs=out_specs,
        out_shape=out_shape,
        scratch_shapes=[pltpu.VMEM((tm, D_MODEL), BF16)],
        compiler_params=pltpu.CompilerParams(
            dimension_semantics=("arbitrary",), vmem_limit_bytes=VMEM_LIMIT),
        name="merge",
    )(yssd, act, act, act, act, act, x2d, gate, ln_g, ln_b, g_final, w_spatial, bsp_e, wbs, wbm, wo)


def _cast_kernel(a_ref, b_ref, c_ref, ao_ref, bo_ref, co_ref):
    ao_ref[...] = a_ref[...].astype(BF16)
    bo_ref[...] = b_ref[...].astype(BF16)
    co_ref[...] = c_ref[...].astype(BF16)


def _cast_bf16(a, b, c, *, tm=512):
    rows, cols = a.shape
    spec = pl.BlockSpec((tm, cols), lambda i: (i, 0))
    return pl.pallas_call(
        _cast_kernel,
        grid=(rows // tm,),
        in_specs=[spec] * 3,
        out_specs=[spec] * 3,
        out_shape=[jax.ShapeDtypeStruct((rows, cols), BF16)] * 3,
        compiler_params=pltpu.CompilerParams(
            dimension_semantics=("arbitrary",), vmem_limit_bytes=VMEM_LIMIT),
        name="cast",
    )(a, b, c)


def _expansion_matrix():
    e = np.zeros((LANES, D_MODEL), np.float32)
    cols = np.arange(D_MODEL)
    for piece in range(3):
        e[piece * SSD_HEADS + cols // SSD_HEAD_DIM, cols] = 1.0
    return jnp.asarray(e, BF16)


def kernel(x_prompt, x_sample, state_ssm, cache_conv, c_prompt, c_sample, w_ada, b_ada, g_pre, w_in,
           conv_w, conv_b, dt_bias, a_log, d_skip, norm_ssd, ln_g, ln_b, w_spatial, b_spatial,
           w_bp_ssd, w_bp_mlp, w_o, g_final):
    depth = w_ada.shape[0]
    assert depth == 1, "single-layer trunk only"
    bp, seq, d = x_prompt.shape
    bs, dec_seq, _ = x_sample.shape
    assert d == D_MODEL and seq % 1024 == 0 and dec_seq % 16 == 0 and dec_seq <= SSD_BLOCK

    w_t = jnp.swapaxes(w_in[0], 0, 1)
    w_dt = jnp.pad(w_t[DT_OFFSET:DT_OFFSET + SSD_HEADS], ((0, LANES - SSD_HEADS), (0, 0))).astype(BF16)
    pad_heads = lambda v: jnp.pad(v.reshape(1, SSD_HEADS), ((0, 0), (0, LANES - SSD_HEADS)))
    wts = dict(
        g_pre=g_pre[0].reshape(1, d), w_t=w_t, w_dt=w_dt, dt_bias=pad_heads(dt_bias[0]),
        conv_w=conv_w[0], conv_b=conv_b[0].reshape(1, CONV_DIM),
        a_log=pad_heads(a_log[0]), dskip_e=jnp.repeat(d_skip[0], SSD_HEAD_DIM).reshape(1, d),
        norm_ssd=norm_ssd[0].reshape(1, d), e3=_expansion_matrix(),
        ln_g=ln_g[0].reshape(1, d), ln_b=ln_b[0].reshape(1, d), g_final=g_final.reshape(1, d),
        w_spatial=w_spatial[0], bsp_e=jnp.repeat(b_spatial[0].T, MLP_GROUP_DIM, axis=1),
    )

    c_all = jnp.concatenate([c_prompt, c_sample], axis=0)
    c_pad = jnp.pad(c_all, ((0, 16 - (bp + bs)), (0, 0)))
    mods = _mods(c_pad, w_ada[0], b_ada[0].reshape(1, 3 * d))

    split_mods = lambda m: (m[:, k * d:(k + 1) * d].reshape(m.shape[0], 1, d) for k in range(3))
    shift_p, scale_p, gate_p = split_mods(mods[:bp])
    shift_s, scale_s, gate_s = split_mods(mods[bp:bp + bs])
    x2p = x_prompt.reshape(bp * seq, d)
    x2s = x_sample.reshape(bs * dec_seq, d)
    rows_s = bs * dec_seq

    hist_p = jnp.zeros((bp, SUBLANES, CONV_DIM), F32)
    h0_p = jnp.zeros((bp, d, D_STATE), F32)
    hist_s = jnp.pad(cache_conv[0], ((0, 0), (SUBLANES - (CONV_W - 1), 0), (0, 0)))
    h0_s = state_ssm[0].reshape(bs, d, D_STATE)

    norm_w = (wts["g_pre"], wts["w_dt"], wts["dt_bias"])
    hp, dt_p = _norm(x2p, scale_p, shift_p, *norm_w, tm=NORM_TM, spt=1)
    hs, dt_s = _norm(x2s, scale_s, shift_s, *norm_w, tm=rows_s, spt=bs)
    act_p, act_s, tail_p, tail_s = _proj(hp, hs, wts["w_t"], wts["conv_w"], wts["conv_b"], hist_p, hist_s,
                                         tm_p=PROJ_TM, nsplit=PROJ_NSPLIT, msplit_p=PROJ_MSPLIT)

    ssd_w = (wts["a_log"], wts["dskip_e"], wts["norm_ssd"], wts["e3"])
    yssd_p, ssm_p = _ssd(act_p, dt_p, *ssd_w, h0_p, n_seq=bp, q_in=SSD_BLOCK,
                         blocks_per_step=SSD_BLOCKS_PER_STEP)
    yssd_s, ssm_s = _ssd(act_s, dt_s, *ssd_w, h0_s, n_seq=bs, q_in=dec_seq, blocks_per_step=1)

    merge_w = (wts["ln_g"], wts["ln_b"], wts["g_final"], wts["w_spatial"])
    proj_w = _cast_bf16(w_bp_ssd[0], w_bp_mlp[0], w_o[0])
    (yp,) = _merge(yssd_p, act_p, x2p, gate_p, *merge_w, wts["bsp_e"], *proj_w,
                   tm=MERGE_TM, qm=MLP_CHUNK, spt=1, emit_vn=False)
    ys, vn_s = _merge(yssd_s, act_s, x2s, gate_s, *merge_w, wts["bsp_e"][:dec_seq], *proj_w,
                      tm=rows_s, qm=dec_seq, spt=bs, emit_vn=True)

    conv_cols = slice(CONV_STEPS[0] * PROJ_TN, CONV_STEPS[0] * PROJ_TN + CONV_DIM)
    hist_rows = slice(SUBLANES - (CONV_W - 1), SUBLANES)
    tiles_per_seq = seq // PROJ_TM
    conv_p = tail_p[tiles_per_seq - 1::tiles_per_seq, hist_rows, conv_cols]
    conv_s = tail_s[:, hist_rows, conv_cols]

    state_shape = (SSD_HEADS, SSD_HEAD_DIM, D_STATE)
    return (yp.reshape(bp, seq, d), ys.reshape(bs, dec_seq, d),
            ssm_p.reshape(1, bp, *state_shape), conv_p[None],
            ssm_s.reshape(1, bs, *state_shape), conv_s[None], vn_s.reshape(1, bs, dec_seq, d))
```

```python
import functools
from typing import Any, NamedTuple

import numpy as np
import jax
import jax.numpy as jnp
from jax import lax
from jax.experimental import pallas as pl
from jax.experimental.pallas import tpu as pltpu

F32 = jnp.float32
BF16 = jnp.bfloat16

D_MODEL = 2048
SSD_HEADS = 32
SSD_HEAD_DIM = 64
SSD_GROUPS = 4
HEADS_PER_GROUP = SSD_HEADS // SSD_GROUPS
D_STATE = 128
GROUP_COLS = HEADS_PER_GROUP * SSD_HEAD_DIM
CONV_W = 4
CONV_DIM = D_MODEL + 2 * SSD_GROUPS * D_STATE
MLP_CHUNK = 128
MLP_GROUPS = 8
MLP_GROUP_DIM = D_MODEL // MLP_GROUPS
EPS = 1e-5

SUBLANES = 8
LANES = 128
SSD_BLOCK = 128
SSD_BLOCKS_PER_STEP = 8
PROJ_TN = 1024
VMEM_LIMIT = 56 * 1024 * 1024

N_ACT = 7 * D_MODEL + 2 * SSD_GROUPS * D_STATE
N_PROJ_TILES = N_ACT // PROJ_TN
DT_OFFSET = D_MODEL + CONV_DIM
FIRST_SHIFTED_TILE = DT_OFFSET // PROJ_TN
CONV_STEPS = (2, 3, 4)
IDENT_STEPS = (5, 6, 7, 8)
FIRST_SIGMOID_STEP = 11
PROJ_TM = 1024
PROJ_NSPLIT = 2
PROJ_MSPLIT = 4
NORM_TM = 512
MERGE_TM = 256


def _silu(x):
    hx = 0.5 * x
    return hx + hx * jnp.tanh(hx)


def _softplus(x):
    return jnp.maximum(x, 0.0) + jnp.log1p(jnp.exp(-jnp.abs(x)))


def _mods_kernel(c_ref, w_ref, b_ref, o_ref):
    s = _silu(c_ref[...]).astype(BF16)
    o_ref[...] = jnp.dot(s, w_ref[...].astype(BF16), preferred_element_type=F32) + b_ref[...]


def _mods(c_pad, w_ada, b_ada):
    rows, d = c_pad.shape
    n = w_ada.shape[1]
    tn = 1024
    return pl.pallas_call(
        _mods_kernel,
        grid=(n // tn,),
        in_specs=[
            pl.BlockSpec((rows, d), lambda j: (0, 0)),
            pl.BlockSpec((d, tn), lambda j: (0, j)),
            pl.BlockSpec((1, tn), lambda j: (0, j)),
        ],
        out_specs=pl.BlockSpec((rows, tn), lambda j: (0, j)),
        out_shape=jax.ShapeDtypeStruct((rows, n), F32),
        compiler_params=pltpu.CompilerParams(
            dimension_semantics=("arbitrary",), vmem_limit_bytes=VMEM_LIMIT),
        name="mods",
    )(c_pad, w_ada, b_ada)


def _norm_kernel(x_ref, sc_ref, sh_ref, gpre_ref, wdt_ref, dtb_ref, h_ref, dt_ref, *, tm, spt):
    rps = tm // spt
    for s in range(spt):
        rows = slice(s * rps, (s + 1) * rps)
        x = x_ref[rows, :]
        y = x * lax.rsqrt(jnp.mean(x * x, axis=-1, keepdims=True) + EPS)
        y = y * gpre_ref[...]
        h_ref[rows, :] = (y * (1.0 + sc_ref[s]) + sh_ref[s]).astype(BF16)
    dt_raw = lax.dot_general(h_ref[...], wdt_ref[...], (((1,), (1,)), ((), ())),
                             preferred_element_type=F32) + dtb_ref[...]
    lane = lax.broadcasted_iota(jnp.int32, dt_raw.shape, 1)
    dt_ref[...] = jnp.where(lane < SSD_HEADS, _softplus(dt_raw), 0.0)


def _norm(x2d, scale, shift, g_pre, w_dt, dt_bias, *, tm, spt):
    rows = x2d.shape[0]
    n_seq = scale.shape[0]
    n_tiles = rows // tm
    tiles_per_seq = max(1, n_tiles // n_seq) if spt == 1 else 1
    seq_blk = (lambda i: i // tiles_per_seq) if spt == 1 else (lambda i: i)
    return pl.pallas_call(
        functools.partial(_norm_kernel, tm=tm, spt=spt),
        grid=(n_tiles,),
        in_specs=[
            pl.BlockSpec((tm, D_MODEL), lambda i: (i, 0)),
            pl.BlockSpec((spt, 1, D_MODEL), lambda i: (seq_blk(i), 0, 0)),
            pl.BlockSpec((spt, 1, D_MODEL), lambda i: (seq_blk(i), 0, 0)),
            pl.BlockSpec((1, D_MODEL), lambda i: (0, 0)),
            pl.BlockSpec((LANES, D_MODEL), lambda i: (0, 0)),
            pl.BlockSpec((1, LANES), lambda i: (0, 0)),
        ],
        out_specs=[
            pl.BlockSpec((tm, D_MODEL), lambda i: (i, 0)),
            pl.BlockSpec((tm, LANES), lambda i: (i, 0)),
        ],
        out_shape=[
            jax.ShapeDtypeStruct((rows, D_MODEL), BF16),
            jax.ShapeDtypeStruct((rows, LANES), F32),
        ],
        compiler_params=pltpu.CompilerParams(
            dimension_semantics=("arbitrary",), vmem_limit_bytes=VMEM_LIMIT),
        name="norm",
    )(x2d, scale, shift, g_pre, w_dt, dt_bias)


def _act_tile(j):
    return jnp.where(j < CONV_STEPS[2], j, jnp.where(j == CONV_STEPS[2], N_PROJ_TILES - 1, j - 1))


def _wt_row(j):
    return pl.multiple_of(j * PROJ_TN + jnp.where(j >= FIRST_SHIFTED_TILE, SSD_HEADS, 0), SSD_HEADS)


def _conv_in_tile(j):
    return jnp.clip(j - CONV_STEPS[0], 0, len(CONV_STEPS) - 1)


class _Rows(NamedTuple):
    h_ref: Any
    act_ref: Any
    tail_ref: Any
    hist_ref: Any
    tm: int
    spt: int
    msplit: int
    first: Any


def _proj_step(j, row_sets, cw_ref, cb_ref, wbf_scr, halo_scr, stgs, *, nsplit):
    cw = PROJ_TN // nsplit
    blocks = [(rs, nb, mb) for nb in range(nsplit) for rs in row_sets for mb in range(rs.msplit)]

    def dot_block(k):
        rs, nb, mb = blocks[k]
        rb = rs.tm // rs.msplit
        stgs[k % 2][SUBLANES:SUBLANES + rb, :] = lax.dot_general(
            rs.h_ref[mb * rb:(mb + 1) * rb, :], wbf_scr[nb * cw:(nb + 1) * cw, :],
            (((1,), (1,)), ((), ())), preferred_element_type=F32)

    def run(epilogue):
        dot_block(0)
        for k, (rs, nb, mb) in enumerate(blocks):
            if k + 1 < len(blocks):
                dot_block(k + 1)
            epilogue(rs, nb, mb, stgs[k % 2])

    is_conv = jnp.logical_and(j >= CONV_STEPS[0], j <= CONV_STEPS[2])
    is_ident = jnp.logical_and(j >= IDENT_STEPS[0], j <= IDENT_STEPS[-1])
    is_sigmoid = j >= FIRST_SIGMOID_STEP

    p_x = jnp.where(is_ident, 1.0, 0.5).astype(F32)
    p_xt = jnp.where(is_ident, 0.0, 0.5).astype(F32)

    def elementwise(rs, nb, mb, stg):
        rb = rs.tm // rs.msplit
        x = stg[SUBLANES:SUBLANES + rb, :]
        t = jnp.tanh(0.5 * x)
        y = jnp.where(is_sigmoid, 0.5 + 0.5 * t, x * (p_x + p_xt * t))
        rs.act_ref[mb * rb:(mb + 1) * rb, nb * cw:(nb + 1) * cw] = y.astype(BF16)

    @pl.when(jnp.logical_not(is_conv))
    def _():
        for rs in row_sets:
            rs.tail_ref[...] = jnp.zeros_like(rs.tail_ref)
        run(elementwise)

    @pl.when(is_conv)
    def _():
        for rs in row_sets:
            if rs.spt == 1:
                halo_scr[...] = jnp.where(rs.first, rs.hist_ref[0], halo_scr[...])

        def conv(ext, cols):
            w0, w1, w2, w3 = (0.5 * cw_ref[k:k + 1, cols] for k in range(CONV_W))
            t = ext * w0
            t = pltpu.roll(t, 1, axis=0) + ext * w1
            t = pltpu.roll(t, 1, axis=0) + ext * w2
            t = pltpu.roll(t, 1, axis=0) + ext * w3 + 0.5 * cb_ref[:, cols]
            t = t[SUBLANES:, :]
            return (t + t * jnp.tanh(t)).astype(BF16)

        def ep(rs, nb, mb, stg):
            cols = slice(nb * cw, (nb + 1) * cw)
            rb = rs.tm // rs.msplit
            if rs.spt == 1:
                stg[0:SUBLANES, :] = halo_scr[:, cols]
                rs.act_ref[mb * rb:(mb + 1) * rb, cols] = conv(stg[0:SUBLANES + rb, :], cols)
                last = stg[rb:rb + SUBLANES, :]
                halo_scr[:, cols] = last
                rs.tail_ref[0, :, cols] = last
            else:
                rps = rs.tm // rs.spt
                for s in range(rb // rps):
                    rows = slice(SUBLANES + s * rps, SUBLANES + (s + 1) * rps)
                    seq = mb * (rb // rps) + s
                    ext = jnp.concatenate([rs.hist_ref[seq, :, cols], stg[rows, :]], axis=0)
                    rs.act_ref[mb * rb + s * rps:mb * rb + (s + 1) * rps, cols] = conv(ext, cols)
                    rs.tail_ref[seq, :, cols] = stg[rows.stop - SUBLANES:rows.stop, :]

        run(ep)


def _proj_kernel(hp_ref, hs_ref, wt_ref, cw_ref, cb_ref, histp_ref, hists_ref,
                 actp_ref, acts_ref, tailp_ref, tails_ref, wbf_scr, halo_scr, stg0_scr, stg1_scr,
                 *, n_p, tm_p, tm_s, spt_s, nsplit, msplit_p, tiles_per_seq):
    j = pl.program_id(0)
    i = pl.program_id(1)
    wrows = 256

    @pl.when(i == 0)
    def _():
        for r in range(PROJ_TN // wrows):
            rows = slice(r * wrows, (r + 1) * wrows)
            wbf_scr[rows, :] = wt_ref[rows, :].astype(BF16)

    shared = (cw_ref, cb_ref, wbf_scr, halo_scr, (stg0_scr, stg1_scr))
    prompt = _Rows(hp_ref, actp_ref, tailp_ref, histp_ref, tm=tm_p, spt=1, msplit=msplit_p,
                   first=(i % tiles_per_seq) == 0)
    sample = _Rows(hs_ref, acts_ref, tails_ref, hists_ref, tm=tm_s, spt=spt_s, msplit=1, first=None)

    @pl.when(i < n_p - 1)
    def _():
        _proj_step(j, [prompt], *shared, nsplit=nsplit)

    @pl.when(i == n_p - 1)
    def _():
        _proj_step(j, [prompt, sample], *shared, nsplit=nsplit)


def _proj(hp, hs, w_t, conv_w, conv_b, hist_p, hist_s, *, tm_p, nsplit, msplit_p):
    rows_p, rows_s = hp.shape[0], hs.shape[0]
    n_p = rows_p // tm_p
    tiles_per_seq = n_p // hist_p.shape[0]
    spt_s = hist_s.shape[0]
    rb = max(tm_p // msplit_p, rows_s)
    kern = functools.partial(_proj_kernel, n_p=n_p, tm_p=tm_p, tm_s=rows_s, spt_s=spt_s,
                             nsplit=nsplit, msplit_p=msplit_p, tiles_per_seq=tiles_per_seq)
    ip = lambda i: i
    return pl.pallas_call(
        kern,
        grid=(N_PROJ_TILES, n_p),
        in_specs=[
            pl.BlockSpec((tm_p, D_MODEL), lambda j, i: (ip(i), 0)),
            pl.BlockSpec((rows_s, D_MODEL), lambda j, i: (0, 0)),
            pl.BlockSpec((pl.Element(PROJ_TN), pl.Element(D_MODEL)), lambda j, i: (_wt_row(j), 0)),
            pl.BlockSpec((CONV_W, PROJ_TN), lambda j, i: (0, _conv_in_tile(j))),
            pl.BlockSpec((1, PROJ_TN), lambda j, i: (0, _conv_in_tile(j))),
            pl.BlockSpec((1, SUBLANES, PROJ_TN), lambda j, i: (ip(i) // tiles_per_seq, 0, _conv_in_tile(j))),
            pl.BlockSpec((spt_s, SUBLANES, PROJ_TN), lambda j, i: (0, 0, _conv_in_tile(j))),
        ],
        out_specs=[
            pl.BlockSpec((tm_p, PROJ_TN), lambda j, i: (ip(i), _act_tile(j))),
            pl.BlockSpec((rows_s, PROJ_TN), lambda j, i: (0, _act_tile(j))),
            pl.BlockSpec((1, SUBLANES, PROJ_TN), lambda j, i: (ip(i), 0, j)),
            pl.BlockSpec((spt_s, SUBLANES, PROJ_TN), lambda j, i: (0, 0, j)),
        ],
        out_shape=[
            jax.ShapeDtypeStruct((rows_p, N_ACT), BF16),
            jax.ShapeDtypeStruct((rows_s, N_ACT), BF16),
            jax.ShapeDtypeStruct((n_p, SUBLANES, N_PROJ_TILES * PROJ_TN), F32),
            jax.ShapeDtypeStruct((spt_s, SUBLANES, N_PROJ_TILES * PROJ_TN), F32),
        ],
        scratch_shapes=[
            pltpu.VMEM((PROJ_TN, D_MODEL), BF16),
            pltpu.VMEM((SUBLANES, PROJ_TN), F32),
            pltpu.VMEM((rb + SUBLANES, PROJ_TN // nsplit), F32),
            pltpu.VMEM((rb + SUBLANES, PROJ_TN // nsplit), F32),
        ],
        compiler_params=pltpu.CompilerParams(
            dimension_semantics=("arbitrary", "arbitrary"), vmem_limit_bytes=VMEM_LIMIT),
        name="proj",
    )(hp, hs, w_t, conv_w, conv_b, hist_p, hist_s)


def _pad_rows(v, rows):
    if v.shape[0] == rows:
        return v
    return jnp.concatenate([v, jnp.zeros((rows - v.shape[0], v.shape[1]), v.dtype)], axis=0)


def _ssd_kernel(szs_ref, xs_ref, b_ref, c_ref, dt_ref, alog_ref, dskip_ref, nrm_ref, e3_ref, h0_ref,
                y_ref, hout_ref, st_scr, *, q_in, blocks_per_step, n_steps):
    ci = pl.program_id(1)

    @pl.when(ci == 0)
    def _():
        st_scr[...] = h0_ref[0]

    for k in range(blocks_per_step):
        rs = pl.ds(k * q_in, q_in)
        _ssd_block(szs_ref.at[rs], xs_ref.at[rs], b_ref.at[rs], c_ref.at[rs], dt_ref.at[rs],
                   alog_ref, dskip_ref, nrm_ref, e3_ref, y_ref.at[rs], st_scr, q_in=q_in)

    @pl.when(ci == n_steps - 1)
    def _():
        hout_ref[0] = st_scr[...]


def _ssd_block(szs_ref, xs_ref, b_ref, c_ref, dt_ref, alog_ref, dskip_ref, nrm_ref, e3_ref,
               y_ref, st_scr, *, q_in):
    q = SSD_BLOCK
    dt = _pad_rows(dt_ref[...], q)
    a = -jnp.exp(alog_ref[...])
    adt = dt * a
    row = lax.broadcasted_iota(jnp.int32, (q, q), 0)
    col = lax.broadcasted_iota(jnp.int32, (q, q), 1)
    causal = row >= col
    acum = jnp.dot(causal.astype(F32), adt, precision=lax.Precision.HIGHEST,
                   preferred_element_type=F32)
    alast = acum[q - 1:q, :]
    dte = jnp.exp(alast - acum)
    ea = jnp.exp(acum)

    stack = jnp.concatenate([dt, dt * dte, ea], axis=0)
    lane3 = lax.broadcasted_iota(jnp.int32, stack.shape, 1)
    stack = jnp.where(lane3 < SSD_HEADS, stack, 0.0)
    hi = stack.astype(BF16).astype(F32)
    r1 = stack - hi
    mid = r1.astype(BF16).astype(F32)
    lo = (r1 - mid).astype(BF16).astype(F32)
    comb = hi + pltpu.roll(mid, SSD_HEADS, axis=1) + pltpu.roll(lo, 2 * SSD_HEADS, axis=1)
    expd = jnp.dot(comb.astype(BF16), e3_ref[...], preferred_element_type=F32)
    dt_e, w_e, ea_e = expd[0:q], expd[q:2 * q], expd[2 * q:3 * q]

    x = _pad_rows(xs_ref[...], q).astype(F32)
    xdt = (x * dt_e).astype(BF16)
    xdtd = (x * w_e).astype(BF16)
    bm = _pad_rows(b_ref[...], q)
    cm = _pad_rows(c_ref[...], q)

    acum_t = acum.T
    dec = jnp.exp(acum_t[0:SSD_HEADS, q - 1:q])
    dec = jnp.broadcast_to(dec, (SSD_HEADS, D_STATE))
    dec = jnp.broadcast_to(dec[:, None, :], (SSD_HEADS, SSD_HEAD_DIM, D_STATE))
    dec = dec.reshape(SSD_HEADS * SSD_HEAD_DIM, D_STATE)

    lane = lax.broadcasted_iota(jnp.int32, (q, LANES), 1)
    nt = (((1,), (1,)), ((), ()))
    tn = (((0,), (0,)), ((), ()))
    y_groups = []
    for g in range(SSD_GROUPS):
        gc = slice(g * GROUP_COLS, (g + 1) * GROUP_COLS)
        bg = bm[:, g * D_STATE:(g + 1) * D_STATE]
        cg = cm[:, g * D_STATE:(g + 1) * D_STATE]
        cb = lax.dot_general(cg, bg, nt, preferred_element_type=F32)
        sg = st_scr[gc, :]
        y_off = lax.dot_general(cg, sg.astype(BF16), nt, preferred_element_type=F32)
        pieces = []
        for pr in range(HEADS_PER_GROUP // 2):
            ms = []
            for hh in range(2):
                h = g * HEADS_PER_GROUP + 2 * pr + hh
                seg = jnp.broadcast_to(acum[:, h:h + 1], (q, q)) - acum_t[h:h + 1, :]
                lm = jnp.exp(jnp.where(causal, seg, -jnp.inf))
                ms.append((cb * lm).astype(BF16))
            m = jnp.concatenate(ms, axis=1)
            c0 = g * GROUP_COLS + pr * LANES
            xp = xdt[:, c0:c0 + LANES]
            rhs = jnp.concatenate([jnp.where(lane < SSD_HEAD_DIM, xp, jnp.zeros_like(xp)),
                                   jnp.where(lane >= SSD_HEAD_DIM, xp, jnp.zeros_like(xp))], axis=0)
            pieces.append(jnp.dot(m, rhs, preferred_element_type=F32))
        y_diag = jnp.concatenate(pieces, axis=1)
        y_groups.append(y_diag + y_off * ea_e[:, gc])
        new = lax.dot_general(xdtd[:, gc], bg, tn, preferred_element_type=F32)
        st_scr[gc, :] = sg * dec[gc, :] + new
    y = jnp.concatenate(y_groups, axis=1)
    y = y + dskip_ref[...] * x
    yz = y * _pad_rows(szs_ref[...], q).astype(F32)
    yn = yz * lax.rsqrt(jnp.mean(yz * yz, axis=-1, keepdims=True) + EPS) * nrm_ref[...]
    y_ref[...] = yn[0:q_in].astype(BF16)


def _ssd(act, dt, a_log, dskip_e, norm_ssd, e3, h0, *, n_seq, q_in, blocks_per_step):
    rows = act.shape[0]
    qb = q_in * blocks_per_step
    n_chunks = rows // (n_seq * qb)
    kern = functools.partial(_ssd_kernel, q_in=q_in, blocks_per_step=blocks_per_step, n_steps=n_chunks)
    rblk = lambda b, c: b * n_chunks + c
    bc_blk0 = 7 * D_MODEL // GROUP_COLS
    return pl.pallas_call(
        kern,
        grid=(n_seq, n_chunks),
        in_specs=[
            pl.BlockSpec((qb,D_MODEL), lambda b, c: (rblk(b, c), 0)),
            pl.BlockSpec((qb,D_MODEL), lambda b, c: (rblk(b, c), 1)),
            pl.BlockSpec((qb,GROUP_COLS), lambda b, c: (rblk(b, c), bc_blk0)),
            pl.BlockSpec((qb,GROUP_COLS), lambda b, c: (rblk(b, c), bc_blk0 + 1)),
            pl.BlockSpec((qb,LANES), lambda b, c: (rblk(b, c), 0)),
            pl.BlockSpec((1, LANES), lambda b, c: (0, 0)),
            pl.BlockSpec((1, D_MODEL), lambda b, c: (0, 0)),
            pl.BlockSpec((1, D_MODEL), lambda b, c: (0, 0)),
            pl.BlockSpec((LANES, D_MODEL), lambda b, c: (0, 0)),
            pl.BlockSpec((1, D_MODEL, D_STATE), lambda b, c: (b, 0, 0)),
        ],
        out_specs=[
            pl.BlockSpec((qb,D_MODEL), lambda b, c: (rblk(b, c), 0)),
            pl.BlockSpec((1, D_MODEL, D_STATE), lambda b, c: (b, 0, 0)),
        ],
        out_shape=[
            jax.ShapeDtypeStruct((rows, D_MODEL), BF16),
            jax.ShapeDtypeStruct((n_seq, D_MODEL, D_STATE), F32),
        ],
        scratch_shapes=[pltpu.VMEM((D_MODEL, D_STATE), F32)],
        compiler_params=pltpu.CompilerParams(
            dimension_semantics=("arbitrary", "arbitrary"), vmem_limit_bytes=VMEM_LIMIT),
        name="ssd",
    )(act, act, act, act, dt, a_log, dskip_e, norm_ssd, e3, h0)


def _mlp_stage(u_ref, v_ref, szm_ref, lng_ref, lnb_ref, wsp_ref, bsp_ref, ymlp_ref, vn_ref, *, tm, qm):
    v = v_ref[...].astype(F32)
    mu = jnp.mean(v, axis=-1, keepdims=True)
    vc = v - mu
    var = jnp.mean(vc * vc, axis=-1, keepdims=True)
    vn = vc * lax.rsqrt(var + EPS) * lng_ref[...] + lnb_ref[...]
    if vn_ref is not None:
        vn_ref[...] = vn
    vn_b = vn.astype(BF16)

    row = lax.broadcasted_iota(jnp.int32, (MLP_CHUNK, MLP_CHUNK), 0)
    col = lax.broadcasted_iota(jnp.int32, (MLP_CHUNK, MLP_CHUNK), 1)
    wms = [jnp.where(row >= col, wsp_ref[g], 0.0).astype(BF16) for g in range(MLP_GROUPS)]
    for ck in range(tm // qm):
        rows = slice(ck * qm, (ck + 1) * qm)
        vck = _pad_rows(vn_b[rows, :], MLP_CHUNK)
        mixed = jnp.concatenate(
            [jnp.dot(wms[g], vck[:, g * MLP_GROUP_DIM:(g + 1) * MLP_GROUP_DIM],
                     preferred_element_type=F32) for g in range(MLP_GROUPS)], axis=1)
        mixed = mixed[0:qm] + bsp_ref[...]
        y_mlp = u_ref[rows, :].astype(F32) * mixed * szm_ref[rows, :].astype(F32)
        ymlp_ref[rows, :] = y_mlp.astype(BF16)


def _mix_stage(yssd_ref, ymlp_ref, sgs_ref, sgm_ref, wbs_ref, wbm_ref, wo_ref):
    a = jnp.dot(yssd_ref[...], wbs_ref[...], preferred_element_type=F32)
    b = jnp.dot(ymlp_ref[...], wbm_ref[...], preferred_element_type=F32)
    merged = sgs_ref[...].astype(F32) * a + sgm_ref[...].astype(F32) * b
    return jnp.dot(merged.astype(BF16), wo_ref[...], preferred_element_type=F32)


def _out_stage(o, x_ref, gate_ref, gfin_ref, y_ref, *, tm, spt):
    rps = tm // spt
    for s in range(spt):
        rows = slice(s * rps, (s + 1) * rps)
        out = x_ref[rows, :] + gate_ref[s] * o[rows, :]
        y = out * lax.rsqrt(jnp.mean(out * out, axis=-1, keepdims=True) + EPS)
        y_ref[rows, :] = y * gfin_ref[...]


def _merge_kernel(yssd_ref, u_ref, v_ref, szm_ref, sgs_ref, sgm_ref, x_ref, gate_ref,
                  lng_ref, lnb_ref, gfin_ref, wsp_ref, bsp_ref, wbs_ref, wbm_ref, wo_ref,
                  y_ref, *rest, tm, qm, spt, emit_vn):
    if emit_vn:
        vn_ref, ymlp_scr = rest
    else:
        vn_ref, (ymlp_scr,) = None, rest
    _mlp_stage(u_ref, v_ref, szm_ref, lng_ref, lnb_ref, wsp_ref, bsp_ref, ymlp_scr, vn_ref, tm=tm, qm=qm)
    o = _mix_stage(yssd_ref, ymlp_scr, sgs_ref, sgm_ref, wbs_ref, wbm_ref, wo_ref)
    _out_stage(o, x_ref, gate_ref, gfin_ref, y_ref, tm=tm, spt=spt)


def _merge(yssd, act, x2d, gate, ln_g, ln_b, g_final, w_spatial, bsp_e, wbs, wbm, wo,
           *, tm, qm, spt, emit_vn):
    rows = x2d.shape[0]
    n_seq = gate.shape[0]
    n_tiles = rows // tm
    tiles_per_seq = max(1, n_tiles // n_seq) if spt == 1 else 1
    seq_blk = (lambda i: i // tiles_per_seq) if spt == 1 else (lambda i: i)
    kern = functools.partial(_merge_kernel, tm=tm, qm=qm, spt=spt, emit_vn=emit_vn)
    const = lambda shape: pl.BlockSpec(shape, lambda i: (0,) * len(shape),
                                       pipeline_mode=pl.Buffered(1))
    seg = lambda k: pl.BlockSpec((tm, D_MODEL), lambda i: (i, k))
    out_specs = [pl.BlockSpec((tm, D_MODEL), lambda i: (i, 0))]
    out_shape = [jax.ShapeDtypeStruct((rows, D_MODEL), F32)]
    if emit_vn:
        out_specs.append(pl.BlockSpec((tm, D_MODEL), lambda i: (i, 0)))
        out_shape.append(jax.ShapeDtypeStruct((rows, D_MODEL), F32))
    return pl.pallas_call(
        kern,
        grid=(n_tiles,),
        in_specs=[
            pl.BlockSpec((tm, D_MODEL), lambda i: (i, 0)),
            seg(2), seg(3), seg(4), seg(5), seg(6),
            pl.BlockSpec((tm, D_MODEL), lambda i: (i, 0)),
            pl.BlockSpec((spt, 1, D_MODEL), lambda i: (seq_blk(i), 0, 0)),
            const((1, D_MODEL)), const((1, D_MODEL)), const((1, D_MODEL)),
            const((MLP_GROUPS, MLP_CHUNK, MLP_CHUNK)),
            const((qm, D_MODEL)),
            const((D_MODEL, D_MODEL)), const((D_MODEL, D_MODEL)), const((D_MODEL, D_MODEL)),
        ],
        out_specs=out_specs,
        out_shape=out_shape,
        scratch_shapes=[pltpu.VMEM((tm, D_MODEL), BF16)],
        compiler_params=pltpu.CompilerParams(
            dimension_semantics=("arbitrary",), vmem_limit_bytes=VMEM_LIMIT),
        name="merge",
    )(yssd, act, act, act, act, act, x2d, gate, ln_g, ln_b, g_final, w_spatial, bsp_e, wbs, wbm, wo)


def _cast_kernel(a_ref, b_ref, c_ref, ao_ref, bo_ref, co_ref):
    ao_ref[...] = a_ref[...].astype(BF16)
    bo_ref[...] = b_ref[...].astype(BF16)
    co_ref[...] = c_ref[...].astype(BF16)


def _cast_bf16(a, b, c, *, tm=512):
    rows, cols = a.shape
    spec = pl.BlockSpec((tm, cols), lambda i: (i, 0))
    return pl.pallas_call(
        _cast_kernel,
        grid=(rows // tm,),
        in_specs=[spec] * 3,
        out_specs=[spec] * 3,
        out_shape=[jax.ShapeDtypeStruct((rows, cols), BF16)] * 3,
        compiler_params=pltpu.CompilerParams(
            dimension_semantics=("arbitrary",), vmem_limit_bytes=VMEM_LIMIT),
        name="cast",
    )(a, b, c)


def _expansion_matrix():
    e = np.zeros((LANES, D_MODEL), np.float32)
    cols = np.arange(D_MODEL)
    for piece in range(3):
        e[piece * SSD_HEADS + cols // SSD_HEAD_DIM, cols] = 1.0
    return jnp.asarray(e, BF16)


def kernel(x_prompt, x_sample, state_ssm, cache_conv, c_prompt, c_sample, w_ada, b_ada, g_pre, w_in,
           conv_w, conv_b, dt_bias, a_log, d_skip, norm_ssd, ln_g, ln_b, w_spatial, b_spatial,
           w_bp_ssd, w_bp_mlp, w_o, g_final):
    depth = w_ada.shape[0]
    assert depth == 1, "single-layer trunk only"
    bp, seq, d = x_prompt.shape
    bs, dec_seq, _ = x_sample.shape
    assert d == D_MODEL and seq % 1024 == 0 and dec_seq % 16 == 0 and dec_seq <= SSD_BLOCK

    w_t = jnp.swapaxes(w_in[0], 0, 1)
    w_dt = jnp.pad(w_t[DT_OFFSET:DT_OFFSET + SSD_HEADS], ((0, LANES - SSD_HEADS), (0, 0))).astype(BF16)
    pad_heads = lambda v: jnp.pad(v.reshape(1, SSD_HEADS), ((0, 0), (0, LANES - SSD_HEADS)))
    wts = dict(
        g_pre=g_pre[0].reshape(1, d), w_t=w_t, w_dt=w_dt, dt_bias=pad_heads(dt_bias[0]),
        conv_w=conv_w[0], conv_b=conv_b[0].reshape(1, CONV_DIM),
        a_log=pad_heads(a_log[0]), dskip_e=jnp.repeat(d_skip[0], SSD_HEAD_DIM).reshape(1, d),
        norm_ssd=norm_ssd[0].reshape(1, d), e3=_expansion_matrix(),
        ln_g=ln_g[0].reshape(1, d), ln_b=ln_b[0].reshape(1, d), g_final=g_final.reshape(1, d),
        w_spatial=w_spatial[0], bsp_e=jnp.repeat(b_spatial[0].T, MLP_GROUP_DIM, axis=1),
    )

    c_all = jnp.concatenate([c_prompt, c_sample], axis=0)
    c_pad = jnp.pad(c_all, ((0, 16 - (bp + bs)), (0, 0)))
    mods = _mods(c_pad, w_ada[0], b_ada[0].reshape(1, 3 * d))

    split_mods = lambda m: (m[:, k * d:(k + 1) * d].reshape(m.shape[0], 1, d) for k in range(3))
    shift_p, scale_p, gate_p = split_mods(mods[:bp])
    shift_s, scale_s, gate_s = split_mods(mods[bp:bp + bs])
    x2p = x_prompt.reshape(bp * seq, d)
    x2s = x_sample.reshape(bs * dec_seq, d)
    rows_s = bs * dec_seq

    hist_p = jnp.zeros((bp, SUBLANES, CONV_DIM), F32)
    h0_p = jnp.zeros((bp, d, D_STATE), F32)
    hist_s = jnp.pad(cache_conv[0], ((0, 0), (SUBLANES - (CONV_W - 1), 0), (0, 0)))
    h0_s = state_ssm[0].reshape(bs, d, D_STATE)

    norm_w = (wts["g_pre"], wts["w_dt"], wts["dt_bias"])
    hp, dt_p = _norm(x2p, scale_p, shift_p, *norm_w, tm=NORM_TM, spt=1)
    hs, dt_s = _norm(x2s, scale_s, shift_s, *norm_w, tm=rows_s, spt=bs)
    act_p, act_s, tail_p, tail_s = _proj(hp, hs, wts["w_t"], wts["conv_w"], wts["conv_b"], hist_p, hist_s,
                                         tm_p=PROJ_TM, nsplit=PROJ_NSPLIT, msplit_p=PROJ_MSPLIT)

    ssd_w = (wts["a_log"], wts["dskip_e"], wts["norm_ssd"], wts["e3"])
    yssd_p, ssm_p = _ssd(act_p, dt_p, *ssd_w, h0_p, n_seq=bp, q_in=SSD_BLOCK,
                         blocks_per_step=SSD_BLOCKS_PER_STEP)
    yssd_s, ssm_s = _ssd(act_s, dt_s, *ssd_w, h0_s, n_seq=bs, q_in=dec_seq, blocks_per_step=1)

    merge_w = (wts["ln_g"], wts["ln_b"], wts["g_final"], wts["w_spatial"])
    proj_w = _cast_bf16(w_bp_ssd[0], w_bp_mlp[0], w_o[0])
    (yp,) = _merge(yssd_p, act_p, x2p, gate_p, *merge_w, wts["bsp_e"], *proj_w,
                   tm=MERGE_TM, qm=MLP_CHUNK, spt=1, emit_vn=False)
    ys, vn_s = _merge(yssd_s, act_s, x2s, gate_s, *merge_w, wts["bsp_e"][:dec_seq], *proj_w,
                      tm=rows_s, qm=dec_seq, spt=bs, emit_vn=True)

    conv_cols = slice(CONV_STEPS[0] * PROJ_TN, CONV_STEPS[0] * PROJ_TN + CONV_DIM)
    hist_rows = slice(SUBLANES - (CONV_W - 1), SUBLANES)
    tiles_per_seq = seq // PROJ_TM
    conv_p = tail_p[tiles_per_seq - 1::tiles_per_seq, hist_rows, conv_cols]
    conv_s = tail_s[:, hist_rows, conv_cols]

    state_shape = (SSD_HEADS, SSD_HEAD_DIM, D_STATE)
    return (yp.reshape(bp, seq, d), ys.reshape(bs, dec_seq, d),
            ssm_p.reshape(1, bp, *state_shape), conv_p[None],
            ssm_s.reshape(1, bs, *state_shape), conv_s[None], vn_s.reshape(1, bs, dec_seq, d))
```

```python
import functools
from typing import Any, NamedTuple

import numpy as np
import jax
import jax.numpy as jnp
from jax import lax
from jax.experimental import pallas as pl
from jax.experimental.pallas import tpu as pltpu

F32 = jnp.float32
BF16 = jnp.bfloat16

D_MODEL = 2048
SSD_HEADS = 32
SSD_HEAD_DIM = 64
SSD_GROUPS = 4
HEADS_PER_GROUP = SSD_HEADS // SSD_GROUPS
D_STATE = 128
GROUP_COLS = HEADS_PER_GROUP * SSD_HEAD_DIM
CONV_W = 4
CONV_DIM = D_MODEL + 2 * SSD_GROUPS * D_STATE
MLP_CHUNK = 128
MLP_GROUPS = 8
MLP_GROUP_DIM = D_MODEL // MLP_GROUPS
EPS = 1e-5

SUBLANES = 8
LANES = 128
SSD_BLOCK = 128
SSD_BLOCKS_PER_STEP = 8
PROJ_TN = 1024
VMEM_LIMIT = 56 * 1024 * 1024

N_ACT = 7 * D_MODEL + 2 * SSD_GROUPS * D_STATE
N_PROJ_TILES = N_ACT // PROJ_TN
DT_OFFSET = D_MODEL + CONV_DIM
FIRST_SHIFTED_TILE = DT_OFFSET // PROJ_TN
CONV_STEPS = (2, 3, 4)
IDENT_STEPS = (5, 6, 7, 8)
FIRST_SIGMOID_STEP = 11
PROJ_TM = 1024
PROJ_NSPLIT = 2
PROJ_MSPLIT = 4
MERGE_TM = 256


def _silu(x):
    hx = 0.5 * x
    return hx + hx * jnp.tanh(hx)


def _softplus(x):
    return jnp.maximum(x, 0.0) + jnp.log1p(jnp.exp(-jnp.abs(x)))


def _mods_kernel(c_ref, w_ref, b_ref, o_ref):
    s = _silu(c_ref[...]).astype(BF16)
    o_ref[...] = jnp.dot(s, w_ref[...].astype(BF16), preferred_element_type=F32) + b_ref[...]


def _mods(c_pad, w_ada, b_ada):
    rows, d = c_pad.shape
    n = w_ada.shape[1]
    tn = 1024
    return pl.pallas_call(
        _mods_kernel,
        grid=(n // tn,),
        in_specs=[
            pl.BlockSpec((rows, d), lambda j: (0, 0)),
            pl.BlockSpec((d, tn), lambda j: (0, j)),
            pl.BlockSpec((1, tn), lambda j: (0, j)),
        ],
        out_specs=pl.BlockSpec((rows, tn), lambda j: (0, j)),
        out_shape=jax.ShapeDtypeStruct((rows, n), F32),
        compiler_params=pltpu.CompilerParams(
            dimension_semantics=("arbitrary",), vmem_limit_bytes=VMEM_LIMIT),
        name="mods",
    )(c_pad, w_ada, b_ada)


def _act_tile(j):
    return jnp.where(j < CONV_STEPS[2], j, jnp.where(j == CONV_STEPS[2], N_PROJ_TILES - 1, j - 1))


def _wt_row(j):
    return pl.multiple_of(j * PROJ_TN + jnp.where(j >= FIRST_SHIFTED_TILE, SSD_HEADS, 0), SSD_HEADS)


def _conv_in_tile(j):
    return jnp.clip(j - CONV_STEPS[0], 0, len(CONV_STEPS) - 1)


class _Rows(NamedTuple):
    h_ref: Any
    act_ref: Any
    tail_ref: Any
    hist_ref: Any
    tm: int
    spt: int
    msplit: int
    first: Any


def _proj_step(j, row_sets, cw_ref, cb_ref, wbf_scr, halo_scr, stgs, *, nsplit):
    cw = PROJ_TN // nsplit
    blocks = [(rs, nb, mb) for nb in range(nsplit) for rs in row_sets for mb in range(rs.msplit)]

    def dot_block(k):
        rs, nb, mb = blocks[k]
        rb = rs.tm // rs.msplit
        stgs[k % 2][SUBLANES:SUBLANES + rb, :] = lax.dot_general(
            rs.h_ref[mb * rb:(mb + 1) * rb, :], wbf_scr[nb * cw:(nb + 1) * cw, :],
            (((1,), (1,)), ((), ())), preferred_element_type=F32)

    def run(epilogue):
        dot_block(0)
        for k, (rs, nb, mb) in enumerate(blocks):
            if k + 1 < len(blocks):
                dot_block(k + 1)
            epilogue(rs, nb, mb, stgs[k % 2])

    is_conv = jnp.logical_and(j >= CONV_STEPS[0], j <= CONV_STEPS[2])
    is_ident = jnp.logical_and(j >= IDENT_STEPS[0], j <= IDENT_STEPS[-1])
    is_sigmoid = j >= FIRST_SIGMOID_STEP

    p_x = jnp.where(is_ident, 1.0, 0.5).astype(F32)
    p_xt = jnp.where(is_ident, 0.0, 0.5).astype(F32)

    def elementwise(rs, nb, mb, stg):
        rb = rs.tm // rs.msplit
        x = stg[SUBLANES:SUBLANES + rb, :]
        t = jnp.tanh(0.5 * x)
        y = jnp.where(is_sigmoid, 0.5 + 0.5 * t, x * (p_x + p_xt * t))
        rs.act_ref[mb * rb:(mb + 1) * rb, nb * cw:(nb + 1) * cw] = y.astype(BF16)

    if isinstance(j, int):
        assert not CONV_STEPS[0] <= j <= CONV_STEPS[2]
        run(elementwise)
        return

    @pl.when(jnp.logical_not(is_conv))
    def _():
        for rs in row_sets:
            rs.tail_ref[...] = jnp.zeros_like(rs.tail_ref)
        run(elementwise)

    @pl.when(is_conv)
    def _():
        for rs in row_sets:
            if rs.spt == 1:
                halo_scr[...] = jnp.where(rs.first, rs.hist_ref[0], halo_scr[...])

        def conv(ext, cols):
            w0, w1, w2, w3 = (0.5 * cw_ref[k:k + 1, cols] for k in range(CONV_W))
            t = ext * w0
            t = pltpu.roll(t, 1, axis=0) + ext * w1
            t = pltpu.roll(t, 1, axis=0) + ext * w2
            t = pltpu.roll(t, 1, axis=0) + ext * w3 + 0.5 * cb_ref[:, cols]
            t = t[SUBLANES:, :]
            return (t + t * jnp.tanh(t)).astype(BF16)

        def ep(rs, nb, mb, stg):
            cols = slice(nb * cw, (nb + 1) * cw)
            rb = rs.tm // rs.msplit
            if rs.spt == 1:
                stg[0:SUBLANES, :] = halo_scr[:, cols]
                rs.act_ref[mb * rb:(mb + 1) * rb, cols] = conv(stg[0:SUBLANES + rb, :], cols)
                last = stg[rb:rb + SUBLANES, :]
                halo_scr[:, cols] = last
                rs.tail_ref[0, :, cols] = last
            else:
                rps = rs.tm // rs.spt
                for s in range(rb // rps):
                    rows = slice(SUBLANES + s * rps, SUBLANES + (s + 1) * rps)
                    seq = mb * (rb // rps) + s
                    ext = jnp.concatenate([rs.hist_ref[seq, :, cols], stg[rows, :]], axis=0)
                    rs.act_ref[mb * rb + s * rps:mb * rb + (s + 1) * rps, cols] = conv(ext, cols)
                    rs.tail_ref[seq, :, cols] = stg[rows.stop - SUBLANES:rows.stop, :]

        run(ep)


def _cast_weight_tile(wt_ref, wbf_scr):
    wrows = 256
    for r in range(PROJ_TN // wrows):
        rows = slice(r * wrows, (r + 1) * wrows)
        wbf_scr[rows, :] = wt_ref[rows, :].astype(BF16)


def _norm_rows(x_ref, sc_ref, sh_ref, gpre_ref, wdt_ref, dtb_ref, h_ref, dt_ref, *, tm, spt):
    rps = tm // spt
    chunk = min(rps, 256)
    for s in range(spt):
        for c in range(rps // chunk):
            rows = slice(s * rps + c * chunk, s * rps + (c + 1) * chunk)
            x = x_ref[rows, :]
            y = x * lax.rsqrt(jnp.mean(x * x, axis=-1, keepdims=True) + EPS)
            y = y * gpre_ref[...]
            h_ref[rows, :] = (y * (1.0 + sc_ref[s]) + sh_ref[s]).astype(BF16)
    dt_raw = lax.dot_general(h_ref[...], wdt_ref[...], (((1,), (1,)), ((), ())),
                             preferred_element_type=F32) + dtb_ref[...]
    lane = lax.broadcasted_iota(jnp.int32, dt_raw.shape, 1)
    dt_ref[...] = jnp.where(lane < SSD_HEADS, _softplus(dt_raw), 0.0)


def _proj_first_kernel(xp_ref, xs_ref, scp_ref, shp_ref, scs_ref, shs_ref, gpre_ref, wdt_ref, dtb_ref,
                       wt_ref, hp_ref, hs_ref, dtp_ref, dts_ref, actp_ref, acts_ref,
                       wbf_scr, stg0_scr, stg1_scr, *, n_p, tm_p, tm_s, spt_s, nsplit, msplit_p):
    i = pl.program_id(0)

    @pl.when(i == 0)
    def _():
        _cast_weight_tile(wt_ref, wbf_scr)

    norm_w = (gpre_ref, wdt_ref, dtb_ref)
    _norm_rows(xp_ref, scp_ref, shp_ref, *norm_w, hp_ref, dtp_ref, tm=tm_p, spt=1)
    shared = (None, None, wbf_scr, None, (stg0_scr, stg1_scr))
    prompt = _Rows(hp_ref, actp_ref, None, None, tm=tm_p, spt=1, msplit=msplit_p, first=None)

    @pl.when(i < n_p - 1)
    def _():
        _proj_step(0, [prompt], *shared, nsplit=nsplit)

    @pl.when(i == n_p - 1)
    def _():
        _norm_rows(xs_ref, scs_ref, shs_ref, *norm_w, hs_ref, dts_ref, tm=tm_s, spt=spt_s)
        sample = _Rows(hs_ref, acts_ref, None, None, tm=tm_s, spt=spt_s, msplit=1, first=None)
        _proj_step(0, [prompt, sample], *shared, nsplit=nsplit)


def _proj_first(xp, xs, scale_p, shift_p, scale_s, shift_s, g_pre, w_dt, dt_bias, w_t,
                *, tm_p, nsplit, msplit_p):
    rows_p, rows_s = xp.shape[0], xs.shape[0]
    n_p = rows_p // tm_p
    tiles_per_seq = n_p // scale_p.shape[0]
    spt_s = scale_s.shape[0]
    rb = max(tm_p // msplit_p, rows_s)
    kern = functools.partial(_proj_first_kernel, n_p=n_p, tm_p=tm_p, tm_s=rows_s, spt_s=spt_s,
                             nsplit=nsplit, msplit_p=msplit_p)
    const = lambda shape: pl.BlockSpec(shape, lambda i: (0,) * len(shape))
    seq_spec = pl.BlockSpec((1, 1, D_MODEL), lambda i: (i // tiles_per_seq, 0, 0))
    return pl.pallas_call(
        kern,
        grid=(n_p,),
        in_specs=[
            pl.BlockSpec((tm_p, D_MODEL), lambda i: (i, 0)),
            const((rows_s, D_MODEL)),
            seq_spec, seq_spec,
            const((spt_s, 1, D_MODEL)), const((spt_s, 1, D_MODEL)),
            const((1, D_MODEL)), const((LANES, D_MODEL)), const((1, LANES)),
            pl.BlockSpec((pl.Element(PROJ_TN), pl.Element(D_MODEL)), lambda i: (0, 0),
                         pipeline_mode=pl.Buffered(1)),
        ],
        out_specs=[
            pl.BlockSpec((tm_p, D_MODEL), lambda i: (i, 0)),
            const((rows_s, D_MODEL)),
            pl.BlockSpec((tm_p, LANES), lambda i: (i, 0)),
            const((rows_s, LANES)),
            pl.BlockSpec((tm_p, PROJ_TN), lambda i: (i, 0)),
            const((rows_s, PROJ_TN)),
        ],
        out_shape=[
            jax.ShapeDtypeStruct((rows_p, D_MODEL), BF16),
            jax.ShapeDtypeStruct((rows_s, D_MODEL), BF16),
            jax.ShapeDtypeStruct((rows_p, LANES), F32),
            jax.ShapeDtypeStruct((rows_s, LANES), F32),
            jax.ShapeDtypeStruct((rows_p, N_ACT), BF16),
            jax.ShapeDtypeStruct((rows_s, N_ACT), BF16),
        ],
        scratch_shapes=[
            pltpu.VMEM((PROJ_TN, D_MODEL), BF16),
            pltpu.VMEM((rb + SUBLANES, PROJ_TN // nsplit), F32),
            pltpu.VMEM((rb + SUBLANES, PROJ_TN // nsplit), F32),
        ],
        compiler_params=pltpu.CompilerParams(
            dimension_semantics=("arbitrary",), vmem_limit_bytes=VMEM_LIMIT),
        name="proj_first",
    )(xp, xs, scale_p, shift_p, scale_s, shift_s, g_pre, w_dt, dt_bias, w_t)


def _proj_kernel(hp_ref, hs_ref, wt_ref, cw_ref, cb_ref, histp_ref, hists_ref, actp_in, acts_in,
                 actp_ref, acts_ref, tailp_ref, tails_ref, wbf_scr, halo_scr, stg0_scr, stg1_scr,
                 *, n_p, tm_p, tm_s, spt_s, nsplit, msplit_p, tiles_per_seq):
    del actp_in, acts_in
    j = pl.program_id(0) + 1
    i = pl.program_id(1)

    @pl.when(i == 0)
    def _():
        _cast_weight_tile(wt_ref, wbf_scr)

    shared = (cw_ref, cb_ref, wbf_scr, halo_scr, (stg0_scr, stg1_scr))
    prompt = _Rows(hp_ref, actp_ref, tailp_ref, histp_ref, tm=tm_p, spt=1, msplit=msplit_p,
                   first=(i % tiles_per_seq) == 0)
    sample = _Rows(hs_ref, acts_ref, tails_ref, hists_ref, tm=tm_s, spt=spt_s, msplit=1, first=None)

    @pl.when(i < n_p - 1)
    def _():
        _proj_step(j, [prompt], *shared, nsplit=nsplit)

    @pl.when(i == n_p - 1)
    def _():
        _proj_step(j, [prompt, sample], *shared, nsplit=nsplit)


def _proj(hp, hs, w_t, conv_w, conv_b, hist_p, hist_s, act_p, act_s, *, tm_p, nsplit, msplit_p):
    rows_p, rows_s = hp.shape[0], hs.shape[0]
    n_p = rows_p // tm_p
    tiles_per_seq = n_p // hist_p.shape[0]
    spt_s = hist_s.shape[0]
    rb = max(tm_p // msplit_p, rows_s)
    n_steps = N_PROJ_TILES - 1
    kern = functools.partial(_proj_kernel, n_p=n_p, tm_p=tm_p, tm_s=rows_s, spt_s=spt_s,
                             nsplit=nsplit, msplit_p=msplit_p, tiles_per_seq=tiles_per_seq)
    step = lambda g: g + 1
    return pl.pallas_call(
        kern,
        grid=(n_steps, n_p),
        in_specs=[
            pl.BlockSpec((tm_p, D_MODEL), lambda g, i: (i, 0)),
            pl.BlockSpec((rows_s, D_MODEL), lambda g, i: (0, 0)),
            pl.BlockSpec((pl.Element(PROJ_TN), pl.Element(D_MODEL)), lambda g, i: (_wt_row(step(g)), 0)),
            pl.BlockSpec((CONV_W, PROJ_TN), lambda g, i: (0, _conv_in_tile(step(g)))),
            pl.BlockSpec((1, PROJ_TN), lambda g, i: (0, _conv_in_tile(step(g)))),
            pl.BlockSpec((1, SUBLANES, PROJ_TN),
                         lambda g, i: (i // tiles_per_seq, 0, _conv_in_tile(step(g)))),
            pl.BlockSpec((spt_s, SUBLANES, PROJ_TN), lambda g, i: (0, 0, _conv_in_tile(step(g)))),
            pl.BlockSpec(memory_space=pl.ANY),
            pl.BlockSpec(memory_space=pl.ANY),
        ],
        out_specs=[
            pl.BlockSpec((tm_p, PROJ_TN), lambda g, i: (i, _act_tile(step(g)))),
            pl.BlockSpec((rows_s, PROJ_TN), lambda g, i: (0, _act_tile(step(g)))),
            pl.BlockSpec((1, SUBLANES, PROJ_TN), lambda g, i: (i, 0, g)),
            pl.BlockSpec((spt_s, SUBLANES, PROJ_TN), lambda g, i: (0, 0, g)),
        ],
        out_shape=[
            jax.ShapeDtypeStruct((rows_p, N_ACT), BF16),
            jax.ShapeDtypeStruct((rows_s, N_ACT), BF16),
            jax.ShapeDtypeStruct((n_p, SUBLANES, n_steps * PROJ_TN), F32),
            jax.ShapeDtypeStruct((spt_s, SUBLANES, n_steps * PROJ_TN), F32),
        ],
        input_output_aliases={7: 0, 8: 1},
        scratch_shapes=[
            pltpu.VMEM((PROJ_TN, D_MODEL), BF16),
            pltpu.VMEM((SUBLANES, PROJ_TN), F32),
            pltpu.VMEM((rb + SUBLANES, PROJ_TN // nsplit), F32),
            pltpu.VMEM((rb + SUBLANES, PROJ_TN // nsplit), F32),
        ],
        compiler_params=pltpu.CompilerParams(
            dimension_semantics=("arbitrary", "arbitrary"), vmem_limit_bytes=VMEM_LIMIT),
        name="proj",
    )(hp, hs, w_t, conv_w, conv_b, hist_p, hist_s, act_p, act_s)


def _pad_rows(v, rows):
    if v.shape[0] == rows:
        return v
    return jnp.concatenate([v, jnp.zeros((rows - v.shape[0], v.shape[1]), v.dtype)], axis=0)


def _ssd_kernel(szs_ref, xs_ref, b_ref, c_ref, dt_ref, alog_ref, dskip_ref, nrm_ref, e3_ref, h0_ref,
                y_ref, hout_ref, st_scr, *, q_in, blocks_per_step, n_steps):
    ci = pl.program_id(1)

    @pl.when(ci == 0)
    def _():
        st_scr[...] = h0_ref[0]

    for k in range(blocks_per_step):
        rs = pl.ds(k * q_in, q_in)
        _ssd_block(szs_ref.at[rs], xs_ref.at[rs], b_ref.at[rs], c_ref.at[rs], dt_ref.at[rs],
                   alog_ref, dskip_ref, nrm_ref, e3_ref, y_ref.at[rs], st_scr, q_in=q_in)

    @pl.when(ci == n_steps - 1)
    def _():
        hout_ref[0] = st_scr[...]


def _ssd_block(szs_ref, xs_ref, b_ref, c_ref, dt_ref, alog_ref, dskip_ref, nrm_ref, e3_ref,
               y_ref, st_scr, *, q_in):
    q = SSD_BLOCK
    dt = _pad_rows(dt_ref[...], q)
    a = -jnp.exp(alog_ref[...])
    adt = dt * a
    row = lax.broadcasted_iota(jnp.int32, (q, q), 0)
    col = lax.broadcasted_iota(jnp.int32, (q, q), 1)
    causal = row >= col
    acum = jnp.dot(causal.astype(F32), adt, precision=lax.Precision.HIGHEST,
                   preferred_element_type=F32)
    alast = acum[q - 1:q, :]
    dte = jnp.exp(alast - acum)
    ea = jnp.exp(acum)

    stack = jnp.concatenate([dt, dt * dte, ea], axis=0)
    lane3 = lax.broadcasted_iota(jnp.int32, stack.shape, 1)
    stack = jnp.where(lane3 < SSD_HEADS, stack, 0.0)
    hi = stack.astype(BF16).astype(F32)
    r1 = stack - hi
    mid = r1.astype(BF16).astype(F32)
    lo = (r1 - mid).astype(BF16).astype(F32)
    comb = hi + pltpu.roll(mid, SSD_HEADS, axis=1) + pltpu.roll(lo, 2 * SSD_HEADS, axis=1)
    expd = jnp.dot(comb.astype(BF16), e3_ref[...], preferred_element_type=F32)
    dt_e, w_e, ea_e = expd[0:q], expd[q:2 * q], expd[2 * q:3 * q]

    x = _pad_rows(xs_ref[...], q).astype(F32)
    xdt = (x * dt_e).astype(BF16)
    xdtd = (x * w_e).astype(BF16)
    bm = _pad_rows(b_ref[...], q)
    cm = _pad_rows(c_ref[...], q)

    acum_t = acum.T
    dec = jnp.exp(acum_t[0:SSD_HEADS, q - 1:q])
    dec = jnp.broadcast_to(dec, (SSD_HEADS, D_STATE))
    dec = jnp.broadcast_to(dec[:, None, :], (SSD_HEADS, SSD_HEAD_DIM, D_STATE))
    dec = dec.reshape(SSD_HEADS * SSD_HEAD_DIM, D_STATE)

    lane = lax.broadcasted_iota(jnp.int32, (q, LANES), 1)
    nt = (((1,), (1,)), ((), ()))
    tn = (((0,), (0,)), ((), ()))
    y_groups = []
    for g in range(SSD_GROUPS):
        gc = slice(g * GROUP_COLS, (g + 1) * GROUP_COLS)
        bg = bm[:, g * D_STATE:(g + 1) * D_STATE]
        cg = cm[:, g * D_STATE:(g + 1) * D_STATE]
        cb = lax.dot_general(cg, bg, nt, preferred_element_type=F32)
        sg = st_scr[gc, :]
        y_off = lax.dot_general(cg, sg.astype(BF16), nt, preferred_element_type=F32)
        pieces = []
        for pr in range(HEADS_PER_GROUP // 2):
            ms = []
            for hh in range(2):
                h = g * HEADS_PER_GROUP + 2 * pr + hh
                seg = jnp.broadcast_to(acum[:, h:h + 1], (q, q)) - acum_t[h:h + 1, :]
                lm = jnp.exp(jnp.where(causal, seg, -jnp.inf))
                ms.append((cb * lm).astype(BF16))
            m = jnp.concatenate(ms, axis=1)
            c0 = g * GROUP_COLS + pr * LANES
            xp = xdt[:, c0:c0 + LANES]
            rhs = jnp.concatenate([jnp.where(lane < SSD_HEAD_DIM, xp, jnp.zeros_like(xp)),
                                   jnp.where(lane >= SSD_HEAD_DIM, xp, jnp.zeros_like(xp))], axis=0)
            pieces.append(jnp.dot(m, rhs, preferred_element_type=F32))
        y_diag = jnp.concatenate(pieces, axis=1)
        y_groups.append(y_diag + y_off * ea_e[:, gc])
        new = lax.dot_general(xdtd[:, gc], bg, tn, preferred_element_type=F32)
        st_scr[gc, :] = sg * dec[gc, :] + new
    y = jnp.concatenate(y_groups, axis=1)
    y = y + dskip_ref[...] * x
    yz = y * _pad_rows(szs_ref[...], q).astype(F32)
    yn = yz * lax.rsqrt(jnp.mean(yz * yz, axis=-1, keepdims=True) + EPS) * nrm_ref[...]
    y_ref[...] = yn[0:q_in].astype(BF16)


def _ssd(act, dt, a_log, dskip_e, norm_ssd, e3, h0, *, n_seq, q_in, blocks_per_step):
    rows = act.shape[0]
    qb = q_in * blocks_per_step
    n_chunks = rows // (n_seq * qb)
    kern = functools.partial(_ssd_kernel, q_in=q_in, blocks_per_step=blocks_per_step, n_steps=n_chunks)
    rblk = lambda b, c: b * n_chunks + c
    bc_blk0 = 7 * D_MODEL // GROUP_COLS
    return pl.pallas_call(
        kern,
        grid=(n_seq, n_chunks),
        in_specs=[
            pl.BlockSpec((qb,D_MODEL), lambda b, c: (rblk(b, c), 0)),
            pl.BlockSpec((qb,D_MODEL), lambda b, c: (rblk(b, c), 1)),
            pl.BlockSpec((qb,GROUP_COLS), lambda b, c: (rblk(b, c), bc_blk0)),
            pl.BlockSpec((qb,GROUP_COLS), lambda b, c: (rblk(b, c), bc_blk0 + 1)),
            pl.BlockSpec((qb,LANES), lambda b, c: (rblk(b, c), 0)),
            pl.BlockSpec((1, LANES), lambda b, c: (0, 0)),
            pl.BlockSpec((1, D_MODEL), lambda b, c: (0, 0)),
            pl.BlockSpec((1, D_MODEL), lambda b, c: (0, 0)),
            pl.BlockSpec((LANES, D_MODEL), lambda b, c: (0, 0)),
            pl.BlockSpec((1, D_MODEL, D_STATE), lambda b, c: (b, 0, 0)),
        ],
        out_specs=[
            pl.BlockSpec((qb,D_MODEL), lambda b, c: (rblk(b, c), 0)),
            pl.BlockSpec((1, D_MODEL, D_STATE), lambda b, c: (b, 0, 0)),
        ],
        out_shape=[
            jax.ShapeDtypeStruct((rows, D_MODEL), BF16),
            jax.ShapeDtypeStruct((n_seq, D_MODEL, D_STATE), F32),
        ],
        scratch_shapes=[pltpu.VMEM((D_MODEL, D_STATE), F32)],
        compiler_params=pltpu.CompilerParams(
            dimension_semantics=("arbitrary", "arbitrary"), vmem_limit_bytes=VMEM_LIMIT),
        name="ssd",
    )(act, act, act, act, dt, a_log, dskip_e, norm_ssd, e3, h0)


def _mlp_stage(u_ref, v_ref, szm_ref, lng_ref, lnb_ref, wsp_ref, bsp_ref, ymlp_ref, vn_ref, *, tm, qm):
    v = v_ref[...].astype(F32)
    mu = jnp.mean(v, axis=-1, keepdims=True)
    vc = v - mu
    var = jnp.mean(vc * vc, axis=-1, keepdims=True)
    vn = vc * lax.rsqrt(var + EPS) * lng_ref[...] + lnb_ref[...]
    if vn_ref is not None:
        vn_ref[...] = vn
    vn_b = vn.astype(BF16)

    row = lax.broadcasted_iota(jnp.int32, (MLP_CHUNK, MLP_CHUNK), 0)
    col = lax.broadcasted_iota(jnp.int32, (MLP_CHUNK, MLP_CHUNK), 1)
    wms = [jnp.where(row >= col, wsp_ref[g], 0.0).astype(BF16) for g in range(MLP_GROUPS)]
    for ck in range(tm // qm):
        rows = slice(ck * qm, (ck + 1) * qm)
        vck = _pad_rows(vn_b[rows, :], MLP_CHUNK)
        mixed = jnp.concatenate(
            [jnp.dot(wms[g], vck[:, g * MLP_GROUP_DIM:(g + 1) * MLP_GROUP_DIM],
                     preferred_element_type=F32) for g in range(MLP_GROUPS)], axis=1)
        mixed = mixed[0:qm] + bsp_ref[...]
        y_mlp = u_ref[rows, :].astype(F32) * mixed * szm_ref[rows, :].astype(F32)
        ymlp_ref[rows, :] = y_mlp.astype(BF16)


def _mix_stage(yssd_ref, ymlp_ref, sgs_ref, sgm_ref, wbs_ref, wbm_ref, wo_ref):
    a = jnp.dot(yssd_ref[...], wbs_ref[...], preferred_element_type=F32)
    b = jnp.dot(ymlp_ref[...], wbm_ref[...], preferred_element_type=F32)
    merged = sgs_ref[...].astype(F32) * a + sgm_ref[...].astype(F32) * b
    return jnp.dot(merged.astype(BF16), wo_ref[...], preferred_element_type=F32)


def _out_stage(o, x_ref, gate_ref, gfin_ref, y_ref, *, tm, spt):
    rps = tm // spt
    for s in range(spt):
        rows = slice(s * rps, (s + 1) * rps)
        out = x_ref[rows, :] + gate_ref[s] * o[rows, :]
        y = out * lax.rsqrt(jnp.mean(out * out, axis=-1, keepdims=True) + EPS)
        y_ref[rows, :] = y * gfin_ref[...]


def _merge_kernel(yssd_ref, u_ref, v_ref, szm_ref, sgs_ref, sgm_ref, x_ref, gate_ref,
                  lng_ref, lnb_ref, gfin_ref, wsp_ref, bsp_ref, wbs_ref, wbm_ref, wo_ref,
                  y_ref, *rest, tm, qm, spt, emit_vn):
    if emit_vn:
        vn_ref, ymlp_scr = rest
    else:
        vn_ref, (ymlp_scr,) = None, rest
    _mlp_stage(u_ref, v_ref, szm_ref, lng_ref, lnb_ref, wsp_ref, bsp_ref, ymlp_scr, vn_ref, tm=tm, qm=qm)
    o = _mix_stage(yssd_ref, ymlp_scr, sgs_ref, sgm_ref, wbs_ref, wbm_ref, wo_ref)
    _out_stage(o, x_ref, gate_ref, gfin_ref, y_ref, tm=tm, spt=spt)


def _merge(yssd, act, x2d, gate, ln_g, ln_b, g_final, w_spatial, bsp_e, wbs, wbm, wo,
           *, tm, qm, spt, emit_vn):
    rows = x2d.shape[0]
    n_seq = gate.shape[0]
    n_tiles = rows // tm
    tiles_per_seq = max(1, n_tiles // n_seq) if spt == 1 else 1
    seq_blk = (lambda i: i // tiles_per_seq) if spt == 1 else (lambda i: i)
    kern = functools.partial(_merge_kernel, tm=tm, qm=qm, spt=spt, emit_vn=emit_vn)
    const = lambda shape: pl.BlockSpec(shape, lambda i: (0,) * len(shape),
                                       pipeline_mode=pl.Buffered(1))
    seg = lambda k: pl.BlockSpec((tm, D_MODEL), lambda i: (i, k))
    out_specs = [pl.BlockSpec((tm, D_MODEL), lambda i: (i, 0))]
    out_shape = [jax.ShapeDtypeStruct((rows, D_MODEL), F32)]
    if emit_vn:
        out_specs.append(pl.BlockSpec((tm, D_MODEL), lambda i: (i, 0)))
        out_shape.append(jax.ShapeDtypeStruct((rows, D_MODEL), F32))
    return pl.pallas_call(
        kern,
        grid=(n_tiles,),
        in_specs=[
            pl.BlockSpec((tm, D_MODEL), lambda i: (i, 0)),
            seg(2), seg(3), seg(4), seg(5), seg(6),
            pl.BlockSpec((tm, D_MODEL), lambda i: (i, 0)),
            pl.BlockSpec((spt, 1, D_MODEL), lambda i: (seq_blk(i), 0, 0)),
            const((1, D_MODEL)), const((1, D_MODEL)), const((1, D_MODEL)),
            const((MLP_GROUPS, MLP_CHUNK, MLP_CHUNK)),
            const((qm, D_MODEL)),
            const((D_MODEL, D_MODEL)), const((D_MODEL, D_MODEL)), const((D_MODEL, D_MODEL)),
        ],
        out_specs=out_specs,
        out_shape=out_shape,
        scratch_shapes=[pltpu.VMEM((tm, D_MODEL), BF16)],
        compiler_params=pltpu.CompilerParams(
            dimension_semantics=("arbitrary",), vmem_limit_bytes=VMEM_LIMIT),
        name="merge",
    )(yssd, act, act, act, act, act, x2d, gate, ln_g, ln_b, g_final, w_spatial, bsp_e, wbs, wbm, wo)


def _cast_kernel(a_ref, b_ref, c_ref, ao_ref, bo_ref, co_ref):
    ao_ref[...] = a_ref[...].astype(BF16)
    bo_ref[...] = b_ref[...].astype(BF16)
    co_ref[...] = c_ref[...].astype(BF16)


def _cast_bf16(a, b, c, *, tm=512):
    rows, cols = a.shape
    spec = pl.BlockSpec((tm, cols), lambda i: (i, 0))
    return pl.pallas_call(
        _cast_kernel,
        grid=(rows // tm,),
        in_specs=[spec] * 3,
        out_specs=[spec] * 3,
        out_shape=[jax.ShapeDtypeStruct((rows, cols), BF16)] * 3,
        compiler_params=pltpu.CompilerParams(
            dimension_semantics=("arbitrary",), vmem_limit_bytes=VMEM_LIMIT),
        name="cast",
    )(a, b, c)


def _expansion_matrix():
    e = np.zeros((LANES, D_MODEL), np.float32)
    cols = np.arange(D_MODEL)
    for piece in range(3):
        e[piece * SSD_HEADS + cols // SSD_HEAD_DIM, cols] = 1.0
    return jnp.asarray(e, BF16)


def kernel(x_prompt, x_sample, state_ssm, cache_conv, c_prompt, c_sample, w_ada, b_ada, g_pre, w_in,
           conv_w, conv_b, dt_bias, a_log, d_skip, norm_ssd, ln_g, ln_b, w_spatial, b_spatial,
           w_bp_ssd, w_bp_mlp, w_o, g_final):
    depth = w_ada.shape[0]
    assert depth == 1, "single-layer trunk only"
    bp, seq, d = x_prompt.shape
    bs, dec_seq, _ = x_sample.shape
    assert d == D_MODEL and seq % 1024 == 0 and dec_seq % 16 == 0 and dec_seq <= SSD_BLOCK

    w_t = jnp.swapaxes(w_in[0], 0, 1)
    w_dt = jnp.pad(w_t[DT_OFFSET:DT_OFFSET + SSD_HEADS], ((0, LANES - SSD_HEADS), (0, 0))).astype(BF16)
    pad_heads = lambda v: jnp.pad(v.reshape(1, SSD_HEADS), ((0, 0), (0, LANES - SSD_HEADS)))
    wts = dict(
        g_pre=g_pre[0].reshape(1, d), w_t=w_t, w_dt=w_dt, dt_bias=pad_heads(dt_bias[0]),
        conv_w=conv_w[0], conv_b=conv_b[0].reshape(1, CONV_DIM),
        a_log=pad_heads(a_log[0]), dskip_e=jnp.repeat(d_skip[0], SSD_HEAD_DIM).reshape(1, d),
        norm_ssd=norm_ssd[0].reshape(1, d), e3=_expansion_matrix(),
        ln_g=ln_g[0].reshape(1, d), ln_b=ln_b[0].reshape(1, d), g_final=g_final.reshape(1, d),
        w_spatial=w_spatial[0], bsp_e=jnp.repeat(b_spatial[0].T, MLP_GROUP_DIM, axis=1),
    )

    c_all = jnp.concatenate([c_prompt, c_sample], axis=0)
    c_pad = jnp.pad(c_all, ((0, 16 - (bp + bs)), (0, 0)))
    mods = _mods(c_pad, w_ada[0], b_ada[0].reshape(1, 3 * d))

    split_mods = lambda m: (m[:, k * d:(k + 1) * d].reshape(m.shape[0], 1, d) for k in range(3))
    shift_p, scale_p, gate_p = split_mods(mods[:bp])
    shift_s, scale_s, gate_s = split_mods(mods[bp:bp + bs])
    x2p = x_prompt.reshape(bp * seq, d)
    x2s = x_sample.reshape(bs * dec_seq, d)
    rows_s = bs * dec_seq

    hist_p = jnp.zeros((bp, SUBLANES, CONV_DIM), F32)
    h0_p = jnp.zeros((bp, d, D_STATE), F32)
    hist_s = jnp.pad(cache_conv[0], ((0, 0), (SUBLANES - (CONV_W - 1), 0), (0, 0)))
    h0_s = state_ssm[0].reshape(bs, d, D_STATE)

    norm_w = (wts["g_pre"], wts["w_dt"], wts["dt_bias"])
    tiling = dict(tm_p=PROJ_TM, nsplit=PROJ_NSPLIT, msplit_p=PROJ_MSPLIT)
    hp, hs, dt_p, dt_s, act_p, act_s = _proj_first(
        x2p, x2s, scale_p, shift_p, scale_s, shift_s, *norm_w, wts["w_t"], **tiling)
    act_p, act_s, tail_p, tail_s = _proj(hp, hs, wts["w_t"], wts["conv_w"], wts["conv_b"], hist_p, hist_s,
                                         act_p, act_s, **tiling)

    ssd_w = (wts["a_log"], wts["dskip_e"], wts["norm_ssd"], wts["e3"])
    yssd_p, ssm_p = _ssd(act_p, dt_p, *ssd_w, h0_p, n_seq=bp, q_in=SSD_BLOCK,
                         blocks_per_step=SSD_BLOCKS_PER_STEP)
    yssd_s, ssm_s = _ssd(act_s, dt_s, *ssd_w, h0_s, n_seq=bs, q_in=dec_seq, blocks_per_step=1)

    merge_w = (wts["ln_g"], wts["ln_b"], wts["g_final"], wts["w_spatial"])
    proj_w = _cast_bf16(w_bp_ssd[0], w_bp_mlp[0], w_o[0])
    (yp,) = _merge(yssd_p, act_p, x2p, gate_p, *merge_w, wts["bsp_e"], *proj_w,
                   tm=MERGE_TM, qm=MLP_CHUNK, spt=1, emit_vn=False)
    ys, vn_s = _merge(yssd_s, act_s, x2s, gate_s, *merge_w, wts["bsp_e"][:dec_seq], *proj_w,
                      tm=rows_s, qm=dec_seq, spt=bs, emit_vn=True)

    conv_cols = slice((CONV_STEPS[0] - 1) * PROJ_TN, (CONV_STEPS[0] - 1) * PROJ_TN + CONV_DIM)
    hist_rows = slice(SUBLANES - (CONV_W - 1), SUBLANES)
    tiles_per_seq = seq // PROJ_TM
    conv_p = tail_p[tiles_per_seq - 1::tiles_per_seq, hist_rows, conv_cols]
    conv_s = tail_s[:, hist_rows, conv_cols]

    state_shape = (SSD_HEADS, SSD_HEAD_DIM, D_STATE)
    return (yp.reshape(bp, seq, d), ys.reshape(bs, dec_seq, d),
            ssm_p.reshape(1, bp, *state_shape), conv_p[None],
            ssm_s.reshape(1, bs, *state_shape), conv_s[None], vn_s.reshape(1, bs, dec_seq, d))
```

```python
import functools
from typing import Any, NamedTuple

import numpy as np
import jax
import jax.numpy as jnp
from jax import lax
from jax.experimental import pallas as pl
from jax.experimental.pallas import tpu as pltpu

F32 = jnp.float32
BF16 = jnp.bfloat16

D_MODEL = 2048
SSD_HEADS = 32
SSD_HEAD_DIM = 64
SSD_GROUPS = 4
HEADS_PER_GROUP = SSD_HEADS // SSD_GROUPS
D_STATE = 128
GROUP_COLS = HEADS_PER_GROUP * SSD_HEAD_DIM
CONV_W = 4
CONV_DIM = D_MODEL + 2 * SSD_GROUPS * D_STATE
MLP_CHUNK = 128
MLP_GROUPS = 8
MLP_GROUP_DIM = D_MODEL // MLP_GROUPS
EPS = 1e-5

SUBLANES = 8
LANES = 128
SSD_BLOCK = 128
SSD_BLOCKS_PER_STEP = 8
PROJ_TN = 1024
VMEM_LIMIT = 56 * 1024 * 1024

N_ACT = 7 * D_MODEL + 2 * SSD_GROUPS * D_STATE
N_PROJ_TILES = N_ACT // PROJ_TN
DT_OFFSET = D_MODEL + CONV_DIM
FIRST_SHIFTED_TILE = DT_OFFSET // PROJ_TN
CONV_STEPS = (2, 3, 4)
IDENT_STEPS = (5, 6, 7, 8)
FIRST_SIGMOID_STEP = 11
PROJ_TM = 1024
PROJ_NSPLIT = 2
PROJ_MSPLIT = 4
MERGE_TM = 256


def _silu(x):
    hx = 0.5 * x
    return hx + hx * jnp.tanh(hx)


def _softplus(x):
    return jnp.maximum(x, 0.0) + jnp.log1p(jnp.exp(-jnp.abs(x)))


def _mods_kernel(c_ref, w_ref, b_ref, o_ref):
    s = _silu(c_ref[...]).astype(BF16)
    o_ref[...] = jnp.dot(s, w_ref[...].astype(BF16), preferred_element_type=F32) + b_ref[...]


def _mods(c_pad, w_ada, b_ada):
    rows, d = c_pad.shape
    n = w_ada.shape[1]
    tn = 2048
    return pl.pallas_call(
        _mods_kernel,
        grid=(n // tn,),
        in_specs=[
            pl.BlockSpec((rows, d), lambda j: (0, 0)),
            pl.BlockSpec((d, tn), lambda j: (0, j)),
            pl.BlockSpec((1, tn), lambda j: (0, j)),
        ],
        out_specs=pl.BlockSpec((rows, tn), lambda j: (0, j)),
        out_shape=jax.ShapeDtypeStruct((rows, n), F32),
        compiler_params=pltpu.CompilerParams(
            dimension_semantics=("arbitrary",), vmem_limit_bytes=VMEM_LIMIT),
        name="mods",
    )(c_pad, w_ada, b_ada)


def _act_tile(j):
    return jnp.where(j < CONV_STEPS[2], j, jnp.where(j == CONV_STEPS[2], N_PROJ_TILES - 1, j - 1))


def _wt_row(j):
    return pl.multiple_of(j * PROJ_TN + jnp.where(j >= FIRST_SHIFTED_TILE, SSD_HEADS, 0), SSD_HEADS)


def _conv_in_tile(j):
    return jnp.clip(j - CONV_STEPS[0], 0, len(CONV_STEPS) - 1)


class _Rows(NamedTuple):
    h_ref: Any
    act_ref: Any
    tail_ref: Any
    hist_ref: Any
    tm: int
    spt: int
    msplit: int
    first: Any


def _proj_step(j, row_sets, cw_ref, cb_ref, wbf_scr, halo_scr, stgs, *, nsplit):
    cw = PROJ_TN // nsplit
    blocks = [(rs, nb, mb) for nb in range(nsplit) for rs in row_sets for mb in range(rs.msplit)]

    def dot_block(k):
        rs, nb, mb = blocks[k]
        rb = rs.tm // rs.msplit
        stgs[k % 2][SUBLANES:SUBLANES + rb, :] = lax.dot_general(
            rs.h_ref[mb * rb:(mb + 1) * rb, :], wbf_scr[nb * cw:(nb + 1) * cw, :],
            (((1,), (1,)), ((), ())), preferred_element_type=F32)

    def run(epilogue):
        dot_block(0)
        for k, (rs, nb, mb) in enumerate(blocks):
            if k + 1 < len(blocks):
                dot_block(k + 1)
            epilogue(rs, nb, mb, stgs[k % 2])

    is_conv = jnp.logical_and(j >= CONV_STEPS[0], j <= CONV_STEPS[2])
    is_ident = jnp.logical_and(j >= IDENT_STEPS[0], j <= IDENT_STEPS[-1])
    is_sigmoid = j >= FIRST_SIGMOID_STEP

    p_x = jnp.where(is_ident, 1.0, 0.5).astype(F32)
    p_xt = jnp.where(is_ident, 0.0, 0.5).astype(F32)

    def elementwise(rs, nb, mb, stg):
        rb = rs.tm // rs.msplit
        x = stg[SUBLANES:SUBLANES + rb, :]
        t = jnp.tanh(0.5 * x)
        y = jnp.where(is_sigmoid, 0.5 + 0.5 * t, x * (p_x + p_xt * t))
        rs.act_ref[mb * rb:(mb + 1) * rb, nb * cw:(nb + 1) * cw] = y.astype(BF16)

    if isinstance(j, int):
        assert not CONV_STEPS[0] <= j <= CONV_STEPS[2]
        run(elementwise)
        return

    @pl.when(jnp.logical_not(is_conv))
    def _():
        for rs in row_sets:
            rs.tail_ref[...] = jnp.zeros_like(rs.tail_ref)
        run(elementwise)

    @pl.when(is_conv)
    def _():
        for rs in row_sets:
            if rs.spt == 1:
                halo_scr[...] = jnp.where(rs.first, rs.hist_ref[0], halo_scr[...])

        def conv(ext, cols):
            w0, w1, w2, w3 = (0.5 * cw_ref[k:k + 1, cols] for k in range(CONV_W))
            t = ext * w0
            t = pltpu.roll(t, 1, axis=0) + ext * w1
            t = pltpu.roll(t, 1, axis=0) + ext * w2
            t = pltpu.roll(t, 1, axis=0) + ext * w3 + 0.5 * cb_ref[:, cols]
            t = t[SUBLANES:, :]
            return (t + t * jnp.tanh(t)).astype(BF16)

        def ep(rs, nb, mb, stg):
            cols = slice(nb * cw, (nb + 1) * cw)
            rb = rs.tm // rs.msplit
            if rs.spt == 1:
                stg[0:SUBLANES, :] = halo_scr[:, cols]
                rs.act_ref[mb * rb:(mb + 1) * rb, cols] = conv(stg[0:SUBLANES + rb, :], cols)
                last = stg[rb:rb + SUBLANES, :]
                halo_scr[:, cols] = last
                rs.tail_ref[0, :, cols] = last
            else:
                rps = rs.tm // rs.spt
                for s in range(rb // rps):
                    rows = slice(SUBLANES + s * rps, SUBLANES + (s + 1) * rps)
                    seq = mb * (rb // rps) + s
                    ext = jnp.concatenate([rs.hist_ref[seq, :, cols], stg[rows, :]], axis=0)
                    rs.act_ref[mb * rb + s * rps:mb * rb + (s + 1) * rps, cols] = conv(ext, cols)
                    rs.tail_ref[seq, :, cols] = stg[rows.stop - SUBLANES:rows.stop, :]

        run(ep)


def _cast_weight_tile(wt_ref, wbf_scr):
    wrows = 256
    for r in range(PROJ_TN // wrows):
        rows = slice(r * wrows, (r + 1) * wrows)
        wbf_scr[rows, :] = wt_ref[rows, :].astype(BF16)


def _norm_rows(x_ref, sc_ref, sh_ref, gpre_ref, wdt_ref, dtb_ref, h_ref, dt_ref, *, tm, spt):
    rps = tm // spt
    chunk = min(rps, 256)
    for s in range(spt):
        for c in range(rps // chunk):
            rows = slice(s * rps + c * chunk, s * rps + (c + 1) * chunk)
            x = x_ref[rows, :]
            y = x * lax.rsqrt(jnp.mean(x * x, axis=-1, keepdims=True) + EPS)
            y = y * gpre_ref[...]
            h_ref[rows, :] = (y * (1.0 + sc_ref[s]) + sh_ref[s]).astype(BF16)
    dt_raw = lax.dot_general(h_ref[...], wdt_ref[...], (((1,), (1,)), ((), ())),
                             preferred_element_type=F32) + dtb_ref[...]
    lane = lax.broadcasted_iota(jnp.int32, dt_raw.shape, 1)
    dt_ref[...] = jnp.where(lane < SSD_HEADS, _softplus(dt_raw), 0.0)


def _proj_first_kernel(xp_ref, xs_ref, scp_ref, shp_ref, scs_ref, shs_ref, gpre_ref, wdt_ref, dtb_ref,
                       wt_ref, hp_ref, hs_ref, dtp_ref, dts_ref, actp_ref, acts_ref,
                       wbf_scr, stg0_scr, stg1_scr, *, n_p, tm_p, tm_s, spt_s, nsplit, msplit_p):
    i = pl.program_id(0)

    @pl.when(i == 0)
    def _():
        _cast_weight_tile(wt_ref, wbf_scr)

    norm_w = (gpre_ref, wdt_ref, dtb_ref)
    _norm_rows(xp_ref, scp_ref, shp_ref, *norm_w, hp_ref, dtp_ref, tm=tm_p, spt=1)
    shared = (None, None, wbf_scr, None, (stg0_scr, stg1_scr))
    prompt = _Rows(hp_ref, actp_ref, None, None, tm=tm_p, spt=1, msplit=msplit_p, first=None)

    @pl.when(i < n_p - 1)
    def _():
        _proj_step(0, [prompt], *shared, nsplit=nsplit)

    @pl.when(i == n_p - 1)
    def _():
        _norm_rows(xs_ref, scs_ref, shs_ref, *norm_w, hs_ref, dts_ref, tm=tm_s, spt=spt_s)
        sample = _Rows(hs_ref, acts_ref, None, None, tm=tm_s, spt=spt_s, msplit=1, first=None)
        _proj_step(0, [prompt, sample], *shared, nsplit=nsplit)


def _proj_first(xp, xs, scale_p, shift_p, scale_s, shift_s, g_pre, w_dt, dt_bias, w_t,
                *, tm_p, nsplit, msplit_p):
    rows_p, rows_s = xp.shape[0], xs.shape[0]
    n_p = rows_p // tm_p
    tiles_per_seq = n_p // scale_p.shape[0]
    spt_s = scale_s.shape[0]
    rb = max(tm_p // msplit_p, rows_s)
    kern = functools.partial(_proj_first_kernel, n_p=n_p, tm_p=tm_p, tm_s=rows_s, spt_s=spt_s,
                             nsplit=nsplit, msplit_p=msplit_p)
    const = lambda shape: pl.BlockSpec(shape, lambda i: (0,) * len(shape))
    seq_spec = pl.BlockSpec((1, 1, D_MODEL), lambda i: (i // tiles_per_seq, 0, 0))
    return pl.pallas_call(
        kern,
        grid=(n_p,),
        in_specs=[
            pl.BlockSpec((tm_p, D_MODEL), lambda i: (i, 0)),
            const((rows_s, D_MODEL)),
            seq_spec, seq_spec,
            const((spt_s, 1, D_MODEL)), const((spt_s, 1, D_MODEL)),
            const((1, D_MODEL)), const((LANES, D_MODEL)), const((1, LANES)),
            pl.BlockSpec((pl.Element(PROJ_TN), pl.Element(D_MODEL)), lambda i: (0, 0),
                         pipeline_mode=pl.Buffered(1)),
        ],
        out_specs=[
            pl.BlockSpec((tm_p, D_MODEL), lambda i: (i, 0)),
            const((rows_s, D_MODEL)),
            pl.BlockSpec((tm_p, LANES), lambda i: (i, 0)),
            const((rows_s, LANES)),
            pl.BlockSpec((tm_p, PROJ_TN), lambda i: (i, 0)),
            const((rows_s, PROJ_TN)),
        ],
        out_shape=[
            jax.ShapeDtypeStruct((rows_p, D_MODEL), BF16),
            jax.ShapeDtypeStruct((rows_s, D_MODEL), BF16),
            jax.ShapeDtypeStruct((rows_p, LANES), F32),
            jax.ShapeDtypeStruct((rows_s, LANES), F32),
            jax.ShapeDtypeStruct((rows_p, N_ACT), BF16),
            jax.ShapeDtypeStruct((rows_s, N_ACT), BF16),
        ],
        scratch_shapes=[
            pltpu.VMEM((PROJ_TN, D_MODEL), BF16),
            pltpu.VMEM((rb + SUBLANES, PROJ_TN // nsplit), F32),
            pltpu.VMEM((rb + SUBLANES, PROJ_TN // nsplit), F32),
        ],
        compiler_params=pltpu.CompilerParams(
            dimension_semantics=("arbitrary",), vmem_limit_bytes=VMEM_LIMIT),
        name="proj_first",
    )(xp, xs, scale_p, shift_p, scale_s, shift_s, g_pre, w_dt, dt_bias, w_t)


def _proj_kernel(hp_ref, hs_ref, wt_ref, cw_ref, cb_ref, histp_ref, hists_ref, actp_in, acts_in,
                 actp_ref, acts_ref, tailp_ref, tails_ref, wbf_scr, halo_scr, stg0_scr, stg1_scr,
                 *, n_p, tm_p, tm_s, spt_s, nsplit, msplit_p, tiles_per_seq):
    del actp_in, acts_in
    j = pl.program_id(0) + 1
    i = pl.program_id(1)

    @pl.when(i == 0)
    def _():
        _cast_weight_tile(wt_ref, wbf_scr)

    shared = (cw_ref, cb_ref, wbf_scr, halo_scr, (stg0_scr, stg1_scr))
    prompt = _Rows(hp_ref, actp_ref, tailp_ref, histp_ref, tm=tm_p, spt=1, msplit=msplit_p,
                   first=(i % tiles_per_seq) == 0)
    sample = _Rows(hs_ref, acts_ref, tails_ref, hists_ref, tm=tm_s, spt=spt_s, msplit=1, first=None)

    @pl.when(i < n_p - 1)
    def _():
        _proj_step(j, [prompt], *shared, nsplit=nsplit)

    @pl.when(i == n_p - 1)
    def _():
        _proj_step(j, [prompt, sample], *shared, nsplit=nsplit)


def _proj(hp, hs, w_t, conv_w, conv_b, hist_p, hist_s, act_p, act_s, *, tm_p, nsplit, msplit_p):
    rows_p, rows_s = hp.shape[0], hs.shape[0]
    n_p = rows_p // tm_p
    tiles_per_seq = n_p // hist_p.shape[0]
    spt_s = hist_s.shape[0]
    rb = max(tm_p // msplit_p, rows_s)
    n_steps = N_PROJ_TILES - 1
    kern = functools.partial(_proj_kernel, n_p=n_p, tm_p=tm_p, tm_s=rows_s, spt_s=spt_s,
                             nsplit=nsplit, msplit_p=msplit_p, tiles_per_seq=tiles_per_seq)
    step = lambda g: g + 1
    return pl.pallas_call(
        kern,
        grid=(n_steps, n_p),
        in_specs=[
            pl.BlockSpec((tm_p, D_MODEL), lambda g, i: (i, 0)),
            pl.BlockSpec((rows_s, D_MODEL), lambda g, i: (0, 0)),
            pl.BlockSpec((pl.Element(PROJ_TN), pl.Element(D_MODEL)), lambda g, i: (_wt_row(step(g)), 0)),
            pl.BlockSpec((CONV_W, PROJ_TN), lambda g, i: (0, _conv_in_tile(step(g)))),
            pl.BlockSpec((1, PROJ_TN), lambda g, i: (0, _conv_in_tile(step(g)))),
            pl.BlockSpec((1, SUBLANES, PROJ_TN),
                         lambda g, i: (i // tiles_per_seq, 0, _conv_in_tile(step(g)))),
            pl.BlockSpec((spt_s, SUBLANES, PROJ_TN), lambda g, i: (0, 0, _conv_in_tile(step(g)))),
            pl.BlockSpec(memory_space=pl.ANY),
            pl.BlockSpec(memory_space=pl.ANY),
        ],
        out_specs=[
            pl.BlockSpec((tm_p, PROJ_TN), lambda g, i: (i, _act_tile(step(g)))),
            pl.BlockSpec((rows_s, PROJ_TN), lambda g, i: (0, _act_tile(step(g)))),
            pl.BlockSpec((1, SUBLANES, PROJ_TN), lambda g, i: (i, 0, g)),
            pl.BlockSpec((spt_s, SUBLANES, PROJ_TN), lambda g, i: (0, 0, g)),
        ],
        out_shape=[
            jax.ShapeDtypeStruct((rows_p, N_ACT), BF16),
            jax.ShapeDtypeStruct((rows_s, N_ACT), BF16),
            jax.ShapeDtypeStruct((n_p, SUBLANES, n_steps * PROJ_TN), F32),
            jax.ShapeDtypeStruct((spt_s, SUBLANES, n_steps * PROJ_TN), F32),
        ],
        input_output_aliases={7: 0, 8: 1},
        scratch_shapes=[
            pltpu.VMEM((PROJ_TN, D_MODEL), BF16),
            pltpu.VMEM((SUBLANES, PROJ_TN), F32),
            pltpu.VMEM((rb + SUBLANES, PROJ_TN // nsplit), F32),
            pltpu.VMEM((rb + SUBLANES, PROJ_TN // nsplit), F32),
        ],
        compiler_params=pltpu.CompilerParams(
            dimension_semantics=("arbitrary", "arbitrary"), vmem_limit_bytes=VMEM_LIMIT),
        name="proj",
    )(hp, hs, w_t, conv_w, conv_b, hist_p, hist_s, act_p, act_s)


def _pad_rows(v, rows):
    if v.shape[0] == rows:
        return v
    return jnp.concatenate([v, jnp.zeros((rows - v.shape[0], v.shape[1]), v.dtype)], axis=0)


def _ssd_kernel(szs_ref, xs_ref, b_ref, c_ref, dt_ref, alog_ref, dskip_ref, nrm_ref, e3_ref, h0_ref,
                y_ref, hout_ref, st_scr, *, q_in, blocks_per_step, n_steps):
    ci = pl.program_id(1)

    @pl.when(ci == 0)
    def _():
        st_scr[...] = h0_ref[0]

    for k in range(blocks_per_step):
        rs = pl.ds(k * q_in, q_in)
        _ssd_block(szs_ref.at[rs], xs_ref.at[rs], b_ref.at[rs], c_ref.at[rs], dt_ref.at[rs],
                   alog_ref, dskip_ref, nrm_ref, e3_ref, y_ref.at[rs], st_scr, q_in=q_in)

    @pl.when(ci == n_steps - 1)
    def _():
        hout_ref[0] = st_scr[...]


def _ssd_block(szs_ref, xs_ref, b_ref, c_ref, dt_ref, alog_ref, dskip_ref, nrm_ref, e3_ref,
               y_ref, st_scr, *, q_in):
    q = SSD_BLOCK
    dt = _pad_rows(dt_ref[...], q)
    a = -jnp.exp(alog_ref[...])
    adt = dt * a
    row = lax.broadcasted_iota(jnp.int32, (q, q), 0)
    col = lax.broadcasted_iota(jnp.int32, (q, q), 1)
    causal = row >= col
    acum = jnp.dot(causal.astype(F32), adt, precision=lax.Precision.HIGHEST,
                   preferred_element_type=F32)
    alast = acum[q - 1:q, :]
    dte = jnp.exp(alast - acum)
    ea = jnp.exp(acum)

    stack = jnp.concatenate([dt, dt * dte, ea], axis=0)
    lane3 = lax.broadcasted_iota(jnp.int32, stack.shape, 1)
    stack = jnp.where(lane3 < SSD_HEADS, stack, 0.0)
    hi = stack.astype(BF16).astype(F32)
    r1 = stack - hi
    mid = r1.astype(BF16).astype(F32)
    lo = (r1 - mid).astype(BF16).astype(F32)
    comb = hi + pltpu.roll(mid, SSD_HEADS, axis=1) + pltpu.roll(lo, 2 * SSD_HEADS, axis=1)
    expd = jnp.dot(comb.astype(BF16), e3_ref[...], preferred_element_type=F32)
    dt_e, w_e, ea_e = expd[0:q], expd[q:2 * q], expd[2 * q:3 * q]

    x = _pad_rows(xs_ref[...], q).astype(F32)
    xdt = (x * dt_e).astype(BF16)
    xdtd = (x * w_e).astype(BF16)
    bm = _pad_rows(b_ref[...], q)
    cm = _pad_rows(c_ref[...], q)

    acum_t = acum.T
    dec = jnp.exp(acum_t[0:SSD_HEADS, q - 1:q])
    dec = jnp.broadcast_to(dec, (SSD_HEADS, D_STATE))
    dec = jnp.broadcast_to(dec[:, None, :], (SSD_HEADS, SSD_HEAD_DIM, D_STATE))
    dec = dec.reshape(SSD_HEADS * SSD_HEAD_DIM, D_STATE)

    lane = lax.broadcasted_iota(jnp.int32, (q, LANES), 1)
    nt = (((1,), (1,)), ((), ()))
    tn = (((0,), (0,)), ((), ()))
    y_groups = []
    for g in range(SSD_GROUPS):
        gc = slice(g * GROUP_COLS, (g + 1) * GROUP_COLS)
        bg = bm[:, g * D_STATE:(g + 1) * D_STATE]
        cg = cm[:, g * D_STATE:(g + 1) * D_STATE]
        cb = lax.dot_general(cg, bg, nt, preferred_element_type=F32)
        sg = st_scr[gc, :]
        y_off = lax.dot_general(cg, sg.astype(BF16), nt, preferred_element_type=F32)
        pieces = []
        for pr in range(HEADS_PER_GROUP // 2):
            ms = []
            for hh in range(2):
                h = g * HEADS_PER_GROUP + 2 * pr + hh
                seg = jnp.broadcast_to(acum[:, h:h + 1], (q, q)) - acum_t[h:h + 1, :]
                lm = jnp.exp(jnp.where(causal, seg, -jnp.inf))
                ms.append((cb * lm).astype(BF16))
            m = jnp.concatenate(ms, axis=1)
            c0 = g * GROUP_COLS + pr * LANES
            xp = xdt[:, c0:c0 + LANES]
            rhs = jnp.concatenate([jnp.where(lane < SSD_HEAD_DIM, xp, jnp.zeros_like(xp)),
                                   jnp.where(lane >= SSD_HEAD_DIM, xp, jnp.zeros_like(xp))], axis=0)
            pieces.append(jnp.dot(m, rhs, preferred_element_type=F32))
        y_diag = jnp.concatenate(pieces, axis=1)
        y_groups.append(y_diag + y_off * ea_e[:, gc])
        new = lax.dot_general(xdtd[:, gc], bg, tn, preferred_element_type=F32)
        st_scr[gc, :] = sg * dec[gc, :] + new
    y = jnp.concatenate(y_groups, axis=1)
    y = y + dskip_ref[...] * x
    yz = y * _pad_rows(szs_ref[...], q).astype(F32)
    yn = yz * lax.rsqrt(jnp.mean(yz * yz, axis=-1, keepdims=True) + EPS) * nrm_ref[...]
    y_ref[...] = yn[0:q_in].astype(BF16)


def _ssd(act, dt, a_log, dskip_e, norm_ssd, e3, h0, *, n_seq, q_in, blocks_per_step):
    rows = act.shape[0]
    qb = q_in * blocks_per_step
    n_chunks = rows // (n_seq * qb)
    kern = functools.partial(_ssd_kernel, q_in=q_in, blocks_per_step=blocks_per_step, n_steps=n_chunks)
    rblk = lambda b, c: b * n_chunks + c
    bc_blk0 = 7 * D_MODEL // GROUP_COLS
    return pl.pallas_call(
        kern,
        grid=(n_seq, n_chunks),
        in_specs=[
            pl.BlockSpec((qb,D_MODEL), lambda b, c: (rblk(b, c), 0)),
            pl.BlockSpec((qb,D_MODEL), lambda b, c: (rblk(b, c), 1)),
            pl.BlockSpec((qb,GROUP_COLS), lambda b, c: (rblk(b, c), bc_blk0)),
            pl.BlockSpec((qb,GROUP_COLS), lambda b, c: (rblk(b, c), bc_blk0 + 1)),
            pl.BlockSpec((qb,LANES), lambda b, c: (rblk(b, c), 0)),
            pl.BlockSpec((1, LANES), lambda b, c: (0, 0)),
            pl.BlockSpec((1, D_MODEL), lambda b, c: (0, 0)),
            pl.BlockSpec((1, D_MODEL), lambda b, c: (0, 0)),
            pl.BlockSpec((LANES, D_MODEL), lambda b, c: (0, 0)),
            pl.BlockSpec((1, D_MODEL, D_STATE), lambda b, c: (b, 0, 0)),
        ],
        out_specs=[
            pl.BlockSpec((qb,D_MODEL), lambda b, c: (rblk(b, c), 0)),
            pl.BlockSpec((1, D_MODEL, D_STATE), lambda b, c: (b, 0, 0)),
        ],
        out_shape=[
            jax.ShapeDtypeStruct((rows, D_MODEL), BF16),
            jax.ShapeDtypeStruct((n_seq, D_MODEL, D_STATE), F32),
        ],
        scratch_shapes=[pltpu.VMEM((D_MODEL, D_STATE), F32)],
        compiler_params=pltpu.CompilerParams(
            dimension_semantics=("arbitrary", "arbitrary"), vmem_limit_bytes=VMEM_LIMIT),
        name="ssd",
    )(act, act, act, act, dt, a_log, dskip_e, norm_ssd, e3, h0)


def _mlp_stage(u_ref, v_ref, szm_ref, lng_ref, lnb_ref, wsp_ref, bsp_ref, ymlp_ref, vn_ref, *, tm, qm):
    v = v_ref[...].astype(F32)
    mu = jnp.mean(v, axis=-1, keepdims=True)
    vc = v - mu
    var = jnp.mean(vc * vc, axis=-1, keepdims=True)
    vn = vc * lax.rsqrt(var + EPS) * lng_ref[...] + lnb_ref[...]
    if vn_ref is not None:
        vn_ref[...] = vn
    vn_b = vn.astype(BF16)

    row = lax.broadcasted_iota(jnp.int32, (MLP_CHUNK, MLP_CHUNK), 0)
    col = lax.broadcasted_iota(jnp.int32, (MLP_CHUNK, MLP_CHUNK), 1)
    wms = [jnp.where(row >= col, wsp_ref[g], 0.0).astype(BF16) for g in range(MLP_GROUPS)]
    for ck in range(tm // qm):
        rows = slice(ck * qm, (ck + 1) * qm)
        vck = _pad_rows(vn_b[rows, :], MLP_CHUNK)
        mixed = jnp.concatenate(
            [jnp.dot(wms[g], vck[:, g * MLP_GROUP_DIM:(g + 1) * MLP_GROUP_DIM],
                     preferred_element_type=F32) for g in range(MLP_GROUPS)], axis=1)
        mixed = mixed[0:qm] + bsp_ref[...]
        y_mlp = u_ref[rows, :].astype(F32) * mixed * szm_ref[rows, :].astype(F32)
        ymlp_ref[rows, :] = y_mlp.astype(BF16)


def _mix_stage(yssd_ref, ymlp_ref, sgs_ref, sgm_ref, wbs_ref, wbm_ref, wo_ref):
    a = jnp.dot(yssd_ref[...], wbs_ref[...], preferred_element_type=F32)
    b = jnp.dot(ymlp_ref[...], wbm_ref[...], preferred_element_type=F32)
    merged = sgs_ref[...].astype(F32) * a + sgm_ref[...].astype(F32) * b
    return jnp.dot(merged.astype(BF16), wo_ref[...], preferred_element_type=F32)


def _out_stage(o, x_ref, gate_ref, gfin_ref, y_ref, *, tm, spt):
    rps = tm // spt
    for s in range(spt):
        rows = slice(s * rps, (s + 1) * rps)
        out = x_ref[rows, :] + gate_ref[s] * o[rows, :]
        y = out * lax.rsqrt(jnp.mean(out * out, axis=-1, keepdims=True) + EPS)
        y_ref[rows, :] = y * gfin_ref[...]


def _merge_kernel(yssd_ref, u_ref, v_ref, szm_ref, sgs_ref, sgm_ref, x_ref, gate_ref,
                  lng_ref, lnb_ref, gfin_ref, wsp_ref, bsp_ref, wbs_ref, wbm_ref, wo_ref,
                  y_ref, *rest, tm, qm, spt, emit_vn):
    if emit_vn:
        vn_ref, ymlp_scr = rest
    else:
        vn_ref, (ymlp_scr,) = None, rest
    _mlp_stage(u_ref, v_ref, szm_ref, lng_ref, lnb_ref, wsp_ref, bsp_ref, ymlp_scr, vn_ref, tm=tm, qm=qm)
    o = _mix_stage(yssd_ref, ymlp_scr, sgs_ref, sgm_ref, wbs_ref, wbm_ref, wo_ref)
    _out_stage(o, x_ref, gate_ref, gfin_ref, y_ref, tm=tm, spt=spt)


def _merge(yssd, act, x2d, gate, ln_g, ln_b, g_final, w_spatial, bsp_e, wbs, wbm, wo,
           *, tm, qm, spt, emit_vn):
    rows = x2d.shape[0]
    n_seq = gate.shape[0]
    n_tiles = rows // tm
    tiles_per_seq = max(1, n_tiles // n_seq) if spt == 1 else 1
    seq_blk = (lambda i: i // tiles_per_seq) if spt == 1 else (lambda i: i)
    kern = functools.partial(_merge_kernel, tm=tm, qm=qm, spt=spt, emit_vn=emit_vn)
    const = lambda shape: pl.BlockSpec(shape, lambda i: (0,) * len(shape),
                                       pipeline_mode=pl.Buffered(1))
    seg = lambda k: pl.BlockSpec((tm, D_MODEL), lambda i: (i, k))
    out_specs = [pl.BlockSpec((tm, D_MODEL), lambda i: (i, 0))]
    out_shape = [jax.ShapeDtypeStruct((rows, D_MODEL), F32)]
    if emit_vn:
        out_specs.append(pl.BlockSpec((tm, D_MODEL), lambda i: (i, 0)))
        out_shape.append(jax.ShapeDtypeStruct((rows, D_MODEL), F32))
    return pl.pallas_call(
        kern,
        grid=(n_tiles,),
        in_specs=[
            pl.BlockSpec((tm, D_MODEL), lambda i: (i, 0)),
            seg(2), seg(3), seg(4), seg(5), seg(6),
            pl.BlockSpec((tm, D_MODEL), lambda i: (i, 0)),
            pl.BlockSpec((spt, 1, D_MODEL), lambda i: (seq_blk(i), 0, 0)),
            const((1, D_MODEL)), const((1, D_MODEL)), const((1, D_MODEL)),
            const((MLP_GROUPS, MLP_CHUNK, MLP_CHUNK)),
            const((qm, D_MODEL)),
            const((D_MODEL, D_MODEL)), const((D_MODEL, D_MODEL)), const((D_MODEL, D_MODEL)),
        ],
        out_specs=out_specs,
        out_shape=out_shape,
        scratch_shapes=[pltpu.VMEM((tm, D_MODEL), BF16)],
        compiler_params=pltpu.CompilerParams(
            dimension_semantics=("arbitrary",), vmem_limit_bytes=VMEM_LIMIT),
        name="merge",
    )(yssd, act, act, act, act, act, x2d, gate, ln_g, ln_b, g_final, w_spatial, bsp_e, wbs, wbm, wo)


def _cast_kernel(a_ref, b_ref, c_ref, ao_ref, bo_ref, co_ref):
    ao_ref[...] = a_ref[...].astype(BF16)
    bo_ref[...] = b_ref[...].astype(BF16)
    co_ref[...] = c_ref[...].astype(BF16)


def _cast_bf16(a, b, c, *, tm=512):
    rows, cols = a.shape
    spec = pl.BlockSpec((tm, cols), lambda i: (i, 0))
    return pl.pallas_call(
        _cast_kernel,
        grid=(rows // tm,),
        in_specs=[spec] * 3,
        out_specs=[spec] * 3,
        out_shape=[jax.ShapeDtypeStruct((rows, cols), BF16)] * 3,
        compiler_params=pltpu.CompilerParams(
            dimension_semantics=("arbitrary",), vmem_limit_bytes=VMEM_LIMIT),
        name="cast",
    )(a, b, c)


def _expansion_matrix():
    e = np.zeros((LANES, D_MODEL), np.float32)
    cols = np.arange(D_MODEL)
    for piece in range(3):
        e[piece * SSD_HEADS + cols // SSD_HEAD_DIM, cols] = 1.0
    return jnp.asarray(e, BF16)


def kernel(x_prompt, x_sample, state_ssm, cache_conv, c_prompt, c_sample, w_ada, b_ada, g_pre, w_in,
           conv_w, conv_b, dt_bias, a_log, d_skip, norm_ssd, ln_g, ln_b, w_spatial, b_spatial,
           w_bp_ssd, w_bp_mlp, w_o, g_final):
    depth = w_ada.shape[0]
    assert depth == 1, "single-layer trunk only"
    bp, seq, d = x_prompt.shape
    bs, dec_seq, _ = x_sample.shape
    assert d == D_MODEL and seq % 1024 == 0 and dec_seq % 16 == 0 and dec_seq <= SSD_BLOCK

    w_t = jnp.swapaxes(w_in[0], 0, 1)
    w_dt = jnp.pad(w_t[DT_OFFSET:DT_OFFSET + SSD_HEADS], ((0, LANES - SSD_HEADS), (0, 0))).astype(BF16)
    pad_heads = lambda v: jnp.pad(v.reshape(1, SSD_HEADS), ((0, 0), (0, LANES - SSD_HEADS)))
    wts = dict(
        g_pre=g_pre[0].reshape(1, d), w_t=w_t, w_dt=w_dt, dt_bias=pad_heads(dt_bias[0]),
        conv_w=conv_w[0], conv_b=conv_b[0].reshape(1, CONV_DIM),
        a_log=pad_heads(a_log[0]), dskip_e=jnp.repeat(d_skip[0], SSD_HEAD_DIM).reshape(1, d),
        norm_ssd=norm_ssd[0].reshape(1, d), e3=_expansion_matrix(),
        ln_g=ln_g[0].reshape(1, d), ln_b=ln_b[0].reshape(1, d), g_final=g_final.reshape(1, d),
        w_spatial=w_spatial[0], bsp_e=jnp.repeat(b_spatial[0].T, MLP_GROUP_DIM, axis=1),
    )

    c_all = jnp.concatenate([c_prompt, c_sample], axis=0)
    c_pad = jnp.pad(c_all, ((0, 16 - (bp + bs)), (0, 0)))
    mods = _mods(c_pad, w_ada[0], b_ada[0].reshape(1, 3 * d))

    split_mods = lambda m: (m[:, k * d:(k + 1) * d].reshape(m.shape[0], 1, d) for k in range(3))
    shift_p, scale_p, gate_p = split_mods(mods[:bp])
    shift_s, scale_s, gate_s = split_mods(mods[bp:bp + bs])
    x2p = x_prompt.reshape(bp * seq, d)
    x2s = x_sample.reshape(bs * dec_seq, d)
    rows_s = bs * dec_seq

    hist_p = jnp.zeros((bp, SUBLANES, CONV_DIM), F32)
    h0_p = jnp.zeros((bp, d, D_STATE), F32)
    hist_s = jnp.pad(cache_conv[0], ((0, 0), (SUBLANES - (CONV_W - 1), 0), (0, 0)))
    h0_s = state_ssm[0].reshape(bs, d, D_STATE)

    norm_w = (wts["g_pre"], wts["w_dt"], wts["dt_bias"])
    tiling = dict(tm_p=PROJ_TM, nsplit=PROJ_NSPLIT, msplit_p=PROJ_MSPLIT)
    hp, hs, dt_p, dt_s, act_p, act_s = _proj_first(
        x2p, x2s, scale_p, shift_p, scale_s, shift_s, *norm_w, wts["w_t"], **tiling)
    act_p, act_s, tail_p, tail_s = _proj(hp, hs, wts["w_t"], wts["conv_w"], wts["conv_b"], hist_p, hist_s,
                                         act_p, act_s, **tiling)

    ssd_w = (wts["a_log"], wts["dskip_e"], wts["norm_ssd"], wts["e3"])
    yssd_p, ssm_p = _ssd(act_p, dt_p, *ssd_w, h0_p, n_seq=bp, q_in=SSD_BLOCK,
                         blocks_per_step=SSD_BLOCKS_PER_STEP)
    yssd_s, ssm_s = _ssd(act_s, dt_s, *ssd_w, h0_s, n_seq=bs, q_in=dec_seq, blocks_per_step=1)

    merge_w = (wts["ln_g"], wts["ln_b"], wts["g_final"], wts["w_spatial"])
    proj_w = _cast_bf16(w_bp_ssd[0], w_bp_mlp[0], w_o[0])
    (yp,) = _merge(yssd_p, act_p, x2p, gate_p, *merge_w, wts["bsp_e"], *proj_w,
                   tm=MERGE_TM, qm=MLP_CHUNK, spt=1, emit_vn=False)
    ys, vn_s = _merge(yssd_s, act_s, x2s, gate_s, *merge_w, wts["bsp_e"][:dec_seq], *proj_w,
                      tm=rows_s, qm=dec_seq, spt=bs, emit_vn=True)

    conv_cols = slice((CONV_STEPS[0] - 1) * PROJ_TN, (CONV_STEPS[0] - 1) * PROJ_TN + CONV_DIM)
    hist_rows = slice(SUBLANES - (CONV_W - 1), SUBLANES)
    tiles_per_seq = seq // PROJ_TM
    conv_p = tail_p[tiles_per_seq - 1::tiles_per_seq, hist_rows, conv_cols]
    conv_s = tail_s[:, hist_rows, conv_cols]

    state_shape = (SSD_HEADS, SSD_HEAD_DIM, D_STATE)
    return (yp.reshape(bp, seq, d), ys.reshape(bs, dec_seq, d),
            ssm_p.reshape(1, bp, *state_shape), conv_p[None],
            ssm_s.reshape(1, bs, *state_shape), conv_s[None], vn_s.reshape(1, bs, dec_seq, d))
```

```python
import functools
from typing import Any, NamedTuple

import numpy as np
import jax
import jax.numpy as jnp
from jax import lax
from jax.experimental import pallas as pl
from jax.experimental.pallas import tpu as pltpu

F32 = jnp.float32
BF16 = jnp.bfloat16

D_MODEL = 2048
SSD_HEADS = 32
SSD_HEAD_DIM = 64
SSD_GROUPS = 4
HEADS_PER_GROUP = SSD_HEADS // SSD_GROUPS
D_STATE = 128
GROUP_COLS = HEADS_PER_GROUP * SSD_HEAD_DIM
CONV_W = 4
CONV_DIM = D_MODEL + 2 * SSD_GROUPS * D_STATE
MLP_CHUNK = 128
MLP_GROUPS = 8
MLP_GROUP_DIM = D_MODEL // MLP_GROUPS
EPS = 1e-5

SUBLANES = 8
LANES = 128
SSD_BLOCK = 128
SSD_BLOCKS_PER_STEP = 8
PROJ_TN = 1024
VMEM_LIMIT = 56 * 1024 * 1024

N_ACT = 7 * D_MODEL + 2 * SSD_GROUPS * D_STATE
N_PROJ_TILES = N_ACT // PROJ_TN
DT_OFFSET = D_MODEL + CONV_DIM
FIRST_SHIFTED_TILE = DT_OFFSET // PROJ_TN
CONV_STEPS = (2, 3, 4)
IDENT_STEPS = (5, 6, 7, 8)
FIRST_SIGMOID_STEP = 11
PROJ_TM = 1024
PROJ_NSPLIT = 2
PROJ_MSPLIT = 4
NORM_TM = 512
MERGE_TM = 256


def _silu(x):
    hx = 0.5 * x
    return hx + hx * jnp.tanh(hx)


def _softplus(x):
    return jnp.maximum(x, 0.0) + jnp.log1p(jnp.exp(-jnp.abs(x)))


def _mods_kernel(c_ref, w_ref, b_ref, o_ref):
    s = _silu(c_ref[...]).astype(BF16)
    o_ref[...] = jnp.dot(s, w_ref[...].astype(BF16), preferred_element_type=F32) + b_ref[...]


def _mods(c_pad, w_ada, b_ada):
    rows, d = c_pad.shape
    n = w_ada.shape[1]
    tn = 1024
    return pl.pallas_call(
        _mods_kernel,
        grid=(n // tn,),
        in_specs=[
            pl.BlockSpec((rows, d), lambda j: (0, 0)),
            pl.BlockSpec((d, tn), lambda j: (0, j)),
            pl.BlockSpec((1, tn), lambda j: (0, j)),
        ],
        out_specs=pl.BlockSpec((rows, tn), lambda j: (0, j)),
        out_shape=jax.ShapeDtypeStruct((rows, n), F32),
        compiler_params=pltpu.CompilerParams(
            dimension_semantics=("arbitrary",), vmem_limit_bytes=VMEM_LIMIT),
        name="mods",
    )(c_pad, w_ada, b_ada)


def _norm_kernel(x_ref, sc_ref, sh_ref, gpre_ref, wdt_ref, dtb_ref, h_ref, dt_ref, *, tm, spt):
    rps = tm // spt
    for s in range(spt):
        rows = slice(s * rps, (s + 1) * rps)
        x = x_ref[rows, :]
        y = x * lax.rsqrt(jnp.mean(x * x, axis=-1, keepdims=True) + EPS)
        y = y * gpre_ref[...]
        h_ref[rows, :] = (y * (1.0 + sc_ref[s]) + sh_ref[s]).astype(BF16)
    dt_raw = lax.dot_general(h_ref[...], wdt_ref[...], (((1,), (1,)), ((), ())),
                             preferred_element_type=F32) + dtb_ref[...]
    lane = lax.broadcasted_iota(jnp.int32, dt_raw.shape, 1)
    dt_ref[...] = jnp.where(lane < SSD_HEADS, _softplus(dt_raw), 0.0)


def _norm(x2d, scale, shift, g_pre, w_dt, dt_bias, *, tm, spt):
    rows = x2d.shape[0]
    n_seq = scale.shape[0]
    n_tiles = rows // tm
    tiles_per_seq = max(1, n_tiles // n_seq) if spt == 1 else 1
    seq_blk = (lambda i: i // tiles_per_seq) if spt == 1 else (lambda i: i)
    return pl.pallas_call(
        functools.partial(_norm_kernel, tm=tm, spt=spt),
        grid=(n_tiles,),
        in_specs=[
            pl.BlockSpec((tm, D_MODEL), lambda i: (i, 0)),
            pl.BlockSpec((spt, 1, D_MODEL), lambda i: (seq_blk(i), 0, 0)),
            pl.BlockSpec((spt, 1, D_MODEL), lambda i: (seq_blk(i), 0, 0)),
            pl.BlockSpec((1, D_MODEL), lambda i: (0, 0)),
            pl.BlockSpec((LANES, D_MODEL), lambda i: (0, 0)),
            pl.BlockSpec((1, LANES), lambda i: (0, 0)),
        ],
        out_specs=[
            pl.BlockSpec((tm, D_MODEL), lambda i: (i, 0)),
            pl.BlockSpec((tm, LANES), lambda i: (i, 0)),
        ],
        out_shape=[
            jax.ShapeDtypeStruct((rows, D_MODEL), BF16),
            jax.ShapeDtypeStruct((rows, LANES), F32),
        ],
        compiler_params=pltpu.CompilerParams(
            dimension_semantics=("arbitrary",), vmem_limit_bytes=VMEM_LIMIT),
        name="norm",
    )(x2d, scale, shift, g_pre, w_dt, dt_bias)


def _act_tile(j):
    return jnp.where(j < CONV_STEPS[2], j, jnp.where(j == CONV_STEPS[2], N_PROJ_TILES - 1, j - 1))


def _wt_row(j):
    return pl.multiple_of(j * PROJ_TN + jnp.where(j >= FIRST_SHIFTED_TILE, SSD_HEADS, 0), SSD_HEADS)


def _conv_in_tile(j):
    return jnp.clip(j - CONV_STEPS[0], 0, len(CONV_STEPS) - 1)


class _Rows(NamedTuple):
    h_ref: Any
    act_ref: Any
    tail_ref: Any
    hist_ref: Any
    tm: int
    spt: int
    msplit: int
    first: Any


def _proj_step(j, row_sets, cw_ref, cb_ref, wbf_scr, halo_scr, stgs, *, nsplit):
    cw = PROJ_TN // nsplit
    blocks = [(rs, nb, mb) for nb in range(nsplit) for rs in row_sets for mb in range(rs.msplit)]

    def dot_block(k):
        rs, nb, mb = blocks[k]
        rb = rs.tm // rs.msplit
        stgs[k % 2][SUBLANES:SUBLANES + rb, :] = lax.dot_general(
            rs.h_ref[mb * rb:(mb + 1) * rb, :], wbf_scr[nb * cw:(nb + 1) * cw, :],
            (((1,), (1,)), ((), ())), preferred_element_type=F32)

    def run(epilogue):
        dot_block(0)
        for k, (rs, nb, mb) in enumerate(blocks):
            if k + 1 < len(blocks):
                dot_block(k + 1)
            epilogue(rs, nb, mb, stgs[k % 2])

    is_conv = jnp.logical_and(j >= CONV_STEPS[0], j <= CONV_STEPS[2])
    is_ident = jnp.logical_and(j >= IDENT_STEPS[0], j <= IDENT_STEPS[-1])
    is_sigmoid = j >= FIRST_SIGMOID_STEP

    p_x = jnp.where(is_ident, 1.0, 0.5).astype(F32)
    p_xt = jnp.where(is_ident, 0.0, 0.5).astype(F32)

    def elementwise(rs, nb, mb, stg):
        rb = rs.tm // rs.msplit
        x = stg[SUBLANES:SUBLANES + rb, :]
        t = jnp.tanh(0.5 * x)
        y = jnp.where(is_sigmoid, 0.5 + 0.5 * t, x * (p_x + p_xt * t))
        rs.act_ref[mb * rb:(mb + 1) * rb, nb * cw:(nb + 1) * cw] = y.astype(BF16)

    @pl.when(jnp.logical_not(is_conv))
    def _():
        for rs in row_sets:
            rs.tail_ref[...] = jnp.zeros_like(rs.tail_ref)
        run(elementwise)

    @pl.when(is_conv)
    def _():
        for rs in row_sets:
            if rs.spt == 1:
                halo_scr[...] = jnp.where(rs.first, rs.hist_ref[0], halo_scr[...])

        def conv(ext, cols):
            w0, w1, w2, w3 = (0.5 * cw_ref[k:k + 1, cols] for k in range(CONV_W))
            t = ext * w0
            t = pltpu.roll(t, 1, axis=0) + ext * w1
            t = pltpu.roll(t, 1, axis=0) + ext * w2
            t = pltpu.roll(t, 1, axis=0) + ext * w3 + 0.5 * cb_ref[:, cols]
            t = t[SUBLANES:, :]
            return (t + t * jnp.tanh(t)).astype(BF16)

        def ep(rs, nb, mb, stg):
            cols = slice(nb * cw, (nb + 1) * cw)
            rb = rs.tm // rs.msplit
            if rs.spt == 1:
                stg[0:SUBLANES, :] = halo_scr[:, cols]
                rs.act_ref[mb * rb:(mb + 1) * rb, cols] = conv(stg[0:SUBLANES + rb, :], cols)
                last = stg[rb:rb + SUBLANES, :]
                halo_scr[:, cols] = last
                rs.tail_ref[0, :, cols] = last
            else:
                rps = rs.tm // rs.spt
                for s in range(rb // rps):
                    rows = slice(SUBLANES + s * rps, SUBLANES + (s + 1) * rps)
                    seq = mb * (rb // rps) + s
                    ext = jnp.concatenate([rs.hist_ref[seq, :, cols], stg[rows, :]], axis=0)
                    rs.act_ref[mb * rb + s * rps:mb * rb + (s + 1) * rps, cols] = conv(ext, cols)
                    rs.tail_ref[seq, :, cols] = stg[rows.stop - SUBLANES:rows.stop, :]

        run(ep)


def _proj_kernel(hp_ref, hs_ref, wt_ref, cw_ref, cb_ref, histp_ref, hists_ref,
                 actp_ref, acts_ref, tailp_ref, tails_ref, wbf_scr, halo_scr, stg0_scr, stg1_scr,
                 *, n_p, tm_p, tm_s, spt_s, nsplit, msplit_p, tiles_per_seq):
    j = pl.program_id(0)
    i = pl.program_id(1)
    wrows = 256

    @pl.when(i == 0)
    def _():
        for r in range(PROJ_TN // wrows):
            rows = slice(r * wrows, (r + 1) * wrows)
            wbf_scr[rows, :] = wt_ref[rows, :].astype(BF16)

    shared = (cw_ref, cb_ref, wbf_scr, halo_scr, (stg0_scr, stg1_scr))
    prompt = _Rows(hp_ref, actp_ref, tailp_ref, histp_ref, tm=tm_p, spt=1, msplit=msplit_p,
                   first=(i % tiles_per_seq) == 0)
    sample = _Rows(hs_ref, acts_ref, tails_ref, hists_ref, tm=tm_s, spt=spt_s, msplit=1, first=None)

    @pl.when(i < n_p - 1)
    def _():
        _proj_step(j, [prompt], *shared, nsplit=nsplit)

    @pl.when(i == n_p - 1)
    def _():
        _proj_step(j, [prompt, sample], *shared, nsplit=nsplit)


def _proj(hp, hs, w_t, conv_w, conv_b, hist_p, hist_s, *, tm_p, nsplit, msplit_p):
    rows_p, rows_s = hp.shape[0], hs.shape[0]
    n_p = rows_p // tm_p
    tiles_per_seq = n_p // hist_p.shape[0]
    spt_s = hist_s.shape[0]
    rb = max(tm_p // msplit_p, rows_s)
    kern = functools.partial(_proj_kernel, n_p=n_p, tm_p=tm_p, tm_s=rows_s, spt_s=spt_s,
                             nsplit=nsplit, msplit_p=msplit_p, tiles_per_seq=tiles_per_seq)
    return pl.pallas_call(
        kern,
        grid=(N_PROJ_TILES, n_p),
        in_specs=[
            pl.BlockSpec((tm_p, D_MODEL), lambda j, i: (i, 0)),
            pl.BlockSpec((rows_s, D_MODEL), lambda j, i: (0, 0)),
            pl.BlockSpec((pl.Element(PROJ_TN), pl.Element(D_MODEL)), lambda j, i: (_wt_row(j), 0)),
            pl.BlockSpec((CONV_W, PROJ_TN), lambda j, i: (0, _conv_in_tile(j))),
            pl.BlockSpec((1, PROJ_TN), lambda j, i: (0, _conv_in_tile(j))),
            pl.BlockSpec((1, SUBLANES, PROJ_TN), lambda j, i: (i // tiles_per_seq, 0, _conv_in_tile(j))),
            pl.BlockSpec((spt_s, SUBLANES, PROJ_TN), lambda j, i: (0, 0, _conv_in_tile(j))),
        ],
        out_specs=[
            pl.BlockSpec((tm_p, PROJ_TN), lambda j, i: (i, _act_tile(j))),
            pl.BlockSpec((rows_s, PROJ_TN), lambda j, i: (0, _act_tile(j))),
            pl.BlockSpec((1, SUBLANES, PROJ_TN), lambda j, i: (i, 0, j)),
            pl.BlockSpec((spt_s, SUBLANES, PROJ_TN), lambda j, i: (0, 0, j)),
        ],
        out_shape=[
            jax.ShapeDtypeStruct((rows_p, N_ACT), BF16),
            jax.ShapeDtypeStruct((rows_s, N_ACT), BF16),
            jax.ShapeDtypeStruct((n_p, SUBLANES, N_PROJ_TILES * PROJ_TN), F32),
            jax.ShapeDtypeStruct((spt_s, SUBLANES, N_PROJ_TILES * PROJ_TN), F32),
        ],
        scratch_shapes=[
            pltpu.VMEM((PROJ_TN, D_MODEL), BF16),
            pltpu.VMEM((SUBLANES, PROJ_TN), F32),
            pltpu.VMEM((rb + SUBLANES, PROJ_TN // nsplit), F32),
            pltpu.VMEM((rb + SUBLANES, PROJ_TN // nsplit), F32),
        ],
        compiler_params=pltpu.CompilerParams(
            dimension_semantics=("arbitrary", "arbitrary"), vmem_limit_bytes=VMEM_LIMIT),
        name="proj",
    )(hp, hs, w_t, conv_w, conv_b, hist_p, hist_s)


def _pad_rows(v, rows):
    if v.shape[0] == rows:
        return v
    return jnp.concatenate([v, jnp.zeros((rows - v.shape[0], v.shape[1]), v.dtype)], axis=0)


def _ssd_kernel(szs_ref, xs_ref, b_ref, c_ref, dt_ref, alog_ref, dskip_ref, nrm_ref, e3_ref, h0_ref,
                y_ref, hout_ref, st_scr, *, q_in, blocks_per_step, n_steps):
    ci = pl.program_id(1)

    @pl.when(ci == 0)
    def _():
        st_scr[...] = h0_ref[0]

    for k in range(blocks_per_step):
        rs = pl.ds(k * q_in, q_in)
        _ssd_block(szs_ref.at[rs], xs_ref.at[rs], b_ref.at[rs], c_ref.at[rs], dt_ref.at[rs],
                   alog_ref, dskip_ref, nrm_ref, e3_ref, y_ref.at[rs], st_scr, q_in=q_in)

    @pl.when(ci == n_steps - 1)
    def _():
        hout_ref[0] = st_scr[...]


def _ssd_block(szs_ref, xs_ref, b_ref, c_ref, dt_ref, alog_ref, dskip_ref, nrm_ref, e3_ref,
               y_ref, st_scr, *, q_in):
    q = SSD_BLOCK
    dt = _pad_rows(dt_ref[...], q)
    a = -jnp.exp(alog_ref[...])
    adt = dt * a
    row = lax.broadcasted_iota(jnp.int32, (q, q), 0)
    col = lax.broadcasted_iota(jnp.int32, (q, q), 1)
    causal = row >= col
    acum = jnp.dot(causal.astype(F32), adt, precision=lax.Precision.HIGHEST,
                   preferred_element_type=F32)
    alast = acum[q - 1:q, :]
    dte = jnp.exp(alast - acum)
    ea = jnp.exp(acum)

    stack = jnp.concatenate([dt, dt * dte, ea], axis=0)
    lane3 = lax.broadcasted_iota(jnp.int32, stack.shape, 1)
    stack = jnp.where(lane3 < SSD_HEADS, stack, 0.0)
    hi = stack.astype(BF16).astype(F32)
    r1 = stack - hi
    mid = r1.astype(BF16).astype(F32)
    lo = (r1 - mid).astype(BF16).astype(F32)
    comb = hi + pltpu.roll(mid, SSD_HEADS, axis=1) + pltpu.roll(lo, 2 * SSD_HEADS, axis=1)
    expd = jnp.dot(comb.astype(BF16), e3_ref[...], preferred_element_type=F32)
    dt_e, w_e, ea_e = expd[0:q], expd[q:2 * q], expd[2 * q:3 * q]

    x = _pad_rows(xs_ref[...], q).astype(F32)
    xdt = (x * dt_e).astype(BF16)
    xdtd = (x * w_e).astype(BF16)
    bm = _pad_rows(b_ref[...], q)
    cm = _pad_rows(c_ref[...], q)

    acum_t = acum.T
    dec = jnp.exp(acum_t[0:SSD_HEADS, q - 1:q])
    dec = jnp.broadcast_to(dec, (SSD_HEADS, D_STATE))
    dec = jnp.broadcast_to(dec[:, None, :], (SSD_HEADS, SSD_HEAD_DIM, D_STATE))
    dec = dec.reshape(SSD_HEADS * SSD_HEAD_DIM, D_STATE)

    lane = lax.broadcasted_iota(jnp.int32, (q, LANES), 1)
    nt = (((1,), (1,)), ((), ()))
    tn = (((0,), (0,)), ((), ()))
    y_groups = []
    for g in range(SSD_GROUPS):
        gc = slice(g * GROUP_COLS, (g + 1) * GROUP_COLS)
        bg = bm[:, g * D_STATE:(g + 1) * D_STATE]
        cg = cm[:, g * D_STATE:(g + 1) * D_STATE]
        cb = lax.dot_general(cg, bg, nt, preferred_element_type=F32)
        sg = st_scr[gc, :]
        y_off = lax.dot_general(cg, sg.astype(BF16), nt, preferred_element_type=F32)
        pieces = []
        for pr in range(HEADS_PER_GROUP // 2):
            ms = []
            for hh in range(2):
                h = g * HEADS_PER_GROUP + 2 * pr + hh
                seg = jnp.broadcast_to(acum[:, h:h + 1], (q, q)) - acum_t[h:h + 1, :]
                lm = jnp.exp(jnp.where(causal, seg, -jnp.inf))
                ms.append((cb * lm).astype(BF16))
            m = jnp.concatenate(ms, axis=1)
            c0 = g * GROUP_COLS + pr * LANES
            xp = xdt[:, c0:c0 + LANES]
            rhs = jnp.concatenate([jnp.where(lane < SSD_HEAD_DIM, xp, jnp.zeros_like(xp)),
                                   jnp.where(lane >= SSD_HEAD_DIM, xp, jnp.zeros_like(xp))], axis=0)
            pieces.append(jnp.dot(m, rhs, preferred_element_type=F32))
        y_diag = jnp.concatenate(pieces, axis=1)
        y_groups.append(y_diag + y_off * ea_e[:, gc])
        new = lax.dot_general(xdtd[:, gc], bg, tn, preferred_element_type=F32)
        st_scr[gc, :] = sg * dec[gc, :] + new
    y = jnp.concatenate(y_groups, axis=1)
    y = y + dskip_ref[...] * x
    yz = y * _pad_rows(szs_ref[...], q).astype(F32)
    yn = yz * lax.rsqrt(jnp.mean(yz * yz, axis=-1, keepdims=True) + EPS) * nrm_ref[...]
    y_ref[...] = yn[0:q_in].astype(BF16)


def _ssd(act, dt, a_log, dskip_e, norm_ssd, e3, h0, *, n_seq, q_in, blocks_per_step):
    rows = act.shape[0]
    qb = q_in * blocks_per_step
    n_chunks = rows // (n_seq * qb)
    kern = functools.partial(_ssd_kernel, q_in=q_in, blocks_per_step=blocks_per_step, n_steps=n_chunks)
    rblk = lambda b, c: b * n_chunks + c
    bc_blk0 = 7 * D_MODEL // GROUP_COLS
    return pl.pallas_call(
        kern,
        grid=(n_seq, n_chunks),
        in_specs=[
            pl.BlockSpec((qb, D_MODEL), lambda b, c: (rblk(b, c), 0)),
            pl.BlockSpec((qb, D_MODEL), lambda b, c: (rblk(b, c), 1)),
            pl.BlockSpec((qb, GROUP_COLS), lambda b, c: (rblk(b, c), bc_blk0)),
            pl.BlockSpec((qb, GROUP_COLS), lambda b, c: (rblk(b, c), bc_blk0 + 1)),
            pl.BlockSpec((qb, LANES), lambda b, c: (rblk(b, c), 0)),
            pl.BlockSpec((1, LANES), lambda b, c: (0, 0)),
            pl.BlockSpec((1, D_MODEL), lambda b, c: (0, 0)),
            pl.BlockSpec((1, D_MODEL), lambda b, c: (0, 0)),
            pl.BlockSpec((LANES, D_MODEL), lambda b, c: (0, 0)),
            pl.BlockSpec((1, D_MODEL, D_STATE), lambda b, c: (b, 0, 0)),
        ],
        out_specs=[
            pl.BlockSpec((qb, D_MODEL), lambda b, c: (rblk(b, c), 0)),
            pl.BlockSpec((1, D_MODEL, D_STATE), lambda b, c: (b, 0, 0)),
        ],
        out_shape=[
            jax.ShapeDtypeStruct((rows, D_MODEL), BF16),
            jax.ShapeDtypeStruct((n_seq, D_MODEL, D_STATE), F32),
        ],
        scratch_shapes=[pltpu.VMEM((D_MODEL, D_STATE), F32)],
        compiler_params=pltpu.CompilerParams(
            dimension_semantics=("arbitrary", "arbitrary"), vmem_limit_bytes=VMEM_LIMIT),
        name="ssd",
    )(act, act, act, act, dt, a_log, dskip_e, norm_ssd, e3, h0)


def _mlp_stage(u_ref, v_ref, szm_ref, lng_ref, lnb_ref, wsp_ref, bsp_ref, ymlp_ref, vn_ref, *, tm, qm):
    v = v_ref[...].astype(F32)
    mu = jnp.mean(v, axis=-1, keepdims=True)
    vc = v - mu
    var = jnp.mean(vc * vc, axis=-1, keepdims=True)
    vn = vc * lax.rsqrt(var + EPS) * lng_ref[...] + lnb_ref[...]
    if vn_ref is not None:
        vn_ref[...] = vn
    vn_b = vn.astype(BF16)

    row = lax.broadcasted_iota(jnp.int32, (MLP_CHUNK, MLP_CHUNK), 0)
    col = lax.broadcasted_iota(jnp.int32, (MLP_CHUNK, MLP_CHUNK), 1)
    wms = [jnp.where(row >= col, wsp_ref[g], 0.0).astype(BF16) for g in range(MLP_GROUPS)]
    for ck in range(tm // qm):
        rows = slice(ck * qm, (ck + 1) * qm)
        vck = _pad_rows(vn_b[rows, :], MLP_CHUNK)
        mixed = jnp.concatenate(
            [jnp.dot(wms[g], vck[:, g * MLP_GROUP_DIM:(g + 1) * MLP_GROUP_DIM],
                     preferred_element_type=F32) for g in range(MLP_GROUPS)], axis=1)
        mixed = mixed[0:qm] + bsp_ref[...]
        y_mlp = u_ref[rows, :].astype(F32) * mixed * szm_ref[rows, :].astype(F32)
        ymlp_ref[rows, :] = y_mlp.astype(BF16)


def _mix_stage(yssd_ref, ymlp_ref, sgs_ref, sgm_ref, wbs_ref, wbm_ref, wo_ref):
    a = jnp.dot(yssd_ref[...], wbs_ref[...], preferred_element_type=F32)
    b = jnp.dot(ymlp_ref[...], wbm_ref[...], preferred_element_type=F32)
    merged = sgs_ref[...].astype(F32) * a + sgm_ref[...].astype(F32) * b
    return jnp.dot(merged.astype(BF16), wo_ref[...], preferred_element_type=F32)


def _out_stage(o, x_ref, gate_ref, gfin_ref, y_ref, *, tm, spt):
    rps = tm // spt
    for s in range(spt):
        rows = slice(s * rps, (s + 1) * rps)
        out = x_ref[rows, :] + gate_ref[s] * o[rows, :]
        y = out * lax.rsqrt(jnp.mean(out * out, axis=-1, keepdims=True) + EPS)
        y_ref[rows, :] = y * gfin_ref[...]


def _merge_kernel(yssd_ref, u_ref, v_ref, szm_ref, sgs_ref, sgm_ref, x_ref, gate_ref,
                  lng_ref, lnb_ref, gfin_ref, wsp_ref, bsp_ref, wbs_ref, wbm_ref, wo_ref,
                  y_ref, *rest, tm, qm, spt, emit_vn):
    if emit_vn:
        vn_ref, ymlp_scr = rest
    else:
        vn_ref, (ymlp_scr,) = None, rest
    _mlp_stage(u_ref, v_ref, szm_ref, lng_ref, lnb_ref, wsp_ref, bsp_ref, ymlp_scr, vn_ref, tm=tm, qm=qm)
    o = _mix_stage(yssd_ref, ymlp_scr, sgs_ref, sgm_ref, wbs_ref, wbm_ref, wo_ref)
    _out_stage(o, x_ref, gate_ref, gfin_ref, y_ref, tm=tm, spt=spt)


def _merge(yssd, act, x2d, gate, ln_g, ln_b, g_final, w_spatial, bsp_e, wbs, wbm, wo,
           *, tm, qm, spt, emit_vn):
    rows = x2d.shape[0]
    n_seq = gate.shape[0]
    n_tiles = rows // tm
    tiles_per_seq = max(1, n_tiles // n_seq) if spt == 1 else 1
    seq_blk = (lambda i: i // tiles_per_seq) if spt == 1 else (lambda i: i)
    kern = functools.partial(_merge_kernel, tm=tm, qm=qm, spt=spt, emit_vn=emit_vn)
    const = lambda shape: pl.BlockSpec(shape, lambda i: (0,) * len(shape),
                                       pipeline_mode=pl.Buffered(1))
    seg = lambda k: pl.BlockSpec((tm, D_MODEL), lambda i: (i, k))
    out_specs = [pl.BlockSpec((tm, D_MODEL), lambda i: (i, 0))]
    out_shape = [jax.ShapeDtypeStruct((rows, D_MODEL), F32)]
    if emit_vn:
        out_specs.append(pl.BlockSpec((tm, D_MODEL), lambda i: (i, 0)))
        out_shape.append(jax.ShapeDtypeStruct((rows, D_MODEL), F32))
    return pl.pallas_call(
        kern,
        grid=(n_tiles,),
        in_specs=[
            pl.BlockSpec((tm, D_MODEL), lambda i: (i, 0)),
            seg(2), seg(3), seg(4), seg(5), seg(6),
            pl.BlockSpec((tm, D_MODEL), lambda i: (i, 0)),
            pl.BlockSpec((spt, 1, D_MODEL), lambda i: (seq_blk(i), 0, 0)),
            const((1, D_MODEL)), const((1, D_MODEL)), const((1, D_MODEL)),
            const((MLP_GROUPS, MLP_CHUNK, MLP_CHUNK)),
            const((qm, D_MODEL)),
            const((D_MODEL, D_MODEL)), const((D_MODEL, D_MODEL)), const((D_MODEL, D_MODEL)),
        ],
        out_specs=out_specs,
        out_shape=out_shape,
        scratch_shapes=[pltpu.VMEM((tm, D_MODEL), BF16)],
        compiler_params=pltpu.CompilerParams(
            dimension_semantics=("arbitrary",), vmem_limit_bytes=VMEM_LIMIT),
        name="merge",
    )(yssd, act, act, act, act, act, x2d, gate, ln_g, ln_b, g_final, w_spatial, bsp_e, wbs, wbm, wo)


def _cast_kernel(a_ref, b_ref, c_ref, ao_ref, bo_ref, co_ref):
    ao_ref[...] = a_ref[...].astype(BF16)
    bo_ref[...] = b_ref[...].astype(BF16)
    co_ref[...] = c_ref[...].astype(BF16)


def _cast_bf16(a, b, c, *, tm=512):
    rows, cols = a.shape
    spec = pl.BlockSpec((tm, cols), lambda i: (i, 0))
    return pl.pallas_call(
        _cast_kernel,
        grid=(rows // tm,),
        in_specs=[spec] * 3,
        out_specs=[spec] * 3,
        out_shape=[jax.ShapeDtypeStruct((rows, cols), BF16)] * 3,
        compiler_params=pltpu.CompilerParams(
            dimension_semantics=("arbitrary",), vmem_limit_bytes=VMEM_LIMIT),
        name="cast",
    )(a, b, c)


def _expansion_matrix():
    e = np.zeros((LANES, D_MODEL), np.float32)
    cols = np.arange(D_MODEL)
    for piece in range(3):
        e[piece * SSD_HEADS + cols // SSD_HEAD_DIM, cols] = 1.0
    return jnp.asarray(e, BF16)


def kernel(x_prompt, x_sample, state_ssm, cache_conv, c_prompt, c_sample, w_ada, b_ada, g_pre, w_in,
           conv_w, conv_b, dt_bias, a_log, d_skip, norm_ssd, ln_g, ln_b, w_spatial, b_spatial,
           w_bp_ssd, w_bp_mlp, w_o, g_final):
    depth = w_ada.shape[0]
    assert depth == 1, "single-layer trunk only"
    bp, seq, d = x_prompt.shape
    bs, dec_seq, _ = x_sample.shape
    assert d == D_MODEL and seq % 1024 == 0 and dec_seq % 16 == 0 and dec_seq <= SSD_BLOCK

    w_t = jnp.swapaxes(w_in[0], 0, 1)
    w_dt = jnp.pad(w_t[DT_OFFSET:DT_OFFSET + SSD_HEADS], ((0, LANES - SSD_HEADS), (0, 0))).astype(BF16)
    pad_heads = lambda v: jnp.pad(v.reshape(1, SSD_HEADS), ((0, 0), (0, LANES - SSD_HEADS)))
    wts = dict(
        g_pre=g_pre[0].reshape(1, d), w_t=w_t, w_dt=w_dt, dt_bias=pad_heads(dt_bias[0]),
        conv_w=conv_w[0], conv_b=conv_b[0].reshape(1, CONV_DIM),
        a_log=pad_heads(a_log[0]), dskip_e=jnp.repeat(d_skip[0], SSD_HEAD_DIM).reshape(1, d),
        norm_ssd=norm_ssd[0].reshape(1, d), e3=_expansion_matrix(),
        ln_g=ln_g[0].reshape(1, d), ln_b=ln_b[0].reshape(1, d), g_final=g_final.reshape(1, d),
        w_spatial=w_spatial[0], bsp_e=jnp.repeat(b_spatial[0].T, MLP_GROUP_DIM, axis=1),
    )

    c_all = jnp.concatenate([c_prompt, c_sample], axis=0)
    c_pad = jnp.pad(c_all, ((0, 16 - (bp + bs)), (0, 0)))
    mods = _mods(c_pad, w_ada[0], b_ada[0].reshape(1, 3 * d))

    split_mods = lambda m: (m[:, k * d:(k + 1) * d].reshape(m.shape[0], 1, d) for k in range(3))
    shift_p, scale_p, gate_p = split_mods(mods[:bp])
    shift_s, scale_s, gate_s = split_mods(mods[bp:bp + bs])
    x2p = x_prompt.reshape(bp * seq, d)
    x2s = x_sample.reshape(bs * dec_seq, d)
    rows_s = bs * dec_seq

    hist_p = jnp.zeros((bp, SUBLANES, CONV_DIM), F32)
    h0_p = jnp.zeros((bp, d, D_STATE), F32)
    hist_s = jnp.pad(cache_conv[0], ((0, 0), (SUBLANES - (CONV_W - 1), 0), (0, 0)))
    h0_s = state_ssm[0].reshape(bs, d, D_STATE)

    norm_w = (wts["g_pre"], wts["w_dt"], wts["dt_bias"])
    hp, dt_p = _norm(x2p, scale_p, shift_p, *norm_w, tm=NORM_TM, spt=1)
    hs, dt_s = _norm(x2s, scale_s, shift_s, *norm_w, tm=rows_s, spt=bs)
    act_p, act_s, tail_p, tail_s = _proj(hp, hs, wts["w_t"], wts["conv_w"], wts["conv_b"], hist_p, hist_s,
                                         tm_p=PROJ_TM, nsplit=PROJ_NSPLIT, msplit_p=PROJ_MSPLIT)

    ssd_w = (wts["a_log"], wts["dskip_e"], wts["norm_ssd"], wts["e3"])
    yssd_p, ssm_p = _ssd(act_p, dt_p, *ssd_w, h0_p, n_seq=bp, q_in=SSD_BLOCK,
                         blocks_per_step=SSD_BLOCKS_PER_STEP)
    yssd_s, ssm_s = _ssd(act_s, dt_s, *ssd_w, h0_s, n_seq=bs, q_in=dec_seq, blocks_per_step=1)

    merge_w = (wts["ln_g"], wts["ln_b"], wts["g_final"], wts["w_spatial"])
    proj_w = _cast_bf16(w_bp_ssd[0], w_bp_mlp[0], w_o[0])
    (yp,) = _merge(yssd_p, act_p, x2p, gate_p, *merge_w, wts["bsp_e"], *proj_w,
                   tm=MERGE_TM, qm=MLP_CHUNK, spt=1, emit_vn=False)
    ys, vn_s = _merge(yssd_s, act_s, x2s, gate_s, *merge_w, wts["bsp_e"][:dec_seq], *proj_w,
                      tm=rows_s, qm=dec_seq, spt=bs, emit_vn=True)

    conv_cols = slice(CONV_STEPS[0] * PROJ_TN, CONV_STEPS[0] * PROJ_TN + CONV_DIM)
    hist_rows = slice(SUBLANES - (CONV_W - 1), SUBLANES)
    tiles_per_seq = seq // PROJ_TM
    conv_p = tail_p[tiles_per_seq - 1::tiles_per_seq, hist_rows, conv_cols]
    conv_s = tail_s[:, hist_rows, conv_cols]

    state_shape = (SSD_HEADS, SSD_HEAD_DIM, D_STATE)
    return (yp.reshape(bp, seq, d), ys.reshape(bs, dec_seq, d),
            ssm_p.reshape(1, bp, *state_shape), conv_p[None],
            ssm_s.reshape(1, bs, *state_shape), conv_s[None], vn_s.reshape(1, bs, dec_seq, d))
```

```python
import functools
from typing import Any, NamedTuple

import numpy as np
import jax
import jax.numpy as jnp
from jax import lax
from jax.experimental import pallas as pl
from jax.experimental.pallas import tpu as pltpu

F32 = jnp.float32
BF16 = jnp.bfloat16

D_MODEL = 2048
SSD_HEADS = 32
SSD_HEAD_DIM = 64
SSD_GROUPS = 4
HEADS_PER_GROUP = SSD_HEADS // SSD_GROUPS
D_STATE = 128
GROUP_COLS = HEADS_PER_GROUP * SSD_HEAD_DIM
CONV_W = 4
CONV_DIM = D_MODEL + 2 * SSD_GROUPS * D_STATE
MLP_CHUNK = 128
MLP_GROUPS = 8
MLP_GROUP_DIM = D_MODEL // MLP_GROUPS
EPS = 1e-5

SUBLANES = 8
LANES = 128
SSD_BLOCK = 128
SSD_BLOCKS_PER_STEP = 8
PROJ_TN = 1024
VMEM_LIMIT = 56 * 1024 * 1024

N_ACT = 7 * D_MODEL + 2 * SSD_GROUPS * D_STATE
N_PROJ_TILES = N_ACT // PROJ_TN
DT_OFFSET = D_MODEL + CONV_DIM
FIRST_SHIFTED_TILE = DT_OFFSET // PROJ_TN
CONV_STEPS = (2, 3, 4)
IDENT_STEPS = (5, 6, 7, 8)
FIRST_SIGMOID_STEP = 11
PROJ_TM = 1024
PROJ_NSPLIT = 2
PROJ_MSPLIT = 4
NORM_TM = 512
MERGE_TM = 256


def _silu(x):
    hx = 0.5 * x
    return hx + hx * jnp.tanh(hx)


def _softplus(x):
    return jnp.maximum(x, 0.0) + jnp.log1p(jnp.exp(-jnp.abs(x)))


def _mods_kernel(c_ref, w_ref, b_ref, o_ref):
    s = _silu(c_ref[...]).astype(BF16)
    o_ref[...] = jnp.dot(s, w_ref[...].astype(BF16), preferred_element_type=F32) + b_ref[...]


def _mods(c_pad, w_ada, b_ada):
    rows, d = c_pad.shape
    n = w_ada.shape[1]
    tn = 1024
    return pl.pallas_call(
        _mods_kernel,
        grid=(n // tn,),
        in_specs=[
            pl.BlockSpec((rows, d), lambda j: (0, 0)),
            pl.BlockSpec((d, tn), lambda j: (0, j)),
            pl.BlockSpec((1, tn), lambda j: (0, j)),
        ],
        out_specs=pl.BlockSpec((rows, tn), lambda j: (0, j)),
        out_shape=jax.ShapeDtypeStruct((rows, n), F32),
        compiler_params=pltpu.CompilerParams(
            dimension_semantics=("arbitrary",), vmem_limit_bytes=VMEM_LIMIT),
        name="mods",
    )(c_pad, w_ada, b_ada)


def _norm_kernel(x_ref, sc_ref, sh_ref, gpre_ref, wdt_ref, dtb_ref, h_ref, dt_ref, *, tm, spt):
    rps = tm // spt
    for s in range(spt):
        rows = slice(s * rps, (s + 1) * rps)
        x = x_ref[rows, :]
        y = x * lax.rsqrt(jnp.mean(x * x, axis=-1, keepdims=True) + EPS)
        y = y * gpre_ref[...]
        h_ref[rows, :] = (y * (1.0 + sc_ref[s]) + sh_ref[s]).astype(BF16)
    dt_raw = lax.dot_general(h_ref[...], wdt_ref[...], (((1,), (1,)), ((), ())),
                             preferred_element_type=F32) + dtb_ref[...]
    lane = lax.broadcasted_iota(jnp.int32, dt_raw.shape, 1)
    dt_ref[...] = jnp.where(lane < SSD_HEADS, _softplus(dt_raw), 0.0)


def _norm(x2d, scale, shift, g_pre, w_dt, dt_bias, *, tm, spt):
    rows = x2d.shape[0]
    n_seq = scale.shape[0]
    n_tiles = rows // tm
    tiles_per_seq = max(1, n_tiles // n_seq) if spt == 1 else 1
    seq_blk = (lambda i: i // tiles_per_seq) if spt == 1 else (lambda i: i)
    return pl.pallas_call(
        functools.partial(_norm_kernel, tm=tm, spt=spt),
        grid=(n_tiles,),
        in_specs=[
            pl.BlockSpec((tm, D_MODEL), lambda i: (i, 0)),
            pl.BlockSpec((spt, 1, D_MODEL), lambda i: (seq_blk(i), 0, 0)),
            pl.BlockSpec((spt, 1, D_MODEL), lambda i: (seq_blk(i), 0, 0)),
            pl.BlockSpec((1, D_MODEL), lambda i: (0, 0)),
            pl.BlockSpec((LANES, D_MODEL), lambda i: (0, 0)),
            pl.BlockSpec((1, LANES), lambda i: (0, 0)),
        ],
        out_specs=[
            pl.BlockSpec((tm, D_MODEL), lambda i: (i, 0)),
            pl.BlockSpec((tm, LANES), lambda i: (i, 0)),
        ],
        out_shape=[
            jax.ShapeDtypeStruct((rows, D_MODEL), BF16),
            jax.ShapeDtypeStruct((rows, LANES), F32),
        ],
        compiler_params=pltpu.CompilerParams(
            dimension_semantics=("arbitrary",), vmem_limit_bytes=VMEM_LIMIT),
        name="norm",
    )(x2d, scale, shift, g_pre, w_dt, dt_bias)


def _act_tile(j):
    return jnp.where(j < CONV_STEPS[2], j, jnp.where(j == CONV_STEPS[2], N_PROJ_TILES - 1, j - 1))


def _wt_row(j):
    return pl.multiple_of(j * PROJ_TN + jnp.where(j >= FIRST_SHIFTED_TILE, SSD_HEADS, 0), SSD_HEADS)


def _conv_in_tile(j):
    return jnp.clip(j - CONV_STEPS[0], 0, len(CONV_STEPS) - 1)


class _Rows(NamedTuple):
    h_ref: Any
    act_ref: Any
    tail_ref: Any
    hist_ref: Any
    tm: int
    spt: int
    msplit: int
    first: Any


def _proj_step(j, row_sets, cw_ref, cb_ref, wbf_scr, halo_scr, stgs, shift_scr, *, nsplit):
    cw = PROJ_TN // nsplit
    blocks = [(rs, nb, mb) for nb in range(nsplit) for rs in row_sets for mb in range(rs.msplit)]

    def dot_block(k):
        rs, nb, mb = blocks[k]
        rb = rs.tm // rs.msplit
        stgs[k % 2][SUBLANES:SUBLANES + rb, :] = lax.dot_general(
            rs.h_ref[mb * rb:(mb + 1) * rb, :], wbf_scr[nb * cw:(nb + 1) * cw, :],
            (((1,), (1,)), ((), ())), preferred_element_type=F32)

    def run(epilogue):
        dot_block(0)
        for k, (rs, nb, mb) in enumerate(blocks):
            if k + 1 < len(blocks):
                dot_block(k + 1)
            epilogue(rs, nb, mb, stgs[k % 2])

    is_conv = jnp.logical_and(j >= CONV_STEPS[0], j <= CONV_STEPS[2])
    is_ident = jnp.logical_and(j >= IDENT_STEPS[0], j <= IDENT_STEPS[-1])
    is_sigmoid = j >= FIRST_SIGMOID_STEP

    p_x = jnp.where(is_ident, 1.0, 0.5).astype(F32)
    p_xt = jnp.where(is_ident, 0.0, 0.5).astype(F32)

    def elementwise(rs, nb, mb, stg):
        rb = rs.tm // rs.msplit
        x = stg[SUBLANES:SUBLANES + rb, :]
        t = jnp.tanh(0.5 * x)
        y = jnp.where(is_sigmoid, 0.5 + 0.5 * t, x * (p_x + p_xt * t))
        rs.act_ref[mb * rb:(mb + 1) * rb, nb * cw:(nb + 1) * cw] = y.astype(BF16)

    @pl.when(jnp.logical_not(is_conv))
    def _():
        for rs in row_sets:
            rs.tail_ref[...] = jnp.zeros_like(rs.tail_ref)
        run(elementwise)

    @pl.when(is_conv)
    def _():
        for rs in row_sets:
            if rs.spt == 1:
                halo_scr[...] = jnp.where(rs.first, rs.hist_ref[0], halo_scr[...])

        shift_scr[0:SUBLANES, :] = jnp.zeros((SUBLANES, cw), F32)

        def shift_down(v):
            n = v.shape[0]
            shift_scr[1:1 + n, :] = v
            return shift_scr[0:n, :]

        def conv(ext, cols):
            w0, w1, w2, w3 = (0.5 * cw_ref[k:k + 1, cols] for k in range(CONV_W))
            t = ext * w0
            t = shift_down(t) + ext * w1
            t = shift_down(t) + ext * w2
            t = shift_down(t) + ext * w3 + 0.5 * cb_ref[:, cols]
            t = t[SUBLANES:, :]
            return (t + t * jnp.tanh(t)).astype(BF16)

        def ep(rs, nb, mb, stg):
            cols = slice(nb * cw, (nb + 1) * cw)
            rb = rs.tm // rs.msplit
            if rs.spt == 1:
                stg[0:SUBLANES, :] = halo_scr[:, cols]
                rs.act_ref[mb * rb:(mb + 1) * rb, cols] = conv(stg[0:SUBLANES + rb, :], cols)
                last = stg[rb:rb + SUBLANES, :]
                halo_scr[:, cols] = last
                rs.tail_ref[0, :, cols] = last
            else:
                rps = rs.tm // rs.spt
                for s in range(rb // rps):
                    rows = slice(SUBLANES + s * rps, SUBLANES + (s + 1) * rps)
                    seq = mb * (rb // rps) + s
                    ext = jnp.concatenate([rs.hist_ref[seq, :, cols], stg[rows, :]], axis=0)
                    rs.act_ref[mb * rb + s * rps:mb * rb + (s + 1) * rps, cols] = conv(ext, cols)
                    rs.tail_ref[seq, :, cols] = stg[rows.stop - SUBLANES:rows.stop, :]

        run(ep)


def _proj_kernel(hp_ref, hs_ref, wt_ref, cw_ref, cb_ref, histp_ref, hists_ref,
                 actp_ref, acts_ref, tailp_ref, tails_ref, wbf_scr, halo_scr, stg0_scr, stg1_scr,
                 shift_scr, *, n_p, tm_p, tm_s, spt_s, nsplit, msplit_p, tiles_per_seq):
    j = pl.program_id(0)
    i = pl.program_id(1)
    wrows = 256

    @pl.when(i == 0)
    def _():
        for r in range(PROJ_TN // wrows):
            rows = slice(r * wrows, (r + 1) * wrows)
            wbf_scr[rows, :] = wt_ref[rows, :].astype(BF16)

    shared = (cw_ref, cb_ref, wbf_scr, halo_scr, (stg0_scr, stg1_scr), shift_scr)
    prompt = _Rows(hp_ref, actp_ref, tailp_ref, histp_ref, tm=tm_p, spt=1, msplit=msplit_p,
                   first=(i % tiles_per_seq) == 0)
    sample = _Rows(hs_ref, acts_ref, tails_ref, hists_ref, tm=tm_s, spt=spt_s, msplit=1, first=None)

    @pl.when(i < n_p - 1)
    def _():
        _proj_step(j, [prompt], *shared, nsplit=nsplit)

    @pl.when(i == n_p - 1)
    def _():
        _proj_step(j, [prompt, sample], *shared, nsplit=nsplit)


def _proj(hp, hs, w_t, conv_w, conv_b, hist_p, hist_s, *, tm_p, nsplit, msplit_p):
    rows_p, rows_s = hp.shape[0], hs.shape[0]
    n_p = rows_p // tm_p
    tiles_per_seq = n_p // hist_p.shape[0]
    spt_s = hist_s.shape[0]
    rb = max(tm_p // msplit_p, rows_s)
    kern = functools.partial(_proj_kernel, n_p=n_p, tm_p=tm_p, tm_s=rows_s, spt_s=spt_s,
                             nsplit=nsplit, msplit_p=msplit_p, tiles_per_seq=tiles_per_seq)
    return pl.pallas_call(
        kern,
        grid=(N_PROJ_TILES, n_p),
        in_specs=[
            pl.BlockSpec((tm_p, D_MODEL), lambda j, i: (i, 0)),
            pl.BlockSpec((rows_s, D_MODEL), lambda j, i: (0, 0)),
            pl.BlockSpec((pl.Element(PROJ_TN), pl.Element(D_MODEL)), lambda j, i: (_wt_row(j), 0)),
            pl.BlockSpec((CONV_W, PROJ_TN), lambda j, i: (0, _conv_in_tile(j))),
            pl.BlockSpec((1, PROJ_TN), lambda j, i: (0, _conv_in_tile(j))),
            pl.BlockSpec((1, SUBLANES, PROJ_TN), lambda j, i: (i // tiles_per_seq, 0, _conv_in_tile(j))),
            pl.BlockSpec((spt_s, SUBLANES, PROJ_TN), lambda j, i: (0, 0, _conv_in_tile(j))),
        ],
        out_specs=[
            pl.BlockSpec((tm_p, PROJ_TN), lambda j, i: (i, _act_tile(j))),
            pl.BlockSpec((rows_s, PROJ_TN), lambda j, i: (0, _act_tile(j))),
            pl.BlockSpec((1, SUBLANES, PROJ_TN), lambda j, i: (i, 0, j)),
            pl.BlockSpec((spt_s, SUBLANES, PROJ_TN), lambda j, i: (0, 0, j)),
        ],
        out_shape=[
            jax.ShapeDtypeStruct((rows_p, N_ACT), BF16),
            jax.ShapeDtypeStruct((rows_s, N_ACT), BF16),
            jax.ShapeDtypeStruct((n_p, SUBLANES, N_PROJ_TILES * PROJ_TN), F32),
            jax.ShapeDtypeStruct((spt_s, SUBLANES, N_PROJ_TILES * PROJ_TN), F32),
        ],
        scratch_shapes=[
            pltpu.VMEM((PROJ_TN, D_MODEL), BF16),
            pltpu.VMEM((SUBLANES, PROJ_TN), F32),
            pltpu.VMEM((rb + SUBLANES, PROJ_TN // nsplit), F32),
            pltpu.VMEM((rb + SUBLANES, PROJ_TN // nsplit), F32),
            pltpu.VMEM((rb + 2 * SUBLANES, PROJ_TN // nsplit), F32),
        ],
        compiler_params=pltpu.CompilerParams(
            dimension_semantics=("arbitrary", "arbitrary"), vmem_limit_bytes=VMEM_LIMIT),
        name="proj",
    )(hp, hs, w_t, conv_w, conv_b, hist_p, hist_s)


def _pad_rows(v, rows):
    if v.shape[0] == rows:
        return v
    return jnp.concatenate([v, jnp.zeros((rows - v.shape[0], v.shape[1]), v.dtype)], axis=0)


def _ssd_kernel(szs_ref, xs_ref, b_ref, c_ref, dt_ref, alog_ref, dskip_ref, nrm_ref, e3_ref, h0_ref,
                y_ref, hout_ref, st_scr, *, q_in, blocks_per_step, n_steps):
    ci = pl.program_id(1)

    @pl.when(ci == 0)
    def _():
        st_scr[...] = h0_ref[0]

    for k in range(blocks_per_step):
        rs = pl.ds(k * q_in, q_in)
        _ssd_block(szs_ref.at[rs], xs_ref.at[rs], b_ref.at[rs], c_ref.at[rs], dt_ref.at[rs],
                   alog_ref, dskip_ref, nrm_ref, e3_ref, y_ref.at[rs], st_scr, q_in=q_in)

    @pl.when(ci == n_steps - 1)
    def _():
        hout_ref[0] = st_scr[...]


def _ssd_block(szs_ref, xs_ref, b_ref, c_ref, dt_ref, alog_ref, dskip_ref, nrm_ref, e3_ref,
               y_ref, st_scr, *, q_in):
    q = SSD_BLOCK
    dt = _pad_rows(dt_ref[...], q)
    a = -jnp.exp(alog_ref[...])
    adt = dt * a
    row = lax.broadcasted_iota(jnp.int32, (q, q), 0)
    col = lax.broadcasted_iota(jnp.int32, (q, q), 1)
    causal = row >= col
    acum = jnp.dot(causal.astype(F32), adt, precision=lax.Precision.HIGHEST,
                   preferred_element_type=F32)
    alast = acum[q - 1:q, :]
    dte = jnp.exp(alast - acum)
    ea = jnp.exp(acum)

    stack = jnp.concatenate([dt, dt * dte, ea], axis=0)
    lane3 = lax.broadcasted_iota(jnp.int32, stack.shape, 1)
    stack = jnp.where(lane3 < SSD_HEADS, stack, 0.0)
    hi = stack.astype(BF16).astype(F32)
    r1 = stack - hi
    mid = r1.astype(BF16).astype(F32)
    lo = (r1 - mid).astype(BF16).astype(F32)
    comb = hi + pltpu.roll(mid, SSD_HEADS, axis=1) + pltpu.roll(lo, 2 * SSD_HEADS, axis=1)
    expd = jnp.dot(comb.astype(BF16), e3_ref[...], preferred_element_type=F32)
    dt_e, w_e, ea_e = expd[0:q], expd[q:2 * q], expd[2 * q:3 * q]

    x = _pad_rows(xs_ref[...], q).astype(F32)
    xdt = (x * dt_e).astype(BF16)
    xdtd = (x * w_e).astype(BF16)
    bm = _pad_rows(b_ref[...], q)
    cm = _pad_rows(c_ref[...], q)

    acum_t = acum.T
    dec = jnp.exp(acum_t[0:SSD_HEADS, q - 1:q])
    dec = jnp.broadcast_to(dec, (SSD_HEADS, D_STATE))
    dec = jnp.broadcast_to(dec[:, None, :], (SSD_HEADS, SSD_HEAD_DIM, D_STATE))
    dec = dec.reshape(SSD_HEADS * SSD_HEAD_DIM, D_STATE)

    lane = lax.broadcasted_iota(jnp.int32, (q, LANES), 1)
    nt = (((1,), (1,)), ((), ()))
    tn = (((0,), (0,)), ((), ()))
    y_groups = []
    for g in range(SSD_GROUPS):
        gc = slice(g * GROUP_COLS, (g + 1) * GROUP_COLS)
        bg = bm[:, g * D_STATE:(g + 1) * D_STATE]
        cg = cm[:, g * D_STATE:(g + 1) * D_STATE]
        cb = lax.dot_general(cg, bg, nt, preferred_element_type=F32)
        sg = st_scr[gc, :]
        y_off = lax.dot_general(cg, sg.astype(BF16), nt, preferred_element_type=F32)
        pieces = []
        for pr in range(HEADS_PER_GROUP // 2):
            ms = []
            for hh in range(2):
                h = g * HEADS_PER_GROUP + 2 * pr + hh
                seg = jnp.broadcast_to(acum[:, h:h + 1], (q, q)) - acum_t[h:h + 1, :]
                lm = jnp.exp(jnp.where(causal, seg, -jnp.inf))
                ms.append((cb * lm).astype(BF16))
            m = jnp.concatenate(ms, axis=1)
            c0 = g * GROUP_COLS + pr * LANES
            xp = xdt[:, c0:c0 + LANES]
            rhs = jnp.concatenate([jnp.where(lane < SSD_HEAD_DIM, xp, jnp.zeros_like(xp)),
                                   jnp.where(lane >= SSD_HEAD_DIM, xp, jnp.zeros_like(xp))], axis=0)
            pieces.append(jnp.dot(m, rhs, preferred_element_type=F32))
        y_diag = jnp.concatenate(pieces, axis=1)
        y_groups.append(y_diag + y_off * ea_e[:, gc])
        new = lax.dot_general(xdtd[:, gc], bg, tn, preferred_element_type=F32)
        st_scr[gc, :] = sg * dec[gc, :] + new
    y = jnp.concatenate(y_groups, axis=1)
    y = y + dskip_ref[...] * x
    yz = y * _pad_rows(szs_ref[...], q).astype(F32)
    yn = yz * lax.rsqrt(jnp.mean(yz * yz, axis=-1, keepdims=True) + EPS) * nrm_ref[...]
    y_ref[...] = yn[0:q_in].astype(BF16)


def _ssd(act, dt, a_log, dskip_e, norm_ssd, e3, h0, *, n_seq, q_in, blocks_per_step):
    rows = act.shape[0]
    qb = q_in * blocks_per_step
    n_chunks = rows // (n_seq * qb)
    kern = functools.partial(_ssd_kernel, q_in=q_in, blocks_per_step=blocks_per_step, n_steps=n_chunks)
    rblk = lambda b, c: b * n_chunks + c
    bc_blk0 = 7 * D_MODEL // GROUP_COLS
    return pl.pallas_call(
        kern,
        grid=(n_seq, n_chunks),
        in_specs=[
            pl.BlockSpec((qb, D_MODEL), lambda b, c: (rblk(b, c), 0)),
            pl.BlockSpec((qb, D_MODEL), lambda b, c: (rblk(b, c), 1)),
            pl.BlockSpec((qb, GROUP_COLS), lambda b, c: (rblk(b, c), bc_blk0)),
            pl.BlockSpec((qb, GROUP_COLS), lambda b, c: (rblk(b, c), bc_blk0 + 1)),
            pl.BlockSpec((qb, LANES), lambda b, c: (rblk(b, c), 0)),
            pl.BlockSpec((1, LANES), lambda b, c: (0, 0)),
            pl.BlockSpec((1, D_MODEL), lambda b, c: (0, 0)),
            pl.BlockSpec((1, D_MODEL), lambda b, c: (0, 0)),
            pl.BlockSpec((LANES, D_MODEL), lambda b, c: (0, 0)),
            pl.BlockSpec((1, D_MODEL, D_STATE), lambda b, c: (b, 0, 0)),
        ],
        out_specs=[
            pl.BlockSpec((qb, D_MODEL), lambda b, c: (rblk(b, c), 0)),
            pl.BlockSpec((1, D_MODEL, D_STATE), lambda b, c: (b, 0, 0)),
        ],
        out_shape=[
            jax.ShapeDtypeStruct((rows, D_MODEL), BF16),
            jax.ShapeDtypeStruct((n_seq, D_MODEL, D_STATE), F32),
        ],
        scratch_shapes=[pltpu.VMEM((D_MODEL, D_STATE), F32)],
        compiler_params=pltpu.CompilerParams(
            dimension_semantics=("arbitrary", "arbitrary"), vmem_limit_bytes=VMEM_LIMIT),
        name="ssd",
    )(act, act, act, act, dt, a_log, dskip_e, norm_ssd, e3, h0)


def _mlp_stage(u_ref, v_ref, szm_ref, lng_ref, lnb_ref, wsp_ref, bsp_ref, ymlp_ref, vn_ref, *, tm, qm):
    v = v_ref[...].astype(F32)
    mu = jnp.mean(v, axis=-1, keepdims=True)
    vc = v - mu
    var = jnp.mean(vc * vc, axis=-1, keepdims=True)
    vn = vc * lax.rsqrt(var + EPS) * lng_ref[...] + lnb_ref[...]
    if vn_ref is not None:
        vn_ref[...] = vn
    vn_b = vn.astype(BF16)

    row = lax.broadcasted_iota(jnp.int32, (MLP_CHUNK, MLP_CHUNK), 0)
    col = lax.broadcasted_iota(jnp.int32, (MLP_CHUNK, MLP_CHUNK), 1)
    wms = [jnp.where(row >= col, wsp_ref[g], 0.0).astype(BF16) for g in range(MLP_GROUPS)]
    for ck in range(tm // qm):
        rows = slice(ck * qm, (ck + 1) * qm)
        vck = _pad_rows(vn_b[rows, :], MLP_CHUNK)
        mixed = jnp.concatenate(
            [jnp.dot(wms[g], vck[:, g * MLP_GROUP_DIM:(g + 1) * MLP_GROUP_DIM],
                     preferred_element_type=F32) for g in range(MLP_GROUPS)], axis=1)
        mixed = mixed[0:qm] + bsp_ref[...]
        y_mlp = u_ref[rows, :].astype(F32) * mixed * szm_ref[rows, :].astype(F32)
        ymlp_ref[rows, :] = y_mlp.astype(BF16)


def _mix_stage(yssd_ref, ymlp_ref, sgs_ref, sgm_ref, wbs_ref, wbm_ref, wo_ref):
    a = jnp.dot(yssd_ref[...], wbs_ref[...], preferred_element_type=F32)
    b = jnp.dot(ymlp_ref[...], wbm_ref[...], preferred_element_type=F32)
    merged = sgs_ref[...].astype(F32) * a + sgm_ref[...].astype(F32) * b
    return jnp.dot(merged.astype(BF16), wo_ref[...], preferred_element_type=F32)


def _out_stage(o, x_ref, gate_ref, gfin_ref, y_ref, *, tm, spt):
    rps = tm // spt
    for s in range(spt):
        rows = slice(s * rps, (s + 1) * rps)
        out = x_ref[rows, :] + gate_ref[s] * o[rows, :]
        y = out * lax.rsqrt(jnp.mean(out * out, axis=-1, keepdims=True) + EPS)
        y_ref[rows, :] = y * gfin_ref[...]


def _merge_kernel(yssd_ref, u_ref, v_ref, szm_ref, sgs_ref, sgm_ref, x_ref, gate_ref,
                  lng_ref, lnb_ref, gfin_ref, wsp_ref, bsp_ref, wbs_ref, wbm_ref, wo_ref,
                  y_ref, *rest, tm, qm, spt, emit_vn):
    if emit_vn:
        vn_ref, ymlp_scr = rest
    else:
        vn_ref, (ymlp_scr,) = None, rest
    _mlp_stage(u_ref, v_ref, szm_ref, lng_ref, lnb_ref, wsp_ref, bsp_ref, ymlp_scr, vn_ref, tm=tm, qm=qm)
    o = _mix_stage(yssd_ref, ymlp_scr, sgs_ref, sgm_ref, wbs_ref, wbm_ref, wo_ref)
    _out_stage(o, x_ref, gate_ref, gfin_ref, y_ref, tm=tm, spt=spt)


def _merge(yssd, act, x2d, gate, ln_g, ln_b, g_final, w_spatial, bsp_e, wbs, wbm, wo,
           *, tm, qm, spt, emit_vn):
    rows = x2d.shape[0]
    n_seq = gate.shape[0]
    n_tiles = rows // tm
    tiles_per_seq = max(1, n_tiles // n_seq) if spt == 1 else 1
    seq_blk = (lambda i: i // tiles_per_seq) if spt == 1 else (lambda i: i)
    kern = functools.partial(_merge_kernel, tm=tm, qm=qm, spt=spt, emit_vn=emit_vn)
    const = lambda shape: pl.BlockSpec(shape, lambda i: (0,) * len(shape),
                                       pipeline_mode=pl.Buffered(1))
    seg = lambda k: pl.BlockSpec((tm, D_MODEL), lambda i: (i, k))
    out_specs = [pl.BlockSpec((tm, D_MODEL), lambda i: (i, 0))]
    out_shape = [jax.ShapeDtypeStruct((rows, D_MODEL), F32)]
    if emit_vn:
        out_specs.append(pl.BlockSpec((tm, D_MODEL), lambda i: (i, 0)))
        out_shape.append(jax.ShapeDtypeStruct((rows, D_MODEL), F32))
    return pl.pallas_call(
        kern,
        grid=(n_tiles,),
        in_specs=[
            pl.BlockSpec((tm, D_MODEL), lambda i: (i, 0)),
            seg(2), seg(3), seg(4), seg(5), seg(6),
            pl.BlockSpec((tm, D_MODEL), lambda i: (i, 0)),
            pl.BlockSpec((spt, 1, D_MODEL), lambda i: (seq_blk(i), 0, 0)),
            const((1, D_MODEL)), const((1, D_MODEL)), const((1, D_MODEL)),
            const((MLP_GROUPS, MLP_CHUNK, MLP_CHUNK)),
            const((qm, D_MODEL)),
            const((D_MODEL, D_MODEL)), const((D_MODEL, D_MODEL)), const((D_MODEL, D_MODEL)),
        ],
        out_specs=out_specs,
        out_shape=out_shape,
        scratch_shapes=[pltpu.VMEM((tm, D_MODEL), BF16)],
        compiler_params=pltpu.CompilerParams(
            dimension_semantics=("arbitrary",), vmem_limit_bytes=VMEM_LIMIT),
        name="merge",
    )(yssd, act, act, act, act, act, x2d, gate, ln_g, ln_b, g_final, w_spatial, bsp_e, wbs, wbm, wo)


def _cast_kernel(a_ref, b_ref, c_ref, ao_ref, bo_ref, co_ref):
    ao_ref[...] = a_ref[...].astype(BF16)
    bo_ref[...] = b_ref[...].astype(BF16)
    co_ref[...] = c_ref[...].astype(BF16)


def _cast_bf16(a, b, c, *, tm=512):
    rows, cols = a.shape
    spec = pl.BlockSpec((tm, cols), lambda i: (i, 0))
    return pl.pallas_call(
        _cast_kernel,
        grid=(rows // tm,),
        in_specs=[spec] * 3,
        out_specs=[spec] * 3,
        out_shape=[jax.ShapeDtypeStruct((rows, cols), BF16)] * 3,
        compiler_params=pltpu.CompilerParams(
            dimension_semantics=("arbitrary",), vmem_limit_bytes=VMEM_LIMIT),
        name="cast",
    )(a, b, c)


def _expansion_matrix():
    e = np.zeros((LANES, D_MODEL), np.float32)
    cols = np.arange(D_MODEL)
    for piece in range(3):
        e[piece * SSD_HEADS + cols // SSD_HEAD_DIM, cols] = 1.0
    return jnp.asarray(e, BF16)


def kernel(x_prompt, x_sample, state_ssm, cache_conv, c_prompt, c_sample, w_ada, b_ada, g_pre, w_in,
           conv_w, conv_b, dt_bias, a_log, d_skip, norm_ssd, ln_g, ln_b, w_spatial, b_spatial,
           w_bp_ssd, w_bp_mlp, w_o, g_final):
    depth = w_ada.shape[0]
    assert depth == 1, "single-layer trunk only"
    bp, seq, d = x_prompt.shape
    bs, dec_seq, _ = x_sample.shape
    assert d == D_MODEL and seq % 1024 == 0 and dec_seq % 16 == 0 and dec_seq <= SSD_BLOCK

    w_t = jnp.swapaxes(w_in[0], 0, 1)
    w_dt = jnp.pad(w_t[DT_OFFSET:DT_OFFSET + SSD_HEADS], ((0, LANES - SSD_HEADS), (0, 0))).astype(BF16)
    pad_heads = lambda v: jnp.pad(v.reshape(1, SSD_HEADS), ((0, 0), (0, LANES - SSD_HEADS)))
    wts = dict(
        g_pre=g_pre[0].reshape(1, d), w_t=w_t, w_dt=w_dt, dt_bias=pad_heads(dt_bias[0]),
        conv_w=conv_w[0], conv_b=conv_b[0].reshape(1, CONV_DIM),
        a_log=pad_heads(a_log[0]), dskip_e=jnp.repeat(d_skip[0], SSD_HEAD_DIM).reshape(1, d),
        norm_ssd=norm_ssd[0].reshape(1, d), e3=_expansion_matrix(),
        ln_g=ln_g[0].reshape(1, d), ln_b=ln_b[0].reshape(1, d), g_final=g_final.reshape(1, d),
        w_spatial=w_spatial[0], bsp_e=jnp.repeat(b_spatial[0].T, MLP_GROUP_DIM, axis=1),
    )

    c_all = jnp.concatenate([c_prompt, c_sample], axis=0)
    c_pad = jnp.pad(c_all, ((0, 16 - (bp + bs)), (0, 0)))
    mods = _mods(c_pad, w_ada[0], b_ada[0].reshape(1, 3 * d))

    split_mods = lambda m: (m[:, k * d:(k + 1) * d].reshape(m.shape[0], 1, d) for k in range(3))
    shift_p, scale_p, gate_p = split_mods(mods[:bp])
    shift_s, scale_s, gate_s = split_mods(mods[bp:bp + bs])
    x2p = x_prompt.reshape(bp * seq, d)
    x2s = x_sample.reshape(bs * dec_seq, d)
    rows_s = bs * dec_seq

    hist_p = jnp.zeros((bp, SUBLANES, CONV_DIM), F32)
    h0_p = jnp.zeros((bp, d, D_STATE), F32)
    hist_s = jnp.pad(cache_conv[0], ((0, 0), (SUBLANES - (CONV_W - 1), 0), (0, 0)))
    h0_s = state_ssm[0].reshape(bs, d, D_STATE)

    norm_w = (wts["g_pre"], wts["w_dt"], wts["dt_bias"])
    hp, dt_p = _norm(x2p, scale_p, shift_p, *norm_w, tm=NORM_TM, spt=1)
    hs, dt_s = _norm(x2s, scale_s, shift_s, *norm_w, tm=rows_s, spt=bs)
    act_p, act_s, tail_p, tail_s = _proj(hp, hs, wts["w_t"], wts["conv_w"], wts["conv_b"], hist_p, hist_s,
                                         tm_p=PROJ_TM, nsplit=PROJ_NSPLIT, msplit_p=PROJ_MSPLIT)

    ssd_w = (wts["a_log"], wts["dskip_e"], wts["norm_ssd"], wts["e3"])
    yssd_p, ssm_p = _ssd(act_p, dt_p, *ssd_w, h0_p, n_seq=bp, q_in=SSD_BLOCK,
                         blocks_per_step=SSD_BLOCKS_PER_STEP)
    yssd_s, ssm_s = _ssd(act_s, dt_s, *ssd_w, h0_s, n_seq=bs, q_in=dec_seq, blocks_per_step=1)

    merge_w = (wts["ln_g"], wts["ln_b"], wts["g_final"], wts["w_spatial"])
    proj_w = _cast_bf16(w_bp_ssd[0], w_bp_mlp[0], w_o[0])
    (yp,) = _merge(yssd_p, act_p, x2p, gate_p, *merge_w, wts["bsp_e"], *proj_w,
                   tm=MERGE_TM, qm=MLP_CHUNK, spt=1, emit_vn=False)
    ys, vn_s = _merge(yssd_s, act_s, x2s, gate_s, *merge_w, wts["bsp_e"][:dec_seq], *proj_w,
                      tm=rows_s, qm=dec_seq, spt=bs, emit_vn=True)

    conv_cols = slice(CONV_STEPS[0] * PROJ_TN, CONV_STEPS[0] * PROJ_TN + CONV_DIM)
    hist_rows = slice(SUBLANES - (CONV_W - 1), SUBLANES)
    tiles_per_seq = seq // PROJ_TM
    conv_p = tail_p[tiles_per_seq - 1::tiles_per_seq, hist_rows, conv_cols]
    conv_s = tail_s[:, hist_rows, conv_cols]

    state_shape = (SSD_HEADS, SSD_HEAD_DIM, D_STATE)
    return (yp.reshape(bp, seq, d), ys.reshape(bs, dec_seq, d),
            ssm_p.reshape(1, bp, *state_shape), conv_p[None],
            ssm_s.reshape(1, bs, *state_shape), conv_s[None], vn_s.reshape(1, bs, dec_seq, d))
```

```python
import functools
from typing import Any, NamedTuple

import numpy as np
import jax
import jax.numpy as jnp
from jax import lax
from jax.experimental import pallas as pl
from jax.experimental.pallas import tpu as pltpu

F32 = jnp.float32
BF16 = jnp.bfloat16

D_MODEL = 2048
SSD_HEADS = 32
SSD_HEAD_DIM = 64
SSD_GROUPS = 4
HEADS_PER_GROUP = SSD_HEADS // SSD_GROUPS
D_STATE = 128
GROUP_COLS = HEADS_PER_GROUP * SSD_HEAD_DIM
CONV_W = 4
CONV_DIM = D_MODEL + 2 * SSD_GROUPS * D_STATE
MLP_CHUNK = 128
MLP_GROUPS = 8
MLP_GROUP_DIM = D_MODEL // MLP_GROUPS
EPS = 1e-5

SUBLANES = 8
LANES = 128
SSD_BLOCK = 128
SSD_BLOCKS_PER_STEP = 8
PROJ_TN = 1024
VMEM_LIMIT = 56 * 1024 * 1024

N_ACT = 7 * D_MODEL + 2 * SSD_GROUPS * D_STATE
N_PROJ_TILES = N_ACT // PROJ_TN
DT_OFFSET = D_MODEL + CONV_DIM
FIRST_SHIFTED_TILE = DT_OFFSET // PROJ_TN
CONV_STEPS = (2, 3, 4)
IDENT_STEPS = (5, 6, 7, 8)
FIRST_SIGMOID_STEP = 11
PROJ_TM = 1024
PROJ_NSPLIT = 2
PROJ_MSPLIT = 4
NORM_TM = 512
MERGE_TM = 256


def _silu(x):
    hx = 0.5 * x
    return hx + hx * jnp.tanh(hx)


def _softplus(x):
    return jnp.maximum(x, 0.0) + jnp.log1p(jnp.exp(-jnp.abs(x)))


def _mods_kernel(c_ref, w_ref, b_ref, o_ref):
    s = _silu(c_ref[...]).astype(BF16)
    o_ref[...] = jnp.dot(s, w_ref[...].astype(BF16), preferred_element_type=F32) + b_ref[...]


def _mods(c_pad, w_ada, b_ada):
    rows, d = c_pad.shape
    n = w_ada.shape[1]
    tn = 1024
    return pl.pallas_call(
        _mods_kernel,
        grid=(n // tn,),
        in_specs=[
            pl.BlockSpec((rows, d), lambda j: (0, 0)),
            pl.BlockSpec((d, tn), lambda j: (0, j)),
            pl.BlockSpec((1, tn), lambda j: (0, j)),
        ],
        out_specs=pl.BlockSpec((rows, tn), lambda j: (0, j)),
        out_shape=jax.ShapeDtypeStruct((rows, n), F32),
        compiler_params=pltpu.CompilerParams(
            dimension_semantics=("arbitrary",), vmem_limit_bytes=VMEM_LIMIT),
        name="mods",
    )(c_pad, w_ada, b_ada)


def _norm_kernel(x_ref, sc_ref, sh_ref, gpre_ref, wdt_ref, dtb_ref, h_ref, dt_ref, *, tm, spt):
    rps = tm // spt
    for s in range(spt):
        rows = slice(s * rps, (s + 1) * rps)
        x = x_ref[rows, :]
        y = x * lax.rsqrt(jnp.mean(x * x, axis=-1, keepdims=True) + EPS)
        y = y * gpre_ref[...]
        h_ref[rows, :] = (y * (1.0 + sc_ref[s]) + sh_ref[s]).astype(BF16)
    dt_raw = lax.dot_general(h_ref[...], wdt_ref[...], (((1,), (1,)), ((), ())),
                             preferred_element_type=F32) + dtb_ref[...]
    lane = lax.broadcasted_iota(jnp.int32, dt_raw.shape, 1)
    dt_ref[...] = jnp.where(lane < SSD_HEADS, _softplus(dt_raw), 0.0)


def _norm(x2d, scale, shift, g_pre, w_dt, dt_bias, *, tm, spt):
    rows = x2d.shape[0]
    n_seq = scale.shape[0]
    n_tiles = rows // tm
    tiles_per_seq = max(1, n_tiles // n_seq) if spt == 1 else 1
    seq_blk = (lambda i: i // tiles_per_seq) if spt == 1 else (lambda i: i)
    return pl.pallas_call(
        functools.partial(_norm_kernel, tm=tm, spt=spt),
        grid=(n_tiles,),
        in_specs=[
            pl.BlockSpec((tm, D_MODEL), lambda i: (i, 0)),
            pl.BlockSpec((spt, 1, D_MODEL), lambda i: (seq_blk(i), 0, 0)),
            pl.BlockSpec((spt, 1, D_MODEL), lambda i: (seq_blk(i), 0, 0)),
            pl.BlockSpec((1, D_MODEL), lambda i: (0, 0)),
            pl.BlockSpec((LANES, D_MODEL), lambda i: (0, 0)),
            pl.BlockSpec((1, LANES), lambda i: (0, 0)),
        ],
        out_specs=[
            pl.BlockSpec((tm, D_MODEL), lambda i: (i, 0)),
            pl.BlockSpec((tm, LANES), lambda i: (i, 0)),
        ],
        out_shape=[
            jax.ShapeDtypeStruct((rows, D_MODEL), BF16),
            jax.ShapeDtypeStruct((rows, LANES), F32),
        ],
        compiler_params=pltpu.CompilerParams(
            dimension_semantics=("arbitrary",), vmem_limit_bytes=VMEM_LIMIT),
        name="norm",
    )(x2d, scale, shift, g_pre, w_dt, dt_bias)


def _act_tile(j):
    return jnp.where(j < CONV_STEPS[2], j, jnp.where(j == CONV_STEPS[2], N_PROJ_TILES - 1, j - 1))


def _wt_row(j):
    return pl.multiple_of(j * PROJ_TN + jnp.where(j >= FIRST_SHIFTED_TILE, SSD_HEADS, 0), SSD_HEADS)


def _conv_in_tile(j):
    return jnp.clip(j - CONV_STEPS[0], 0, len(CONV_STEPS) - 1)


class _Rows(NamedTuple):
    h_ref: Any
    act_ref: Any
    tail_ref: Any
    hist_ref: Any
    tm: int
    spt: int
    msplit: int
    first: Any


def _proj_step(j, row_sets, cw_ref, cb_ref, wbf_scr, halo_scr, stgs, shift_scr, *, nsplit):
    cw = PROJ_TN // nsplit
    blocks = [(rs, nb, mb) for nb in range(nsplit) for rs in row_sets for mb in range(rs.msplit)]

    def dot_block(k):
        rs, nb, mb = blocks[k]
        rb = rs.tm // rs.msplit
        stgs[k % 2][SUBLANES:SUBLANES + rb, :] = lax.dot_general(
            rs.h_ref[mb * rb:(mb + 1) * rb, :], wbf_scr[nb * cw:(nb + 1) * cw, :],
            (((1,), (1,)), ((), ())), preferred_element_type=F32)

    def run(epilogue):
        dot_block(0)
        for k, (rs, nb, mb) in enumerate(blocks):
            if k + 1 < len(blocks):
                dot_block(k + 1)
            epilogue(rs, nb, mb, stgs[k % 2])

    is_conv = jnp.logical_and(j >= CONV_STEPS[0], j <= CONV_STEPS[2])
    is_ident = jnp.logical_and(j >= IDENT_STEPS[0], j <= IDENT_STEPS[-1])
    is_sigmoid = j >= FIRST_SIGMOID_STEP

    p_x = jnp.where(is_ident, 1.0, 0.5).astype(F32)
    p_xt = jnp.where(is_ident, 0.0, 0.5).astype(F32)

    def elementwise(rs, nb, mb, stg):
        rb = rs.tm // rs.msplit
        x = stg[SUBLANES:SUBLANES + rb, :]
        t = jnp.tanh(0.5 * x)
        y = jnp.where(is_sigmoid, 0.5 + 0.5 * t, x * (p_x + p_xt * t))
        rs.act_ref[mb * rb:(mb + 1) * rb, nb * cw:(nb + 1) * cw] = y.astype(BF16)

    @pl.when(jnp.logical_not(is_conv))
    def _():
        for rs in row_sets:
            rs.tail_ref[...] = jnp.zeros_like(rs.tail_ref)
        run(elementwise)

    @pl.when(is_conv)
    def _():
        for rs in row_sets:
            if rs.spt == 1:
                halo_scr[...] = jnp.where(rs.first, rs.hist_ref[0], halo_scr[...])

        shift_scr[0:SUBLANES, :] = jnp.zeros((SUBLANES, cw), F32)

        def shift_down(v):
            n = v.shape[0]
            shift_scr[1:1 + n, :] = v
            return shift_scr[0:n, :]

        def conv(ext, cols):
            w0, w1, w2, w3 = (0.5 * cw_ref[k:k + 1, cols] for k in range(CONV_W))
            t = ext * w0
            t = shift_down(t) + ext * w1
            t = shift_down(t) + ext * w2
            t = shift_down(t) + ext * w3 + 0.5 * cb_ref[:, cols]
            t = t[SUBLANES:, :]
            return (t + t * jnp.tanh(t)).astype(BF16)

        def ep(rs, nb, mb, stg):
            cols = slice(nb * cw, (nb + 1) * cw)
            rb = rs.tm // rs.msplit
            if rs.spt == 1:
                stg[0:SUBLANES, :] = halo_scr[:, cols]
                rs.act_ref[mb * rb:(mb + 1) * rb, cols] = conv(stg[0:SUBLANES + rb, :], cols)
                last = stg[rb:rb + SUBLANES, :]
                halo_scr[:, cols] = last
                rs.tail_ref[0, :, cols] = last
            else:
                rps = rs.tm // rs.spt
                for s in range(rb // rps):
                    rows = slice(SUBLANES + s * rps, SUBLANES + (s + 1) * rps)
                    seq = mb * (rb // rps) + s
                    ext = jnp.concatenate([rs.hist_ref[seq, :, cols], stg[rows, :]], axis=0)
                    rs.act_ref[mb * rb + s * rps:mb * rb + (s + 1) * rps, cols] = conv(ext, cols)
                    rs.tail_ref[seq, :, cols] = stg[rows.stop - SUBLANES:rows.stop, :]

        run(ep)


def _proj_kernel(hp_ref, hs_ref, wt_ref, cw_ref, cb_ref, histp_ref, hists_ref,
                 actp_ref, acts_ref, tailp_ref, tails_ref, wbf_scr, halo_scr, stg0_scr, stg1_scr,
                 shift_scr, *, n_p, tm_p, tm_s, spt_s, nsplit, msplit_p, tiles_per_seq):
    j = pl.program_id(0)
    i = pl.program_id(1)
    wrows = 256

    @pl.when(i == 0)
    def _():
        for r in range(PROJ_TN // wrows):
            rows = slice(r * wrows, (r + 1) * wrows)
            wbf_scr[rows, :] = wt_ref[rows, :].astype(BF16)

    shared = (cw_ref, cb_ref, wbf_scr, halo_scr, (stg0_scr, stg1_scr), shift_scr)
    prompt = _Rows(hp_ref, actp_ref, tailp_ref, histp_ref, tm=tm_p, spt=1, msplit=msplit_p,
                   first=(i % tiles_per_seq) == 0)
    sample = _Rows(hs_ref, acts_ref, tails_ref, hists_ref, tm=tm_s, spt=spt_s, msplit=1, first=None)

    @pl.when(i < n_p - 1)
    def _():
        _proj_step(j, [prompt], *shared, nsplit=nsplit)

    @pl.when(i == n_p - 1)
    def _():
        _proj_step(j, [prompt, sample], *shared, nsplit=nsplit)


def _proj(hp, hs, w_t, conv_w, conv_b, hist_p, hist_s, *, tm_p, nsplit, msplit_p):
    rows_p, rows_s = hp.shape[0], hs.shape[0]
    n_p = rows_p // tm_p
    tiles_per_seq = n_p // hist_p.shape[0]
    spt_s = hist_s.shape[0]
    rb = max(tm_p // msplit_p, rows_s)
    kern = functools.partial(_proj_kernel, n_p=n_p, tm_p=tm_p, tm_s=rows_s, spt_s=spt_s,
                             nsplit=nsplit, msplit_p=msplit_p, tiles_per_seq=tiles_per_seq)
    return pl.pallas_call(
        kern,
        grid=(N_PROJ_TILES, n_p),
        in_specs=[
            pl.BlockSpec((tm_p, D_MODEL), lambda j, i: (i, 0)),
            pl.BlockSpec((rows_s, D_MODEL), lambda j, i: (0, 0)),
            pl.BlockSpec((pl.Element(PROJ_TN), pl.Element(D_MODEL)), lambda j, i: (_wt_row(j), 0)),
            pl.BlockSpec((CONV_W, PROJ_TN), lambda j, i: (0, _conv_in_tile(j))),
            pl.BlockSpec((1, PROJ_TN), lambda j, i: (0, _conv_in_tile(j))),
            pl.BlockSpec((1, SUBLANES, PROJ_TN), lambda j, i: (i // tiles_per_seq, 0, _conv_in_tile(j))),
            pl.BlockSpec((spt_s, SUBLANES, PROJ_TN), lambda j, i: (0, 0, _conv_in_tile(j))),
        ],
        out_specs=[
            pl.BlockSpec((tm_p, PROJ_TN), lambda j, i: (i, _act_tile(j))),
            pl.BlockSpec((rows_s, PROJ_TN), lambda j, i: (0, _act_tile(j))),
            pl.BlockSpec((1, SUBLANES, PROJ_TN), lambda j, i: (i, 0, j)),
            pl.BlockSpec((spt_s, SUBLANES, PROJ_TN), lambda j, i: (0, 0, j)),
        ],
        out_shape=[
            jax.ShapeDtypeStruct((rows_p, N_ACT), BF16),
            jax.ShapeDtypeStruct((rows_s, N_ACT), BF16),
            jax.ShapeDtypeStruct((n_p, SUBLANES, N_PROJ_TILES * PROJ_TN), F32),
            jax.ShapeDtypeStruct((spt_s, SUBLANES, N_PROJ_TILES * PROJ_TN), F32),
        ],
        scratch_shapes=[
            pltpu.VMEM((PROJ_TN, D_MODEL), BF16),
            pltpu.VMEM((SUBLANES, PROJ_TN), F32),
            pltpu.VMEM((rb + SUBLANES, PROJ_TN // nsplit), F32),
            pltpu.VMEM((rb + SUBLANES, PROJ_TN // nsplit), F32),
            pltpu.VMEM((rb + 2 * SUBLANES, PROJ_TN // nsplit), F32),
        ],
        compiler_params=pltpu.CompilerParams(
            dimension_semantics=("arbitrary", "arbitrary"), vmem_limit_bytes=VMEM_LIMIT),
        name="proj",
    )(hp, hs, w_t, conv_w, conv_b, hist_p, hist_s)


def _pad_rows(v, rows):
    if v.shape[0] == rows:
        return v
    return jnp.concatenate([v, jnp.zeros((rows - v.shape[0], v.shape[1]), v.dtype)], axis=0)


def _ssd_kernel(szs_ref, xs_ref, b_ref, c_ref, dt_ref, alog_ref, dskip_ref, nrm_ref, e3_ref, h0_ref,
                y_ref, hout_ref, st_scr, *, q_in, blocks_per_step, n_steps):
    ci = pl.program_id(1)

    @pl.when(ci == 0)
    def _():
        st_scr[...] = h0_ref[0]

    for k in range(blocks_per_step):
        rs = pl.ds(k * q_in, q_in)
        _ssd_block(szs_ref.at[rs], xs_ref.at[rs], b_ref.at[rs], c_ref.at[rs], dt_ref.at[rs],
                   alog_ref, dskip_ref, nrm_ref, e3_ref, y_ref.at[rs], st_scr, q_in=q_in)

    @pl.when(ci == n_steps - 1)
    def _():
        hout_ref[0] = st_scr[...]


def _ssd_block(szs_ref, xs_ref, b_ref, c_ref, dt_ref, alog_ref, dskip_ref, nrm_ref, e3_ref,
               y_ref, st_scr, *, q_in):
    q = SSD_BLOCK
    dt = _pad_rows(dt_ref[...], q)
    a = -jnp.exp(alog_ref[...])
    adt = dt * a
    row = lax.broadcasted_iota(jnp.int32, (q, q), 0)
    col = lax.broadcasted_iota(jnp.int32, (q, q), 1)
    causal = row >= col
    acum = jnp.dot(causal.astype(F32), adt, precision=lax.Precision.HIGHEST,
                   preferred_element_type=F32)
    alast = acum[q - 1:q, :]
    dte = jnp.exp(alast - acum)
    ea = jnp.exp(acum)

    stack = jnp.concatenate([dt, dt * dte, ea], axis=0)
    lane3 = lax.broadcasted_iota(jnp.int32, stack.shape, 1)
    stack = jnp.where(lane3 < SSD_HEADS, stack, 0.0)
    hi = stack.astype(BF16).astype(F32)
    r1 = stack - hi
    mid = r1.astype(BF16).astype(F32)
    lo = (r1 - mid).astype(BF16).astype(F32)
    comb = hi + pltpu.roll(mid, SSD_HEADS, axis=1) + pltpu.roll(lo, 2 * SSD_HEADS, axis=1)
    expd = jnp.dot(comb.astype(BF16), e3_ref[...], preferred_element_type=F32)
    dt_e, w_e, ea_e = expd[0:q], expd[q:2 * q], expd[2 * q:3 * q]

    x = _pad_rows(xs_ref[...], q).astype(F32)
    xdt = (x * dt_e).astype(BF16)
    xdtd = (x * w_e).astype(BF16)
    bm = _pad_rows(b_ref[...], q)
    cm = _pad_rows(c_ref[...], q)

    acum_t = acum.T
    dec = jnp.exp(acum_t[0:SSD_HEADS, q - 1:q])
    dec = jnp.broadcast_to(dec, (SSD_HEADS, D_STATE))
    dec = jnp.broadcast_to(dec[:, None, :], (SSD_HEADS, SSD_HEAD_DIM, D_STATE))
    dec = dec.reshape(SSD_HEADS * SSD_HEAD_DIM, D_STATE)

    lane = lax.broadcasted_iota(jnp.int32, (q, LANES), 1)
    nt = (((1,), (1,)), ((), ()))
    tn = (((0,), (0,)), ((), ()))
    y_groups = []
    for g in range(SSD_GROUPS):
        gc = slice(g * GROUP_COLS, (g + 1) * GROUP_COLS)
        bg = bm[:, g * D_STATE:(g + 1) * D_STATE]
        cg = cm[:, g * D_STATE:(g + 1) * D_STATE]
        cb = lax.dot_general(cg, bg, nt, preferred_element_type=F32)
        sg = st_scr[gc, :]
        y_off = lax.dot_general(cg, sg.astype(BF16), nt, preferred_element_type=F32)
        pieces = []
        for pr in range(HEADS_PER_GROUP // 2):
            ms = []
            for hh in range(2):
                h = g * HEADS_PER_GROUP + 2 * pr + hh
                seg = jnp.broadcast_to(acum[:, h:h + 1], (q, q)) - acum_t[h:h + 1, :]
                lm = jnp.exp(jnp.where(causal, seg, -jnp.inf))
                ms.append((cb * lm).astype(BF16))
            m = jnp.concatenate(ms, axis=1)
            c0 = g * GROUP_COLS + pr * LANES
            xp = xdt[:, c0:c0 + LANES]
            rhs = jnp.concatenate([jnp.where(lane < SSD_HEAD_DIM, xp, jnp.zeros_like(xp)),
                                   jnp.where(lane >= SSD_HEAD_DIM, xp, jnp.zeros_like(xp))], axis=0)
            pieces.append(jnp.dot(m, rhs, preferred_element_type=F32))
        y_diag = jnp.concatenate(pieces, axis=1)
        y_groups.append(y_diag + y_off * ea_e[:, gc])
        new = lax.dot_general(xdtd[:, gc], bg, tn, preferred_element_type=F32)
        st_scr[gc, :] = sg * dec[gc, :] + new
    y = jnp.concatenate(y_groups, axis=1)
    y = y + dskip_ref[...] * x
    yz = y * _pad_rows(szs_ref[...], q).astype(F32)
    yn = yz * lax.rsqrt(jnp.mean(yz * yz, axis=-1, keepdims=True) + EPS) * nrm_ref[...]
    y_ref[...] = yn[0:q_in].astype(BF16)


def _ssd(act, dt, a_log, dskip_e, norm_ssd, e3, h0, *, n_seq, q_in, blocks_per_step):
    rows = act.shape[0]
    qb = q_in * blocks_per_step
    n_chunks = rows // (n_seq * qb)
    kern = functools.partial(_ssd_kernel, q_in=q_in, blocks_per_step=blocks_per_step, n_steps=n_chunks)
    rblk = lambda b, c: b * n_chunks + c
    bc_blk0 = 7 * D_MODEL // GROUP_COLS
    return pl.pallas_call(
        kern,
        grid=(n_seq, n_chunks),
        in_specs=[
            pl.BlockSpec((qb, D_MODEL), lambda b, c: (rblk(b, c), 0)),
            pl.BlockSpec((qb, D_MODEL), lambda b, c: (rblk(b, c), 1)),
            pl.BlockSpec((qb, GROUP_COLS), lambda b, c: (rblk(b, c), bc_blk0)),
            pl.BlockSpec((qb, GROUP_COLS), lambda b, c: (rblk(b, c), bc_blk0 + 1)),
            pl.BlockSpec((qb, LANES), lambda b, c: (rblk(b, c), 0)),
            pl.BlockSpec((1, LANES), lambda b, c: (0, 0)),
            pl.BlockSpec((1, D_MODEL), lambda b, c: (0, 0)),
            pl.BlockSpec((1, D_MODEL), lambda b, c: (0, 0)),
            pl.BlockSpec((LANES, D_MODEL), lambda b, c: (0, 0)),
            pl.BlockSpec((1, D_MODEL, D_STATE), lambda b, c: (b, 0, 0)),
        ],
        out_specs=[
            pl.BlockSpec((qb, D_MODEL), lambda b, c: (rblk(b, c), 0)),
            pl.BlockSpec((1, D_MODEL, D_STATE), lambda b, c: (b, 0, 0)),
        ],
        out_shape=[
            jax.ShapeDtypeStruct((rows, D_MODEL), BF16),
            jax.ShapeDtypeStruct((n_seq, D_MODEL, D_STATE), F32),
        ],
        scratch_shapes=[pltpu.VMEM((D_MODEL, D_STATE), F32)],
        compiler_params=pltpu.CompilerParams(
            dimension_semantics=("arbitrary", "arbitrary"), vmem_limit_bytes=VMEM_LIMIT),
        name="ssd",
    )(act, act, act, act, dt, a_log, dskip_e, norm_ssd, e3, h0)


def _mlp_stage(u_ref, v_ref, szm_ref, lng_ref, lnb_ref, wsp_ref, bsp_ref, ymlp_ref, vn_ref, *, tm, qm):
    v = v_ref[...].astype(F32)
    mu = jnp.mean(v, axis=-1, keepdims=True)
    vc = v - mu
    var = jnp.mean(vc * vc, axis=-1, keepdims=True)
    vn = vc * lax.rsqrt(var + EPS) * lng_ref[...] + lnb_ref[...]
    if vn_ref is not None:
        vn_ref[...] = vn
    vn_b = vn.astype(BF16)

    row = lax.broadcasted_iota(jnp.int32, (MLP_CHUNK, MLP_CHUNK), 0)
    col = lax.broadcasted_iota(jnp.int32, (MLP_CHUNK, MLP_CHUNK), 1)
    wms = [jnp.where(row >= col, wsp_ref[g], 0.0).astype(BF16) for g in range(MLP_GROUPS)]
    for ck in range(tm // qm):
        rows = slice(ck * qm, (ck + 1) * qm)
        vck = _pad_rows(vn_b[rows, :], MLP_CHUNK)
        mixed = jnp.concatenate(
            [jnp.dot(wms[g], vck[:, g * MLP_GROUP_DIM:(g + 1) * MLP_GROUP_DIM],
                     preferred_element_type=F32) for g in range(MLP_GROUPS)], axis=1)
        mixed = mixed[0:qm] + bsp_ref[...]
        y_mlp = u_ref[rows, :].astype(F32) * mixed * szm_ref[rows, :].astype(F32)
        ymlp_ref[rows, :] = y_mlp.astype(BF16)


def _mix_stage(yssd_ref, ymlp_ref, sgs_ref, sgm_ref, wbs_ref, wbm_ref, wo_ref):
    a = jnp.dot(yssd_ref[...], wbs_ref[...], preferred_element_type=F32)
    b = jnp.dot(ymlp_ref[...], wbm_ref[...], preferred_element_type=F32)
    merged = sgs_ref[...].astype(F32) * a + sgm_ref[...].astype(F32) * b
    return jnp.dot(merged.astype(BF16), wo_ref[...], preferred_element_type=F32)


def _out_stage(o, x_ref, gate_ref, gfin_ref, y_ref, *, tm, spt):
    rps = tm // spt
    for s in range(spt):
        rows = slice(s * rps, (s + 1) * rps)
        out = x_ref[rows, :] + gate_ref[s] * o[rows, :]
        y = out * lax.rsqrt(jnp.mean(out * out, axis=-1, keepdims=True) + EPS)
        y_ref[rows, :] = y * gfin_ref[...]


def _merge_kernel(yssd_ref, u_ref, v_ref, szm_ref, sgs_ref, sgm_ref, x_ref, gate_ref,
                  lng_ref, lnb_ref, gfin_ref, wsp_ref, bsp_ref, wbs_ref, wbm_ref, wo_ref,
                  y_ref, *rest, tm, qm, spt, emit_vn):
    if emit_vn:
        vn_ref, ymlp_scr = rest
    else:
        vn_ref, (ymlp_scr,) = None, rest
    _mlp_stage(u_ref, v_ref, szm_ref, lng_ref, lnb_ref, wsp_ref, bsp_ref, ymlp_scr, vn_ref, tm=tm, qm=qm)
    o = _mix_stage(yssd_ref, ymlp_scr, sgs_ref, sgm_ref, wbs_ref, wbm_ref, wo_ref)
    _out_stage(o, x_ref, gate_ref, gfin_ref, y_ref, tm=tm, spt=spt)


def _merge(yssd, act, x2d, gate, ln_g, ln_b, g_final, w_spatial, bsp_e, wbs, wbm, wo,
           *, tm, qm, spt, emit_vn):
    rows = x2d.shape[0]
    n_seq = gate.shape[0]
    n_tiles = rows // tm
    tiles_per_seq = max(1, n_tiles // n_seq) if spt == 1 else 1
    seq_blk = (lambda i: i // tiles_per_seq) if spt == 1 else (lambda i: i)
    kern = functools.partial(_merge_kernel, tm=tm, qm=qm, spt=spt, emit_vn=emit_vn)
    const = lambda shape: pl.BlockSpec(shape, lambda i: (0,) * len(shape),
                                       pipeline_mode=pl.Buffered(1))
    seg = lambda k: pl.BlockSpec((tm, D_MODEL), lambda i: (i, k))
    out_specs = [pl.BlockSpec((tm, D_MODEL), lambda i: (i, 0))]
    out_shape = [jax.ShapeDtypeStruct((rows, D_MODEL), F32)]
    if emit_vn:
        out_specs.append(pl.BlockSpec((tm, D_MODEL), lambda i: (i, 0)))
        out_shape.append(jax.ShapeDtypeStruct((rows, D_MODEL), F32))
    return pl.pallas_call(
        kern,
        grid=(n_tiles,),
        in_specs=[
            pl.BlockSpec((tm, D_MODEL), lambda i: (i, 0)),
            seg(2), seg(3), seg(4), seg(5), seg(6),
            pl.BlockSpec((tm, D_MODEL), lambda i: (i, 0)),
            pl.BlockSpec((spt, 1, D_MODEL), lambda i: (seq_blk(i), 0, 0)),
            const((1, D_MODEL)), const((1, D_MODEL)), const((1, D_MODEL)),
            const((MLP_GROUPS, MLP_CHUNK, MLP_CHUNK)),
            const((qm, D_MODEL)),
            const((D_MODEL, D_MODEL)), const((D_MODEL, D_MODEL)), const((D_MODEL, D_MODEL)),
        ],
        out_specs=out_specs,
        out_shape=out_shape,
        scratch_shapes=[pltpu.VMEM((tm, D_MODEL), BF16)],
        compiler_params=pltpu.CompilerParams(
            dimension_semantics=("arbitrary",), vmem_limit_bytes=VMEM_LIMIT),
        name="merge",
    )(yssd, act, act, act, act, act, x2d, gate, ln_g, ln_b, g_final, w_spatial, bsp_e, wbs, wbm, wo)


WEIGHT_CHUNK = 256


def _merge_stream_kernel(yssd_ref, u_ref, v_ref, szm_ref, sgs_ref, sgm_ref, x_ref, gate_ref,
                         lng_ref, lnb_ref, gfin_ref, wsp_ref, bsp_ref, wbs_hbm, wbm_hbm, wo_hbm,
                         y_ref, wbs_out, wbm_out, wo_out,
                         ymlp_scr, w_scr, stage_scr, in_sem, out_sem, *, tm, qm, n_steps):
    i = pl.program_id(0)
    w_hbm = (wbs_hbm, wbm_hbm, wo_hbm)
    w_out = (wbs_out, wbm_out, wo_out)
    chunks = [(w, c) for w in range(3) for c in range(D_MODEL // WEIGHT_CHUNK)]

    def chunk_copy(k):
        w, c = chunks[k]
        return pltpu.make_async_copy(w_hbm[w].at[pl.ds(c * WEIGHT_CHUNK, WEIGHT_CHUNK), :],
                                     stage_scr.at[k % 2], in_sem.at[k % 2])

    def out_copy(w):
        return pltpu.make_async_copy(w_scr.at[w], w_out[w], out_sem.at[w])

    @pl.when(i == 0)
    def _():
        chunk_copy(0).start()
        for k, (w, c) in enumerate(chunks):
            if k + 1 < len(chunks):
                chunk_copy(k + 1).start()
            chunk_copy(k).wait()
            w_scr[w, c * WEIGHT_CHUNK:(c + 1) * WEIGHT_CHUNK, :] = stage_scr[k % 2].astype(BF16)
        for w in range(3):
            out_copy(w).start()

    _mlp_stage(u_ref, v_ref, szm_ref, lng_ref, lnb_ref, wsp_ref, bsp_ref, ymlp_scr, None, tm=tm, qm=qm)
    o = _mix_stage(yssd_ref, ymlp_scr, sgs_ref, sgm_ref, w_scr.at[0], w_scr.at[1], w_scr.at[2])
    _out_stage(o, x_ref, gate_ref, gfin_ref, y_ref, tm=tm, spt=1)

    @pl.when(i == n_steps - 1)
    def _():
        for w in range(3):
            out_copy(w).wait()


def _merge_stream(yssd, act, x2d, gate, ln_g, ln_b, g_final, w_spatial, bsp_e, wbs, wbm, wo, *, tm, qm):
    rows = x2d.shape[0]
    n_tiles = rows // tm
    assert n_tiles > 1
    tiles_per_seq = n_tiles // gate.shape[0]
    kern = functools.partial(_merge_stream_kernel, tm=tm, qm=qm, n_steps=n_tiles)
    const = lambda shape: pl.BlockSpec(shape, lambda i: (0,) * len(shape),
                                       pipeline_mode=pl.Buffered(1))
    seg = lambda k: pl.BlockSpec((tm, D_MODEL), lambda i: (i, k))
    hbm = pl.BlockSpec(memory_space=pl.ANY)
    return pl.pallas_call(
        kern,
        grid=(n_tiles,),
        in_specs=[
            pl.BlockSpec((tm, D_MODEL), lambda i: (i, 0)),
            seg(2), seg(3), seg(4), seg(5), seg(6),
            pl.BlockSpec((tm, D_MODEL), lambda i: (i, 0)),
            pl.BlockSpec((1, 1, D_MODEL), lambda i: (i // tiles_per_seq, 0, 0)),
            const((1, D_MODEL)), const((1, D_MODEL)), const((1, D_MODEL)),
            const((MLP_GROUPS, MLP_CHUNK, MLP_CHUNK)),
            const((qm, D_MODEL)),
            hbm, hbm, hbm,
        ],
        out_specs=[pl.BlockSpec((tm, D_MODEL), lambda i: (i, 0)), hbm, hbm, hbm],
        out_shape=[jax.ShapeDtypeStruct((rows, D_MODEL), F32)]
        + [jax.ShapeDtypeStruct((D_MODEL, D_MODEL), BF16)] * 3,
        scratch_shapes=[
            pltpu.VMEM((tm, D_MODEL), BF16),
            pltpu.VMEM((3, D_MODEL, D_MODEL), BF16),
            pltpu.VMEM((2, WEIGHT_CHUNK, D_MODEL), F32),
            pltpu.SemaphoreType.DMA((2,)),
            pltpu.SemaphoreType.DMA((3,)),
        ],
        compiler_params=pltpu.CompilerParams(
            dimension_semantics=("arbitrary",), vmem_limit_bytes=VMEM_LIMIT),
        name="merge_stream",
    )(yssd, act, act, act, act, act, x2d, gate, ln_g, ln_b, g_final, w_spatial, bsp_e, wbs, wbm, wo)


def _expansion_matrix():
    e = np.zeros((LANES, D_MODEL), np.float32)
    cols = np.arange(D_MODEL)
    for piece in range(3):
        e[piece * SSD_HEADS + cols // SSD_HEAD_DIM, cols] = 1.0
    return jnp.asarray(e, BF16)


def kernel(x_prompt, x_sample, state_ssm, cache_conv, c_prompt, c_sample, w_ada, b_ada, g_pre, w_in,
           conv_w, conv_b, dt_bias, a_log, d_skip, norm_ssd, ln_g, ln_b, w_spatial, b_spatial,
           w_bp_ssd, w_bp_mlp, w_o, g_final):
    depth = w_ada.shape[0]
    assert depth == 1, "single-layer trunk only"
    bp, seq, d = x_prompt.shape
    bs, dec_seq, _ = x_sample.shape
    assert d == D_MODEL and seq % 1024 == 0 and dec_seq % 16 == 0 and dec_seq <= SSD_BLOCK

    w_t = jnp.swapaxes(w_in[0], 0, 1)
    w_dt = jnp.pad(w_t[DT_OFFSET:DT_OFFSET + SSD_HEADS], ((0, LANES - SSD_HEADS), (0, 0))).astype(BF16)
    pad_heads = lambda v: jnp.pad(v.reshape(1, SSD_HEADS), ((0, 0), (0, LANES - SSD_HEADS)))
    wts = dict(
        g_pre=g_pre[0].reshape(1, d), w_t=w_t, w_dt=w_dt, dt_bias=pad_heads(dt_bias[0]),
        conv_w=conv_w[0], conv_b=conv_b[0].reshape(1, CONV_DIM),
        a_log=pad_heads(a_log[0]), dskip_e=jnp.repeat(d_skip[0], SSD_HEAD_DIM).reshape(1, d),
        norm_ssd=norm_ssd[0].reshape(1, d), e3=_expansion_matrix(),
        ln_g=ln_g[0].reshape(1, d), ln_b=ln_b[0].reshape(1, d), g_final=g_final.reshape(1, d),
        w_spatial=w_spatial[0], bsp_e=jnp.repeat(b_spatial[0].T, MLP_GROUP_DIM, axis=1),
    )

    c_all = jnp.concatenate([c_prompt, c_sample], axis=0)
    c_pad = jnp.pad(c_all, ((0, 16 - (bp + bs)), (0, 0)))
    mods = _mods(c_pad, w_ada[0], b_ada[0].reshape(1, 3 * d))

    split_mods = lambda m: (m[:, k * d:(k + 1) * d].reshape(m.shape[0], 1, d) for k in range(3))
    shift_p, scale_p, gate_p = split_mods(mods[:bp])
    shift_s, scale_s, gate_s = split_mods(mods[bp:bp + bs])
    x2p = x_prompt.reshape(bp * seq, d)
    x2s = x_sample.reshape(bs * dec_seq, d)
    rows_s = bs * dec_seq

    hist_p = jnp.zeros((bp, SUBLANES, CONV_DIM), F32)
    h0_p = jnp.zeros((bp, d, D_STATE), F32)
    hist_s = jnp.pad(cache_conv[0], ((0, 0), (SUBLANES - (CONV_W - 1), 0), (0, 0)))
    h0_s = state_ssm[0].reshape(bs, d, D_STATE)

    norm_w = (wts["g_pre"], wts["w_dt"], wts["dt_bias"])
    hp, dt_p = _norm(x2p, scale_p, shift_p, *norm_w, tm=NORM_TM, spt=1)
    hs, dt_s = _norm(x2s, scale_s, shift_s, *norm_w, tm=rows_s, spt=bs)
    act_p, act_s, tail_p, tail_s = _proj(hp, hs, wts["w_t"], wts["conv_w"], wts["conv_b"], hist_p, hist_s,
                                         tm_p=PROJ_TM, nsplit=PROJ_NSPLIT, msplit_p=PROJ_MSPLIT)

    ssd_w = (wts["a_log"], wts["dskip_e"], wts["norm_ssd"], wts["e3"])
    yssd_p, ssm_p = _ssd(act_p, dt_p, *ssd_w, h0_p, n_seq=bp, q_in=SSD_BLOCK,
                         blocks_per_step=SSD_BLOCKS_PER_STEP)
    yssd_s, ssm_s = _ssd(act_s, dt_s, *ssd_w, h0_s, n_seq=bs, q_in=dec_seq, blocks_per_step=1)

    merge_w = (wts["ln_g"], wts["ln_b"], wts["g_final"], wts["w_spatial"])
    yp, *proj_w = _merge_stream(yssd_p, act_p, x2p, gate_p, *merge_w, wts["bsp_e"],
                                w_bp_ssd[0], w_bp_mlp[0], w_o[0], tm=MERGE_TM, qm=MLP_CHUNK)
    ys, vn_s = _merge(yssd_s, act_s, x2s, gate_s, *merge_w, wts["bsp_e"][:dec_seq], *proj_w,
                      tm=rows_s, qm=dec_seq, spt=bs, emit_vn=True)

    conv_cols = slice(CONV_STEPS[0] * PROJ_TN, CONV_STEPS[0] * PROJ_TN + CONV_DIM)
    hist_rows = slice(SUBLANES - (CONV_W - 1), SUBLANES)
    tiles_per_seq = seq // PROJ_TM
    conv_p = tail_p[tiles_per_seq - 1::tiles_per_seq, hist_rows, conv_cols]
    conv_s = tail_s[:, hist_rows, conv_cols]

    state_shape = (SSD_HEADS, SSD_HEAD_DIM, D_STATE)
    return (yp.reshape(bp, seq, d), ys.reshape(bs, dec_seq, d),
            ssm_p.reshape(1, bp, *state_shape), conv_p[None],
            ssm_s.reshape(1, bs, *state_shape), conv_s[None], vn_s.reshape(1, bs, dec_seq, d))
```

```python
import functools
from typing import Any, NamedTuple

import numpy as np
import jax
import jax.numpy as jnp
from jax import lax
from jax.experimental import pallas as pl
from jax.experimental.pallas import tpu as pltpu

F32 = jnp.float32
BF16 = jnp.bfloat16

D_MODEL = 2048
SSD_HEADS = 32
SSD_HEAD_DIM = 64
SSD_GROUPS = 4
HEADS_PER_GROUP = SSD_HEADS // SSD_GROUPS
D_STATE = 128
GROUP_COLS = HEADS_PER_GROUP * SSD_HEAD_DIM
CONV_W = 4
CONV_DIM = D_MODEL + 2 * SSD_GROUPS * D_STATE
MLP_CHUNK = 128
MLP_GROUPS = 8
MLP_GROUP_DIM = D_MODEL // MLP_GROUPS
EPS = 1e-5

SUBLANES = 8
LANES = 128
SSD_BLOCK = 128
SSD_BLOCKS_PER_STEP = 8
PROJ_TN = 1024
VMEM_LIMIT = 56 * 1024 * 1024

N_ACT = 7 * D_MODEL + 2 * SSD_GROUPS * D_STATE
N_PROJ_TILES = N_ACT // PROJ_TN
DT_OFFSET = D_MODEL + CONV_DIM
FIRST_SHIFTED_TILE = DT_OFFSET // PROJ_TN
CONV_STEPS = (2, 3, 4)
IDENT_STEPS = (5, 6, 7, 8)
FIRST_SIGMOID_STEP = 11
PROJ_TM = 1024
PROJ_NSPLIT = 2
PROJ_MSPLIT = 4
NORM_TM = 512
MERGE_TM = 256


def _silu(x):
    hx = 0.5 * x
    return hx + hx * jnp.tanh(hx)


def _softplus(x):
    return jnp.maximum(x, 0.0) + jnp.log1p(jnp.exp(-jnp.abs(x)))


def _mods_kernel(c_ref, w_ref, b_ref, o_ref):
    s = _silu(c_ref[...]).astype(BF16)
    o_ref[...] = jnp.dot(s, w_ref[...].astype(BF16), preferred_element_type=F32) + b_ref[...]


def _mods(c_pad, w_ada, b_ada):
    rows, d = c_pad.shape
    n = w_ada.shape[1]
    tn = 1024
    return pl.pallas_call(
        _mods_kernel,
        grid=(n // tn,),
        in_specs=[
            pl.BlockSpec((rows, d), lambda j: (0, 0)),
            pl.BlockSpec((d, tn), lambda j: (0, j)),
            pl.BlockSpec((1, tn), lambda j: (0, j)),
        ],
        out_specs=pl.BlockSpec((rows, tn), lambda j: (0, j)),
        out_shape=jax.ShapeDtypeStruct((rows, n), F32),
        compiler_params=pltpu.CompilerParams(
            dimension_semantics=("arbitrary",), vmem_limit_bytes=VMEM_LIMIT),
        name="mods",
    )(c_pad, w_ada, b_ada)


def _norm_kernel(x_ref, sc_ref, sh_ref, gpre_ref, wdt_ref, dtb_ref, h_ref, dt_ref, *, tm, spt):
    rps = tm // spt
    for s in range(spt):
        rows = slice(s * rps, (s + 1) * rps)
        x = x_ref[rows, :]
        y = x * lax.rsqrt(jnp.mean(x * x, axis=-1, keepdims=True) + EPS)
        y = y * gpre_ref[...]
        h_ref[rows, :] = (y * (1.0 + sc_ref[s]) + sh_ref[s]).astype(BF16)
    dt_raw = lax.dot_general(h_ref[...], wdt_ref[...], (((1,), (1,)), ((), ())),
                             preferred_element_type=F32) + dtb_ref[...]
    lane = lax.broadcasted_iota(jnp.int32, dt_raw.shape, 1)
    dt_ref[...] = jnp.where(lane < SSD_HEADS, _softplus(dt_raw), 0.0)


def _norm(x2d, scale, shift, g_pre, w_dt, dt_bias, *, tm, spt):
    rows = x2d.shape[0]
    n_seq = scale.shape[0]
    n_tiles = rows // tm
    tiles_per_seq = max(1, n_tiles // n_seq) if spt == 1 else 1
    seq_blk = (lambda i: i // tiles_per_seq) if spt == 1 else (lambda i: i)
    return pl.pallas_call(
        functools.partial(_norm_kernel, tm=tm, spt=spt),
        grid=(n_tiles,),
        in_specs=[
            pl.BlockSpec((tm, D_MODEL), lambda i: (i, 0)),
            pl.BlockSpec((spt, 1, D_MODEL), lambda i: (seq_blk(i), 0, 0)),
            pl.BlockSpec((spt, 1, D_MODEL), lambda i: (seq_blk(i), 0, 0)),
            pl.BlockSpec((1, D_MODEL), lambda i: (0, 0)),
            pl.BlockSpec((LANES, D_MODEL), lambda i: (0, 0)),
            pl.BlockSpec((1, LANES), lambda i: (0, 0)),
        ],
        out_specs=[
            pl.BlockSpec((tm, D_MODEL), lambda i: (i, 0)),
            pl.BlockSpec((tm, LANES), lambda i: (i, 0)),
        ],
        out_shape=[
            jax.ShapeDtypeStruct((rows, D_MODEL), BF16),
            jax.ShapeDtypeStruct((rows, LANES), F32),
        ],
        compiler_params=pltpu.CompilerParams(
            dimension_semantics=("arbitrary",), vmem_limit_bytes=VMEM_LIMIT),
        name="norm",
    )(x2d, scale, shift, g_pre, w_dt, dt_bias)


def _act_tile(j):
    return jnp.where(j < CONV_STEPS[2], j, jnp.where(j == CONV_STEPS[2], N_PROJ_TILES - 1, j - 1))


def _wt_row(j):
    return pl.multiple_of(j * PROJ_TN + jnp.where(j >= FIRST_SHIFTED_TILE, SSD_HEADS, 0), SSD_HEADS)


def _conv_in_tile(j):
    return jnp.clip(j - CONV_STEPS[0], 0, len(CONV_STEPS) - 1)


class _Rows(NamedTuple):
    h_ref: Any
    act_ref: Any
    tail_ref: Any
    hist_ref: Any
    tm: int
    spt: int
    msplit: int
    first: Any


def _proj_step(j, row_sets, cw_ref, cb_ref, wbf_scr, halo_scr, stgs, shift_scr, *, nsplit):
    cw = PROJ_TN // nsplit
    blocks = [(rs, nb, mb) for nb in range(nsplit) for rs in row_sets for mb in range(rs.msplit)]

    def dot_block(k):
        rs, nb, mb = blocks[k]
        rb = rs.tm // rs.msplit
        stgs[k % 2][SUBLANES:SUBLANES + rb, :] = lax.dot_general(
            rs.h_ref[mb * rb:(mb + 1) * rb, :], wbf_scr[nb * cw:(nb + 1) * cw, :],
            (((1,), (1,)), ((), ())), preferred_element_type=F32)

    def run(epilogue):
        dot_block(0)
        for k, (rs, nb, mb) in enumerate(blocks):
            if k + 1 < len(blocks):
                dot_block(k + 1)
            epilogue(rs, nb, mb, stgs[k % 2])

    is_conv = jnp.logical_and(j >= CONV_STEPS[0], j <= CONV_STEPS[2])
    is_ident = jnp.logical_and(j >= IDENT_STEPS[0], j <= IDENT_STEPS[-1])
    is_sigmoid = j >= FIRST_SIGMOID_STEP

    p_x = jnp.where(is_ident, 1.0, 0.5).astype(F32)
    p_xt = jnp.where(is_ident, 0.0, 0.5).astype(F32)

    def elementwise(rs, nb, mb, stg):
        rb = rs.tm // rs.msplit
        x = stg[SUBLANES:SUBLANES + rb, :]
        t = jnp.tanh(0.5 * x)
        y = jnp.where(is_sigmoid, 0.5 + 0.5 * t, x * (p_x + p_xt * t))
        rs.act_ref[mb * rb:(mb + 1) * rb, nb * cw:(nb + 1) * cw] = y.astype(BF16)

    @pl.when(jnp.logical_not(is_conv))
    def _():
        for rs in row_sets:
            rs.tail_ref[...] = jnp.zeros_like(rs.tail_ref)
        run(elementwise)

    @pl.when(is_conv)
    def _():
        for rs in row_sets:
            if rs.spt == 1:
                halo_scr[...] = jnp.where(rs.first, rs.hist_ref[0], halo_scr[...])

        shift_scr[0:SUBLANES, :] = jnp.zeros((SUBLANES, cw), F32)

        def shift_down(v):
            n = v.shape[0]
            shift_scr[1:1 + n, :] = v
            return shift_scr[0:n, :]

        def conv(ext, cols):
            w0, w1, w2, w3 = (0.5 * cw_ref[k:k + 1, cols] for k in range(CONV_W))
            t = ext * w0
            t = shift_down(t) + ext * w1
            t = shift_down(t) + ext * w2
            t = shift_down(t) + ext * w3 + 0.5 * cb_ref[:, cols]
            t = t[SUBLANES:, :]
            return (t + t * jnp.tanh(t)).astype(BF16)

        def ep(rs, nb, mb, stg):
            cols = slice(nb * cw, (nb + 1) * cw)
            rb = rs.tm // rs.msplit
            if rs.spt == 1:
                stg[0:SUBLANES, :] = halo_scr[:, cols]
                rs.act_ref[mb * rb:(mb + 1) * rb, cols] = conv(stg[0:SUBLANES + rb, :], cols)
                last = stg[rb:rb + SUBLANES, :]
                halo_scr[:, cols] = last
                rs.tail_ref[0, :, cols] = last
            else:
                rps = rs.tm // rs.spt
                for s in range(rb // rps):
                    rows = slice(SUBLANES + s * rps, SUBLANES + (s + 1) * rps)
                    seq = mb * (rb // rps) + s
                    ext = jnp.concatenate([rs.hist_ref[seq, :, cols], stg[rows, :]], axis=0)
                    rs.act_ref[mb * rb + s * rps:mb * rb + (s + 1) * rps, cols] = conv(ext, cols)
                    rs.tail_ref[seq, :, cols] = stg[rows.stop - SUBLANES:rows.stop, :]

        run(ep)


def _proj_kernel(hp_ref, hs_ref, wt_ref, cw_ref, cb_ref, histp_ref, hists_ref,
                 actp_ref, acts_ref, tailp_ref, tails_ref, wbf_scr, halo_scr, stg0_scr, stg1_scr,
                 shift_scr, *, n_p, tm_p, tm_s, spt_s, nsplit, msplit_p, tiles_per_seq):
    j = pl.program_id(0)
    i = pl.program_id(1)
    wrows = 256

    @pl.when(i == 0)
    def _():
        for r in range(PROJ_TN // wrows):
            rows = slice(r * wrows, (r + 1) * wrows)
            wbf_scr[rows, :] = wt_ref[rows, :].astype(BF16)

    shared = (cw_ref, cb_ref, wbf_scr, halo_scr, (stg0_scr, stg1_scr), shift_scr)
    prompt = _Rows(hp_ref, actp_ref, tailp_ref, histp_ref, tm=tm_p, spt=1, msplit=msplit_p,
                   first=(i % tiles_per_seq) == 0)
    sample = _Rows(hs_ref, acts_ref, tails_ref, hists_ref, tm=tm_s, spt=spt_s, msplit=1, first=None)

    @pl.when(i < n_p - 1)
    def _():
        _proj_step(j, [prompt], *shared, nsplit=nsplit)

    @pl.when(i == n_p - 1)
    def _():
        _proj_step(j, [prompt, sample], *shared, nsplit=nsplit)


def _proj(hp, hs, w_t, conv_w, conv_b, hist_p, hist_s, *, tm_p, nsplit, msplit_p):
    rows_p, rows_s = hp.shape[0], hs.shape[0]
    n_p = rows_p // tm_p
    tiles_per_seq = n_p // hist_p.shape[0]
    spt_s = hist_s.shape[0]
    rb = max(tm_p // msplit_p, rows_s)
    kern = functools.partial(_proj_kernel, n_p=n_p, tm_p=tm_p, tm_s=rows_s, spt_s=spt_s,
                             nsplit=nsplit, msplit_p=msplit_p, tiles_per_seq=tiles_per_seq)
    return pl.pallas_call(
        kern,
        grid=(N_PROJ_TILES, n_p),
        in_specs=[
            pl.BlockSpec((tm_p, D_MODEL), lambda j, i: (i, 0)),
            pl.BlockSpec((rows_s, D_MODEL), lambda j, i: (0, 0)),
            pl.BlockSpec((pl.Element(PROJ_TN), pl.Element(D_MODEL)), lambda j, i: (_wt_row(j), 0)),
            pl.BlockSpec((CONV_W, PROJ_TN), lambda j, i: (0, _conv_in_tile(j))),
            pl.BlockSpec((1, PROJ_TN), lambda j, i: (0, _conv_in_tile(j))),
            pl.BlockSpec((1, SUBLANES, PROJ_TN), lambda j, i: (i // tiles_per_seq, 0, _conv_in_tile(j))),
            pl.BlockSpec((spt_s, SUBLANES, PROJ_TN), lambda j, i: (0, 0, _conv_in_tile(j))),
        ],
        out_specs=[
            pl.BlockSpec((tm_p, PROJ_TN), lambda j, i: (i, _act_tile(j))),
            pl.BlockSpec((rows_s, PROJ_TN), lambda j, i: (0, _act_tile(j))),
            pl.BlockSpec((1, SUBLANES, PROJ_TN), lambda j, i: (i, 0, j)),
            pl.BlockSpec((spt_s, SUBLANES, PROJ_TN), lambda j, i: (0, 0, j)),
        ],
        out_shape=[
            jax.ShapeDtypeStruct((rows_p, N_ACT), BF16),
            jax.ShapeDtypeStruct((rows_s, N_ACT), BF16),
            jax.ShapeDtypeStruct((n_p, SUBLANES, N_PROJ_TILES * PROJ_TN), F32),
            jax.ShapeDtypeStruct((spt_s, SUBLANES, N_PROJ_TILES * PROJ_TN), F32),
        ],
        scratch_shapes=[
            pltpu.VMEM((PROJ_TN, D_MODEL), BF16),
            pltpu.VMEM((SUBLANES, PROJ_TN), F32),
            pltpu.VMEM((rb + SUBLANES, PROJ_TN // nsplit), F32),
            pltpu.VMEM((rb + SUBLANES, PROJ_TN // nsplit), F32),
            pltpu.VMEM((rb + 2 * SUBLANES, PROJ_TN // nsplit), F32),
        ],
        compiler_params=pltpu.CompilerParams(
            dimension_semantics=("arbitrary", "arbitrary"), vmem_limit_bytes=VMEM_LIMIT),
        name="proj",
    )(hp, hs, w_t, conv_w, conv_b, hist_p, hist_s)


def _pad_rows(v, rows):
    if v.shape[0] == rows:
        return v
    return jnp.concatenate([v, jnp.zeros((rows - v.shape[0], v.shape[1]), v.dtype)], axis=0)


def _ssd_kernel(szs_ref, xs_ref, b_ref, c_ref, dt_ref, alog_ref, dskip_ref, nrm_ref, e3_ref, h0_ref,
                y_ref, hout_ref, st_scr, *, q_in, blocks_per_step, n_steps):
    ci = pl.program_id(1)

    @pl.when(ci == 0)
    def _():
        st_scr[...] = h0_ref[0]

    for k in range(blocks_per_step):
        rs = pl.ds(k * q_in, q_in)
        _ssd_block(szs_ref.at[rs], xs_ref.at[rs], b_ref.at[rs], c_ref.at[rs], dt_ref.at[rs],
                   alog_ref, dskip_ref, nrm_ref, e3_ref, y_ref.at[rs], st_scr, q_in=q_in)

    @pl.when(ci == n_steps - 1)
    def _():
        hout_ref[0] = st_scr[...]


def _ssd_block(szs_ref, xs_ref, b_ref, c_ref, dt_ref, alog_ref, dskip_ref, nrm_ref, e3_ref,
               y_ref, st_scr, *, q_in):
    q = SSD_BLOCK
    dt = _pad_rows(dt_ref[...], q)
    a = -jnp.exp(alog_ref[...])
    adt = dt * a
    row = lax.broadcasted_iota(jnp.int32, (q, q), 0)
    col = lax.broadcasted_iota(jnp.int32, (q, q), 1)
    causal = row >= col
    acum = jnp.dot(causal.astype(F32), adt, precision=lax.Precision.HIGHEST,
                   preferred_element_type=F32)
    alast = acum[q - 1:q, :]
    dte = jnp.exp(alast - acum)
    ea = jnp.exp(acum)

    stack = jnp.concatenate([dt, dt * dte, ea], axis=0)
    lane3 = lax.broadcasted_iota(jnp.int32, stack.shape, 1)
    stack = jnp.where(lane3 < SSD_HEADS, stack, 0.0)
    hi = stack.astype(BF16).astype(F32)
    r1 = stack - hi
    mid = r1.astype(BF16).astype(F32)
    lo = (r1 - mid).astype(BF16).astype(F32)
    comb = hi + pltpu.roll(mid, SSD_HEADS, axis=1) + pltpu.roll(lo, 2 * SSD_HEADS, axis=1)
    expd = jnp.dot(comb.astype(BF16), e3_ref[...], preferred_element_type=F32)
    dt_e, w_e, ea_e = expd[0:q], expd[q:2 * q], expd[2 * q:3 * q]

    x = _pad_rows(xs_ref[...], q).astype(F32)
    xdt = (x * dt_e).astype(BF16)
    xdtd = (x * w_e).astype(BF16)
    bm = _pad_rows(b_ref[...], q)
    cm = _pad_rows(c_ref[...], q)

    acum_t = acum.T
    dec = jnp.exp(acum_t[0:SSD_HEADS, q - 1:q])
    dec = jnp.broadcast_to(dec, (SSD_HEADS, D_STATE))
    dec = jnp.broadcast_to(dec[:, None, :], (SSD_HEADS, SSD_HEAD_DIM, D_STATE))
    dec = dec.reshape(SSD_HEADS * SSD_HEAD_DIM, D_STATE)

    lane = lax.broadcasted_iota(jnp.int32, (q, LANES), 1)
    nt = (((1,), (1,)), ((), ()))
    tn = (((0,), (0,)), ((), ()))
    y_groups = []
    for g in range(SSD_GROUPS):
        gc = slice(g * GROUP_COLS, (g + 1) * GROUP_COLS)
        bg = bm[:, g * D_STATE:(g + 1) * D_STATE]
        cg = cm[:, g * D_STATE:(g + 1) * D_STATE]
        cb = lax.dot_general(cg, bg, nt, preferred_element_type=F32)
        sg = st_scr[gc, :]
        y_off = lax.dot_general(cg, sg.astype(BF16), nt, preferred_element_type=F32)
        pieces = []
        for pr in range(HEADS_PER_GROUP // 2):
            ms = []
            for hh in range(2):
                h = g * HEADS_PER_GROUP + 2 * pr + hh
                seg = jnp.broadcast_to(acum[:, h:h + 1], (q, q)) - acum_t[h:h + 1, :]
                lm = jnp.exp(jnp.where(causal, seg, -jnp.inf))
                ms.append((cb * lm).astype(BF16))
            m = jnp.concatenate(ms, axis=1)
            c0 = g * GROUP_COLS + pr * LANES
            xp = xdt[:, c0:c0 + LANES]
            rhs = jnp.concatenate([jnp.where(lane < SSD_HEAD_DIM, xp, jnp.zeros_like(xp)),
                                   jnp.where(lane >= SSD_HEAD_DIM, xp, jnp.zeros_like(xp))], axis=0)
            pieces.append(jnp.dot(m, rhs, preferred_element_type=F32))
        y_diag = jnp.concatenate(pieces, axis=1)
        y_groups.append(y_diag + y_off * ea_e[:, gc])
        new = lax.dot_general(xdtd[:, gc], bg, tn, preferred_element_type=F32)
        st_scr[gc, :] = sg * dec[gc, :] + new
    y = jnp.concatenate(y_groups, axis=1)
    y = y + dskip_ref[...] * x
    yz = y * _pad_rows(szs_ref[...], q).astype(F32)
    yn = yz * lax.rsqrt(jnp.mean(yz * yz, axis=-1, keepdims=True) + EPS) * nrm_ref[...]
    y_ref[...] = yn[0:q_in].astype(BF16)


def _ssd(act, dt, a_log, dskip_e, norm_ssd, e3, h0, *, n_seq, q_in, blocks_per_step):
    rows = act.shape[0]
    qb = q_in * blocks_per_step
    n_chunks = rows // (n_seq * qb)
    kern = functools.partial(_ssd_kernel, q_in=q_in, blocks_per_step=blocks_per_step, n_steps=n_chunks)
    rblk = lambda b, c: b * n_chunks + c
    bc_blk0 = 7 * D_MODEL // GROUP_COLS
    return pl.pallas_call(
        kern,
        grid=(n_seq, n_chunks),
        in_specs=[
            pl.BlockSpec((qb, D_MODEL), lambda b, c: (rblk(b, c), 0)),
            pl.BlockSpec((qb, D_MODEL), lambda b, c: (rblk(b, c), 1)),
            pl.BlockSpec((qb, GROUP_COLS), lambda b, c: (rblk(b, c), bc_blk0)),
            pl.BlockSpec((qb, GROUP_COLS), lambda b, c: (rblk(b, c), bc_blk0 + 1)),
            pl.BlockSpec((qb, LANES), lambda b, c: (rblk(b, c), 0)),
            pl.BlockSpec((1, LANES), lambda b, c: (0, 0)),
            pl.BlockSpec((1, D_MODEL), lambda b, c: (0, 0)),
            pl.BlockSpec((1, D_MODEL), lambda b, c: (0, 0)),
            pl.BlockSpec((LANES, D_MODEL), lambda b, c: (0, 0)),
            pl.BlockSpec((1, D_MODEL, D_STATE), lambda b, c: (b, 0, 0)),
        ],
        out_specs=[
            pl.BlockSpec((qb, D_MODEL), lambda b, c: (rblk(b, c), 0)),
            pl.BlockSpec((1, D_MODEL, D_STATE), lambda b, c: (b, 0, 0)),
        ],
        out_shape=[
            jax.ShapeDtypeStruct((rows, D_MODEL), BF16),
            jax.ShapeDtypeStruct((n_seq, D_MODEL, D_STATE), F32),
        ],
        scratch_shapes=[pltpu.VMEM((D_MODEL, D_STATE), F32)],
        compiler_params=pltpu.CompilerParams(
            dimension_semantics=("arbitrary", "arbitrary"), vmem_limit_bytes=VMEM_LIMIT),
        name="ssd",
    )(act, act, act, act, dt, a_log, dskip_e, norm_ssd, e3, h0)


def _mlp_stage(u_ref, v_ref, szm_ref, lng_ref, lnb_ref, wsp_ref, bsp_ref, ymlp_ref, vn_ref, *, tm, qm):
    v = v_ref[...].astype(F32)
    mu = jnp.mean(v, axis=-1, keepdims=True)
    vc = v - mu
    var = jnp.mean(vc * vc, axis=-1, keepdims=True)
    vn = vc * lax.rsqrt(var + EPS) * lng_ref[...] + lnb_ref[...]
    if vn_ref is not None:
        vn_ref[...] = vn
    vn_b = vn.astype(BF16)

    row = lax.broadcasted_iota(jnp.int32, (MLP_CHUNK, MLP_CHUNK), 0)
    col = lax.broadcasted_iota(jnp.int32, (MLP_CHUNK, MLP_CHUNK), 1)
    wms = [jnp.where(row >= col, wsp_ref[g], 0.0).astype(BF16) for g in range(MLP_GROUPS)]
    for ck in range(tm // qm):
        rows = slice(ck * qm, (ck + 1) * qm)
        vck = _pad_rows(vn_b[rows, :], MLP_CHUNK)
        mixed = jnp.concatenate(
            [jnp.dot(wms[g], vck[:, g * MLP_GROUP_DIM:(g + 1) * MLP_GROUP_DIM],
                     preferred_element_type=F32) for g in range(MLP_GROUPS)], axis=1)
        mixed = mixed[0:qm] + bsp_ref[...]
        y_mlp = u_ref[rows, :].astype(F32) * mixed * szm_ref[rows, :].astype(F32)
        ymlp_ref[rows, :] = y_mlp.astype(BF16)


def _mix_stage(yssd_ref, ymlp_ref, sgs_ref, sgm_ref, wbs_ref, wbm_ref, wo_ref, ready=lambda w: None):
    ready(0)
    a = jnp.dot(yssd_ref[...], wbs_ref[...], preferred_element_type=F32)
    ready(1)
    b = jnp.dot(ymlp_ref[...], wbm_ref[...], preferred_element_type=F32)
    merged = sgs_ref[...].astype(F32) * a + sgm_ref[...].astype(F32) * b
    ready(2)
    return jnp.dot(merged.astype(BF16), wo_ref[...], preferred_element_type=F32)


def _out_stage(o, x_ref, gate_ref, gfin_ref, y_ref, *, tm, spt):
    rps = tm // spt
    for s in range(spt):
        rows = slice(s * rps, (s + 1) * rps)
        out = x_ref[rows, :] + gate_ref[s] * o[rows, :]
        y = out * lax.rsqrt(jnp.mean(out * out, axis=-1, keepdims=True) + EPS)
        y_ref[rows, :] = y * gfin_ref[...]


def _merge_kernel(yssd_ref, u_ref, v_ref, szm_ref, sgs_ref, sgm_ref, x_ref, gate_ref,
                  lng_ref, lnb_ref, gfin_ref, wsp_ref, bsp_ref, wbs_hbm, wbm_hbm, wo_hbm,
                  y_ref, vn_ref, ymlp_scr, w_scr, w_sem, *, tm, qm, spt):
    w_hbm = (wbs_hbm, wbm_hbm, wo_hbm)

    def w_copy(w):
        return pltpu.make_async_copy(w_hbm[w], w_scr.at[w], w_sem.at[w])

    for w in range(3):
        w_copy(w).start()
    _mlp_stage(u_ref, v_ref, szm_ref, lng_ref, lnb_ref, wsp_ref, bsp_ref, ymlp_scr, vn_ref, tm=tm, qm=qm)
    o = _mix_stage(yssd_ref, ymlp_scr, sgs_ref, sgm_ref, w_scr.at[0], w_scr.at[1], w_scr.at[2],
                   ready=lambda w: w_copy(w).wait())
    _out_stage(o, x_ref, gate_ref, gfin_ref, y_ref, tm=tm, spt=spt)


def _merge(yssd, act, x2d, gate, ln_g, ln_b, g_final, w_spatial, bsp_e, wbs, wbm, wo, *, qm, spt):
    rows = x2d.shape[0]
    tm = rows
    kern = functools.partial(_merge_kernel, tm=tm, qm=qm, spt=spt)
    const = lambda shape: pl.BlockSpec(shape, lambda i: (0,) * len(shape))
    seg = lambda k: pl.BlockSpec((tm, D_MODEL), lambda i: (0, k))
    hbm = pl.BlockSpec(memory_space=pl.ANY)
    return pl.pallas_call(
        kern,
        grid=(1,),
        in_specs=[
            seg(0),
            seg(2), seg(3), seg(4), seg(5), seg(6),
            seg(0),
            const((spt, 1, D_MODEL)),
            const((1, D_MODEL)), const((1, D_MODEL)), const((1, D_MODEL)),
            const((MLP_GROUPS, MLP_CHUNK, MLP_CHUNK)),
            const((qm, D_MODEL)),
            hbm, hbm, hbm,
        ],
        out_specs=[seg(0), seg(0)],
        out_shape=[jax.ShapeDtypeStruct((rows, D_MODEL), F32)] * 2,
        scratch_shapes=[
            pltpu.VMEM((tm, D_MODEL), BF16),
            pltpu.VMEM((3, D_MODEL, D_MODEL), BF16),
            pltpu.SemaphoreType.DMA((3,)),
        ],
        compiler_params=pltpu.CompilerParams(
            dimension_semantics=("arbitrary",), vmem_limit_bytes=VMEM_LIMIT),
        name="merge",
    )(yssd, act, act, act, act, act, x2d, gate, ln_g, ln_b, g_final, w_spatial, bsp_e, wbs, wbm, wo)


WEIGHT_CHUNK = 256


def _merge_stream_kernel(yssd_ref, u_ref, v_ref, szm_ref, sgs_ref, sgm_ref, x_ref, gate_ref,
                         lng_ref, lnb_ref, gfin_ref, wsp_ref, bsp_ref, wbs_hbm, wbm_hbm, wo_hbm,
                         y_ref, wbs_out, wbm_out, wo_out,
                         ymlp_scr, w_scr, stage_scr, in_sem, out_sem, *, tm, qm, n_steps):
    i = pl.program_id(0)
    w_hbm = (wbs_hbm, wbm_hbm, wo_hbm)
    w_out = (wbs_out, wbm_out, wo_out)
    chunks = [(w, c) for w in range(3) for c in range(D_MODEL // WEIGHT_CHUNK)]

    def chunk_copy(k):
        w, c = chunks[k]
        return pltpu.make_async_copy(w_hbm[w].at[pl.ds(c * WEIGHT_CHUNK, WEIGHT_CHUNK), :],
                                     stage_scr.at[k % 2], in_sem.at[k % 2])

    def out_copy(w):
        return pltpu.make_async_copy(w_scr.at[w], w_out[w], out_sem.at[w])

    @pl.when(i == 0)
    def _():
        chunk_copy(0).start()
        for k, (w, c) in enumerate(chunks):
            if k + 1 < len(chunks):
                chunk_copy(k + 1).start()
            chunk_copy(k).wait()
            w_scr[w, c * WEIGHT_CHUNK:(c + 1) * WEIGHT_CHUNK, :] = stage_scr[k % 2].astype(BF16)
        for w in range(3):
            out_copy(w).start()

    _mlp_stage(u_ref, v_ref, szm_ref, lng_ref, lnb_ref, wsp_ref, bsp_ref, ymlp_scr, None, tm=tm, qm=qm)
    o = _mix_stage(yssd_ref, ymlp_scr, sgs_ref, sgm_ref, w_scr.at[0], w_scr.at[1], w_scr.at[2])
    _out_stage(o, x_ref, gate_ref, gfin_ref, y_ref, tm=tm, spt=1)

    @pl.when(i == n_steps - 1)
    def _():
        for w in range(3):
            out_copy(w).wait()


def _merge_stream(yssd, act, x2d, gate, ln_g, ln_b, g_final, w_spatial, bsp_e, wbs, wbm, wo, *, tm, qm):
    rows = x2d.shape[0]
    n_tiles = rows // tm
    assert n_tiles > 1
    tiles_per_seq = n_tiles // gate.shape[0]
    kern = functools.partial(_merge_stream_kernel, tm=tm, qm=qm, n_steps=n_tiles)
    const = lambda shape: pl.BlockSpec(shape, lambda i: (0,) * len(shape),
                                       pipeline_mode=pl.Buffered(1))
    seg = lambda k: pl.BlockSpec((tm, D_MODEL), lambda i: (i, k))
    hbm = pl.BlockSpec(memory_space=pl.ANY)
    return pl.pallas_call(
        kern,
        grid=(n_tiles,),
        in_specs=[
            pl.BlockSpec((tm, D_MODEL), lambda i: (i, 0)),
            seg(2), seg(3), seg(4), seg(5), seg(6),
            pl.BlockSpec((tm, D_MODEL), lambda i: (i, 0)),
            pl.BlockSpec((1, 1, D_MODEL), lambda i: (i // tiles_per_seq, 0, 0)),
            const((1, D_MODEL)), const((1, D_MODEL)), const((1, D_MODEL)),
            const((MLP_GROUPS, MLP_CHUNK, MLP_CHUNK)),
            const((qm, D_MODEL)),
            hbm, hbm, hbm,
        ],
        out_specs=[pl.BlockSpec((tm, D_MODEL), lambda i: (i, 0)), hbm, hbm, hbm],
        out_shape=[jax.ShapeDtypeStruct((rows, D_MODEL), F32)]
        + [jax.ShapeDtypeStruct((D_MODEL, D_MODEL), BF16)] * 3,
        scratch_shapes=[
            pltpu.VMEM((tm, D_MODEL), BF16),
            pltpu.VMEM((3, D_MODEL, D_MODEL), BF16),
            pltpu.VMEM((2, WEIGHT_CHUNK, D_MODEL), F32),
            pltpu.SemaphoreType.DMA((2,)),
            pltpu.SemaphoreType.DMA((3,)),
        ],
        compiler_params=pltpu.CompilerParams(
            dimension_semantics=("arbitrary",), vmem_limit_bytes=VMEM_LIMIT),
        name="merge_stream",
    )(yssd, act, act, act, act, act, x2d, gate, ln_g, ln_b, g_final, w_spatial, bsp_e, wbs, wbm, wo)


def _expansion_matrix():
    e = np.zeros((LANES, D_MODEL), np.float32)
    cols = np.arange(D_MODEL)
    for piece in range(3):
        e[piece * SSD_HEADS + cols // SSD_HEAD_DIM, cols] = 1.0
    return jnp.asarray(e, BF16)


def kernel(x_prompt, x_sample, state_ssm, cache_conv, c_prompt, c_sample, w_ada, b_ada, g_pre, w_in,
           conv_w, conv_b, dt_bias, a_log, d_skip, norm_ssd, ln_g, ln_b, w_spatial, b_spatial,
           w_bp_ssd, w_bp_mlp, w_o, g_final):
    depth = w_ada.shape[0]
    assert depth == 1, "single-layer trunk only"
    bp, seq, d = x_prompt.shape
    bs, dec_seq, _ = x_sample.shape
    assert d == D_MODEL and seq % 1024 == 0 and dec_seq % 16 == 0 and dec_seq <= SSD_BLOCK

    w_t = jnp.swapaxes(w_in[0], 0, 1)
    w_dt = jnp.pad(w_t[DT_OFFSET:DT_OFFSET + SSD_HEADS], ((0, LANES - SSD_HEADS), (0, 0))).astype(BF16)
    pad_heads = lambda v: jnp.pad(v.reshape(1, SSD_HEADS), ((0, 0), (0, LANES - SSD_HEADS)))
    wts = dict(
        g_pre=g_pre[0].reshape(1, d), w_t=w_t, w_dt=w_dt, dt_bias=pad_heads(dt_bias[0]),
        conv_w=conv_w[0], conv_b=conv_b[0].reshape(1, CONV_DIM),
        a_log=pad_heads(a_log[0]), dskip_e=jnp.repeat(d_skip[0], SSD_HEAD_DIM).reshape(1, d),
        norm_ssd=norm_ssd[0].reshape(1, d), e3=_expansion_matrix(),
        ln_g=ln_g[0].reshape(1, d), ln_b=ln_b[0].reshape(1, d), g_final=g_final.reshape(1, d),
        w_spatial=w_spatial[0], bsp_e=jnp.repeat(b_spatial[0].T, MLP_GROUP_DIM, axis=1),
    )

    c_all = jnp.concatenate([c_prompt, c_sample], axis=0)
    c_pad = jnp.pad(c_all, ((0, 16 - (bp + bs)), (0, 0)))
    mods = _mods(c_pad, w_ada[0], b_ada[0].reshape(1, 3 * d))

    split_mods = lambda m: (m[:, k * d:(k + 1) * d].reshape(m.shape[0], 1, d) for k in range(3))
    shift_p, scale_p, gate_p = split_mods(mods[:bp])
    shift_s, scale_s, gate_s = split_mods(mods[bp:bp + bs])
    x2p = x_prompt.reshape(bp * seq, d)
    x2s = x_sample.reshape(bs * dec_seq, d)
    rows_s = bs * dec_seq

    hist_p = jnp.zeros((bp, SUBLANES, CONV_DIM), F32)
    h0_p = jnp.zeros((bp, d, D_STATE), F32)
    hist_s = jnp.pad(cache_conv[0], ((0, 0), (SUBLANES - (CONV_W - 1), 0), (0, 0)))
    h0_s = state_ssm[0].reshape(bs, d, D_STATE)

    norm_w = (wts["g_pre"], wts["w_dt"], wts["dt_bias"])
    hp, dt_p = _norm(x2p, scale_p, shift_p, *norm_w, tm=NORM_TM, spt=1)
    hs, dt_s = _norm(x2s, scale_s, shift_s, *norm_w, tm=rows_s, spt=bs)
    act_p, act_s, tail_p, tail_s = _proj(hp, hs, wts["w_t"], wts["conv_w"], wts["conv_b"], hist_p, hist_s,
                                         tm_p=PROJ_TM, nsplit=PROJ_NSPLIT, msplit_p=PROJ_MSPLIT)

    ssd_w = (wts["a_log"], wts["dskip_e"], wts["norm_ssd"], wts["e3"])
    yssd_p, ssm_p = _ssd(act_p, dt_p, *ssd_w, h0_p, n_seq=bp, q_in=SSD_BLOCK,
                         blocks_per_step=SSD_BLOCKS_PER_STEP)
    yssd_s, ssm_s = _ssd(act_s, dt_s, *ssd_w, h0_s, n_seq=bs, q_in=dec_seq, blocks_per_step=1)

    merge_w = (wts["ln_g"], wts["ln_b"], wts["g_final"], wts["w_spatial"])
    yp, *proj_w = _merge_stream(yssd_p, act_p, x2p, gate_p, *merge_w, wts["bsp_e"],
                                w_bp_ssd[0], w_bp_mlp[0], w_o[0], tm=MERGE_TM, qm=MLP_CHUNK)
    ys, vn_s = _merge(yssd_s, act_s, x2s, gate_s, *merge_w, wts["bsp_e"][:dec_seq], *proj_w,
                      qm=dec_seq, spt=bs)

    conv_cols = slice(CONV_STEPS[0] * PROJ_TN, CONV_STEPS[0] * PROJ_TN + CONV_DIM)
    hist_rows = slice(SUBLANES - (CONV_W - 1), SUBLANES)
    tiles_per_seq = seq // PROJ_TM
    conv_p = tail_p[tiles_per_seq - 1::tiles_per_seq, hist_rows, conv_cols]
    conv_s = tail_s[:, hist_rows, conv_cols]

    state_shape = (SSD_HEADS, SSD_HEAD_DIM, D_STATE)
    return (yp.reshape(bp, seq, d), ys.reshape(bs, dec_seq, d),
            ssm_p.reshape(1, bp, *state_shape), conv_p[None],
            ssm_s.reshape(1, bs, *state_shape), conv_s[None], vn_s.reshape(1, bs, dec_seq, d))
```

```python
import functools
from typing import Any, NamedTuple

import numpy as np
import jax
import jax.numpy as jnp
from jax import lax
from jax.experimental import pallas as pl
from jax.experimental.pallas import tpu as pltpu

F32 = jnp.float32
BF16 = jnp.bfloat16

D_MODEL = 2048
SSD_HEADS = 32
SSD_HEAD_DIM = 64
SSD_GROUPS = 4
HEADS_PER_GROUP = SSD_HEADS // SSD_GROUPS
D_STATE = 128
GROUP_COLS = HEADS_PER_GROUP * SSD_HEAD_DIM
CONV_W = 4
CONV_DIM = D_MODEL + 2 * SSD_GROUPS * D_STATE
MLP_CHUNK = 128
MLP_GROUPS = 8
MLP_GROUP_DIM = D_MODEL // MLP_GROUPS
EPS = 1e-5

SUBLANES = 8
LANES = 128
SSD_BLOCK = 128
SSD_BLOCKS_PER_STEP = 4
PROJ_TN = 1024
VMEM_LIMIT = 56 * 1024 * 1024

N_ACT = 7 * D_MODEL + 2 * SSD_GROUPS * D_STATE
N_PROJ_TILES = N_ACT // PROJ_TN
DT_OFFSET = D_MODEL + CONV_DIM
FIRST_SHIFTED_TILE = DT_OFFSET // PROJ_TN
CONV_STEPS = (2, 3, 4)
IDENT_STEPS = (5, 6, 7, 8)
FIRST_SIGMOID_STEP = 11
PROJ_TM = 1024
PROJ_NSPLIT = 2
PROJ_MSPLIT = 4
NORM_TM = 512
MERGE_TM = 256


def _silu(x):
    hx = 0.5 * x
    return hx + hx * jnp.tanh(hx)


def _softplus(x):
    return jnp.maximum(x, 0.0) + jnp.log1p(jnp.exp(-jnp.abs(x)))


def _mods_kernel(c_ref, w_ref, b_ref, o_ref):
    s = _silu(c_ref[...]).astype(BF16)
    o_ref[...] = jnp.dot(s, w_ref[...].astype(BF16), preferred_element_type=F32) + b_ref[...]


def _mods(c_pad, w_ada, b_ada):
    rows, d = c_pad.shape
    n = w_ada.shape[1]
    tn = 1024
    return pl.pallas_call(
        _mods_kernel,
        grid=(n // tn,),
        in_specs=[
            pl.BlockSpec((rows, d), lambda j: (0, 0)),
            pl.BlockSpec((d, tn), lambda j: (0, j)),
            pl.BlockSpec((1, tn), lambda j: (0, j)),
        ],
        out_specs=pl.BlockSpec((rows, tn), lambda j: (0, j)),
        out_shape=jax.ShapeDtypeStruct((rows, n), F32),
        compiler_params=pltpu.CompilerParams(
            dimension_semantics=("arbitrary",), vmem_limit_bytes=VMEM_LIMIT),
        name="mods",
    )(c_pad, w_ada, b_ada)


def _norm_kernel(x_ref, sc_ref, sh_ref, gpre_ref, wdt_ref, dtb_ref, h_ref, dt_ref, *, tm, spt):
    rps = tm // spt
    for s in range(spt):
        rows = slice(s * rps, (s + 1) * rps)
        x = x_ref[rows, :]
        y = x * lax.rsqrt(jnp.mean(x * x, axis=-1, keepdims=True) + EPS)
        y = y * gpre_ref[...]
        h_ref[rows, :] = (y * (1.0 + sc_ref[s]) + sh_ref[s]).astype(BF16)
    dt_raw = lax.dot_general(h_ref[...], wdt_ref[...], (((1,), (1,)), ((), ())),
                             preferred_element_type=F32) + dtb_ref[...]
    lane = lax.broadcasted_iota(jnp.int32, dt_raw.shape, 1)
    dt_ref[...] = jnp.where(lane < SSD_HEADS, _softplus(dt_raw), 0.0)


def _norm(x2d, scale, shift, g_pre, w_dt, dt_bias, *, tm, spt):
    rows = x2d.shape[0]
    n_seq = scale.shape[0]
    n_tiles = rows // tm
    tiles_per_seq = max(1, n_tiles // n_seq) if spt == 1 else 1
    seq_blk = (lambda i: i // tiles_per_seq) if spt == 1 else (lambda i: i)
    return pl.pallas_call(
        functools.partial(_norm_kernel, tm=tm, spt=spt),
        grid=(n_tiles,),
        in_specs=[
            pl.BlockSpec((tm, D_MODEL), lambda i: (i, 0)),
            pl.BlockSpec((spt, 1, D_MODEL), lambda i: (seq_blk(i), 0, 0)),
            pl.BlockSpec((spt, 1, D_MODEL), lambda i: (seq_blk(i), 0, 0)),
            pl.BlockSpec((1, D_MODEL), lambda i: (0, 0)),
            pl.BlockSpec((LANES, D_MODEL), lambda i: (0, 0)),
            pl.BlockSpec((1, LANES), lambda i: (0, 0)),
        ],
        out_specs=[
            pl.BlockSpec((tm, D_MODEL), lambda i: (i, 0)),
            pl.BlockSpec((tm, LANES), lambda i: (i, 0)),
        ],
        out_shape=[
            jax.ShapeDtypeStruct((rows, D_MODEL), BF16),
            jax.ShapeDtypeStruct((rows, LANES), F32),
        ],
        compiler_params=pltpu.CompilerParams(
            dimension_semantics=("arbitrary",), vmem_limit_bytes=VMEM_LIMIT),
        name="norm",
    )(x2d, scale, shift, g_pre, w_dt, dt_bias)


def _act_tile(j):
    return jnp.where(j < CONV_STEPS[2], j, jnp.where(j == CONV_STEPS[2], N_PROJ_TILES - 1, j - 1))


def _wt_row(j):
    return pl.multiple_of(j * PROJ_TN + jnp.where(j >= FIRST_SHIFTED_TILE, SSD_HEADS, 0), SSD_HEADS)


def _conv_in_tile(j):
    return jnp.clip(j - CONV_STEPS[0], 0, len(CONV_STEPS) - 1)


class _Rows(NamedTuple):
    h_ref: Any
    act_ref: Any
    tail_ref: Any
    hist_ref: Any
    tm: int
    spt: int
    msplit: int
    first: Any


def _proj_step(j, row_sets, cw_ref, cb_ref, wbf_scr, halo_scr, stgs, shift_scr, *, nsplit):
    cw = PROJ_TN // nsplit
    blocks = [(rs, nb, mb) for nb in range(nsplit) for rs in row_sets for mb in range(rs.msplit)]

    def dot_block(k):
        rs, nb, mb = blocks[k]
        rb = rs.tm // rs.msplit
        stgs[k % 2][SUBLANES:SUBLANES + rb, :] = lax.dot_general(
            rs.h_ref[mb * rb:(mb + 1) * rb, :], wbf_scr[nb * cw:(nb + 1) * cw, :],
            (((1,), (1,)), ((), ())), preferred_element_type=F32)

    def run(epilogue):
        dot_block(0)
        for k, (rs, nb, mb) in enumerate(blocks):
            if k + 1 < len(blocks):
                dot_block(k + 1)
            epilogue(rs, nb, mb, stgs[k % 2])

    is_conv = jnp.logical_and(j >= CONV_STEPS[0], j <= CONV_STEPS[2])
    is_ident = jnp.logical_and(j >= IDENT_STEPS[0], j <= IDENT_STEPS[-1])
    is_sigmoid = j >= FIRST_SIGMOID_STEP

    p_x = jnp.where(is_ident, 1.0, 0.5).astype(F32)
    p_xt = jnp.where(is_ident, 0.0, 0.5).astype(F32)

    def elementwise(rs, nb, mb, stg):
        rb = rs.tm // rs.msplit
        x = stg[SUBLANES:SUBLANES + rb, :]
        t = jnp.tanh(0.5 * x)
        y = jnp.where(is_sigmoid, 0.5 + 0.5 * t, x * (p_x + p_xt * t))
        rs.act_ref[mb * rb:(mb + 1) * rb, nb * cw:(nb + 1) * cw] = y.astype(BF16)

    @pl.when(jnp.logical_not(is_conv))
    def _():
        for rs in row_sets:
            rs.tail_ref[...] = jnp.zeros_like(rs.tail_ref)
        run(elementwise)

    @pl.when(is_conv)
    def _():
        for rs in row_sets:
            if rs.spt == 1:
                halo_scr[...] = jnp.where(rs.first, rs.hist_ref[0], halo_scr[...])

        shift_scr[0:SUBLANES, :] = jnp.zeros((SUBLANES, cw), F32)

        def shift_down(v):
            n = v.shape[0]
            shift_scr[1:1 + n, :] = v
            return shift_scr[0:n, :]

        def conv(ext, cols):
            w0, w1, w2, w3 = (0.5 * cw_ref[k:k + 1, cols] for k in range(CONV_W))
            t = ext * w0
            t = shift_down(t) + ext * w1
            t = shift_down(t) + ext * w2
            t = shift_down(t) + ext * w3 + 0.5 * cb_ref[:, cols]
            t = t[SUBLANES:, :]
            return (t + t * jnp.tanh(t)).astype(BF16)

        def ep(rs, nb, mb, stg):
            cols = slice(nb * cw, (nb + 1) * cw)
            rb = rs.tm // rs.msplit
            if rs.spt == 1:
                stg[0:SUBLANES, :] = halo_scr[:, cols]
                rs.act_ref[mb * rb:(mb + 1) * rb, cols] = conv(stg[0:SUBLANES + rb, :], cols)
                last = stg[rb:rb + SUBLANES, :]
                halo_scr[:, cols] = last
                rs.tail_ref[0, :, cols] = last
            else:
                rps = rs.tm // rs.spt
                for s in range(rb // rps):
                    rows = slice(SUBLANES + s * rps, SUBLANES + (s + 1) * rps)
                    seq = mb * (rb // rps) + s
                    ext = jnp.concatenate([rs.hist_ref[seq, :, cols], stg[rows, :]], axis=0)
                    rs.act_ref[mb * rb + s * rps:mb * rb + (s + 1) * rps, cols] = conv(ext, cols)
                    rs.tail_ref[seq, :, cols] = stg[rows.stop - SUBLANES:rows.stop, :]

        run(ep)


def _proj_kernel(hp_ref, hs_ref, wt_ref, cw_ref, cb_ref, histp_ref, hists_ref,
                 actp_ref, acts_ref, tailp_ref, tails_ref, wbf_scr, halo_scr, stg0_scr, stg1_scr,
                 shift_scr, *, n_p, tm_p, tm_s, spt_s, nsplit, msplit_p, tiles_per_seq):
    j = pl.program_id(0)
    i = pl.program_id(1)
    wrows = 256

    @pl.when(i == 0)
    def _():
        for r in range(PROJ_TN // wrows):
            rows = slice(r * wrows, (r + 1) * wrows)
            wbf_scr[rows, :] = wt_ref[rows, :].astype(BF16)

    shared = (cw_ref, cb_ref, wbf_scr, halo_scr, (stg0_scr, stg1_scr), shift_scr)
    prompt = _Rows(hp_ref, actp_ref, tailp_ref, histp_ref, tm=tm_p, spt=1, msplit=msplit_p,
                   first=(i % tiles_per_seq) == 0)
    sample = _Rows(hs_ref, acts_ref, tails_ref, hists_ref, tm=tm_s, spt=spt_s, msplit=1, first=None)

    @pl.when(i < n_p - 1)
    def _():
        _proj_step(j, [prompt], *shared, nsplit=nsplit)

    @pl.when(i == n_p - 1)
    def _():
        _proj_step(j, [prompt, sample], *shared, nsplit=nsplit)


def _proj(hp, hs, w_t, conv_w, conv_b, hist_p, hist_s, *, tm_p, nsplit, msplit_p):
    rows_p, rows_s = hp.shape[0], hs.shape[0]
    n_p = rows_p // tm_p
    tiles_per_seq = n_p // hist_p.shape[0]
    spt_s = hist_s.shape[0]
    rb = max(tm_p // msplit_p, rows_s)
    kern = functools.partial(_proj_kernel, n_p=n_p, tm_p=tm_p, tm_s=rows_s, spt_s=spt_s,
                             nsplit=nsplit, msplit_p=msplit_p, tiles_per_seq=tiles_per_seq)
    return pl.pallas_call(
        kern,
        grid=(N_PROJ_TILES, n_p),
        in_specs=[
            pl.BlockSpec((tm_p, D_MODEL), lambda j, i: (i, 0)),
            pl.BlockSpec((rows_s, D_MODEL), lambda j, i: (0, 0)),
            pl.BlockSpec((pl.Element(PROJ_TN), pl.Element(D_MODEL)), lambda j, i: (_wt_row(j), 0)),
            pl.BlockSpec((CONV_W, PROJ_TN), lambda j, i: (0, _conv_in_tile(j))),
            pl.BlockSpec((1, PROJ_TN), lambda j, i: (0, _conv_in_tile(j))),
            pl.BlockSpec((1, SUBLANES, PROJ_TN), lambda j, i: (i // tiles_per_seq, 0, _conv_in_tile(j))),
            pl.BlockSpec((spt_s, SUBLANES, PROJ_TN), lambda j, i: (0, 0, _conv_in_tile(j))),
        ],
        out_specs=[
            pl.BlockSpec((tm_p, PROJ_TN), lambda j, i: (i, _act_tile(j))),
            pl.BlockSpec((rows_s, PROJ_TN), lambda j, i: (0, _act_tile(j))),
            pl.BlockSpec((1, SUBLANES, PROJ_TN), lambda j, i: (i, 0, j)),
            pl.BlockSpec((spt_s, SUBLANES, PROJ_TN), lambda j, i: (0, 0, j)),
        ],
        out_shape=[
            jax.ShapeDtypeStruct((rows_p, N_ACT), BF16),
            jax.ShapeDtypeStruct((rows_s, N_ACT), BF16),
            jax.ShapeDtypeStruct((n_p, SUBLANES, N_PROJ_TILES * PROJ_TN), F32),
            jax.ShapeDtypeStruct((spt_s, SUBLANES, N_PROJ_TILES * PROJ_TN), F32),
        ],
        scratch_shapes=[
            pltpu.VMEM((PROJ_TN, D_MODEL), BF16),
            pltpu.VMEM((SUBLANES, PROJ_TN), F32),
            pltpu.VMEM((rb + SUBLANES, PROJ_TN // nsplit), F32),
            pltpu.VMEM((rb + SUBLANES, PROJ_TN // nsplit), F32),
            pltpu.VMEM((rb + 2 * SUBLANES, PROJ_TN // nsplit), F32),
        ],
        compiler_params=pltpu.CompilerParams(
            dimension_semantics=("arbitrary", "arbitrary"), vmem_limit_bytes=VMEM_LIMIT),
        name="proj",
    )(hp, hs, w_t, conv_w, conv_b, hist_p, hist_s)


def _pad_rows(v, rows):
    if v.shape[0] == rows:
        return v
    return jnp.concatenate([v, jnp.zeros((rows - v.shape[0], v.shape[1]), v.dtype)], axis=0)


def _ssd_kernel(szs_ref, xs_ref, b_ref, c_ref, dt_ref, alog_ref, dskip_ref, nrm_ref, e3_ref, h0_ref,
                y_ref, hout_ref, st_scr, *, q_in, blocks_per_step, n_steps):
    ci = pl.program_id(1)

    @pl.when(ci == 0)
    def _():
        st_scr[...] = h0_ref[0]

    for k in range(blocks_per_step):
        rs = pl.ds(k * q_in, q_in)
        _ssd_block(szs_ref.at[rs], xs_ref.at[rs], b_ref.at[rs], c_ref.at[rs], dt_ref.at[rs],
                   alog_ref, dskip_ref, nrm_ref, e3_ref, y_ref.at[rs], st_scr, q_in=q_in)

    @pl.when(ci == n_steps - 1)
    def _():
        hout_ref[0] = st_scr[...]


def _ssd_block(szs_ref, xs_ref, b_ref, c_ref, dt_ref, alog_ref, dskip_ref, nrm_ref, e3_ref,
               y_ref, st_scr, *, q_in):
    q = SSD_BLOCK
    dt = _pad_rows(dt_ref[...], q)
    a = -jnp.exp(alog_ref[...])
    adt = dt * a
    row = lax.broadcasted_iota(jnp.int32, (q, q), 0)
    col = lax.broadcasted_iota(jnp.int32, (q, q), 1)
    causal = row >= col
    acum = jnp.dot(causal.astype(F32), adt, precision=lax.Precision.HIGHEST,
                   preferred_element_type=F32)
    alast = acum[q - 1:q, :]
    dte = jnp.exp(alast - acum)
    ea = jnp.exp(acum)

    stack = jnp.concatenate([dt, dt * dte, ea], axis=0)
    lane3 = lax.broadcasted_iota(jnp.int32, stack.shape, 1)
    stack = jnp.where(lane3 < SSD_HEADS, stack, 0.0)
    hi = stack.astype(BF16).astype(F32)
    r1 = stack - hi
    mid = r1.astype(BF16).astype(F32)
    lo = (r1 - mid).astype(BF16).astype(F32)
    comb = hi + pltpu.roll(mid, SSD_HEADS, axis=1) + pltpu.roll(lo, 2 * SSD_HEADS, axis=1)
    expd = jnp.dot(comb.astype(BF16), e3_ref[...], preferred_element_type=F32)
    dt_e, w_e, ea_e = expd[0:q], expd[q:2 * q], expd[2 * q:3 * q]

    x = _pad_rows(xs_ref[...], q).astype(F32)
    xdt = (x * dt_e).astype(BF16)
    xdtd = (x * w_e).astype(BF16)
    bm = _pad_rows(b_ref[...], q)
    cm = _pad_rows(c_ref[...], q)

    acum_t = acum.T
    dec = jnp.exp(acum_t[0:SSD_HEADS, q - 1:q])
    dec = jnp.broadcast_to(dec, (SSD_HEADS, D_STATE))
    dec = jnp.broadcast_to(dec[:, None, :], (SSD_HEADS, SSD_HEAD_DIM, D_STATE))
    dec = dec.reshape(SSD_HEADS * SSD_HEAD_DIM, D_STATE)

    lane = lax.broadcasted_iota(jnp.int32, (q, LANES), 1)
    nt = (((1,), (1,)), ((), ()))
    tn = (((0,), (0,)), ((), ()))
    y_groups = []
    for g in range(SSD_GROUPS):
        gc = slice(g * GROUP_COLS, (g + 1) * GROUP_COLS)
        bg = bm[:, g * D_STATE:(g + 1) * D_STATE]
        cg = cm[:, g * D_STATE:(g + 1) * D_STATE]
        cb = lax.dot_general(cg, bg, nt, preferred_element_type=F32)
        sg = st_scr[gc, :]
        y_off = lax.dot_general(cg, sg.astype(BF16), nt, preferred_element_type=F32)
        pieces = []
        for pr in range(HEADS_PER_GROUP // 2):
            ms = []
            for hh in range(2):
                h = g * HEADS_PER_GROUP + 2 * pr + hh
                seg = jnp.broadcast_to(acum[:, h:h + 1], (q, q)) - acum_t[h:h + 1, :]
                lm = jnp.exp(jnp.where(causal, seg, -jnp.inf))
                ms.append((cb * lm).astype(BF16))
            m = jnp.concatenate(ms, axis=1)
            c0 = g * GROUP_COLS + pr * LANES
            xp = xdt[:, c0:c0 + LANES]
            rhs = jnp.concatenate([jnp.where(lane < SSD_HEAD_DIM, xp, jnp.zeros_like(xp)),
                                   jnp.where(lane >= SSD_HEAD_DIM, xp, jnp.zeros_like(xp))], axis=0)
            pieces.append(jnp.dot(m, rhs, preferred_element_type=F32))
        y_diag = jnp.concatenate(pieces, axis=1)
        y_groups.append(y_diag + y_off * ea_e[:, gc])
        new = lax.dot_general(xdtd[:, gc], bg, tn, preferred_element_type=F32)
        st_scr[gc, :] = sg * dec[gc, :] + new
    y = jnp.concatenate(y_groups, axis=1)
    y = y + dskip_ref[...] * x
    yz = y * _pad_rows(szs_ref[...], q).astype(F32)
    yn = yz * lax.rsqrt(jnp.mean(yz * yz, axis=-1, keepdims=True) + EPS) * nrm_ref[...]
    y_ref[...] = yn[0:q_in].astype(BF16)


def _ssd_cast_kernel(szs_ref, xs_ref, b_ref, c_ref, dt_ref, alog_ref, dskip_ref, nrm_ref, e3_ref, h0_ref,
                     wa_ref, wb_ref, wc_ref, y_ref, hout_ref, wa_out, wb_out, wc_out, st_scr, **kw):
    for src, dst in ((wa_ref, wa_out), (wb_ref, wb_out), (wc_ref, wc_out)):
        dst[...] = src[...].astype(BF16)
    _ssd_kernel(szs_ref, xs_ref, b_ref, c_ref, dt_ref, alog_ref, dskip_ref, nrm_ref, e3_ref, h0_ref,
                y_ref, hout_ref, st_scr, **kw)


def _ssd(act, dt, a_log, dskip_e, norm_ssd, e3, h0, *, n_seq, q_in, blocks_per_step, cast=()):
    rows = act.shape[0]
    qb = q_in * blocks_per_step
    n_chunks = rows // (n_seq * qb)
    kw = dict(q_in=q_in, blocks_per_step=blocks_per_step, n_steps=n_chunks)
    kern = functools.partial(_ssd_cast_kernel if cast else _ssd_kernel, **kw)
    rblk = lambda b, c: b * n_chunks + c
    bc_blk0 = 7 * D_MODEL // GROUP_COLS
    wrows = D_MODEL // (n_seq * n_chunks)
    w_spec = pl.BlockSpec((wrows, D_MODEL), lambda b, c: (rblk(b, c), 0))
    return pl.pallas_call(
        kern,
        grid=(n_seq, n_chunks),
        in_specs=[
            pl.BlockSpec((qb, D_MODEL), lambda b, c: (rblk(b, c), 0)),
            pl.BlockSpec((qb, D_MODEL), lambda b, c: (rblk(b, c), 1)),
            pl.BlockSpec((qb, GROUP_COLS), lambda b, c: (rblk(b, c), bc_blk0)),
            pl.BlockSpec((qb, GROUP_COLS), lambda b, c: (rblk(b, c), bc_blk0 + 1)),
            pl.BlockSpec((qb, LANES), lambda b, c: (rblk(b, c), 0)),
            pl.BlockSpec((1, LANES), lambda b, c: (0, 0)),
            pl.BlockSpec((1, D_MODEL), lambda b, c: (0, 0)),
            pl.BlockSpec((1, D_MODEL), lambda b, c: (0, 0)),
            pl.BlockSpec((LANES, D_MODEL), lambda b, c: (0, 0)),
            pl.BlockSpec((1, D_MODEL, D_STATE), lambda b, c: (b, 0, 0)),
        ] + [w_spec] * len(cast),
        out_specs=[
            pl.BlockSpec((qb, D_MODEL), lambda b, c: (rblk(b, c), 0)),
            pl.BlockSpec((1, D_MODEL, D_STATE), lambda b, c: (b, 0, 0)),
        ] + [w_spec] * len(cast),
        out_shape=[
            jax.ShapeDtypeStruct((rows, D_MODEL), BF16),
            jax.ShapeDtypeStruct((n_seq, D_MODEL, D_STATE), F32),
        ] + [jax.ShapeDtypeStruct((D_MODEL, D_MODEL), BF16)] * len(cast),
        scratch_shapes=[pltpu.VMEM((D_MODEL, D_STATE), F32)],
        compiler_params=pltpu.CompilerParams(
            dimension_semantics=("arbitrary", "arbitrary"), vmem_limit_bytes=VMEM_LIMIT),
        name="ssd",
    )(act, act, act, act, dt, a_log, dskip_e, norm_ssd, e3, h0, *cast)


def _mlp_stage(u_ref, v_ref, szm_ref, lng_ref, lnb_ref, wsp_ref, bsp_ref, ymlp_ref, vn_ref, *, tm, qm):
    v = v_ref[...].astype(F32)
    mu = jnp.mean(v, axis=-1, keepdims=True)
    vc = v - mu
    var = jnp.mean(vc * vc, axis=-1, keepdims=True)
    vn = vc * lax.rsqrt(var + EPS) * lng_ref[...] + lnb_ref[...]
    if vn_ref is not None:
        vn_ref[...] = vn
    vn_b = vn.astype(BF16)

    row = lax.broadcasted_iota(jnp.int32, (MLP_CHUNK, MLP_CHUNK), 0)
    col = lax.broadcasted_iota(jnp.int32, (MLP_CHUNK, MLP_CHUNK), 1)
    wms = [jnp.where(row >= col, wsp_ref[g], 0.0).astype(BF16) for g in range(MLP_GROUPS)]
    for ck in range(tm // qm):
        rows = slice(ck * qm, (ck + 1) * qm)
        vck = _pad_rows(vn_b[rows, :], MLP_CHUNK)
        mixed = jnp.concatenate(
            [jnp.dot(wms[g], vck[:, g * MLP_GROUP_DIM:(g + 1) * MLP_GROUP_DIM],
                     preferred_element_type=F32) for g in range(MLP_GROUPS)], axis=1)
        mixed = mixed[0:qm] + bsp_ref[...]
        y_mlp = u_ref[rows, :].astype(F32) * mixed * szm_ref[rows, :].astype(F32)
        ymlp_ref[rows, :] = y_mlp.astype(BF16)


def _mix_stage(yssd_ref, ymlp_ref, sgs_ref, sgm_ref, wbs_ref, wbm_ref, wo_ref, ready=lambda w: None):
    ready(0)
    a = jnp.dot(yssd_ref[...], wbs_ref[...], preferred_element_type=F32)
    ready(1)
    b = jnp.dot(ymlp_ref[...], wbm_ref[...], preferred_element_type=F32)
    merged = sgs_ref[...].astype(F32) * a + sgm_ref[...].astype(F32) * b
    ready(2)
    return jnp.dot(merged.astype(BF16), wo_ref[...], preferred_element_type=F32)


def _out_stage(o, x_ref, gate_ref, gfin_ref, y_ref, *, tm, spt):
    rps = tm // spt
    for s in range(spt):
        rows = slice(s * rps, (s + 1) * rps)
        out = x_ref[rows, :] + gate_ref[s] * o[rows, :]
        y = out * lax.rsqrt(jnp.mean(out * out, axis=-1, keepdims=True) + EPS)
        y_ref[rows, :] = y * gfin_ref[...]


def _merge_kernel(yssd_ref, u_ref, v_ref, szm_ref, sgs_ref, sgm_ref, x_ref, gate_ref,
                  lng_ref, lnb_ref, gfin_ref, wsp_ref, bsp_ref, wbs_hbm, wbm_hbm, wo_hbm,
                  y_ref, vn_ref, ymlp_scr, w_scr, w_sem, *, tm, qm, spt):
    w_hbm = (wbs_hbm, wbm_hbm, wo_hbm)

    def w_copy(w):
        return pltpu.make_async_copy(w_hbm[w], w_scr.at[w], w_sem.at[w])

    for w in range(3):
        w_copy(w).start()
    _mlp_stage(u_ref, v_ref, szm_ref, lng_ref, lnb_ref, wsp_ref, bsp_ref, ymlp_scr, vn_ref, tm=tm, qm=qm)
    o = _mix_stage(yssd_ref, ymlp_scr, sgs_ref, sgm_ref, w_scr.at[0], w_scr.at[1], w_scr.at[2],
                   ready=lambda w: w_copy(w).wait())
    _out_stage(o, x_ref, gate_ref, gfin_ref, y_ref, tm=tm, spt=spt)


def _merge(yssd, act, x2d, gate, ln_g, ln_b, g_final, w_spatial, bsp_e, wbs, wbm, wo, *, qm, spt):
    rows = x2d.shape[0]
    tm = rows
    kern = functools.partial(_merge_kernel, tm=tm, qm=qm, spt=spt)
    const = lambda shape: pl.BlockSpec(shape, lambda i: (0,) * len(shape))
    seg = lambda k: pl.BlockSpec((tm, D_MODEL), lambda i: (0, k))
    hbm = pl.BlockSpec(memory_space=pl.ANY)
    return pl.pallas_call(
        kern,
        grid=(1,),
        in_specs=[
            seg(0),
            seg(2), seg(3), seg(4), seg(5), seg(6),
            seg(0),
            const((spt, 1, D_MODEL)),
            const((1, D_MODEL)), const((1, D_MODEL)), const((1, D_MODEL)),
            const((MLP_GROUPS, MLP_CHUNK, MLP_CHUNK)),
            const((qm, D_MODEL)),
            hbm, hbm, hbm,
        ],
        out_specs=[seg(0), seg(0)],
        out_shape=[jax.ShapeDtypeStruct((rows, D_MODEL), F32)] * 2,
        scratch_shapes=[
            pltpu.VMEM((tm, D_MODEL), BF16),
            pltpu.VMEM((3, D_MODEL, D_MODEL), BF16),
            pltpu.SemaphoreType.DMA((3,)),
        ],
        compiler_params=pltpu.CompilerParams(
            dimension_semantics=("arbitrary",), vmem_limit_bytes=VMEM_LIMIT),
        name="merge",
    )(yssd, act, act, act, act, act, x2d, gate, ln_g, ln_b, g_final, w_spatial, bsp_e, wbs, wbm, wo)


def _merge_prompt_kernel(yssd_ref, u_ref, v_ref, szm_ref, sgs_ref, sgm_ref, x_ref, gate_ref,
                         lng_ref, lnb_ref, gfin_ref, wsp_ref, bsp_ref, wbs_ref, wbm_ref, wo_ref,
                         y_ref, ymlp_scr, *, tm, qm):
    _mlp_stage(u_ref, v_ref, szm_ref, lng_ref, lnb_ref, wsp_ref, bsp_ref, ymlp_scr, None, tm=tm, qm=qm)
    o = _mix_stage(yssd_ref, ymlp_scr, sgs_ref, sgm_ref, wbs_ref, wbm_ref, wo_ref)
    _out_stage(o, x_ref, gate_ref, gfin_ref, y_ref, tm=tm, spt=1)


def _merge_prompt(yssd, act, x2d, gate, ln_g, ln_b, g_final, w_spatial, bsp_e, wbs, wbm, wo, *, tm, qm):
    rows = x2d.shape[0]
    n_tiles = rows // tm
    tiles_per_seq = n_tiles // gate.shape[0]
    kern = functools.partial(_merge_prompt_kernel, tm=tm, qm=qm)
    const = lambda shape: pl.BlockSpec(shape, lambda i: (0,) * len(shape),
                                       pipeline_mode=pl.Buffered(1))
    seg = lambda k: pl.BlockSpec((tm, D_MODEL), lambda i: (i, k))
    return pl.pallas_call(
        kern,
        grid=(n_tiles,),
        in_specs=[
            pl.BlockSpec((tm, D_MODEL), lambda i: (i, 0)),
            seg(2), seg(3), seg(4), seg(5), seg(6),
            pl.BlockSpec((tm, D_MODEL), lambda i: (i, 0)),
            pl.BlockSpec((1, 1, D_MODEL), lambda i: (i // tiles_per_seq, 0, 0)),
            const((1, D_MODEL)), const((1, D_MODEL)), const((1, D_MODEL)),
            const((MLP_GROUPS, MLP_CHUNK, MLP_CHUNK)),
            const((qm, D_MODEL)),
            const((D_MODEL, D_MODEL)), const((D_MODEL, D_MODEL)), const((D_MODEL, D_MODEL)),
        ],
        out_specs=pl.BlockSpec((tm, D_MODEL), lambda i: (i, 0)),
        out_shape=jax.ShapeDtypeStruct((rows, D_MODEL), F32),
        scratch_shapes=[pltpu.VMEM((tm, D_MODEL), BF16)],
        compiler_params=pltpu.CompilerParams(
            dimension_semantics=("arbitrary",), vmem_limit_bytes=VMEM_LIMIT),
        name="merge_prompt",
    )(yssd, act, act, act, act, act, x2d, gate, ln_g, ln_b, g_final, w_spatial, bsp_e, wbs, wbm, wo)


def _expansion_matrix():
    e = np.zeros((LANES, D_MODEL), np.float32)
    cols = np.arange(D_MODEL)
    for piece in range(3):
        e[piece * SSD_HEADS + cols // SSD_HEAD_DIM, cols] = 1.0
    return jnp.asarray(e, BF16)


def kernel(x_prompt, x_sample, state_ssm, cache_conv, c_prompt, c_sample, w_ada, b_ada, g_pre, w_in,
           conv_w, conv_b, dt_bias, a_log, d_skip, norm_ssd, ln_g, ln_b, w_spatial, b_spatial,
           w_bp_ssd, w_bp_mlp, w_o, g_final):
    depth = w_ada.shape[0]
    assert depth == 1, "single-layer trunk only"
    bp, seq, d = x_prompt.shape
    bs, dec_seq, _ = x_sample.shape
    assert d == D_MODEL and seq % 1024 == 0 and dec_seq % 16 == 0 and dec_seq <= SSD_BLOCK

    w_t = jnp.swapaxes(w_in[0], 0, 1)
    w_dt = jnp.pad(w_t[DT_OFFSET:DT_OFFSET + SSD_HEADS], ((0, LANES - SSD_HEADS), (0, 0))).astype(BF16)
    pad_heads = lambda v: jnp.pad(v.reshape(1, SSD_HEADS), ((0, 0), (0, LANES - SSD_HEADS)))
    wts = dict(
        g_pre=g_pre[0].reshape(1, d), w_t=w_t, w_dt=w_dt, dt_bias=pad_heads(dt_bias[0]),
        conv_w=conv_w[0], conv_b=conv_b[0].reshape(1, CONV_DIM),
        a_log=pad_heads(a_log[0]), dskip_e=jnp.repeat(d_skip[0], SSD_HEAD_DIM).reshape(1, d),
        norm_ssd=norm_ssd[0].reshape(1, d), e3=_expansion_matrix(),
        ln_g=ln_g[0].reshape(1, d), ln_b=ln_b[0].reshape(1, d), g_final=g_final.reshape(1, d),
        w_spatial=w_spatial[0], bsp_e=jnp.repeat(b_spatial[0].T, MLP_GROUP_DIM, axis=1),
    )

    c_all = jnp.concatenate([c_prompt, c_sample], axis=0)
    c_pad = jnp.pad(c_all, ((0, 16 - (bp + bs)), (0, 0)))
    mods = _mods(c_pad, w_ada[0], b_ada[0].reshape(1, 3 * d))

    split_mods = lambda m: (m[:, k * d:(k + 1) * d].reshape(m.shape[0], 1, d) for k in range(3))
    shift_p, scale_p, gate_p = split_mods(mods[:bp])
    shift_s, scale_s, gate_s = split_mods(mods[bp:bp + bs])
    x2p = x_prompt.reshape(bp * seq, d)
    x2s = x_sample.reshape(bs * dec_seq, d)
    rows_s = bs * dec_seq

    hist_p = jnp.zeros((bp, SUBLANES, CONV_DIM), F32)
    h0_p = jnp.zeros((bp, d, D_STATE), F32)
    hist_s = jnp.pad(cache_conv[0], ((0, 0), (SUBLANES - (CONV_W - 1), 0), (0, 0)))
    h0_s = state_ssm[0].reshape(bs, d, D_STATE)

    norm_w = (wts["g_pre"], wts["w_dt"], wts["dt_bias"])
    hp, dt_p = _norm(x2p, scale_p, shift_p, *norm_w, tm=NORM_TM, spt=1)
    hs, dt_s = _norm(x2s, scale_s, shift_s, *norm_w, tm=rows_s, spt=bs)
    act_p, act_s, tail_p, tail_s = _proj(hp, hs, wts["w_t"], wts["conv_w"], wts["conv_b"], hist_p, hist_s,
                                         tm_p=PROJ_TM, nsplit=PROJ_NSPLIT, msplit_p=PROJ_MSPLIT)

    ssd_w = (wts["a_log"], wts["dskip_e"], wts["norm_ssd"], wts["e3"])
    yssd_p, ssm_p, *proj_w = _ssd(act_p, dt_p, *ssd_w, h0_p, n_seq=bp, q_in=SSD_BLOCK,
                                  blocks_per_step=SSD_BLOCKS_PER_STEP,
                                  cast=(w_bp_ssd[0], w_bp_mlp[0], w_o[0]))
    yssd_s, ssm_s = _ssd(act_s, dt_s, *ssd_w, h0_s, n_seq=bs, q_in=dec_seq, blocks_per_step=1)

    merge_w = (wts["ln_g"], wts["ln_b"], wts["g_final"], wts["w_spatial"])
    yp = _merge_prompt(yssd_p, act_p, x2p, gate_p, *merge_w, wts["bsp_e"], *proj_w,
                       tm=MERGE_TM, qm=MLP_CHUNK)
    ys, vn_s = _merge(yssd_s, act_s, x2s, gate_s, *merge_w, wts["bsp_e"][:dec_seq], *proj_w,
                      qm=dec_seq, spt=bs)

    conv_cols = slice(CONV_STEPS[0] * PROJ_TN, CONV_STEPS[0] * PROJ_TN + CONV_DIM)
    hist_rows = slice(SUBLANES - (CONV_W - 1), SUBLANES)
    tiles_per_seq = seq // PROJ_TM
    conv_p = tail_p[tiles_per_seq - 1::tiles_per_seq, hist_rows, conv_cols]
    conv_s = tail_s[:, hist_rows, conv_cols]

    state_shape = (SSD_HEADS, SSD_HEAD_DIM, D_STATE)
    return (yp.reshape(bp, seq, d), ys.reshape(bs, dec_seq, d),
            ssm_p.reshape(1, bp, *state_shape), conv_p[None],
            ssm_s.reshape(1, bs, *state_shape), conv_s[None], vn_s.reshape(1, bs, dec_seq, d))
```

```python
import functools
from typing import Any, NamedTuple

import numpy as np
import jax
import jax.numpy as jnp
from jax import lax
from jax.experimental import pallas as pl
from jax.experimental.pallas import tpu as pltpu

F32 = jnp.float32
BF16 = jnp.bfloat16

D_MODEL = 2048
SSD_HEADS = 32
SSD_HEAD_DIM = 64
SSD_GROUPS = 4
HEADS_PER_GROUP = SSD_HEADS // SSD_GROUPS
D_STATE = 128
GROUP_COLS = HEADS_PER_GROUP * SSD_HEAD_DIM
CONV_W = 4
CONV_DIM = D_MODEL + 2 * SSD_GROUPS * D_STATE
MLP_CHUNK = 128
MLP_GROUPS = 8
MLP_GROUP_DIM = D_MODEL // MLP_GROUPS
EPS = 1e-5

SUBLANES = 8
LANES = 128
SSD_BLOCK = 128
SSD_BLOCKS_PER_STEP = 4
PROJ_TN = 1024
VMEM_LIMIT = 56 * 1024 * 1024

N_ACT = 7 * D_MODEL + 2 * SSD_GROUPS * D_STATE
N_PROJ_TILES = N_ACT // PROJ_TN
DT_OFFSET = D_MODEL + CONV_DIM
FIRST_SHIFTED_TILE = DT_OFFSET // PROJ_TN
CONV_STEPS = (2, 3, 4)
IDENT_STEPS = (5, 6, 7, 8)
FIRST_SIGMOID_STEP = 11
PROJ_TM = 1024
PROJ_NSPLIT = 2
PROJ_MSPLIT = 4
NORM_TM = 512
MERGE_TM = 256


def _silu(x):
    hx = 0.5 * x
    return hx + hx * jnp.tanh(hx)


def _softplus(x):
    return jnp.maximum(x, 0.0) + jnp.log1p(jnp.exp(-jnp.abs(x)))


def _mods_kernel(c_ref, w_ref, b_ref, o_ref):
    s = _silu(c_ref[...]).astype(BF16)
    o_ref[...] = jnp.dot(s, w_ref[...].astype(BF16), preferred_element_type=F32) + b_ref[...]


def _mods(c_pad, w_ada, b_ada):
    rows, d = c_pad.shape
    n = w_ada.shape[1]
    tn = 1024
    return pl.pallas_call(
        _mods_kernel,
        grid=(n // tn,),
        in_specs=[
            pl.BlockSpec((rows, d), lambda j: (0, 0)),
            pl.BlockSpec((d, tn), lambda j: (0, j)),
            pl.BlockSpec((1, tn), lambda j: (0, j)),
        ],
        out_specs=pl.BlockSpec((rows, tn), lambda j: (0, j)),
        out_shape=jax.ShapeDtypeStruct((rows, n), F32),
        compiler_params=pltpu.CompilerParams(
            dimension_semantics=("arbitrary",), vmem_limit_bytes=VMEM_LIMIT),
        name="mods",
    )(c_pad, w_ada, b_ada)


def _norm_kernel(x_ref, sc_ref, sh_ref, gpre_ref, wdt_ref, dtb_ref, h_ref, dt_ref, *, tm, spt):
    rps = tm // spt
    for s in range(spt):
        rows = slice(s * rps, (s + 1) * rps)
        x = x_ref[rows, :]
        y = x * lax.rsqrt(jnp.mean(x * x, axis=-1, keepdims=True) + EPS)
        y = y * gpre_ref[...]
        h_ref[rows, :] = (y * (1.0 + sc_ref[s]) + sh_ref[s]).astype(BF16)
    dt_raw = lax.dot_general(h_ref[...], wdt_ref[...], (((1,), (1,)), ((), ())),
                             preferred_element_type=F32) + dtb_ref[...]
    lane = lax.broadcasted_iota(jnp.int32, dt_raw.shape, 1)
    dt_ref[...] = jnp.where(lane < SSD_HEADS, _softplus(dt_raw), 0.0)


def _norm(x2d, scale, shift, g_pre, w_dt, dt_bias, *, tm, spt):
    rows = x2d.shape[0]
    n_seq = scale.shape[0]
    n_tiles = rows // tm
    tiles_per_seq = max(1, n_tiles // n_seq) if spt == 1 else 1
    seq_blk = (lambda i: i // tiles_per_seq) if spt == 1 else (lambda i: i)
    return pl.pallas_call(
        functools.partial(_norm_kernel, tm=tm, spt=spt),
        grid=(n_tiles,),
        in_specs=[
            pl.BlockSpec((tm, D_MODEL), lambda i: (i, 0)),
            pl.BlockSpec((spt, 1, D_MODEL), lambda i: (seq_blk(i), 0, 0)),
            pl.BlockSpec((spt, 1, D_MODEL), lambda i: (seq_blk(i), 0, 0)),
            pl.BlockSpec((1, D_MODEL), lambda i: (0, 0)),
            pl.BlockSpec((LANES, D_MODEL), lambda i: (0, 0)),
            pl.BlockSpec((1, LANES), lambda i: (0, 0)),
        ],
        out_specs=[
            pl.BlockSpec((tm, D_MODEL), lambda i: (i, 0)),
            pl.BlockSpec((tm, LANES), lambda i: (i, 0)),
        ],
        out_shape=[
            jax.ShapeDtypeStruct((rows, D_MODEL), BF16),
            jax.ShapeDtypeStruct((rows, LANES), F32),
        ],
        compiler_params=pltpu.CompilerParams(
            dimension_semantics=("arbitrary",), vmem_limit_bytes=VMEM_LIMIT),
        name="norm",
    )(x2d, scale, shift, g_pre, w_dt, dt_bias)


def _act_tile(j):
    return jnp.where(j < CONV_STEPS[2], j, jnp.where(j == CONV_STEPS[2], N_PROJ_TILES - 1, j - 1))


def _wt_row(j):
    return pl.multiple_of(j * PROJ_TN + jnp.where(j >= FIRST_SHIFTED_TILE, SSD_HEADS, 0), SSD_HEADS)


def _conv_in_tile(j):
    return jnp.clip(j - CONV_STEPS[0], 0, len(CONV_STEPS) - 1)


class _Rows(NamedTuple):
    h_ref: Any
    act_ref: Any
    tail_ref: Any
    hist_ref: Any
    tm: int
    spt: int
    msplit: int
    first: Any


def _proj_step(j, row_sets, cw_ref, cb_ref, wbf_scr, halo_scr, stgs, shift_scr, *, nsplit):
    cw = PROJ_TN // nsplit
    blocks = [(rs, nb, mb) for nb in range(nsplit) for rs in row_sets for mb in range(rs.msplit)]

    def dot_block(k):
        rs, nb, mb = blocks[k]
        rb = rs.tm // rs.msplit
        stgs[k % 2][SUBLANES:SUBLANES + rb, :] = lax.dot_general(
            rs.h_ref[mb * rb:(mb + 1) * rb, :], wbf_scr[nb * cw:(nb + 1) * cw, :],
            (((1,), (1,)), ((), ())), preferred_element_type=F32)

    def run(epilogue):
        dot_block(0)
        for k, (rs, nb, mb) in enumerate(blocks):
            if k + 1 < len(blocks):
                dot_block(k + 1)
            epilogue(rs, nb, mb, stgs[k % 2])

    is_conv = jnp.logical_and(j >= CONV_STEPS[0], j <= CONV_STEPS[2])
    is_ident = jnp.logical_and(j >= IDENT_STEPS[0], j <= IDENT_STEPS[-1])
    is_sigmoid = j >= FIRST_SIGMOID_STEP

    p_x = jnp.where(is_ident, 1.0, 0.5).astype(F32)
    p_xt = jnp.where(is_ident, 0.0, 0.5).astype(F32)

    def elementwise(rs, nb, mb, stg):
        rb = rs.tm // rs.msplit
        x = stg[SUBLANES:SUBLANES + rb, :]
        t = jnp.tanh(0.5 * x)
        y = jnp.where(is_sigmoid, 0.5 + 0.5 * t, x * (p_x + p_xt * t))
        rs.act_ref[mb * rb:(mb + 1) * rb, nb * cw:(nb + 1) * cw] = y.astype(BF16)

    @pl.when(jnp.logical_not(is_conv))
    def _():
        for rs in row_sets:
            rs.tail_ref[...] = jnp.zeros_like(rs.tail_ref)
        run(elementwise)

    @pl.when(is_conv)
    def _():
        for rs in row_sets:
            if rs.spt == 1:
                halo_scr[...] = jnp.where(rs.first, rs.hist_ref[0], halo_scr[...])

        shift_scr[0:SUBLANES, :] = jnp.zeros((SUBLANES, cw), F32)

        def shift_down(v):
            n = v.shape[0]
            shift_scr[1:1 + n, :] = v
            return shift_scr[0:n, :]

        def conv(load_ext, cols):
            w0, w1, w2, w3 = (0.5 * cw_ref[k:k + 1, cols] for k in range(CONV_W))
            t = load_ext() * w0
            t = shift_down(t) + load_ext() * w1
            t = shift_down(t) + load_ext() * w2
            t = shift_down(t) + load_ext() * w3 + 0.5 * cb_ref[:, cols]
            t = t[SUBLANES:, :]
            return (t + t * jnp.tanh(t)).astype(BF16)

        def ep(rs, nb, mb, stg):
            cols = slice(nb * cw, (nb + 1) * cw)
            rb = rs.tm // rs.msplit
            if rs.spt == 1:
                stg[0:SUBLANES, :] = halo_scr[:, cols]
                rs.act_ref[mb * rb:(mb + 1) * rb, cols] = conv(lambda: stg[0:SUBLANES + rb, :], cols)
                last = stg[rb:rb + SUBLANES, :]
                halo_scr[:, cols] = last
                rs.tail_ref[0, :, cols] = last
            else:
                rps = rs.tm // rs.spt
                for s in range(rb // rps):
                    rows = slice(SUBLANES + s * rps, SUBLANES + (s + 1) * rps)
                    seq = mb * (rb // rps) + s
                    ext = jnp.concatenate([rs.hist_ref[seq, :, cols], stg[rows, :]], axis=0)
                    rs.act_ref[mb * rb + s * rps:mb * rb + (s + 1) * rps, cols] = conv(lambda: ext, cols)
                    rs.tail_ref[seq, :, cols] = stg[rows.stop - SUBLANES:rows.stop, :]

        run(ep)


def _proj_kernel(hp_ref, hs_ref, wt_ref, cw_ref, cb_ref, histp_ref, hists_ref,
                 actp_ref, acts_ref, tailp_ref, tails_ref, wbf_scr, halo_scr, stg0_scr, stg1_scr,
                 shift_scr, *, n_p, tm_p, tm_s, spt_s, nsplit, msplit_p, tiles_per_seq):
    j = pl.program_id(0)
    i = pl.program_id(1)
    wrows = 256

    @pl.when(i == 0)
    def _():
        for r in range(PROJ_TN // wrows):
            rows = slice(r * wrows, (r + 1) * wrows)
            wbf_scr[rows, :] = wt_ref[rows, :].astype(BF16)

    shared = (cw_ref, cb_ref, wbf_scr, halo_scr, (stg0_scr, stg1_scr), shift_scr)
    prompt = _Rows(hp_ref, actp_ref, tailp_ref, histp_ref, tm=tm_p, spt=1, msplit=msplit_p,
                   first=(i % tiles_per_seq) == 0)
    sample = _Rows(hs_ref, acts_ref, tails_ref, hists_ref, tm=tm_s, spt=spt_s, msplit=1, first=None)

    @pl.when(i < n_p - 1)
    def _():
        _proj_step(j, [prompt], *shared, nsplit=nsplit)

    @pl.when(i == n_p - 1)
    def _():
        _proj_step(j, [prompt, sample], *shared, nsplit=nsplit)


def _proj(hp, hs, w_t, conv_w, conv_b, hist_p, hist_s, *, tm_p, nsplit, msplit_p):
    rows_p, rows_s = hp.shape[0], hs.shape[0]
    n_p = rows_p // tm_p
    tiles_per_seq = n_p // hist_p.shape[0]
    spt_s = hist_s.shape[0]
    rb = max(tm_p // msplit_p, rows_s)
    kern = functools.partial(_proj_kernel, n_p=n_p, tm_p=tm_p, tm_s=rows_s, spt_s=spt_s,
                             nsplit=nsplit, msplit_p=msplit_p, tiles_per_seq=tiles_per_seq)
    return pl.pallas_call(
        kern,
        grid=(N_PROJ_TILES, n_p),
        in_specs=[
            pl.BlockSpec((tm_p, D_MODEL), lambda j, i: (i, 0)),
            pl.BlockSpec((rows_s, D_MODEL), lambda j, i: (0, 0)),
            pl.BlockSpec((pl.Element(PROJ_TN), pl.Element(D_MODEL)), lambda j, i: (_wt_row(j), 0)),
            pl.BlockSpec((CONV_W, PROJ_TN), lambda j, i: (0, _conv_in_tile(j))),
            pl.BlockSpec((1, PROJ_TN), lambda j, i: (0, _conv_in_tile(j))),
            pl.BlockSpec((1, SUBLANES, PROJ_TN), lambda j, i: (i // tiles_per_seq, 0, _conv_in_tile(j))),
            pl.BlockSpec((spt_s, SUBLANES, PROJ_TN), lambda j, i: (0, 0, _conv_in_tile(j))),
        ],
        out_specs=[
            pl.BlockSpec((tm_p, PROJ_TN), lambda j, i: (i, _act_tile(j))),
            pl.BlockSpec((rows_s, PROJ_TN), lambda j, i: (0, _act_tile(j))),
            pl.BlockSpec((1, SUBLANES, PROJ_TN), lambda j, i: (i, 0, j)),
            pl.BlockSpec((spt_s, SUBLANES, PROJ_TN), lambda j, i: (0, 0, j)),
        ],
        out_shape=[
            jax.ShapeDtypeStruct((rows_p, N_ACT), BF16),
            jax.ShapeDtypeStruct((rows_s, N_ACT), BF16),
            jax.ShapeDtypeStruct((n_p, SUBLANES, N_PROJ_TILES * PROJ_TN), F32),
            jax.ShapeDtypeStruct((spt_s, SUBLANES, N_PROJ_TILES * PROJ_TN), F32),
        ],
        scratch_shapes=[
            pltpu.VMEM((PROJ_TN, D_MODEL), BF16),
            pltpu.VMEM((SUBLANES, PROJ_TN), F32),
            pltpu.VMEM((rb + SUBLANES, PROJ_TN // nsplit), F32),
            pltpu.VMEM((rb + SUBLANES, PROJ_TN // nsplit), F32),
            pltpu.VMEM((rb + 2 * SUBLANES, PROJ_TN // nsplit), F32),
        ],
        compiler_params=pltpu.CompilerParams(
            dimension_semantics=("arbitrary", "arbitrary"), vmem_limit_bytes=VMEM_LIMIT),
        name="proj",
    )(hp, hs, w_t, conv_w, conv_b, hist_p, hist_s)


def _pad_rows(v, rows):
    if v.shape[0] == rows:
        return v
    return jnp.concatenate([v, jnp.zeros((rows - v.shape[0], v.shape[1]), v.dtype)], axis=0)


def _ssd_kernel(szs_ref, xs_ref, b_ref, c_ref, dt_ref, alog_ref, dskip_ref, nrm_ref, e3_ref, h0_ref,
                y_ref, hout_ref, st_scr, *, q_in, blocks_per_step, n_steps):
    ci = pl.program_id(1)

    @pl.when(ci == 0)
    def _():
        st_scr[...] = h0_ref[0]

    for k in range(blocks_per_step):
        rs = pl.ds(k * q_in, q_in)
        _ssd_block(szs_ref.at[rs], xs_ref.at[rs], b_ref.at[rs], c_ref.at[rs], dt_ref.at[rs],
                   alog_ref, dskip_ref, nrm_ref, e3_ref, y_ref.at[rs], st_scr, q_in=q_in)

    @pl.when(ci == n_steps - 1)
    def _():
        hout_ref[0] = st_scr[...]


def _ssd_block(szs_ref, xs_ref, b_ref, c_ref, dt_ref, alog_ref, dskip_ref, nrm_ref, e3_ref,
               y_ref, st_scr, *, q_in):
    q = SSD_BLOCK
    dt = _pad_rows(dt_ref[...], q)
    a = -jnp.exp(alog_ref[...])
    adt = dt * a
    row = lax.broadcasted_iota(jnp.int32, (q, q), 0)
    col = lax.broadcasted_iota(jnp.int32, (q, q), 1)
    causal = row >= col
    acum = jnp.dot(causal.astype(F32), adt, precision=lax.Precision.HIGHEST,
                   preferred_element_type=F32)
    alast = acum[q - 1:q, :]
    dte = jnp.exp(alast - acum)
    ea = jnp.exp(acum)

    stack = jnp.concatenate([dt, dt * dte, ea], axis=0)
    lane3 = lax.broadcasted_iota(jnp.int32, stack.shape, 1)
    stack = jnp.where(lane3 < SSD_HEADS, stack, 0.0)
    hi = stack.astype(BF16).astype(F32)
    r1 = stack - hi
    mid = r1.astype(BF16).astype(F32)
    lo = (r1 - mid).astype(BF16).astype(F32)
    comb = hi + pltpu.roll(mid, SSD_HEADS, axis=1) + pltpu.roll(lo, 2 * SSD_HEADS, axis=1)
    expd = jnp.dot(comb.astype(BF16), e3_ref[...], preferred_element_type=F32)
    dt_e, w_e, ea_e = expd[0:q], expd[q:2 * q], expd[2 * q:3 * q]

    x = _pad_rows(xs_ref[...], q).astype(F32)
    xdt = (x * dt_e).astype(BF16)
    xdtd = (x * w_e).astype(BF16)
    bm = _pad_rows(b_ref[...], q)
    cm = _pad_rows(c_ref[...], q)

    acum_t = acum.T
    dec = jnp.exp(acum_t[0:SSD_HEADS, q - 1:q])
    dec = jnp.broadcast_to(dec, (SSD_HEADS, D_STATE))
    dec = jnp.broadcast_to(dec[:, None, :], (SSD_HEADS, SSD_HEAD_DIM, D_STATE))
    dec = dec.reshape(SSD_HEADS * SSD_HEAD_DIM, D_STATE)

    lane = lax.broadcasted_iota(jnp.int32, (q, LANES), 1)
    nt = (((1,), (1,)), ((), ()))
    tn = (((0,), (0,)), ((), ()))
    y_groups = []
    for g in range(SSD_GROUPS):
        gc = slice(g * GROUP_COLS, (g + 1) * GROUP_COLS)
        bg = bm[:, g * D_STATE:(g + 1) * D_STATE]
        cg = cm[:, g * D_STATE:(g + 1) * D_STATE]
        cb = lax.dot_general(cg, bg, nt, preferred_element_type=F32)
        sg = st_scr[gc, :]
        y_off = lax.dot_general(cg, sg.astype(BF16), nt, preferred_element_type=F32)
        pieces = []
        for pr in range(HEADS_PER_GROUP // 2):
            ms = []
            for hh in range(2):
                h = g * HEADS_PER_GROUP + 2 * pr + hh
                seg = jnp.broadcast_to(acum[:, h:h + 1], (q, q)) - acum_t[h:h + 1, :]
                lm = jnp.exp(jnp.where(causal, seg, -jnp.inf))
                ms.append((cb * lm).astype(BF16))
            m = jnp.concatenate(ms, axis=1)
            c0 = g * GROUP_COLS + pr * LANES
            xp = xdt[:, c0:c0 + LANES]
            rhs = jnp.concatenate([jnp.where(lane < SSD_HEAD_DIM, xp, jnp.zeros_like(xp)),
                                   jnp.where(lane >= SSD_HEAD_DIM, xp, jnp.zeros_like(xp))], axis=0)
            pieces.append(jnp.dot(m, rhs, preferred_element_type=F32))
        y_diag = jnp.concatenate(pieces, axis=1)
        y_groups.append(y_diag + y_off * ea_e[:, gc])
        new = lax.dot_general(xdtd[:, gc], bg, tn, preferred_element_type=F32)
        st_scr[gc, :] = sg * dec[gc, :] + new
    y = jnp.concatenate(y_groups, axis=1)
    y = y + dskip_ref[...] * x
    yz = y * _pad_rows(szs_ref[...], q).astype(F32)
    yn = yz * lax.rsqrt(jnp.mean(yz * yz, axis=-1, keepdims=True) + EPS) * nrm_ref[...]
    y_ref[...] = yn[0:q_in].astype(BF16)


def _ssd_cast_kernel(szs_ref, xs_ref, b_ref, c_ref, dt_ref, alog_ref, dskip_ref, nrm_ref, e3_ref, h0_ref,
                     wa_ref, wb_ref, wc_ref, y_ref, hout_ref, wa_out, wb_out, wc_out, st_scr, **kw):
    for src, dst in ((wa_ref, wa_out), (wb_ref, wb_out), (wc_ref, wc_out)):
        dst[...] = src[...].astype(BF16)
    _ssd_kernel(szs_ref, xs_ref, b_ref, c_ref, dt_ref, alog_ref, dskip_ref, nrm_ref, e3_ref, h0_ref,
                y_ref, hout_ref, st_scr, **kw)


def _ssd(act, dt, a_log, dskip_e, norm_ssd, e3, h0, *, n_seq, q_in, blocks_per_step, cast=()):
    rows = act.shape[0]
    qb = q_in * blocks_per_step
    n_chunks = rows // (n_seq * qb)
    kw = dict(q_in=q_in, blocks_per_step=blocks_per_step, n_steps=n_chunks)
    kern = functools.partial(_ssd_cast_kernel if cast else _ssd_kernel, **kw)
    rblk = lambda b, c: b * n_chunks + c
    bc_blk0 = 7 * D_MODEL // GROUP_COLS
    wrows = D_MODEL // (n_seq * n_chunks)
    w_spec = pl.BlockSpec((wrows, D_MODEL), lambda b, c: (rblk(b, c), 0))
    return pl.pallas_call(
        kern,
        grid=(n_seq, n_chunks),
        in_specs=[
            pl.BlockSpec((qb, D_MODEL), lambda b, c: (rblk(b, c), 0)),
            pl.BlockSpec((qb, D_MODEL), lambda b, c: (rblk(b, c), 1)),
            pl.BlockSpec((qb, GROUP_COLS), lambda b, c: (rblk(b, c), bc_blk0)),
            pl.BlockSpec((qb, GROUP_COLS), lambda b, c: (rblk(b, c), bc_blk0 + 1)),
            pl.BlockSpec((qb, LANES), lambda b, c: (rblk(b, c), 0)),
            pl.BlockSpec((1, LANES), lambda b, c: (0, 0)),
            pl.BlockSpec((1, D_MODEL), lambda b, c: (0, 0)),
            pl.BlockSpec((1, D_MODEL), lambda b, c: (0, 0)),
            pl.BlockSpec((LANES, D_MODEL), lambda b, c: (0, 0)),
            pl.BlockSpec((1, D_MODEL, D_STATE), lambda b, c: (b, 0, 0)),
        ] + [w_spec] * len(cast),
        out_specs=[
            pl.BlockSpec((qb, D_MODEL), lambda b, c: (rblk(b, c), 0)),
            pl.BlockSpec((1, D_MODEL, D_STATE), lambda b, c: (b, 0, 0)),
        ] + [w_spec] * len(cast),
        out_shape=[
            jax.ShapeDtypeStruct((rows, D_MODEL), BF16),
            jax.ShapeDtypeStruct((n_seq, D_MODEL, D_STATE), F32),
        ] + [jax.ShapeDtypeStruct((D_MODEL, D_MODEL), BF16)] * len(cast),
        scratch_shapes=[pltpu.VMEM((D_MODEL, D_STATE), F32)],
        compiler_params=pltpu.CompilerParams(
            dimension_semantics=("arbitrary", "arbitrary"), vmem_limit_bytes=VMEM_LIMIT),
        name="ssd",
    )(act, act, act, act, dt, a_log, dskip_e, norm_ssd, e3, h0, *cast)


def _mlp_stage(u_ref, v_ref, szm_ref, lng_ref, lnb_ref, wsp_ref, bsp_ref, ymlp_ref, vn_ref, *, tm, qm):
    v = v_ref[...].astype(F32)
    mu = jnp.mean(v, axis=-1, keepdims=True)
    vc = v - mu
    var = jnp.mean(vc * vc, axis=-1, keepdims=True)
    vn = vc * lax.rsqrt(var + EPS) * lng_ref[...] + lnb_ref[...]
    if vn_ref is not None:
        vn_ref[...] = vn
    vn_b = vn.astype(BF16)

    row = lax.broadcasted_iota(jnp.int32, (MLP_CHUNK, MLP_CHUNK), 0)
    col = lax.broadcasted_iota(jnp.int32, (MLP_CHUNK, MLP_CHUNK), 1)
    wms = [jnp.where(row >= col, wsp_ref[g], 0.0).astype(BF16) for g in range(MLP_GROUPS)]
    for ck in range(tm // qm):
        rows = slice(ck * qm, (ck + 1) * qm)
        vck = _pad_rows(vn_b[rows, :], MLP_CHUNK)
        mixed = jnp.concatenate(
            [jnp.dot(wms[g], vck[:, g * MLP_GROUP_DIM:(g + 1) * MLP_GROUP_DIM],
                     preferred_element_type=F32) for g in range(MLP_GROUPS)], axis=1)
        mixed = mixed[0:qm] + bsp_ref[...]
        y_mlp = u_ref[rows, :].astype(F32) * mixed * szm_ref[rows, :].astype(F32)
        ymlp_ref[rows, :] = y_mlp.astype(BF16)


def _mix_stage(yssd_ref, ymlp_ref, sgs_ref, sgm_ref, wbs_ref, wbm_ref, wo_ref, ready=lambda w: None):
    ready(0)
    a = jnp.dot(yssd_ref[...], wbs_ref[...], preferred_element_type=F32)
    ready(1)
    b = jnp.dot(ymlp_ref[...], wbm_ref[...], preferred_element_type=F32)
    merged = sgs_ref[...].astype(F32) * a + sgm_ref[...].astype(F32) * b
    ready(2)
    return jnp.dot(merged.astype(BF16), wo_ref[...], preferred_element_type=F32)


def _out_stage(o, x_ref, gate_ref, gfin_ref, y_ref, *, tm, spt):
    rps = tm // spt
    for s in range(spt):
        rows = slice(s * rps, (s + 1) * rps)
        out = x_ref[rows, :] + gate_ref[s] * o[rows, :]
        y = out * lax.rsqrt(jnp.mean(out * out, axis=-1, keepdims=True) + EPS)
        y_ref[rows, :] = y * gfin_ref[...]


def _merge_kernel(yssd_ref, u_ref, v_ref, szm_ref, sgs_ref, sgm_ref, x_ref, gate_ref,
                  lng_ref, lnb_ref, gfin_ref, wsp_ref, bsp_ref, wbs_hbm, wbm_hbm, wo_hbm,
                  y_ref, vn_ref, ymlp_scr, w_scr, w_sem, *, tm, qm, spt):
    w_hbm = (wbs_hbm, wbm_hbm, wo_hbm)

    def w_copy(w):
        return pltpu.make_async_copy(w_hbm[w], w_scr.at[w], w_sem.at[w])

    for w in range(3):
        w_copy(w).start()
    _mlp_stage(u_ref, v_ref, szm_ref, lng_ref, lnb_ref, wsp_ref, bsp_ref, ymlp_scr, vn_ref, tm=tm, qm=qm)
    o = _mix_stage(yssd_ref, ymlp_scr, sgs_ref, sgm_ref, w_scr.at[0], w_scr.at[1], w_scr.at[2],
                   ready=lambda w: w_copy(w).wait())
    _out_stage(o, x_ref, gate_ref, gfin_ref, y_ref, tm=tm, spt=spt)


def _merge(yssd, act, x2d, gate, ln_g, ln_b, g_final, w_spatial, bsp_e, wbs, wbm, wo, *, qm, spt):
    rows = x2d.shape[0]
    tm = rows
    kern = functools.partial(_merge_kernel, tm=tm, qm=qm, spt=spt)
    const = lambda shape: pl.BlockSpec(shape, lambda i: (0,) * len(shape))
    seg = lambda k: pl.BlockSpec((tm, D_MODEL), lambda i: (0, k))
    hbm = pl.BlockSpec(memory_space=pl.ANY)
    return pl.pallas_call(
        kern,
        grid=(1,),
        in_specs=[
            seg(0),
            seg(2), seg(3), seg(4), seg(5), seg(6),
            seg(0),
            const((spt, 1, D_MODEL)),
            const((1, D_MODEL)), const((1, D_MODEL)), const((1, D_MODEL)),
            const((MLP_GROUPS, MLP_CHUNK, MLP_CHUNK)),
            const((qm, D_MODEL)),
            hbm, hbm, hbm,
        ],
        out_specs=[seg(0), seg(0)],
        out_shape=[jax.ShapeDtypeStruct((rows, D_MODEL), F32)] * 2,
        scratch_shapes=[
            pltpu.VMEM((tm, D_MODEL), BF16),
            pltpu.VMEM((3, D_MODEL, D_MODEL), BF16),
            pltpu.SemaphoreType.DMA((3,)),
        ],
        compiler_params=pltpu.CompilerParams(
            dimension_semantics=("arbitrary",), vmem_limit_bytes=VMEM_LIMIT),
        name="merge",
    )(yssd, act, act, act, act, act, x2d, gate, ln_g, ln_b, g_final, w_spatial, bsp_e, wbs, wbm, wo)


def _merge_prompt_kernel(yssd_ref, u_ref, v_ref, szm_ref, sgs_ref, sgm_ref, x_ref, gate_ref,
                         lng_ref, lnb_ref, gfin_ref, wsp_ref, bsp_ref, wbs_ref, wbm_ref, wo_ref,
                         y_ref, ymlp_scr, *, tm, qm):
    _mlp_stage(u_ref, v_ref, szm_ref, lng_ref, lnb_ref, wsp_ref, bsp_ref, ymlp_scr, None, tm=tm, qm=qm)
    o = _mix_stage(yssd_ref, ymlp_scr, sgs_ref, sgm_ref, wbs_ref, wbm_ref, wo_ref)
    _out_stage(o, x_ref, gate_ref, gfin_ref, y_ref, tm=tm, spt=1)


def _merge_prompt(yssd, act, x2d, gate, ln_g, ln_b, g_final, w_spatial, bsp_e, wbs, wbm, wo, *, tm, qm):
    rows = x2d.shape[0]
    n_tiles = rows // tm
    tiles_per_seq = n_tiles // gate.shape[0]
    kern = functools.partial(_merge_prompt_kernel, tm=tm, qm=qm)
    const = lambda shape: pl.BlockSpec(shape, lambda i: (0,) * len(shape),
                                       pipeline_mode=pl.Buffered(1))
    seg = lambda k: pl.BlockSpec((tm, D_MODEL), lambda i: (i, k))
    return pl.pallas_call(
        kern,
        grid=(n_tiles,),
        in_specs=[
            pl.BlockSpec((tm, D_MODEL), lambda i: (i, 0)),
            seg(2), seg(3), seg(4), seg(5), seg(6),
            pl.BlockSpec((tm, D_MODEL), lambda i: (i, 0)),
            pl.BlockSpec((1, 1, D_MODEL), lambda i: (i // tiles_per_seq, 0, 0)),
            const((1, D_MODEL)), const((1, D_MODEL)), const((1, D_MODEL)),
            const((MLP_GROUPS, MLP_CHUNK, MLP_CHUNK)),
            const((qm, D_MODEL)),
            const((D_MODEL, D_MODEL)), const((D_MODEL, D_MODEL)), const((D_MODEL, D_MODEL)),
        ],
        out_specs=pl.BlockSpec((tm, D_MODEL), lambda i: (i, 0)),
        out_shape=jax.ShapeDtypeStruct((rows, D_MODEL), F32),
        scratch_shapes=[pltpu.VMEM((tm, D_MODEL), BF16)],
        compiler_params=pltpu.CompilerParams(
            dimension_semantics=("arbitrary",), vmem_limit_bytes=VMEM_LIMIT),
        name="merge_prompt",
    )(yssd, act, act, act, act, act, x2d, gate, ln_g, ln_b, g_final, w_spatial, bsp_e, wbs, wbm, wo)


def _expansion_matrix():
    e = np.zeros((LANES, D_MODEL), np.float32)
    cols = np.arange(D_MODEL)
    for piece in range(3):
        e[piece * SSD_HEADS + cols // SSD_HEAD_DIM, cols] = 1.0
    return jnp.asarray(e, BF16)


def kernel(x_prompt, x_sample, state_ssm, cache_conv, c_prompt, c_sample, w_ada, b_ada, g_pre, w_in,
           conv_w, conv_b, dt_bias, a_log, d_skip, norm_ssd, ln_g, ln_b, w_spatial, b_spatial,
           w_bp_ssd, w_bp_mlp, w_o, g_final):
    depth = w_ada.shape[0]
    assert depth == 1, "single-layer trunk only"
    bp, seq, d = x_prompt.shape
    bs, dec_seq, _ = x_sample.shape
    assert d == D_MODEL and seq % 1024 == 0 and dec_seq % 16 == 0 and dec_seq <= SSD_BLOCK

    w_t = jnp.swapaxes(w_in[0], 0, 1)
    w_dt = jnp.pad(w_t[DT_OFFSET:DT_OFFSET + SSD_HEADS], ((0, LANES - SSD_HEADS), (0, 0))).astype(BF16)
    pad_heads = lambda v: jnp.pad(v.reshape(1, SSD_HEADS), ((0, 0), (0, LANES - SSD_HEADS)))
    wts = dict(
        g_pre=g_pre[0].reshape(1, d), w_t=w_t, w_dt=w_dt, dt_bias=pad_heads(dt_bias[0]),
        conv_w=conv_w[0], conv_b=conv_b[0].reshape(1, CONV_DIM),
        a_log=pad_heads(a_log[0]), dskip_e=jnp.repeat(d_skip[0], SSD_HEAD_DIM).reshape(1, d),
        norm_ssd=norm_ssd[0].reshape(1, d), e3=_expansion_matrix(),
        ln_g=ln_g[0].reshape(1, d), ln_b=ln_b[0].reshape(1, d), g_final=g_final.reshape(1, d),
        w_spatial=w_spatial[0], bsp_e=jnp.repeat(b_spatial[0].T, MLP_GROUP_DIM, axis=1),
    )

    c_all = jnp.concatenate([c_prompt, c_sample], axis=0)
    c_pad = jnp.pad(c_all, ((0, 16 - (bp + bs)), (0, 0)))
    mods = _mods(c_pad, w_ada[0], b_ada[0].reshape(1, 3 * d))

    split_mods = lambda m: (m[:, k * d:(k + 1) * d].reshape(m.shape[0], 1, d) for k in range(3))
    shift_p, scale_p, gate_p = split_mods(mods[:bp])
    shift_s, scale_s, gate_s = split_mods(mods[bp:bp + bs])
    x2p = x_prompt.reshape(bp * seq, d)
    x2s = x_sample.reshape(bs * dec_seq, d)
    rows_s = bs * dec_seq

    hist_p = jnp.zeros((bp, SUBLANES, CONV_DIM), F32)
    h0_p = jnp.zeros((bp, d, D_STATE), F32)
    hist_s = jnp.pad(cache_conv[0], ((0, 0), (SUBLANES - (CONV_W - 1), 0), (0, 0)))
    h0_s = state_ssm[0].reshape(bs, d, D_STATE)

    norm_w = (wts["g_pre"], wts["w_dt"], wts["dt_bias"])
    hp, dt_p = _norm(x2p, scale_p, shift_p, *norm_w, tm=NORM_TM, spt=1)
    hs, dt_s = _norm(x2s, scale_s, shift_s, *norm_w, tm=rows_s, spt=bs)
    act_p, act_s, tail_p, tail_s = _proj(hp, hs, wts["w_t"], wts["conv_w"], wts["conv_b"], hist_p, hist_s,
                                         tm_p=PROJ_TM, nsplit=PROJ_NSPLIT, msplit_p=PROJ_MSPLIT)

    ssd_w = (wts["a_log"], wts["dskip_e"], wts["norm_ssd"], wts["e3"])
    yssd_p, ssm_p, *proj_w = _ssd(act_p, dt_p, *ssd_w, h0_p, n_seq=bp, q_in=SSD_BLOCK,
                                  blocks_per_step=SSD_BLOCKS_PER_STEP,
                                  cast=(w_bp_ssd[0], w_bp_mlp[0], w_o[0]))
    yssd_s, ssm_s = _ssd(act_s, dt_s, *ssd_w, h0_s, n_seq=bs, q_in=dec_seq, blocks_per_step=1)

    merge_w = (wts["ln_g"], wts["ln_b"], wts["g_final"], wts["w_spatial"])
    yp = _merge_prompt(yssd_p, act_p, x2p, gate_p, *merge_w, wts["bsp_e"], *proj_w,
                       tm=MERGE_TM, qm=MLP_CHUNK)
    ys, vn_s = _merge(yssd_s, act_s, x2s, gate_s, *merge_w, wts["bsp_e"][:dec_seq], *proj_w,
                      qm=dec_seq, spt=bs)

    conv_cols = slice(CONV_STEPS[0] * PROJ_TN, CONV_STEPS[0] * PROJ_TN + CONV_DIM)
    hist_rows = slice(SUBLANES - (CONV_W - 1), SUBLANES)
    tiles_per_seq = seq // PROJ_TM
    conv_p = tail_p[tiles_per_seq - 1::tiles_per_seq, hist_rows, conv_cols]
    conv_s = tail_s[:, hist_rows, conv_cols]

    state_shape = (SSD_HEADS, SSD_HEAD_DIM, D_STATE)
    return (yp.reshape(bp, seq, d), ys.reshape(bs, dec_seq, d),
            ssm_p.reshape(1, bp, *state_shape), conv_p[None],
            ssm_s.reshape(1, bs, *state_shape), conv_s[None], vn_s.reshape(1, bs, dec_seq, d))
```
